```python
import jax, jax.numpy as jnp
from jax import lax
import numpy as np

D_MODEL = 2048
BATCH = 8
SEQ = 8192
DEPTH = 2

CHUNK = 64
N_MIXERS = 2
N_RET = (DEPTH + 1) // N_MIXERS
N_GDN = DEPTH // N_MIXERS

RET_HEADS = 8
RET_DK = D_MODEL // RET_HEADS
RET_DV = 2 * RET_DK
RET_QK = RET_HEADS * RET_DK
RET_VW = RET_HEADS * RET_DV
ROPE_BASE = 10000.0
GN_EPS = 1e-6

GDN_K_HEADS = 16
GDN_V_HEADS = 32
GDN_DK = D_MODEL // GDN_K_HEADS
GDN_DV = 2 * D_MODEL // GDN_V_HEADS
GDN_QK = GDN_K_HEADS * GDN_DK
GDN_VW = GDN_V_HEADS * GDN_DV
GDN_QKV = 2 * GDN_QK + GDN_VW
GDN_CONV = 4
RMS_EPS = 1e-6
L2_EPS = 1e-6

D_FF = 4 * D_MODEL

DN_ALPHA = (2.0 * DEPTH) ** 0.25
DN_BETA = (8.0 * DEPTH) ** -0.25
LN_EPS = 1e-5

kernel_name = 'chunk_causal_retention_gdn_hybrid'

F32 = jnp.float32


def layer_norm(x, g, b):
    xf = x.astype(F32)
    mu = jnp.mean(xf, -1, keepdims=True)
    var = jnp.mean(jnp.square(xf - mu), -1, keepdims=True)
    return ((xf - mu) * lax.rsqrt(var + LN_EPS) * g + b).astype(x.dtype)


def rotary(t, pos):
    half = t.shape[-1] // 2
    inv = ROPE_BASE ** (-jnp.arange(half, dtype=F32) / half)
    ang = pos.astype(F32)[:, None] * inv[None, :]
    cos = jnp.cos(ang)[None, :, None, :]
    sin = jnp.sin(ang)[None, :, None, :]
    t1, t2 = t[..., :half], t[..., half:]
    return jnp.concatenate([t1 * cos - t2 * sin, t1 * sin + t2 * cos], axis=-1)


def l2norm(t):
    return t * lax.rsqrt(jnp.sum(jnp.square(t), -1, keepdims=True) + L2_EPS)


def to_chunks(t):
    b, s, h, d = t.shape
    return t.reshape(b, s // CHUNK, CHUNK, h, d).transpose(0, 3, 1, 2, 4)


def to_chunks_scalar(t):
    b, s, h = t.shape
    return t.reshape(b, s // CHUNK, CHUNK, h).transpose(0, 3, 1, 2)


def from_chunks(t):
    b, h, n, c, d = t.shape
    return t.transpose(0, 2, 3, 1, 4).reshape(b, n * c, h, d)


def retention_mixer(x, w_in, gn_g, w_out):
    B, S, _ = x.shape
    q, k, v, gate = jnp.split(x @ w_in, [RET_QK, 2 * RET_QK, 2 * RET_QK + RET_VW], axis=-1)
    pos = jnp.arange(S)
    q = rotary(q.reshape(B, S, RET_HEADS, RET_DK).astype(F32), pos)
    k = rotary(k.reshape(B, S, RET_HEADS, RET_DK).astype(F32), pos) * (RET_DK ** -0.5)
    v = v.reshape(B, S, RET_HEADS, RET_DV).astype(F32)
    q, k, v = to_chunks(q), to_chunks(k), to_chunks(v)

    log_gamma = jnp.log1p(-jnp.exp2(-5.0 - jnp.arange(RET_HEADS, dtype=F32)))
    idx = jnp.arange(CHUNK, dtype=F32)
    lg = log_gamma[:, None]
    intra = jnp.exp(lg[..., None] * jnp.abs(idx[:, None] - idx[None, :]))
    scores = jnp.einsum('bhncd,bhnmd->bhncm', q, k) * intra[None, :, None]
    y_intra = jnp.einsum('bhncm,bhnme->bhnce', scores, v)

    q_dec = q * jnp.exp(lg * (idx + 1.0))[None, :, None, :, None]
    k_dec = k * jnp.exp(lg * (CHUNK - 1.0 - idx))[None, :, None, :, None]
    chunk_decay = jnp.exp(log_gamma * CHUNK)[None, :, None, None]

    def step(state, inp):
        qc, kc, vc = inp
        y = jnp.einsum('bhcd,bhde->bhce', qc, state)
        state = state * chunk_decay + jnp.einsum('bhcd,bhce->bhde', kc, vc)
        return state, y

    state0 = jnp.zeros((B, RET_HEADS, RET_DK, RET_DV), F32)
    _, y_inter = lax.scan(step, state0, (jnp.moveaxis(q_dec, 2, 0), jnp.moveaxis(k_dec, 2, 0), jnp.moveaxis(v, 2, 0)))
    y = from_chunks(y_intra + jnp.moveaxis(y_inter, 0, 2))

    mu = jnp.mean(y, -1, keepdims=True)
    var = jnp.mean(jnp.square(y - mu), -1, keepdims=True)
    y = ((y - mu) * lax.rsqrt(var + GN_EPS)).reshape(B, S, RET_VW) * gn_g
    return (jax.nn.silu(gate) * y.astype(x.dtype)) @ w_out


def gated_deltanet_mixer(x, w_in, conv_w, a_log, dt_bias, norm_g, w_out):
    B, S, _ = x.shape
    qkv, z, b, a = jnp.split(x @ w_in, [GDN_QKV, GDN_QKV + GDN_VW, GDN_QKV + GDN_VW + GDN_V_HEADS], axis=-1)
    qkv = jax.nn.silu(lax.conv_general_dilated(
        qkv, conv_w[:, None, :], window_strides=(1,), padding=[(GDN_CONV - 1, 0)],
        dimension_numbers=('NWC', 'WIO', 'NWC'), feature_group_count=GDN_QKV))
    q, k, v = jnp.split(qkv.astype(F32), [GDN_QK, 2 * GDN_QK], axis=-1)
    rep = GDN_V_HEADS // GDN_K_HEADS
    q = jnp.repeat(l2norm(q.reshape(B, S, GDN_K_HEADS, GDN_DK)) * (GDN_DK ** -0.5), rep, axis=2)
    k = jnp.repeat(l2norm(k.reshape(B, S, GDN_K_HEADS, GDN_DK)), rep, axis=2)
    v = v.reshape(B, S, GDN_V_HEADS, GDN_DV)
    beta = jax.nn.sigmoid(b.astype(F32))
    g = -jnp.exp(a_log.astype(F32)) * jax.nn.softplus(a.astype(F32) + dt_bias.astype(F32))

    q, k, v = to_chunks(q), to_chunks(k), to_chunks(v)
    beta, g = to_chunks_scalar(beta), to_chunks_scalar(g)
    g_cum = jnp.cumsum(g, axis=-1)
    idx = jnp.arange(CHUNK)
    causal = idx[:, None] >= idx[None, :]
    strict = idx[:, None] > idx[None, :]
    decay = jnp.exp(jnp.where(causal, g_cum[..., :, None] - g_cum[..., None, :], -jnp.inf))

    k_beta = k * beta[..., None]
    a_mat = jnp.where(strict, jnp.einsum('bhncd,bhnmd->bhncm', k_beta, k) * decay, 0.0)
    eye = jnp.eye(CHUNK, dtype=F32)
    t_mat = lax.linalg.triangular_solve(a_mat + eye, jnp.broadcast_to(eye, a_mat.shape),
                                        left_side=True, lower=True, unit_diagonal=True)
    u = jnp.einsum('bhncm,bhnme->bhnce', t_mat, v * beta[..., None])
    w = jnp.einsum('bhncm,bhnmd->bhncd', t_mat, k_beta * jnp.exp(g_cum)[..., None])
    attn = jnp.einsum('bhncd,bhnmd->bhncm', q, k) * decay
    q_dec = q * jnp.exp(g_cum)[..., None]
    k_dec = k * jnp.exp(g_cum[..., -1:] - g_cum)[..., None]
    chunk_decay = jnp.exp(g_cum[..., -1])

    def step(state, inp):
        qd, kd, wc, uc, ac, cd = inp
        v_new = uc - jnp.einsum('bhcd,bhde->bhce', wc, state)
        y = jnp.einsum('bhcd,bhde->bhce', qd, state) + jnp.einsum('bhcm,bhme->bhce', ac, v_new)
        state = state * cd[..., None, None] + jnp.einsum('bhcd,bhce->bhde', kd, v_new)
        return state, y

    xs = (jnp.moveaxis(q_dec, 2, 0), jnp.moveaxis(k_dec, 2, 0), jnp.moveaxis(w, 2, 0),
          jnp.moveaxis(u, 2, 0), jnp.moveaxis(attn, 2, 0), jnp.moveaxis(chunk_decay, 2, 0))
    state0 = jnp.zeros((B, GDN_V_HEADS, GDN_DK, GDN_DV), F32)
    _, y = lax.scan(step, state0, xs)
    y = from_chunks(jnp.moveaxis(y, 0, 2))

    y = y * lax.rsqrt(jnp.mean(jnp.square(y), -1, keepdims=True) + RMS_EPS) * norm_g
    y = y * jax.nn.silu(z.astype(F32).reshape(B, S, GDN_V_HEADS, GDN_DV))
    return y.reshape(B, S, GDN_VW).astype(x.dtype) @ w_out


def sq_relu_mlp(x, w1, w2):
    return jnp.square(jax.nn.relu(x @ w1)) @ w2


def _fwd_setup_inputs(seed: int = 0) -> dict:
    key = jax.random.key(seed)
    ks = jax.random.split(key, 20)
    nrm = jax.random.normal
    x = nrm(ks[0], (BATCH, SEQ, D_MODEL), F32)

    ret_w_in = nrm(ks[1], (N_RET, D_MODEL, 2 * RET_QK + 2 * RET_VW), F32) * D_MODEL ** -0.5
    ret_gn_g = 1.0 + 0.02 * nrm(ks[2], (N_RET, RET_VW), F32)
    ret_w_out = nrm(ks[3], (N_RET, RET_VW, D_MODEL), F32) * (RET_VW ** -0.5 * DN_BETA)

    gdn_w_in = nrm(ks[4], (N_GDN, D_MODEL, GDN_QKV + GDN_VW + 2 * GDN_V_HEADS), F32) * D_MODEL ** -0.5
    gdn_conv_w = nrm(ks[5], (N_GDN, GDN_CONV, GDN_QKV), F32) * GDN_CONV ** -0.5
    gdn_a_log = jnp.log(jax.random.uniform(ks[6], (N_GDN, GDN_V_HEADS), F32, 1.0, 16.0))
    dt = jnp.exp(jax.random.uniform(ks[7], (N_GDN, GDN_V_HEADS), F32, np.log(1e-3), np.log(1e-1)))
    gdn_dt_bias = dt + jnp.log(-jnp.expm1(-dt))
    gdn_norm_g = 1.0 + 0.02 * nrm(ks[8], (N_GDN, GDN_DV), F32)
    gdn_w_out = nrm(ks[9], (N_GDN, GDN_VW, D_MODEL), F32) * (GDN_VW ** -0.5 * DN_BETA)

    ln_mix_g = 1.0 + 0.02 * nrm(ks[10], (DEPTH, D_MODEL), F32)
    ln_mix_b = 0.02 * nrm(ks[11], (DEPTH, D_MODEL), F32)
    mlp_w1 = nrm(ks[12], (DEPTH, D_MODEL, D_FF), F32) * D_MODEL ** -0.5
    mlp_w2 = nrm(ks[13], (DEPTH, D_FF, D_MODEL), F32) * (D_FF ** -0.5 * DN_BETA)
    ln_ffn_g = 1.0 + 0.02 * nrm(ks[14], (DEPTH, D_MODEL), F32)
    ln_ffn_b = 0.02 * nrm(ks[15], (DEPTH, D_MODEL), F32)
    return {'x': x,
            'ret_w_in': ret_w_in, 'ret_gn_g': ret_gn_g, 'ret_w_out': ret_w_out,
            'gdn_w_in': gdn_w_in, 'gdn_conv_w': gdn_conv_w, 'gdn_a_log': gdn_a_log,
            'gdn_dt_bias': gdn_dt_bias, 'gdn_norm_g': gdn_norm_g, 'gdn_w_out': gdn_w_out,
            'ln_mix_g': ln_mix_g, 'ln_mix_b': ln_mix_b, 'mlp_w1': mlp_w1, 'mlp_w2': mlp_w2,
            'ln_ffn_g': ln_ffn_g, 'ln_ffn_b': ln_ffn_b}


def _fwd_reference(x, ret_w_in, ret_gn_g, ret_w_out, gdn_w_in, gdn_conv_w, gdn_a_log,
              gdn_dt_bias, gdn_norm_g, gdn_w_out, ln_mix_g, ln_mix_b, mlp_w1, mlp_w2,
              ln_ffn_g, ln_ffn_b):
    for i in range(DEPTH):
        j = i // N_MIXERS
        if i % N_MIXERS == 0:
            mix = retention_mixer(x, ret_w_in[j], ret_gn_g[j], ret_w_out[j])
        else:
            mix = gated_deltanet_mixer(x, gdn_w_in[j], gdn_conv_w[j], gdn_a_log[j],
                                       gdn_dt_bias[j], gdn_norm_g[j], gdn_w_out[j])
        x = layer_norm(DN_ALPHA * x + mix, ln_mix_g[i], ln_mix_b[i])
        x = layer_norm(DN_ALPHA * x + sq_relu_mlp(x, mlp_w1[i], mlp_w2[i]), ln_ffn_g[i], ln_ffn_b[i])
    return x


import jax as _jax
import jax.numpy as _jnp

TWIN_FORMAT = 'train_step'
FWD_PARAMS = ['x', 'ret_w_in', 'ret_gn_g', 'ret_w_out', 'gdn_w_in', 'gdn_conv_w', 'gdn_a_log', 'gdn_dt_bias', 'gdn_norm_g', 'gdn_w_out', 'ln_mix_g', 'ln_mix_b', 'mlp_w1', 'mlp_w2', 'ln_ffn_g', 'ln_ffn_b']
TWIN_WEIGHTS = ['ret_w_in', 'ret_gn_g', 'ret_w_out', 'gdn_w_in', 'gdn_conv_w', 'gdn_a_log', 'gdn_dt_bias', 'gdn_norm_g', 'gdn_w_out', 'ln_mix_g', 'ln_mix_b', 'mlp_w1', 'mlp_w2', 'ln_ffn_g', 'ln_ffn_b']
TWIN_DIFF_INPUT = 'x'
TWIN_INPUTS = ['x', 'ret_w_in', 'ret_gn_g', 'ret_w_out', 'gdn_w_in', 'gdn_conv_w', 'gdn_a_log', 'gdn_dt_bias', 'gdn_norm_g', 'gdn_w_out', 'ln_mix_g', 'ln_mix_b', 'mlp_w1', 'mlp_w2', 'ln_ffn_g', 'ln_ffn_b', 'loss_target', 'm_ret_w_in', 'm_ret_gn_g', 'm_ret_w_out', 'm_gdn_w_in', 'm_gdn_conv_w', 'm_gdn_a_log', 'm_gdn_dt_bias', 'm_gdn_norm_g', 'm_gdn_w_out', 'm_ln_mix_g', 'm_ln_mix_b', 'm_mlp_w1', 'm_mlp_w2', 'm_ln_ffn_g', 'm_ln_ffn_b', 'v_ret_w_in', 'v_ret_gn_g', 'v_ret_w_out', 'v_gdn_w_in', 'v_gdn_conv_w', 'v_gdn_a_log', 'v_gdn_dt_bias', 'v_gdn_norm_g', 'v_gdn_w_out', 'v_ln_mix_g', 'v_ln_mix_b', 'v_mlp_w1', 'v_mlp_w2', 'v_ln_ffn_g', 'v_ln_ffn_b']
TWIN_OUTPUTS = ['loss', 'grad_x', 'grad_ret_w_in', 'grad_ret_gn_g', 'grad_ret_w_out', 'grad_gdn_w_in', 'grad_gdn_conv_w', 'grad_gdn_a_log', 'grad_gdn_dt_bias', 'grad_gdn_norm_g', 'grad_gdn_w_out', 'grad_ln_mix_g', 'grad_ln_mix_b', 'grad_mlp_w1', 'grad_mlp_w2', 'grad_ln_ffn_g', 'grad_ln_ffn_b', 'delta_ret_w_in', 'delta_ret_gn_g', 'delta_ret_w_out', 'delta_gdn_w_in', 'delta_gdn_conv_w', 'delta_gdn_a_log', 'delta_gdn_dt_bias', 'delta_gdn_norm_g', 'delta_gdn_w_out', 'delta_ln_mix_g', 'delta_ln_mix_b', 'delta_mlp_w1', 'delta_mlp_w2', 'delta_ln_ffn_g', 'delta_ln_ffn_b', 'new_m_ret_w_in', 'new_m_ret_gn_g', 'new_m_ret_w_out', 'new_m_gdn_w_in', 'new_m_gdn_conv_w', 'new_m_gdn_a_log', 'new_m_gdn_dt_bias', 'new_m_gdn_norm_g', 'new_m_gdn_w_out', 'new_m_ln_mix_g', 'new_m_ln_mix_b', 'new_m_mlp_w1', 'new_m_mlp_w2', 'new_m_ln_ffn_g', 'new_m_ln_ffn_b', 'new_v_ret_w_in', 'new_v_ret_gn_g', 'new_v_ret_w_out', 'new_v_gdn_w_in', 'new_v_gdn_conv_w', 'new_v_gdn_a_log', 'new_v_gdn_dt_bias', 'new_v_gdn_norm_g', 'new_v_gdn_w_out', 'new_v_ln_mix_g', 'new_v_ln_mix_b', 'new_v_mlp_w1', 'new_v_mlp_w2', 'new_v_ln_ffn_g', 'new_v_ln_ffn_b']
TWIN_LEAF_KINDS = {'loss': 'loss', 'grad_x': 'grad_x', 'grad_ret_w_in': 'grad_w', 'grad_ret_gn_g': 'grad_w', 'grad_ret_w_out': 'grad_w', 'grad_gdn_w_in': 'grad_w', 'grad_gdn_conv_w': 'grad_w', 'grad_gdn_a_log': 'grad_w', 'grad_gdn_dt_bias': 'grad_w', 'grad_gdn_norm_g': 'grad_w', 'grad_gdn_w_out': 'grad_w', 'grad_ln_mix_g': 'grad_w', 'grad_ln_mix_b': 'grad_w', 'grad_mlp_w1': 'grad_w', 'grad_mlp_w2': 'grad_w', 'grad_ln_ffn_g': 'grad_w', 'grad_ln_ffn_b': 'grad_w', 'delta_ret_w_in': 'delta_w', 'delta_ret_gn_g': 'delta_w', 'delta_ret_w_out': 'delta_w', 'delta_gdn_w_in': 'delta_w', 'delta_gdn_conv_w': 'delta_w', 'delta_gdn_a_log': 'delta_w', 'delta_gdn_dt_bias': 'delta_w', 'delta_gdn_norm_g': 'delta_w', 'delta_gdn_w_out': 'delta_w', 'delta_ln_mix_g': 'delta_w', 'delta_ln_mix_b': 'delta_w', 'delta_mlp_w1': 'delta_w', 'delta_mlp_w2': 'delta_w', 'delta_ln_ffn_g': 'delta_w', 'delta_ln_ffn_b': 'delta_w', 'new_m_ret_w_in': 'new_m', 'new_m_ret_gn_g': 'new_m', 'new_m_ret_w_out': 'new_m', 'new_m_gdn_w_in': 'new_m', 'new_m_gdn_conv_w': 'new_m', 'new_m_gdn_a_log': 'new_m', 'new_m_gdn_dt_bias': 'new_m', 'new_m_gdn_norm_g': 'new_m', 'new_m_gdn_w_out': 'new_m', 'new_m_ln_mix_g': 'new_m', 'new_m_ln_mix_b': 'new_m', 'new_m_mlp_w1': 'new_m', 'new_m_mlp_w2': 'new_m', 'new_m_ln_ffn_g': 'new_m', 'new_m_ln_ffn_b': 'new_m', 'new_v_ret_w_in': 'new_v', 'new_v_ret_gn_g': 'new_v', 'new_v_ret_w_out': 'new_v', 'new_v_gdn_w_in': 'new_v', 'new_v_gdn_conv_w': 'new_v', 'new_v_gdn_a_log': 'new_v', 'new_v_gdn_dt_bias': 'new_v', 'new_v_gdn_norm_g': 'new_v', 'new_v_gdn_w_out': 'new_v', 'new_v_ln_mix_g': 'new_v', 'new_v_ln_mix_b': 'new_v', 'new_v_mlp_w1': 'new_v', 'new_v_mlp_w2': 'new_v', 'new_v_ln_ffn_g': 'new_v', 'new_v_ln_ffn_b': 'new_v'}


def _forward(args):
    return _fwd_reference(*[args[k] for k in FWD_PARAMS])


def _output_shape():
    def fwd():
        inp = _fwd_setup_inputs(0)
        return _fwd_reference(*[inp[k] for k in FWD_PARAMS])
    out = _jax.eval_shape(fwd)
    return out.shape, out.dtype

N_MICROBATCH = 1
ADAM_LR = 0.001
ADAM_B1 = 0.9
ADAM_B2 = 0.999
ADAM_EPS = 1e-08
ADAM_WD = 0.01
ADAM_STEP = 10
PER_EXAMPLE_BATCH_AXIS = {'x': 0, 'loss_target': 0}
SHARED_INPUTS = []
_WEIGHT_DTYPES = {'ret_w_in': _jnp.float32, 'ret_gn_g': _jnp.float32, 'ret_w_out': _jnp.float32, 'gdn_w_in': _jnp.float32, 'gdn_conv_w': _jnp.float32, 'gdn_a_log': _jnp.float32, 'gdn_dt_bias': _jnp.float32, 'gdn_norm_g': _jnp.float32, 'gdn_w_out': _jnp.float32, 'ln_mix_g': _jnp.float32, 'ln_mix_b': _jnp.float32, 'mlp_w1': _jnp.float32, 'mlp_w2': _jnp.float32, 'ln_ffn_g': _jnp.float32, 'ln_ffn_b': _jnp.float32}
MOMENT_SCALE = {'ret_w_in': 2.335187e-02, 'ret_gn_g': 2.023448e-02, 'ret_w_out': 5.655552e-02, 'gdn_w_in': 1.854090e-02, 'gdn_conv_w': 1.923252e-02, 'gdn_a_log': 5.990319e-02, 'gdn_dt_bias': 5.961704e-02, 'gdn_norm_g': 1.381129e-01, 'gdn_w_out': 6.202584e-02, 'ln_mix_g': 8.273889e-01, 'ln_mix_b': 5.035010e-01, 'mlp_w1': 3.036011e-02, 'mlp_w2': 1.678959e-01, 'ln_ffn_g': 2.271724e+01, 'ln_ffn_b': 5.108644e+00}


def _to_microbatches(a, axis):
    t = _jnp.moveaxis(a, axis, 0)
    t = t.reshape((N_MICROBATCH, t.shape[0] // N_MICROBATCH) + t.shape[1:])
    return _jnp.moveaxis(t, 1, axis + 1)


def setup_inputs(seed: int = 0) -> dict:
    inp = _fwd_setup_inputs(seed)
    key = _jax.random.fold_in(_jax.random.key(seed), 7919)
    shape, _ = _output_shape()
    out = dict(inp)
    out["loss_target"] = _jax.random.normal(_jax.random.fold_in(key, 0), shape, _jnp.float32)
    for i, name in enumerate(TWIN_WEIGHTS):
        w = inp[name].astype(_jnp.float32)
        if MOMENT_SCALE is None:
            s = _jnp.sqrt(_jnp.mean(_jnp.square(w)) + 1e-30)
        else:
            s = MOMENT_SCALE[name]
        km, kv = _jax.random.split(_jax.random.fold_in(key, i + 1))
        out[name] = w
        out["m_" + name] = s * _jax.random.normal(km, w.shape, _jnp.float32)
        out["v_" + name] = (s * s) * _jax.random.uniform(kv, w.shape, _jnp.float32, 0.5, 1.5)
    if N_MICROBATCH > 1:
        for name, axis in PER_EXAMPLE_BATCH_AXIS.items():
            out[name] = _to_microbatches(out[name], axis)
    return {'x': out['x'], 'ret_w_in': out['ret_w_in'], 'ret_gn_g': out['ret_gn_g'], 'ret_w_out': out['ret_w_out'], 'gdn_w_in': out['gdn_w_in'], 'gdn_conv_w': out['gdn_conv_w'], 'gdn_a_log': out['gdn_a_log'], 'gdn_dt_bias': out['gdn_dt_bias'], 'gdn_norm_g': out['gdn_norm_g'], 'gdn_w_out': out['gdn_w_out'], 'ln_mix_g': out['ln_mix_g'], 'ln_mix_b': out['ln_mix_b'], 'mlp_w1': out['mlp_w1'], 'mlp_w2': out['mlp_w2'], 'ln_ffn_g': out['ln_ffn_g'], 'ln_ffn_b': out['ln_ffn_b'], 'loss_target': out['loss_target'], 'm_ret_w_in': out['m_ret_w_in'], 'm_ret_gn_g': out['m_ret_gn_g'], 'm_ret_w_out': out['m_ret_w_out'], 'm_gdn_w_in': out['m_gdn_w_in'], 'm_gdn_conv_w': out['m_gdn_conv_w'], 'm_gdn_a_log': out['m_gdn_a_log'], 'm_gdn_dt_bias': out['m_gdn_dt_bias'], 'm_gdn_norm_g': out['m_gdn_norm_g'], 'm_gdn_w_out': out['m_gdn_w_out'], 'm_ln_mix_g': out['m_ln_mix_g'], 'm_ln_mix_b': out['m_ln_mix_b'], 'm_mlp_w1': out['m_mlp_w1'], 'm_mlp_w2': out['m_mlp_w2'], 'm_ln_ffn_g': out['m_ln_ffn_g'], 'm_ln_ffn_b': out['m_ln_ffn_b'], 'v_ret_w_in': out['v_ret_w_in'], 'v_ret_gn_g': out['v_ret_gn_g'], 'v_ret_w_out': out['v_ret_w_out'], 'v_gdn_w_in': out['v_gdn_w_in'], 'v_gdn_conv_w': out['v_gdn_conv_w'], 'v_gdn_a_log': out['v_gdn_a_log'], 'v_gdn_dt_bias': out['v_gdn_dt_bias'], 'v_gdn_norm_g': out['v_gdn_norm_g'], 'v_gdn_w_out': out['v_gdn_w_out'], 'v_ln_mix_g': out['v_ln_mix_g'], 'v_ln_mix_b': out['v_ln_mix_b'], 'v_mlp_w1': out['v_mlp_w1'], 'v_mlp_w2': out['v_mlp_w2'], 'v_ln_ffn_g': out['v_ln_ffn_g'], 'v_ln_ffn_b': out['v_ln_ffn_b']}


def _loss(weights, diff, rest, loss_target):
    with _jax.named_scope("forward"):
        args = {**rest, TWIN_DIFF_INPUT: diff, **{k: w.astype(_WEIGHT_DTYPES[k]) for k, w in weights.items()}}
        y = _forward(args)
    with _jax.named_scope("loss_head"):
        err = _jnp.square(y.astype(_jnp.float32) - loss_target)
        return 0.5 * _jnp.sum(_jnp.mean(err, axis=-1)) if err.ndim else 0.5 * err


def _adamw(w, g, m, v):
    m = ADAM_B1 * m + (1.0 - ADAM_B1) * g
    v = ADAM_B2 * v + (1.0 - ADAM_B2) * _jnp.square(g)
    m_hat = m / (1.0 - ADAM_B1 ** ADAM_STEP)
    v_hat = v / (1.0 - ADAM_B2 ** ADAM_STEP)
    delta = -ADAM_LR * (m_hat / (_jnp.sqrt(v_hat) + ADAM_EPS) + ADAM_WD * w)
    return delta, m, v


def reference(x, ret_w_in, ret_gn_g, ret_w_out, gdn_w_in, gdn_conv_w, gdn_a_log, gdn_dt_bias, gdn_norm_g, gdn_w_out, ln_mix_g, ln_mix_b, mlp_w1, mlp_w2, ln_ffn_g, ln_ffn_b, loss_target, m_ret_w_in, m_ret_gn_g, m_ret_w_out, m_gdn_w_in, m_gdn_conv_w, m_gdn_a_log, m_gdn_dt_bias, m_gdn_norm_g, m_gdn_w_out, m_ln_mix_g, m_ln_mix_b, m_mlp_w1, m_mlp_w2, m_ln_ffn_g, m_ln_ffn_b, v_ret_w_in, v_ret_gn_g, v_ret_w_out, v_gdn_w_in, v_gdn_conv_w, v_gdn_a_log, v_gdn_dt_bias, v_gdn_norm_g, v_gdn_w_out, v_ln_mix_g, v_ln_mix_b, v_mlp_w1, v_mlp_w2, v_ln_ffn_g, v_ln_ffn_b):
    given = dict(x=x, ret_w_in=ret_w_in, ret_gn_g=ret_gn_g, ret_w_out=ret_w_out, gdn_w_in=gdn_w_in, gdn_conv_w=gdn_conv_w, gdn_a_log=gdn_a_log, gdn_dt_bias=gdn_dt_bias, gdn_norm_g=gdn_norm_g, gdn_w_out=gdn_w_out, ln_mix_g=ln_mix_g, ln_mix_b=ln_mix_b, mlp_w1=mlp_w1, mlp_w2=mlp_w2, ln_ffn_g=ln_ffn_g, ln_ffn_b=ln_ffn_b, loss_target=loss_target, m_ret_w_in=m_ret_w_in, m_ret_gn_g=m_ret_gn_g, m_ret_w_out=m_ret_w_out, m_gdn_w_in=m_gdn_w_in, m_gdn_conv_w=m_gdn_conv_w, m_gdn_a_log=m_gdn_a_log, m_gdn_dt_bias=m_gdn_dt_bias, m_gdn_norm_g=m_gdn_norm_g, m_gdn_w_out=m_gdn_w_out, m_ln_mix_g=m_ln_mix_g, m_ln_mix_b=m_ln_mix_b, m_mlp_w1=m_mlp_w1, m_mlp_w2=m_mlp_w2, m_ln_ffn_g=m_ln_ffn_g, m_ln_ffn_b=m_ln_ffn_b, v_ret_w_in=v_ret_w_in, v_ret_gn_g=v_ret_gn_g, v_ret_w_out=v_ret_w_out, v_gdn_w_in=v_gdn_w_in, v_gdn_conv_w=v_gdn_conv_w, v_gdn_a_log=v_gdn_a_log, v_gdn_dt_bias=v_gdn_dt_bias, v_gdn_norm_g=v_gdn_norm_g, v_gdn_w_out=v_gdn_w_out, v_ln_mix_g=v_ln_mix_g, v_ln_mix_b=v_ln_mix_b, v_mlp_w1=v_mlp_w1, v_mlp_w2=v_mlp_w2, v_ln_ffn_g=v_ln_ffn_g, v_ln_ffn_b=v_ln_ffn_b)
    weights = {n: given[n] for n in TWIN_WEIGHTS}
    shared = {n: given[n] for n in SHARED_INPUTS}
    per_example = {n: given[n] for n in ['x']}
    grad_fn = _jax.value_and_grad(_loss, argnums=(0, 1))

    def one_microbatch(ex, loss_target):
        ex = dict(ex)
        diff = ex.pop(TWIN_DIFF_INPUT)
        return grad_fn(weights, diff, {**shared, **ex}, loss_target)

    if N_MICROBATCH == 1:
        loss, (grad_w, grad_x) = one_microbatch(per_example, given["loss_target"])
    else:
        def body(carry, xs):
            loss_sum, grad_sum = carry
            l_k, (gw_k, gx_k) = one_microbatch(xs[0], xs[1])
            with _jax.named_scope("update"):
                return (loss_sum + l_k, _jax.tree.map(_jnp.add, grad_sum, gw_k)), gx_k

        init = (_jnp.zeros((), _jnp.float32), _jax.tree.map(_jnp.zeros_like, weights))
        (loss, grad_w), grad_x = _jax.lax.scan(body, init, (per_example, given["loss_target"]))
    with _jax.named_scope("update"):
        delta_w, new_m, new_v = {}, {}, {}
        for n in TWIN_WEIGHTS:
            delta_w[n], new_m[n], new_v[n] = _adamw(weights[n], grad_w[n], given["m_" + n], given["v_" + n])
    return (loss, grad_x, *[grad_w[n] for n in TWIN_WEIGHTS], *[delta_w[n] for n in TWIN_WEIGHTS],
            *[new_m[n] for n in TWIN_WEIGHTS], *[new_v[n] for n in TWIN_WEIGHTS])
```

```python
import functools
import math

import jax
import jax.numpy as jnp
import numpy as np
from jax import lax
from jax.experimental import pallas as pl
from jax.experimental.pallas import tpu as pltpu

F32 = jnp.float32
BF16 = jnp.bfloat16

N_DEV = 8
D_MODEL = 2048
CHUNK = 64
RET_HEADS = 8
RET_DK = 256
RET_DV = 512
RET_QK = RET_HEADS * RET_DK
RET_VW = RET_HEADS * RET_DV
ROPE_BASE = 10000.0
GN_EPS = 1e-6
GDN_K_HEADS = 16
GDN_V_HEADS = 32
GDN_DK = 128
GDN_DV = 128
GDN_QK = GDN_K_HEADS * GDN_DK
GDN_VW = GDN_V_HEADS * GDN_DV
GDN_QKV = 2 * GDN_QK + GDN_VW
GDN_CONV = 4
RMS_EPS = 1e-6
L2_EPS = 1e-6
D_FF = 4 * D_MODEL
DEPTH = 2
DN_ALPHA = (2.0 * DEPTH) ** 0.25
LN_EPS = 1e-5
ADAM_LR = 0.001
ADAM_B1 = 0.9
ADAM_B2 = 0.999
ADAM_EPS = 1e-08
ADAM_WD = 0.01
ADAM_STEP = 10

VMEM_LIMIT = 56 * 1024 * 1024
PACK_W = 1024
MESH = pl.DeviceIdType.MESH


def _cparams(sem=None):
    return pltpu.CompilerParams(dimension_semantics=sem, vmem_limit_bytes=VMEM_LIMIT)


_NT = (((2,), (2,)), ((0,), (0,)))
_NN = (((2,), (1,)), ((0,), (0,)))
_TN = (((1,), (1,)), ((0,), (0,)))


def _dg(a, b, dims):
    return lax.dot_general(a.astype(BF16), b.astype(BF16), dims, preferred_element_type=F32)


@jax.custom_vjp
def _nt(a, b):
    return _dg(a, b, _NT)


@jax.custom_vjp
def _nn(a, b):
    return _dg(a, b, _NN)


@jax.custom_vjp
def _tn(a, b):
    return _dg(a, b, _TN)


_nt.defvjp(lambda a, b: (_dg(a, b, _NT), (a, b)), lambda r, g: (_nn(g, r[1]), _tn(g, r[0])))
_nn.defvjp(lambda a, b: (_dg(a, b, _NN), (a, b)), lambda r, g: (_nt(g, r[1]), _tn(r[0], g)))
_tn.defvjp(lambda a, b: (_dg(a, b, _TN), (a, b)), lambda r, g: (_nt(r[1], g), _nn(r[0], g)))


def _split2(a):
    hi = a.astype(BF16)
    lo = (a - hi.astype(F32)).astype(BF16)
    return hi, lo


def _mm3(a, b, dims):
    ah, al = _split2(a)
    bh, bl = _split2(b)
    dg = lambda x, y: lax.dot_general(x, y, dims, preferred_element_type=F32)
    return dg(ah, bh) + (dg(ah, bl) + dg(al, bh))


def _iota2(shape, dim):
    return lax.broadcasted_iota(jnp.int32, shape, dim)


def _inv_unit_lower(a):
    c = a.shape[-1]
    eye = (_iota2((c, c), 0) == _iota2((c, c), 1)).astype(F32)
    m = -a
    p = eye + m
    for _ in range(int(math.log2(c)) - 1):
        m = _mm3(m, m, _NN)
        p = p + _mm3(p, m, _NN)
    return p


def _silu(x):
    return x * jax.nn.sigmoid(x)


def _rep2(t):
    h = t.shape[0]
    return jnp.broadcast_to(t[:, None], (h, 2) + t.shape[1:]).reshape((2 * h,) + t.shape[1:])


def _ret_chunk(q1, q2, k1, k2, v, gate, gn_g, s, cos, sin, intra, qdec, kdec, cdec):
    q = jnp.concatenate([q1 * cos - q2 * sin, q1 * sin + q2 * cos], axis=-1)
    k = jnp.concatenate([k1 * cos - k2 * sin, k1 * sin + k2 * cos], axis=-1) * (RET_DK ** -0.5)
    scores = _nt(q, k) * intra
    y = _nn(scores, v) + _nn(q * qdec, s)
    s_new = s * cdec + _tn(k * kdec, v)
    mu = jnp.mean(y, -1, keepdims=True)
    yc = y - mu
    var = jnp.mean(yc * yc, -1, keepdims=True)
    o = _silu(gate) * (yc * lax.rsqrt(var + GN_EPS) * gn_g)
    return o, s_new


def _ret_consts():
    log_gamma = np.log1p(-np.exp2(-5.0 - np.arange(RET_HEADS, dtype=np.float64)))
    idx = np.arange(CHUNK, dtype=np.float64)
    lg = log_gamma[:, None]
    intra = np.exp(lg[..., None] * np.abs(idx[:, None] - idx[None, :]))
    qdec = np.exp(lg * (idx + 1.0))[..., None]
    kdec = np.exp(lg * (CHUNK - 1.0 - idx))[..., None]
    cdec = np.exp(log_gamma * CHUNK)[:, None, None]
    return [jnp.asarray(t, F32) for t in (intra, qdec, kdec, cdec)]


def _rope_tables(seq):
    half = RET_DK // 2
    inv = ROPE_BASE ** (-jnp.arange(half, dtype=F32) / half)
    ang = jnp.arange(seq).astype(F32)[:, None] * inv[None, :]
    return jnp.cos(ang), jnp.sin(ang)


RET_HB = 2


def _ret_load(q_ref, k_ref, v_ref, gate_ref):
    hb, dk, dv, h = RET_HB, RET_DK, RET_DV, RET_DK // 2
    q, k, v, gate = q_ref[...], k_ref[...], v_ref[...], gate_ref[...]
    q1 = jnp.stack([q[:, i * dk:i * dk + h] for i in range(hb)])
    q2 = jnp.stack([q[:, i * dk + h:(i + 1) * dk] for i in range(hb)])
    k1 = jnp.stack([k[:, i * dk:i * dk + h] for i in range(hb)])
    k2 = jnp.stack([k[:, i * dk + h:(i + 1) * dk] for i in range(hb)])
    vs = jnp.stack([v[:, i * dv:(i + 1) * dv] for i in range(hb)])
    gs = jnp.stack([gate[:, i * dv:(i + 1) * dv] for i in range(hb)])
    return q1, q2, k1, k2, vs, gs


def _ret_specs(n_chunks, rev):
    hb = RET_HB
    cidx = (lambda n: n_chunks - 1 - n) if rev else (lambda n: n)
    qw, vw = hb * RET_DK, hb * RET_DV
    tok = [
        pl.BlockSpec((CHUNK, qw), lambda h, n: (cidx(n), h)),
        pl.BlockSpec((CHUNK, qw), lambda h, n: (cidx(n), RET_QK // qw + h)),
        pl.BlockSpec((CHUNK, vw), lambda h, n: (cidx(n), 2 * RET_QK // vw + h)),
        pl.BlockSpec((CHUNK, vw), lambda h, n: (cidx(n), (2 * RET_QK + RET_VW) // vw + h)),
        pl.BlockSpec((CHUNK, RET_DK // 2), lambda h, n: (cidx(n), 0)),
        pl.BlockSpec((CHUNK, RET_DK // 2), lambda h, n: (cidx(n), 0)),
    ]
    const = [
        pl.BlockSpec((hb, CHUNK, CHUNK), lambda h, n: (h, 0, 0)),
        pl.BlockSpec((hb, CHUNK, 1), lambda h, n: (h, 0, 0)),
        pl.BlockSpec((hb, CHUNK, 1), lambda h, n: (h, 0, 0)),
        pl.BlockSpec((hb, 1, 1), lambda h, n: (h, 0, 0)),
        pl.BlockSpec((hb, 1, RET_DV), lambda h, n: (h, 0, 0)),
    ]
    state = pl.BlockSpec((1, hb, RET_DK, RET_DV), lambda h, n: (cidx(n), h, 0, 0))
    return tok, const, state, cidx


def _ret_fwd(p, cos, sin, gn_g):
    seq = p.shape[0]
    nc = seq // CHUNK
    hb = RET_HB
    tok, const, state, _ = _ret_specs(nc, False)

    def body(q_ref, k_ref, v_ref, gate_ref, cos_ref, sin_ref, intra_ref, qdec_ref, kdec_ref, cdec_ref, gng_ref,
             o_ref, ssave_ref, s_scr):
        @pl.when(pl.program_id(1) == 0)
        def _():
            s_scr[...] = jnp.zeros_like(s_scr)

        q1, q2, k1, k2, v, gate = _ret_load(q_ref, k_ref, v_ref, gate_ref)
        s = s_scr[...]
        ssave_ref[0] = s.astype(BF16)
        o, s_new = _ret_chunk(q1, q2, k1, k2, v, gate, gng_ref[...], s, cos_ref[...], sin_ref[...],
                              intra_ref[...], qdec_ref[...], kdec_ref[...], cdec_ref[...])
        s_scr[...] = s_new
        o_ref[...] = jnp.concatenate([o[i] for i in range(hb)], axis=-1).astype(o_ref.dtype)

    return pl.pallas_call(
        body, name="ret_fwd",
        grid=(RET_HEADS // hb, nc),
        in_specs=tok + const,
        out_specs=[pl.BlockSpec((CHUNK, hb * RET_DV), lambda h, n: (n, h)), state],
        out_shape=[jax.ShapeDtypeStruct((seq, RET_VW), BF16),
                   jax.ShapeDtypeStruct((nc, RET_HEADS, RET_DK, RET_DV), BF16)],
        scratch_shapes=[pltpu.VMEM((hb, RET_DK, RET_DV), F32)],
        compiler_params=_cparams(("parallel", "arbitrary")),
    )(p, p, p, p, cos, sin, *_ret_consts(), gn_g.reshape(RET_HEADS, 1, RET_DV))


def _ret_bwd(p, cos, sin, gn_g, ssave, do):
    seq = p.shape[0]
    nc = seq // CHUNK
    hb = RET_HB
    tok, const, state, cidx = _ret_specs(nc, True)

    def body(q_ref, k_ref, v_ref, gate_ref, cos_ref, sin_ref, intra_ref, qdec_ref, kdec_ref, cdec_ref, gng_ref,
             ssave_ref, do_ref, dq_ref, dk_ref, dv_ref, dgate_ref, dgng_ref, ds_scr):
        @pl.when(pl.program_id(1) == 0)
        def _():
            ds_scr[...] = jnp.zeros_like(ds_scr)
            dgng_ref[...] = jnp.zeros_like(dgng_ref)

        q1, q2, k1, k2, v, gate = _ret_load(q_ref, k_ref, v_ref, gate_ref)
        do = do_ref[...]
        dos = jnp.stack([do[:, i * RET_DV:(i + 1) * RET_DV] for i in range(hb)]).astype(F32)
        fn = functools.partial(_ret_chunk, cos=cos_ref[...], sin=sin_ref[...], intra=intra_ref[...],
                               qdec=qdec_ref[...], kdec=kdec_ref[...], cdec=cdec_ref[...])
        _, vjp = jax.vjp(fn, q1, q2, k1, k2, v, gate, gng_ref[...], ssave_ref[0].astype(F32))
        dq1, dq2, dk1, dk2, dv, dgate, dgng, ds = vjp((dos, ds_scr[...]))
        ds_scr[...] = ds
        dgng_ref[...] += dgng
        dq_ref[...] = jnp.concatenate([t[i] for i in range(hb) for t in (dq1, dq2)], axis=-1).astype(dq_ref.dtype)
        dk_ref[...] = jnp.concatenate([t[i] for i in range(hb) for t in (dk1, dk2)], axis=-1).astype(dk_ref.dtype)
        dv_ref[...] = jnp.concatenate([dv[i] for i in range(hb)], axis=-1).astype(dv_ref.dtype)
        dgate_ref[...] = jnp.concatenate([dgate[i] for i in range(hb)], axis=-1).astype(dgate_ref.dtype)

    qw, vw = hb * RET_DK, hb * RET_DV
    return pl.pallas_call(
        body, name="ret_bwd",
        grid=(RET_HEADS // hb, nc),
        in_specs=tok + const + [state, pl.BlockSpec((CHUNK, vw), lambda h, n: (cidx(n), h))],
        out_specs=[pl.BlockSpec((CHUNK, qw), lambda h, n: (cidx(n), h)),
                   pl.BlockSpec((CHUNK, qw), lambda h, n: (cidx(n), h)),
                   pl.BlockSpec((CHUNK, vw), lambda h, n: (cidx(n), h)),
                   pl.BlockSpec((CHUNK, vw), lambda h, n: (cidx(n), h)),
                   pl.BlockSpec((hb, 1, RET_DV), lambda h, n: (h, 0, 0))],
        out_shape=[jax.ShapeDtypeStruct((seq, RET_QK), BF16), jax.ShapeDtypeStruct((seq, RET_QK), BF16),
                   jax.ShapeDtypeStruct((seq, RET_VW), BF16), jax.ShapeDtypeStruct((seq, RET_VW), BF16),
                   jax.ShapeDtypeStruct((RET_HEADS, 1, RET_DV), F32)],
        scratch_shapes=[pltpu.VMEM((hb, RET_DK, RET_DV), F32)],
        compiler_params=_cparams(("parallel", "arbitrary")),
    )(p, p, p, p, cos, sin, *_ret_consts(), gn_g.reshape(RET_HEADS, 1, RET_DV), ssave, do)


def _gdn_common(qr, kr, gc_c, gc_r):
    qn = qr * lax.rsqrt(jnp.sum(qr * qr, -1, keepdims=True) + L2_EPS) * (GDN_DK ** -0.5)
    kn = kr * lax.rsqrt(jnp.sum(kr * kr, -1, keepdims=True) + L2_EPS)
    causal = _iota2((CHUNK, CHUNK), 0) >= _iota2((CHUNK, CHUNK), 1)
    decay = jnp.exp(jnp.where(causal, gc_c - gc_r, -1e30))
    return _rep2(qn), _rep2(kn), decay


def _gdn_a(qr, kr, beta_c, gc_c, gc_r):
    _, k, decay = _gdn_common(qr, kr, gc_c, gc_r)
    strict = _iota2((CHUNK, CHUNK), 0) > _iota2((CHUNK, CHUNK), 1)
    return jnp.where(strict, _nt(k * beta_c, k) * decay, 0.0)


def _gdn_main(qr, kr, v, z, beta_c, gc_c, gc_r, norm_g, t, s):
    q, k, decay = _gdn_common(qr, kr, gc_c, gc_r)
    eg = jnp.exp(gc_c)
    u = _nn(t, v * beta_c)
    w = _nn(t, k * (beta_c * eg))
    attn = _nt(q, k) * decay
    v_new = u - _nn(w, s)
    y = _nn(q * eg, s) + _nn(attn, v_new)
    last = _iota2((1, CHUNK, 1), 1) == CHUNK - 1
    gl = jnp.sum(jnp.where(last, gc_c, 0.0), axis=1, keepdims=True)
    s_new = s * jnp.exp(gl) + _tn(k * jnp.exp(gl - gc_c), v_new)
    yn = y * lax.rsqrt(jnp.mean(y * y, -1, keepdims=True) + RMS_EPS) * norm_g
    return yn * _silu(z), s_new


GDN_HK = 4


def _gdn_load(q_ref, k_ref, v_ref, z_ref):
    hk, hb, d = GDN_HK, 2 * GDN_HK, GDN_DK
    q, k, v, z = q_ref[...], k_ref[...], v_ref[...], z_ref[...]
    qs = jnp.stack([q[:, i * d:(i + 1) * d] for i in range(hk)])
    ks = jnp.stack([k[:, i * d:(i + 1) * d] for i in range(hk)])
    vs = jnp.stack([v[:, i * d:(i + 1) * d] for i in range(hb)])
    zs = jnp.stack([z[:, i * d:(i + 1) * d] for i in range(hb)])
    return qs, ks, vs, zs


def _gdn_specs(n_chunks, rev):
    hk, hb = GDN_HK, 2 * GDN_HK
    cidx = (lambda n: n_chunks - 1 - n) if rev else (lambda n: n)
    qw, vw = hk * GDN_DK, hb * GDN_DV
    tok = [
        pl.BlockSpec((CHUNK, qw), lambda h, n: (cidx(n), h)),
        pl.BlockSpec((CHUNK, qw), lambda h, n: (cidx(n), GDN_QK // qw + h)),
        pl.BlockSpec((CHUNK, vw), lambda h, n: (cidx(n), 2 * GDN_QK // vw + h)),
        pl.BlockSpec((CHUNK, vw), lambda h, n: (cidx(n), GDN_QKV // vw + h)),
        pl.BlockSpec((1, hb, CHUNK, 1), lambda h, n: (cidx(n), h, 0, 0)),
        pl.BlockSpec((1, hb, CHUNK, 1), lambda h, n: (cidx(n), h, 0, 0)),
        pl.BlockSpec((1, hb, 1, CHUNK), lambda h, n: (cidx(n), h, 0, 0)),
        pl.BlockSpec((1, GDN_DV), lambda h, n: (0, 0)),
    ]
    tsave = pl.BlockSpec((1, hb, CHUNK, CHUNK), lambda h, n: (cidx(n), h, 0, 0))
    ssave = pl.BlockSpec((1, hb, GDN_DK, GDN_DV), lambda h, n: (cidx(n), h, 0, 0))
    return tok, tsave, ssave, cidx


def _gdn_fwd(c, p, beta_c, gc_c, gc_r, norm_g):
    seq = c.shape[0]
    nc = seq // CHUNK
    hk, hb = GDN_HK, 2 * GDN_HK
    tok, tsave, ssave, _ = _gdn_specs(nc, False)

    def body(q_ref, k_ref, v_ref, z_ref, beta_ref, gcc_ref, gcr_ref, ng_ref, o_ref, tsave_ref, ssave_ref, s_scr):
        @pl.when(pl.program_id(1) == 0)
        def _():
            s_scr[...] = jnp.zeros_like(s_scr)

        qr, kr, v, z = _gdn_load(q_ref, k_ref, v_ref, z_ref)
        beta, gcc, gcr = beta_ref[0], gcc_ref[0], gcr_ref[0]
        s = s_scr[...]
        ssave_ref[0] = s.astype(BF16)
        t = _inv_unit_lower(_gdn_a(qr, kr, beta, gcc, gcr))
        tsave_ref[0] = t
        o, s_new = _gdn_main(qr, kr, v, z, beta, gcc, gcr, ng_ref[...], t, s)
        s_scr[...] = s_new
        o_ref[...] = jnp.concatenate([o[i] for i in range(hb)], axis=-1).astype(o_ref.dtype)

    return pl.pallas_call(
        body, name="gdn_fwd",
        grid=(GDN_K_HEADS // hk, nc),
        in_specs=tok,
        out_specs=[pl.BlockSpec((CHUNK, hb * GDN_DV), lambda h, n: (n, h)), tsave, ssave],
        out_shape=[jax.ShapeDtypeStruct((seq, GDN_VW), BF16),
                   jax.ShapeDtypeStruct((nc, GDN_V_HEADS, CHUNK, CHUNK), F32),
                   jax.ShapeDtypeStruct((nc, GDN_V_HEADS, GDN_DK, GDN_DV), BF16)],
        scratch_shapes=[pltpu.VMEM((hb, GDN_DK, GDN_DV), F32)],
        compiler_params=_cparams(("parallel", "arbitrary")),
    )(c, c, c, p, beta_c, gc_c, gc_r, norm_g.reshape(1, GDN_DV))


def _gdn_bwd(c, p, beta_c, gc_c, gc_r, norm_g, tsave, ssave, do):
    seq = c.shape[0]
    nc = seq // CHUNK
    hk, hb = GDN_HK, 2 * GDN_HK
    nhb = GDN_K_HEADS // hk
    tok, tsave_spec, ssave_spec, cidx = _gdn_specs(nc, True)

    def body(q_ref, k_ref, v_ref, z_ref, beta_ref, gcc_ref, gcr_ref, ng_ref, t_ref, s_ref, do_ref,
             dq_ref, dk_ref, dv_ref, dz_ref, dbeta_ref, dgcc_ref, dgcr_ref, dng_ref, ds_scr):
        @pl.when(pl.program_id(1) == 0)
        def _():
            ds_scr[...] = jnp.zeros_like(ds_scr)
            dng_ref[...] = jnp.zeros_like(dng_ref)

        qr, kr, v, z = _gdn_load(q_ref, k_ref, v_ref, z_ref)
        beta, gcc, gcr = beta_ref[0], gcc_ref[0], gcr_ref[0]
        t = t_ref[0]
        do = do_ref[...]
        dos = jnp.stack([do[:, i * GDN_DV:(i + 1) * GDN_DV] for i in range(hb)]).astype(F32)
        _, vjp_main = jax.vjp(_gdn_main, qr, kr, v, z, beta, gcc, gcr, ng_ref[...], t, s_ref[0].astype(F32))
        dqr, dkr, dv, dz, dbeta, dgcc, dgcr, dng, dt, ds = vjp_main((dos, ds_scr[...]))
        ds_scr[...] = ds
        da = -_mm3(_mm3(t, dt, _TN), t, _NT)
        _, vjp_a = jax.vjp(_gdn_a, qr, kr, beta, gcc, gcr)
        dqr2, dkr2, dbeta2, dgcc2, dgcr2 = vjp_a(da)
        dng_ref[...] += dng[None]
        dq_ref[...] = jnp.concatenate([(dqr + dqr2)[i] for i in range(hk)], axis=-1).astype(dq_ref.dtype)
        dk_ref[...] = jnp.concatenate([(dkr + dkr2)[i] for i in range(hk)], axis=-1).astype(dk_ref.dtype)
        dv_ref[...] = jnp.concatenate([dv[i] for i in range(hb)], axis=-1).astype(dv_ref.dtype)
        dz_ref[...] = jnp.concatenate([dz[i] for i in range(hb)], axis=-1).astype(dz_ref.dtype)
        dbeta_ref[0] = dbeta + dbeta2
        dgcc_ref[0] = dgcc + dgcc2
        dgcr_ref[0] = dgcr + dgcr2

    qw, vw = hk * GDN_DK, hb * GDN_DV
    col = pl.BlockSpec((1, hb, CHUNK, 1), lambda h, n: (cidx(n), h, 0, 0))
    row = pl.BlockSpec((1, hb, 1, CHUNK), lambda h, n: (cidx(n), h, 0, 0))
    return pl.pallas_call(
        body, name="gdn_bwd",
        grid=(nhb, nc),
        in_specs=tok + [tsave_spec, ssave_spec, pl.BlockSpec((CHUNK, vw), lambda h, n: (cidx(n), h))],
        out_specs=[pl.BlockSpec((CHUNK, qw), lambda h, n: (cidx(n), h)),
                   pl.BlockSpec((CHUNK, qw), lambda h, n: (cidx(n), h)),
                   pl.BlockSpec((CHUNK, vw), lambda h, n: (cidx(n), h)),
                   pl.BlockSpec((CHUNK, vw), lambda h, n: (cidx(n), h)),
                   col, col, row,
                   pl.BlockSpec((1, 1, GDN_DV), lambda h, n: (h, 0, 0))],
        out_shape=[jax.ShapeDtypeStruct((seq, GDN_QK), F32), jax.ShapeDtypeStruct((seq, GDN_QK), F32),
                   jax.ShapeDtypeStruct((seq, GDN_VW), F32), jax.ShapeDtypeStruct((seq, GDN_VW), BF16),
                   jax.ShapeDtypeStruct((nc, GDN_V_HEADS, CHUNK, 1), F32),
                   jax.ShapeDtypeStruct((nc, GDN_V_HEADS, CHUNK, 1), F32),
                   jax.ShapeDtypeStruct((nc, GDN_V_HEADS, 1, CHUNK), F32),
                   jax.ShapeDtypeStruct((nhb, 1, GDN_DV), F32)],
        scratch_shapes=[pltpu.VMEM((hb, GDN_DK, GDN_DV), F32)],
        compiler_params=_cparams(("parallel", "arbitrary")),
    )(c, c, c, p, beta_c, gc_c, gc_r, norm_g.reshape(1, GDN_DV), tsave, ssave, do)


CONV_TB = 512
CONV_CB = 1024
HALO = 8


def _conv_taps(ext, w):
    acc = w[GDN_CONV - 1:GDN_CONV] * ext
    for j in range(GDN_CONV - 1):
        acc = acc + w[j:j + 1] * pltpu.roll(ext, GDN_CONV - 1 - j, 0)
    return acc


def _conv_fwd(p, w):
    seq = p.shape[0]
    tb, cb = min(CONV_TB, seq), CONV_CB

    def body(prev_ref, cur_ref, w_ref, o_ref):
        first = pl.program_id(1) == 0
        prev = jnp.where(first, 0.0, prev_ref[...])
        ext = jnp.concatenate([prev, cur_ref[...]], axis=0)
        o_ref[...] = _silu(_conv_taps(ext, w_ref[...])[HALO:])

    return pl.pallas_call(
        body, name="conv_fwd",
        grid=(GDN_QKV // cb, seq // tb),
        in_specs=[pl.BlockSpec((HALO, cb), lambda j, i: (jnp.maximum(i * (tb // HALO) - 1, 0), j)),
                  pl.BlockSpec((tb, cb), lambda j, i: (i, j)),
                  pl.BlockSpec((GDN_CONV, cb), lambda j, i: (0, j))],
        out_specs=pl.BlockSpec((tb, cb), lambda j, i: (i, j)),
        out_shape=jax.ShapeDtypeStruct((seq, GDN_QKV), F32),
        compiler_params=_cparams(("parallel", "arbitrary")),
    )(p, p, w)


def _conv_bwd(p, dc, w):
    seq = p.shape[0]
    tb, cb = min(CONV_TB, seq), CONV_CB
    nt = seq // tb
    last_halo = seq // HALO - 1

    def body(prev_ref, cur_ref, next_ref, dcur_ref, dnext_ref, w_ref, du_ref, dw_ref):
        i = pl.program_id(1)

        @pl.when(i == 0)
        def _():
            dw_ref[...] = jnp.zeros_like(dw_ref)

        w = w_ref[...]
        prev = jnp.where(i == 0, 0.0, prev_ref[...])
        ext = jnp.concatenate([prev, cur_ref[...], next_ref[...]], axis=0)
        pre = _conv_taps(ext, w)
        dnext = jnp.where(i == nt - 1, 0.0, dnext_ref[...])
        dext = jnp.concatenate([jnp.zeros((HALO, cb), F32), dcur_ref[...], dnext], axis=0)
        sig = jax.nn.sigmoid(pre)
        dpre = dext * (sig * (1.0 + pre * (1.0 - sig)))
        rows = tb + 2 * HALO
        du = w[GDN_CONV - 1:GDN_CONV] * dpre
        for j in range(GDN_CONV - 1):
            du = du + w[j:j + 1] * pltpu.roll(dpre, rows - (GDN_CONV - 1 - j), 0)
        du_ref[...] = du[HALO:HALO + tb].astype(du_ref.dtype)
        dcore = dpre[HALO:HALO + tb]
        dws = []
        for j in range(GDN_CONV):
            sh = ext if j == GDN_CONV - 1 else pltpu.roll(ext, GDN_CONV - 1 - j, 0)
            dws.append(jnp.sum(dcore * sh[HALO:HALO + tb], axis=0, keepdims=True))
        dw_ref[...] += jnp.concatenate(dws, axis=0)

    hb = tb // HALO
    return pl.pallas_call(
        body, name="conv_bwd",
        grid=(GDN_QKV // cb, nt),
        in_specs=[pl.BlockSpec((HALO, cb), lambda j, i: (jnp.maximum(i * hb - 1, 0), j)),
                  pl.BlockSpec((tb, cb), lambda j, i: (i, j)),
                  pl.BlockSpec((HALO, cb), lambda j, i: (jnp.minimum((i + 1) * hb, last_halo), j)),
                  pl.BlockSpec((tb, cb), lambda j, i: (i, j)),
                  pl.BlockSpec((HALO, cb), lambda j, i: (jnp.minimum((i + 1) * hb, last_halo), j)),
                  pl.BlockSpec((GDN_CONV, cb), lambda j, i: (0, j))],
        out_specs=[pl.BlockSpec((tb, cb), lambda j, i: (i, j)),
                   pl.BlockSpec((GDN_CONV, cb), lambda j, i: (0, j))],
        out_shape=[jax.ShapeDtypeStruct((seq, GDN_QKV), BF16), jax.ShapeDtypeStruct((GDN_CONV, GDN_QKV), F32)],
        compiler_params=_cparams(("parallel", "arbitrary")),
    )(p, p, p, dc, dc, w)


GATE_TB = 512


def _split3(g):
    hi = g.astype(BF16)
    r = g - hi.astype(F32)
    mid = r.astype(BF16)
    lo = (r - mid.astype(F32)).astype(BF16)
    return hi, mid, lo


def _tri_chunks(n, upper):
    i, j = _iota2((n, n), 0), _iota2((n, n), 1)
    tri = (i <= j) if upper else (i >= j)
    return jnp.where(tri & ((i // CHUNK) == (j // CHUNK)), 1.0, 0.0).astype(BF16)


def _tri_apply(g, upper):
    tri = _tri_chunks(g.shape[0], upper)
    return sum(jnp.dot(tri, part, preferred_element_type=F32) for part in _split3(g))


@jax.custom_vjp
def _chunk_cumsum(g):
    return _tri_apply(g, False)


_chunk_cumsum.defvjp(lambda g: (_tri_apply(g, False), None), lambda _, d: (_tri_apply(d, True),))


def _gates(b, a, a_log, dt_bias):
    z = a + dt_bias
    softplus = jnp.maximum(z, 0.0) + jnp.log1p(jnp.exp(-jnp.abs(z)))
    g = -jnp.exp(a_log) * softplus
    return jax.nn.sigmoid(b), _chunk_cumsum(g)


def _gates_fwd(b, a, a_log, dt_bias):
    seq, nh = b.shape
    tb = min(GATE_TB, seq)

    def body(b_ref, a_ref, al_ref, dt_ref, beta_ref, gc_ref):
        beta, gc = _gates(b_ref[...], a_ref[...], al_ref[...], dt_ref[...])
        beta_ref[...] = beta
        gc_ref[...] = gc

    tok = pl.BlockSpec((tb, nh), lambda i: (i, 0))
    vec = pl.BlockSpec((1, nh), lambda i: (0, 0))
    return pl.pallas_call(
        body, name="gates_fwd", grid=(seq // tb,),
        in_specs=[tok, tok, vec, vec], out_specs=[tok, tok],
        out_shape=[jax.ShapeDtypeStruct((seq, nh), F32)] * 2,
        compiler_params=_cparams(("parallel",)),
    )(b, a, a_log, dt_bias)


def _gates_bwd(b, a, a_log, dt_bias, dbeta, dgc):
    seq, nh = b.shape
    tb = min(GATE_TB, seq)

    def body(b_ref, a_ref, al_ref, dt_ref, dbeta_ref, dgc_ref, db_ref, da_ref, dal_ref, ddt_ref):
        @pl.when(pl.program_id(0) == 0)
        def _():
            dal_ref[...] = jnp.zeros_like(dal_ref)
            ddt_ref[...] = jnp.zeros_like(ddt_ref)

        _, vjp = jax.vjp(_gates, b_ref[...], a_ref[...], al_ref[...], dt_ref[...])
        db, da, dal, ddt = vjp((dbeta_ref[...], dgc_ref[...]))
        db_ref[...] = db
        da_ref[...] = da
        dal_ref[...] += dal
        ddt_ref[...] += ddt

    tok = pl.BlockSpec((tb, nh), lambda i: (i, 0))
    vec = pl.BlockSpec((1, nh), lambda i: (0, 0))
    return pl.pallas_call(
        body, name="gates_bwd", grid=(seq // tb,),
        in_specs=[tok, tok, vec, vec, tok, tok], out_specs=[tok, tok, vec, vec],
        out_shape=[jax.ShapeDtypeStruct((seq, nh), F32)] * 2 + [jax.ShapeDtypeStruct((1, nh), F32)] * 2,
        compiler_params=_cparams(("arbitrary",)),
    )(b, a, a_log, dt_bias, dbeta, dgc)


LN_TR = 256


def _ln_stats(x, s):
    z = DN_ALPHA * x + s
    mu = jnp.mean(z, -1, keepdims=True)
    zc = z - mu
    var = jnp.mean(zc * zc, -1, keepdims=True)
    rstd = lax.rsqrt(var + LN_EPS)
    return zc * rstd, rstd


def _ln_fwd(x, s, g, b):
    seq, d = x.shape
    tr = min(LN_TR, seq)

    def body(x_ref, s_ref, g_ref, b_ref, o_ref):
        xhat, _ = _ln_stats(x_ref[...], s_ref[...])
        o_ref[...] = xhat * g_ref[...] + b_ref[...]

    tok = pl.BlockSpec((tr, d), lambda i: (i, 0))
    vec = pl.BlockSpec((1, d), lambda i: (0, 0))
    return pl.pallas_call(
        body, name="ln_fwd", grid=(seq // tr,),
        in_specs=[tok, tok, vec, vec], out_specs=tok,
        out_shape=jax.ShapeDtypeStruct((seq, d), F32),
        compiler_params=_cparams(("parallel",)),
    )(x, s, g.reshape(1, d), b.reshape(1, d))


def _ln_bwd(dy, x, s, g):
    seq, d = x.shape
    tr = min(LN_TR, seq)

    def body(dy_ref, x_ref, s_ref, g_ref, dz_ref, dg_ref, db_ref):
        @pl.when(pl.program_id(0) == 0)
        def _():
            dg_ref[...] = jnp.zeros_like(dg_ref)
            db_ref[...] = jnp.zeros_like(db_ref)

        dy = dy_ref[...]
        xhat, rstd = _ln_stats(x_ref[...], s_ref[...])
        dyg = dy * g_ref[...]
        m1 = jnp.mean(dyg, -1, keepdims=True)
        m2 = jnp.mean(dyg * xhat, -1, keepdims=True)
        dz_ref[...] = rstd * (dyg - m1 - xhat * m2)
        dg_ref[...] += jnp.sum(dy * xhat, axis=0, keepdims=True)
        db_ref[...] += jnp.sum(dy, axis=0, keepdims=True)

    tok = pl.BlockSpec((tr, d), lambda i: (i, 0))
    vec = pl.BlockSpec((1, d), lambda i: (0, 0))
    return pl.pallas_call(
        body, name="ln_bwd", grid=(seq // tr,),
        in_specs=[tok, tok, tok, vec], out_specs=[tok, vec, vec],
        out_shape=[jax.ShapeDtypeStruct((seq, d), F32), jax.ShapeDtypeStruct((1, d), F32),
                   jax.ShapeDtypeStruct((1, d), F32)],
        compiler_params=_cparams(("arbitrary",)),
    )(dy, x, s, g.reshape(1, d))


def _loss_head(y, target):
    seq, d = y.shape
    tr = min(LN_TR, seq)

    def body(y_ref, t_ref, loss_ref, dy_ref):
        @pl.when(pl.program_id(0) == 0)
        def _():
            loss_ref[...] = jnp.zeros_like(loss_ref)

        err = y_ref[...] - t_ref[...]
        dy_ref[...] = err * (1.0 / d)
        part = jnp.sum(jnp.sum(err * err, axis=0, keepdims=True), axis=1, keepdims=True)
        loss_ref[...] += part * (0.5 / d)

    tok = pl.BlockSpec((tr, d), lambda i: (i, 0))
    return pl.pallas_call(
        body, name="loss_head", grid=(seq // tr,),
        in_specs=[tok, tok], out_specs=[pl.BlockSpec((8, 128), lambda i: (0, 0)), tok],
        out_shape=[jax.ShapeDtypeStruct((8, 128), F32), jax.ShapeDtypeStruct((seq, d), F32)],
        compiler_params=_cparams(("arbitrary",)),
    )(y, target)


def _matmul(a, b, *, ta=False, tb=False, out_dtypes=(F32,), epilogue=None, extras=(), tm=1024, tn=512, tk=2048,
            name="matmul"):
    m, k = (a.shape[1], a.shape[0]) if ta else a.shape
    n = b.shape[0] if tb else b.shape[1]
    assert k == (b.shape[1] if tb else b.shape[0])
    tm, tn, tk = min(tm, m), min(tn, n), min(tk, k)
    assert m % tm == 0 and n % tn == 0 and k % tk == 0, (m, n, k, tm, tn, tk)
    nk = k // tk
    dims = (((0 if ta else 1,), (1 if tb else 0,)), ((), ()))
    n_ex = len(extras)

    def body(*refs):
        a_ref, b_ref = refs[0], refs[1]
        ex_refs = refs[2:2 + n_ex]
        out_refs = refs[2 + n_ex:-1]
        acc = refs[-1]
        kk = pl.program_id(2)

        @pl.when(kk == 0)
        def _():
            acc[...] = jnp.zeros_like(acc)

        acc[...] += lax.dot_general(a_ref[...].astype(BF16), b_ref[...].astype(BF16), dims,
                                    preferred_element_type=F32)

        @pl.when(kk == nk - 1)
        def _():
            res = acc[...]
            outs = (res,) if epilogue is None else epilogue(res, *[r[...] for r in ex_refs])
            for o_ref, val in zip(out_refs, outs, strict=True):
                o_ref[...] = val.astype(o_ref.dtype)

    a_spec = pl.BlockSpec((tk, tm), lambda i, j, kk: (kk, i)) if ta else pl.BlockSpec((tm, tk), lambda i, j, kk: (i, kk))
    b_spec = pl.BlockSpec((tn, tk), lambda i, j, kk: (j, kk)) if tb else pl.BlockSpec((tk, tn), lambda i, j, kk: (kk, j))
    o_spec = pl.BlockSpec((tm, tn), lambda i, j, kk: (i, j))
    outs = pl.pallas_call(
        body, name=name, grid=(m // tm, n // tn, nk),
        in_specs=[a_spec, b_spec] + [o_spec] * n_ex,
        out_specs=[o_spec] * len(out_dtypes),
        out_shape=[jax.ShapeDtypeStruct((m, n), dt) for dt in out_dtypes],
        scratch_shapes=[pltpu.VMEM((tm, tn), F32)],
        compiler_params=_cparams(("parallel", "parallel", "arbitrary")),
    )(a, b, *extras)
    return outs[0] if len(out_dtypes) == 1 else outs


def _epi_relu2(acc):
    r = jnp.maximum(acc, 0.0)
    return acc, r * r


def _epi_drelu2(acc, pre):
    return (acc * (2.0 * jnp.maximum(pre, 0.0)),)


def _epi_add(scale):
    return lambda acc, other: (acc + scale * other,)


def _adamw(parts, w, m, v, *, rows_per_step, name):
    n_parts, rows, cols = parts.shape
    tr = min(rows_per_step, rows)
    assert rows % tr == 0

    def body(p_ref, w_ref, m_ref, v_ref, g_ref, d_ref, mo_ref, vo_ref):
        g = p_ref[0].astype(F32)
        for i in range(1, n_parts):
            g = g + p_ref[i].astype(F32)
        m_new = ADAM_B1 * m_ref[...] + (1.0 - ADAM_B1) * g
        v_new = ADAM_B2 * v_ref[...] + (1.0 - ADAM_B2) * (g * g)
        m_hat = m_new / (1.0 - ADAM_B1 ** ADAM_STEP)
        v_hat = v_new / (1.0 - ADAM_B2 ** ADAM_STEP)
        g_ref[...] = g
        d_ref[...] = -ADAM_LR * (m_hat / (jnp.sqrt(v_hat) + ADAM_EPS) + ADAM_WD * w_ref[...])
        mo_ref[...] = m_new
        vo_ref[...] = v_new

    blk = pl.BlockSpec((tr, cols), lambda i: (i, 0))
    return pl.pallas_call(
        body, name=name, grid=(rows // tr,),
        in_specs=[pl.BlockSpec((n_parts, tr, cols), lambda i: (0, i, 0)), blk, blk, blk],
        out_specs=[blk] * 4,
        out_shape=[jax.ShapeDtypeStruct((rows, cols), F32)] * 4,
        compiler_params=_cparams(("parallel",)),
    )(parts, w, m, v)


def _position():
    return lax.axis_index("x"), lax.axis_index("y"), lax.axis_index("c")


def _all_gather(block, name):
    rows, cols = block.shape

    def body(x_ref, out_ref, send_sems, recv_sems, local_sem):
        x, y, c = _position()
        me, sibling = (x, y, c), (x, y, 1 - c)
        chips = [(1 - x, y), (x, 1 - y), (1 - x, 1 - y)]

        def slot(px, py, pc):
            return out_ref.at[4 * px + 2 * py + pc]

        def copy(k, blk, to, src=None):
            return pltpu.make_async_remote_copy(
                src_ref=slot(*blk) if src is None else src, dst_ref=slot(*blk),
                send_sem=send_sems.at[k], recv_sem=recv_sems.at[k], device_id=to, device_id_type=MESH)

        mine = pltpu.make_async_copy(x_ref, slot(*me), local_sem)
        mine.start()
        first = [copy(0, me, sibling, src=x_ref)]
        first += [copy(1 + j, me, (*chip, c), src=x_ref) for j, chip in enumerate(chips)]
        for cp in first:
            cp.start()
        passed = [copy(4 + j, (*chip, c), sibling) for j, chip in enumerate(chips)]
        for j, chip in enumerate(chips):
            copy(1 + j, (*chip, c), me).wait_recv()
            passed[j].start()
        copy(0, sibling, me).wait_recv()
        for j, chip in enumerate(chips):
            copy(4 + j, (*chip, 1 - c), me).wait_recv()
        for cp in first + passed:
            cp.wait_send()
        mine.wait()

    return pl.pallas_call(
        body, name=name,
        out_shape=jax.ShapeDtypeStruct((N_DEV, rows, cols), block.dtype),
        in_specs=[pl.BlockSpec(memory_space=pl.ANY)],
        out_specs=pl.BlockSpec(memory_space=pl.ANY),
        scratch_shapes=[pltpu.SemaphoreType.DMA((7,)), pltpu.SemaphoreType.DMA((7,)), pltpu.SemaphoreType.DMA],
    )(block)


def _exchange(parts, name):
    _, rows, cols = parts.shape

    def body(p_ref, out_ref, send_sems, recv_sems, local_sem):
        x, y, c = _position()
        me = 4 * x + 2 * y + c
        mine = pltpu.make_async_copy(p_ref.at[me], out_ref.at[me], local_sem)
        mine.start()
        peers = []
        for k in range(1, N_DEV):
            px = 1 - x if k & 4 else x
            py = 1 - y if k & 2 else y
            pc = 1 - c if k & 1 else c
            peers.append((k - 1, (px, py, pc), 4 * px + 2 * py + pc))

        def copy(k, peer, peer_slot):
            return pltpu.make_async_remote_copy(
                src_ref=p_ref.at[peer_slot], dst_ref=out_ref.at[me],
                send_sem=send_sems.at[k], recv_sem=recv_sems.at[k], device_id=peer, device_id_type=MESH)

        def arrival(k, peer, peer_slot):
            return pltpu.make_async_remote_copy(
                src_ref=p_ref.at[peer_slot], dst_ref=out_ref.at[peer_slot],
                send_sem=send_sems.at[k], recv_sem=recv_sems.at[k], device_id=peer, device_id_type=MESH)

        sends = [copy(*p) for p in peers]
        for cp in sends:
            cp.start()
        for p in peers:
            arrival(*p).wait_recv()
        for cp in sends:
            cp.wait_send()
        mine.wait()

    return pl.pallas_call(
        body, name=name,
        out_shape=jax.ShapeDtypeStruct(parts.shape, parts.dtype),
        in_specs=[pl.BlockSpec(memory_space=pl.ANY)],
        out_specs=pl.BlockSpec(memory_space=pl.ANY),
        scratch_shapes=[pltpu.SemaphoreType.DMA((7,)), pltpu.SemaphoreType.DMA((7,)), pltpu.SemaphoreType.DMA],
    )(parts)


RET_IN_W = 2 * RET_QK + 2 * RET_VW
GDN_IN_W = GDN_QKV + GDN_VW + 2 * GDN_V_HEADS
GDN_TAIL = 2 * GDN_V_HEADS
TAIL_PAD = 128


def _pack(ret_in, ret_out, gdn_in, gdn_out, w1, w2):
    return jnp.concatenate([t.reshape(-1, PACK_W) for t in (ret_in, ret_out, gdn_in, gdn_out, w1, w2)], axis=0)


def _pack_rows():
    shard = lambda n: n // N_DEV // PACK_W
    sizes = [D_MODEL * RET_IN_W, RET_VW * D_MODEL, D_MODEL * GDN_IN_W, GDN_VW * D_MODEL,
             DEPTH * D_MODEL * D_FF, DEPTH * D_FF * D_MODEL]
    rows = [shard(s) for s in sizes]
    return rows, np.concatenate([[0], np.cumsum(rows)])


def _unpack_shards(flat):
    _, off = _pack_rows()
    seg = lambda i: flat[off[i]:off[i + 1]]
    return (seg(0).reshape(1, D_MODEL, RET_IN_W // N_DEV), seg(1).reshape(1, RET_VW // N_DEV, D_MODEL),
            seg(2).reshape(1, D_MODEL, GDN_IN_W // N_DEV), seg(3).reshape(1, GDN_VW // N_DEV, D_MODEL),
            seg(4).reshape(DEPTH, D_MODEL, D_FF // N_DEV), seg(5).reshape(DEPTH, D_FF // N_DEV, D_MODEL))


def _unpack_full(allw):
    _, off = _pack_rows()
    seg = lambda i: allw[:, off[i]:off[i + 1]]
    cols = lambda t, n: t.reshape(N_DEV, D_MODEL, n // N_DEV).transpose(1, 0, 2).reshape(D_MODEL, n)
    ret_in = cols(seg(0), RET_IN_W)
    ret_out = seg(1).reshape(RET_VW, D_MODEL)
    gdn_in = cols(seg(2), GDN_IN_W)
    gdn_out = seg(3).reshape(GDN_VW, D_MODEL)
    w1 = seg(4).reshape(N_DEV, DEPTH, D_MODEL, D_FF // N_DEV).transpose(1, 2, 0, 3).reshape(DEPTH, D_MODEL, D_FF)
    w2 = seg(5).reshape(N_DEV, DEPTH, D_FF // N_DEV, D_MODEL).transpose(1, 0, 2, 3).reshape(DEPTH, D_FF, D_MODEL)
    return ret_in, ret_out, gdn_in, gdn_out, w1, w2


def _pack_parts(ret_in, ret_out, gdn_in, gdn_out, w1, w2):
    cols = lambda t, n: t.reshape(D_MODEL, N_DEV, n // N_DEV).transpose(1, 0, 2).reshape(N_DEV, -1, PACK_W)
    return jnp.concatenate([
        cols(ret_in, RET_IN_W), ret_out.reshape(N_DEV, -1, PACK_W),
        cols(gdn_in, GDN_IN_W), gdn_out.reshape(N_DEV, -1, PACK_W),
        w1.reshape(DEPTH, D_MODEL, N_DEV, D_FF // N_DEV).transpose(2, 0, 1, 3).reshape(N_DEV, -1, PACK_W),
        w2.reshape(DEPTH, N_DEV, D_FF // N_DEV, D_MODEL).transpose(1, 0, 2, 3).reshape(N_DEV, -1, PACK_W)], axis=1)


SMALL_SIZES = (RET_VW, GDN_V_HEADS, GDN_V_HEADS, GDN_DV, DEPTH * D_MODEL, DEPTH * D_MODEL, DEPTH * D_MODEL,
               DEPTH * D_MODEL, GDN_CONV * GDN_QKV)
SMALL_LANES = 128
SMALL_ROWS = -(-sum(SMALL_SIZES) // (8 * SMALL_LANES)) * 8


def _pack_small(*vecs):
    flat = jnp.concatenate([v.reshape(-1).astype(F32) for v in vecs])
    return jnp.pad(flat, (0, SMALL_ROWS * SMALL_LANES - flat.shape[0])).reshape(SMALL_ROWS, SMALL_LANES)


def _unpack_small(buf, shapes):
    flat, out, at = buf.reshape(-1), [], 0
    for shp in shapes:
        n = int(np.prod(shp))
        out.append(flat[at:at + n].reshape(shp))
        at += n
    return out


def _mlp_fwd(h, w1, w2, name):
    a, r = _matmul(h, w1, out_dtypes=(F32, BF16), epilogue=_epi_relu2, name=name + "_up")
    return a, r, _matmul(r, w2, name=name + "_down")


def _mlp_bwd(dz, h, a, r, w1, w2, name):
    da = _matmul(dz, w2, tb=True, out_dtypes=(BF16,), epilogue=_epi_drelu2, extras=(a,), name=name + "_da")
    dw2 = _matmul(r, dz, ta=True, out_dtypes=(BF16,), name=name + "_dw2")
    dw1 = _matmul(h, da, ta=True, out_dtypes=(BF16,), name=name + "_dw1")
    dh = _matmul(da, w1, tb=True, epilogue=_epi_add(DN_ALPHA), extras=(dz,), name=name + "_dh")
    return dh, dw1, dw2


def _chunk_cols(t):
    seq, nh = t.shape
    t = t.reshape(seq // CHUNK, CHUNK, nh).transpose(0, 2, 1)
    return t[..., None], t[:, :, None, :]


def _unchunk(col, row=None):
    t = col[..., 0] if row is None else col[..., 0] + row[:, :, 0, :]
    nc, nh, _ = t.shape
    return t.transpose(0, 2, 1).reshape(nc * CHUNK, nh)


def kernel(x, ret_w_in, ret_gn_g, ret_w_out, gdn_w_in, gdn_conv_w, gdn_a_log, gdn_dt_bias, gdn_norm_g, gdn_w_out, ln_mix_g, ln_mix_b, mlp_w1, mlp_w2, ln_ffn_g, ln_ffn_b, loss_target, m_ret_w_in, m_ret_gn_g, m_ret_w_out, m_gdn_w_in, m_gdn_conv_w, m_gdn_a_log, m_gdn_dt_bias, m_gdn_norm_g, m_gdn_w_out, m_ln_mix_g, m_ln_mix_b, m_mlp_w1, m_mlp_w2, m_ln_ffn_g, m_ln_ffn_b, v_ret_w_in, v_ret_gn_g, v_ret_w_out, v_gdn_w_in, v_gdn_conv_w, v_gdn_a_log, v_gdn_dt_bias, v_gdn_norm_g, v_gdn_w_out, v_ln_mix_g, v_ln_mix_b, v_mlp_w1, v_mlp_w2, v_ln_ffn_g, v_ln_ffn_b):
    xt, target = x[0], loss_target[0]
    seq = xt.shape[0]
    me = 4 * lax.axis_index("x") + 2 * lax.axis_index("y") + lax.axis_index("c")

    big = (ret_w_in, ret_w_out, gdn_w_in, gdn_w_out, mlp_w1, mlp_w2)
    allw = _all_gather(_pack(*big).astype(BF16), "gather_weights")
    w_ret_in, w_ret_out, w_gdn_in, w_gdn_out, w1, w2 = _unpack_full(allw)
    w_gdn_main = w_gdn_in[:, :GDN_IN_W - GDN_TAIL]
    w_gdn_tail = jnp.pad(w_gdn_in[:, GDN_IN_W - GDN_TAIL:], ((0, 0), (0, TAIL_PAD - GDN_TAIL)))
    conv_blk = jnp.pad(gdn_conv_w[0], ((0, HALO - GDN_CONV), (0, 0)))
    conv_all = _all_gather(conv_blk, "gather_conv")
    conv_w = conv_all[:, :GDN_CONV].transpose(1, 0, 2).reshape(GDN_CONV, GDN_QKV)

    cos, sin = _rope_tables(seq)
    p0 = _matmul(xt, w_ret_in, name="ret_in")
    o0, s0 = _ret_fwd(p0, cos, sin, ret_gn_g[0])
    mix0 = _matmul(o0, w_ret_out, name="ret_out")
    h1 = _ln_fwd(xt, mix0, ln_mix_g[0], ln_mix_b[0])
    a0, r0, m0 = _mlp_fwd(h1, w1[0], w2[0], "mlp0")
    h2 = _ln_fwd(h1, m0, ln_ffn_g[0], ln_ffn_b[0])

    p1 = _matmul(h2, w_gdn_main, name="gdn_in")
    pt = _matmul(h2, w_gdn_tail, name="gdn_in_tail")
    c1 = _conv_fwd(p1, conv_w)
    b_in, a_in = pt[:, :GDN_V_HEADS], pt[:, GDN_V_HEADS:GDN_TAIL]
    beta, gc = _gates_fwd(b_in, a_in, gdn_a_log, gdn_dt_bias)
    beta_c, _ = _chunk_cols(beta)
    gc_c, gc_r = _chunk_cols(gc)
    o1, t1, s1 = _gdn_fwd(c1, p1, beta_c, gc_c, gc_r, gdn_norm_g[0])
    mix1 = _matmul(o1, w_gdn_out, name="gdn_out")
    h3 = _ln_fwd(h2, mix1, ln_mix_g[1], ln_mix_b[1])
    a1, r1, m1 = _mlp_fwd(h3, w1[1], w2[1], "mlp1")
    h4 = _ln_fwd(h3, m1, ln_ffn_g[1], ln_ffn_b[1])
    loss_blk, dh4 = _loss_head(h4, target)
    loss = lax.psum(loss_blk[0, 0], ("x", "y", "c"))

    dz, dg_ffn1, db_ffn1 = _ln_bwd(dh4, h3, m1, ln_ffn_g[1])
    dh3, dw1_1, dw2_1 = _mlp_bwd(dz, h3, a1, r1, w1[1], w2[1], "mlp1")
    dz, dg_mix1, db_mix1 = _ln_bwd(dh3, h2, mix1, ln_mix_g[1])
    do1 = _matmul(dz, w_gdn_out, tb=True, name="gdn_out_do")
    dw_gdn_out = _matmul(o1, dz, ta=True, out_dtypes=(BF16,), name="gdn_out_dw")
    dq, dk, dv, dzg, dbeta_c, dgc_c, dgc_r, dng = _gdn_bwd(c1, p1, beta_c, gc_c, gc_r, gdn_norm_g[0], t1, s1, do1)
    du, dconv = _conv_bwd(p1, jnp.concatenate([dq, dk, dv], axis=-1), conv_w)
    db_in, da_in, dalog, ddt = _gates_bwd(b_in, a_in, gdn_a_log, gdn_dt_bias, _unchunk(dbeta_c), _unchunk(dgc_c, dgc_r))
    dpt = jnp.concatenate([db_in, da_in, jnp.zeros((seq, TAIL_PAD - GDN_TAIL), F32)], axis=-1)
    dp1 = jnp.concatenate([du, dzg], axis=-1)
    dw_gdn_main = _matmul(h2, dp1, ta=True, out_dtypes=(BF16,), name="gdn_in_dw")
    dw_gdn_tail = _matmul(h2, dpt, ta=True, out_dtypes=(BF16,), name="gdn_in_tail_dw")
    dw_gdn_in = jnp.concatenate([dw_gdn_main, dw_gdn_tail[:, :GDN_TAIL]], axis=-1)
    dh2 = _matmul(dpt, w_gdn_tail, tb=True, epilogue=_epi_add(DN_ALPHA), extras=(dz,), name="gdn_in_tail_dh")
    dh2 = _matmul(dp1, w_gdn_main, tb=True, epilogue=_epi_add(1.0), extras=(dh2,), name="gdn_in_dh")

    dz, dg_ffn0, db_ffn0 = _ln_bwd(dh2, h1, m0, ln_ffn_g[0])
    dh1, dw1_0, dw2_0 = _mlp_bwd(dz, h1, a0, r0, w1[0], w2[0], "mlp0")
    dz, dg_mix0, db_mix0 = _ln_bwd(dh1, xt, mix0, ln_mix_g[0])
    do0 = _matmul(dz, w_ret_out, tb=True, name="ret_out_do")
    dw_ret_out = _matmul(o0, dz, ta=True, out_dtypes=(BF16,), name="ret_out_dw")
    dq, dk, dv, dgate, dgng = _ret_bwd(p0, cos, sin, ret_gn_g[0], s0, do0)
    dp0 = jnp.concatenate([dq, dk, dv, dgate], axis=-1)
    dw_ret_in = _matmul(xt, dp0, ta=True, out_dtypes=(BF16,), name="ret_in_dw")
    dx = _matmul(dp0, w_ret_in, tb=True, epilogue=_epi_add(DN_ALPHA), extras=(dz,), name="ret_in_dx")

    parts = _pack_parts(dw_ret_in, dw_ret_out, dw_gdn_in, dw_gdn_out,
                        jnp.stack([dw1_0, dw1_1]), jnp.stack([dw2_0, dw2_1]))
    summed = _exchange(parts, "exchange_grads")
    rows, _ = _pack_rows()
    big_out = _adamw(summed, _pack(*big), _pack(m_ret_w_in, m_ret_w_out, m_gdn_w_in, m_gdn_w_out, m_mlp_w1, m_mlp_w2),
                     _pack(v_ret_w_in, v_ret_w_out, v_gdn_w_in, v_gdn_w_out, v_mlp_w1, v_mlp_w2),
                     rows_per_step=400, name="adamw_big")
    big_out = [_unpack_shards(t) for t in big_out]

    small_w = (ret_gn_g, gdn_a_log, gdn_dt_bias, gdn_norm_g, ln_mix_g, ln_mix_b, ln_ffn_g, ln_ffn_b)
    small_m = (m_ret_gn_g, m_gdn_a_log, m_gdn_dt_bias, m_gdn_norm_g, m_ln_mix_g, m_ln_mix_b, m_ln_ffn_g, m_ln_ffn_b)
    small_v = (v_ret_gn_g, v_gdn_a_log, v_gdn_dt_bias, v_gdn_norm_g, v_ln_mix_g, v_ln_mix_b, v_ln_ffn_g, v_ln_ffn_b)
    small_g = (dgng, dalog, ddt, jnp.sum(dng, axis=0),
               jnp.concatenate([dg_mix0, dg_mix1]), jnp.concatenate([db_mix0, db_mix1]),
               jnp.concatenate([dg_ffn0, dg_ffn1]), jnp.concatenate([db_ffn0, db_ffn1]), dconv)
    small_parts = _all_gather(_pack_small(*small_g), "gather_small_grads")
    zero_conv = jnp.zeros((GDN_CONV, GDN_QKV), F32)
    small_out = _adamw(small_parts, _pack_small(*small_w, zero_conv), _pack_small(*small_m, zero_conv),
                       _pack_small(*small_v, zero_conv), rows_per_step=SMALL_ROWS, name="adamw_small")
    shapes = [t.shape for t in small_w] + [(GDN_CONV, GDN_QKV)]
    small_out = [_unpack_small(t, shapes) for t in small_out]
    conv_g = lax.dynamic_slice(small_out[0][-1], (0, me * (GDN_QKV // N_DEV)), (GDN_CONV, GDN_QKV // N_DEV))
    conv_out = _adamw(conv_g[None], gdn_conv_w[0], m_gdn_conv_w[0], v_gdn_conv_w[0],
                      rows_per_step=GDN_CONV, name="adamw_conv")

    def ordered(kind):
        b, s, cv = big_out[kind], small_out[kind], conv_out[kind][None]
        return [b[0], s[0], b[1], b[2], cv, s[1], s[2], s[3], b[3], s[4], s[5], b[4], b[5], s[6], s[7]]

    return (loss, dx[None], *ordered(0), *ordered(1), *ordered(2), *ordered(3))
```

```python
import functools
import math

import jax
import jax.numpy as jnp
import numpy as np
from jax import lax
from jax.experimental import pallas as pl
from jax.experimental.pallas import tpu as pltpu

F32 = jnp.float32
BF16 = jnp.bfloat16

N_DEV = 8
D_MODEL = 2048
CHUNK = 64
RET_HEADS = 8
RET_DK = 256
RET_DV = 512
RET_QK = RET_HEADS * RET_DK
RET_VW = RET_HEADS * RET_DV
ROPE_BASE = 10000.0
GN_EPS = 1e-6
GDN_K_HEADS = 16
GDN_V_HEADS = 32
GDN_DK = 128
GDN_DV = 128
GDN_QK = GDN_K_HEADS * GDN_DK
GDN_VW = GDN_V_HEADS * GDN_DV
GDN_QKV = 2 * GDN_QK + GDN_VW
GDN_CONV = 4
RMS_EPS = 1e-6
L2_EPS = 1e-6
D_FF = 4 * D_MODEL
DEPTH = 2
DN_ALPHA = (2.0 * DEPTH) ** 0.25
LN_EPS = 1e-5
ADAM_LR = 0.001
ADAM_B1 = 0.9
ADAM_B2 = 0.999
ADAM_EPS = 1e-08
ADAM_WD = 0.01
ADAM_STEP = 10

VMEM_LIMIT = 56 * 1024 * 1024
PACK_W = 1024
MESH = pl.DeviceIdType.MESH


def _cparams(sem=None):
    return pltpu.CompilerParams(dimension_semantics=sem, vmem_limit_bytes=VMEM_LIMIT)


_NT = (((2,), (2,)), ((0,), (0,)))
_NN = (((2,), (1,)), ((0,), (0,)))
_TN = (((1,), (1,)), ((0,), (0,)))


def _dg(a, b, dims):
    return lax.dot_general(a.astype(BF16), b.astype(BF16), dims, preferred_element_type=F32)


@jax.custom_vjp
def _nt(a, b):
    return _dg(a, b, _NT)


@jax.custom_vjp
def _nn(a, b):
    return _dg(a, b, _NN)


@jax.custom_vjp
def _tn(a, b):
    return _dg(a, b, _TN)


_nt.defvjp(lambda a, b: (_dg(a, b, _NT), (a, b)), lambda r, g: (_nn(g, r[1]), _tn(g, r[0])))
_nn.defvjp(lambda a, b: (_dg(a, b, _NN), (a, b)), lambda r, g: (_nt(g, r[1]), _tn(r[0], g)))
_tn.defvjp(lambda a, b: (_dg(a, b, _TN), (a, b)), lambda r, g: (_nt(r[1], g), _nn(r[0], g)))


def _split2(a):
    hi = a.astype(BF16)
    lo = (a - hi.astype(F32)).astype(BF16)
    return hi, lo


def _mm3(a, b, dims):
    ah, al = _split2(a)
    bh, bl = _split2(b)
    dg = lambda x, y: lax.dot_general(x, y, dims, preferred_element_type=F32)
    return dg(ah, bh) + (dg(ah, bl) + dg(al, bh))


def _iota2(shape, dim):
    return lax.broadcasted_iota(jnp.int32, shape, dim)


def _inv_unit_lower(a):
    c = a.shape[-1]
    eye = (_iota2((c, c), 0) == _iota2((c, c), 1)).astype(F32)
    m = -a
    p = eye + m
    for _ in range(int(math.log2(c)) - 1):
        m = _mm3(m, m, _NN)
        p = p + _mm3(p, m, _NN)
    return p


def _silu(x):
    return x * jax.nn.sigmoid(x)


def _rep2(t):
    h = t.shape[0]
    return jnp.broadcast_to(t[:, None], (h, 2) + t.shape[1:]).reshape((2 * h,) + t.shape[1:])


def _ret_chunk(q1, q2, k1, k2, v, gate, gn_g, s, cos, sin, intra, qdec, kdec, cdec):
    q = jnp.concatenate([q1 * cos - q2 * sin, q1 * sin + q2 * cos], axis=-1)
    k = jnp.concatenate([k1 * cos - k2 * sin, k1 * sin + k2 * cos], axis=-1) * (RET_DK ** -0.5)
    scores = _nt(q, k) * intra
    y = _nn(scores, v) + _nn(q * qdec, s)
    s_new = s * cdec + _tn(k * kdec, v)
    mu = jnp.mean(y, -1, keepdims=True)
    yc = y - mu
    var = jnp.mean(yc * yc, -1, keepdims=True)
    o = _silu(gate) * (yc * lax.rsqrt(var + GN_EPS) * gn_g)
    return o, s_new


def _ret_consts():
    log_gamma = np.log1p(-np.exp2(-5.0 - np.arange(RET_HEADS, dtype=np.float64)))
    idx = np.arange(CHUNK, dtype=np.float64)
    lg = log_gamma[:, None]
    intra = np.exp(lg[..., None] * np.abs(idx[:, None] - idx[None, :]))
    qdec = np.exp(lg * (idx + 1.0))[..., None]
    kdec = np.exp(lg * (CHUNK - 1.0 - idx))[..., None]
    cdec = np.exp(log_gamma * CHUNK)[:, None, None]
    return [jnp.asarray(t, F32) for t in (intra, qdec, kdec, cdec)]


def _rope_tables(seq):
    half = RET_DK // 2
    inv = ROPE_BASE ** (-jnp.arange(half, dtype=F32) / half)
    ang = jnp.arange(seq).astype(F32)[:, None] * inv[None, :]
    return jnp.cos(ang), jnp.sin(ang)


RET_HB = 2


def _ret_load(q_ref, k_ref, v_ref, gate_ref):
    hb, dk, dv, h = RET_HB, RET_DK, RET_DV, RET_DK // 2
    q, k, v, gate = q_ref[...], k_ref[...], v_ref[...], gate_ref[...]
    q1 = jnp.stack([q[:, i * dk:i * dk + h] for i in range(hb)])
    q2 = jnp.stack([q[:, i * dk + h:(i + 1) * dk] for i in range(hb)])
    k1 = jnp.stack([k[:, i * dk:i * dk + h] for i in range(hb)])
    k2 = jnp.stack([k[:, i * dk + h:(i + 1) * dk] for i in range(hb)])
    vs = jnp.stack([v[:, i * dv:(i + 1) * dv] for i in range(hb)])
    gs = jnp.stack([gate[:, i * dv:(i + 1) * dv] for i in range(hb)])
    return q1, q2, k1, k2, vs, gs


def _ret_specs(n_chunks, rev):
    hb = RET_HB
    cidx = (lambda n: n_chunks - 1 - n) if rev else (lambda n: n)
    qw, vw = hb * RET_DK, hb * RET_DV
    tok = [
        pl.BlockSpec((CHUNK, qw), lambda h, n: (cidx(n), h)),
        pl.BlockSpec((CHUNK, qw), lambda h, n: (cidx(n), RET_QK // qw + h)),
        pl.BlockSpec((CHUNK, vw), lambda h, n: (cidx(n), 2 * RET_QK // vw + h)),
        pl.BlockSpec((CHUNK, vw), lambda h, n: (cidx(n), (2 * RET_QK + RET_VW) // vw + h)),
        pl.BlockSpec((CHUNK, RET_DK // 2), lambda h, n: (cidx(n), 0)),
        pl.BlockSpec((CHUNK, RET_DK // 2), lambda h, n: (cidx(n), 0)),
    ]
    const = [
        pl.BlockSpec((hb, CHUNK, CHUNK), lambda h, n: (h, 0, 0)),
        pl.BlockSpec((hb, CHUNK, 1), lambda h, n: (h, 0, 0)),
        pl.BlockSpec((hb, CHUNK, 1), lambda h, n: (h, 0, 0)),
        pl.BlockSpec((hb, 1, 1), lambda h, n: (h, 0, 0)),
        pl.BlockSpec((hb, 1, RET_DV), lambda h, n: (h, 0, 0)),
    ]
    state = pl.BlockSpec((1, hb, RET_DK, RET_DV), lambda h, n: (cidx(n), h, 0, 0))
    return tok, const, state, cidx


def _ret_fwd(p, cos, sin, gn_g):
    seq = p.shape[0]
    nc = seq // CHUNK
    hb = RET_HB
    tok, const, state, _ = _ret_specs(nc, False)

    def body(q_ref, k_ref, v_ref, gate_ref, cos_ref, sin_ref, intra_ref, qdec_ref, kdec_ref, cdec_ref, gng_ref,
             o_ref, ssave_ref, s_scr):
        @pl.when(pl.program_id(1) == 0)
        def _():
            s_scr[...] = jnp.zeros_like(s_scr)

        q1, q2, k1, k2, v, gate = _ret_load(q_ref, k_ref, v_ref, gate_ref)
        s = s_scr[...]
        ssave_ref[0] = s.astype(BF16)
        o, s_new = _ret_chunk(q1, q2, k1, k2, v, gate, gng_ref[...], s, cos_ref[...], sin_ref[...],
                              intra_ref[...], qdec_ref[...], kdec_ref[...], cdec_ref[...])
        s_scr[...] = s_new
        o_ref[...] = jnp.concatenate([o[i] for i in range(hb)], axis=-1).astype(o_ref.dtype)

    return pl.pallas_call(
        body, name="ret_fwd",
        grid=(RET_HEADS // hb, nc),
        in_specs=tok + const,
        out_specs=[pl.BlockSpec((CHUNK, hb * RET_DV), lambda h, n: (n, h)), state],
        out_shape=[jax.ShapeDtypeStruct((seq, RET_VW), BF16),
                   jax.ShapeDtypeStruct((nc, RET_HEADS, RET_DK, RET_DV), BF16)],
        scratch_shapes=[pltpu.VMEM((hb, RET_DK, RET_DV), F32)],
        compiler_params=_cparams(("parallel", "arbitrary")),
    )(p, p, p, p, cos, sin, *_ret_consts(), gn_g.reshape(RET_HEADS, 1, RET_DV))


def _ret_bwd(p, cos, sin, gn_g, ssave, do):
    seq = p.shape[0]
    nc = seq // CHUNK
    hb = RET_HB
    tok, const, state, cidx = _ret_specs(nc, True)

    def body(q_ref, k_ref, v_ref, gate_ref, cos_ref, sin_ref, intra_ref, qdec_ref, kdec_ref, cdec_ref, gng_ref,
             ssave_ref, do_ref, dq_ref, dk_ref, dv_ref, dgate_ref, dgng_ref, ds_scr):
        @pl.when(pl.program_id(1) == 0)
        def _():
            ds_scr[...] = jnp.zeros_like(ds_scr)
            dgng_ref[...] = jnp.zeros_like(dgng_ref)

        q1, q2, k1, k2, v, gate = _ret_load(q_ref, k_ref, v_ref, gate_ref)
        do = do_ref[...]
        dos = jnp.stack([do[:, i * RET_DV:(i + 1) * RET_DV] for i in range(hb)]).astype(F32)
        fn = functools.partial(_ret_chunk, cos=cos_ref[...], sin=sin_ref[...], intra=intra_ref[...],
                               qdec=qdec_ref[...], kdec=kdec_ref[...], cdec=cdec_ref[...])
        _, vjp = jax.vjp(fn, q1, q2, k1, k2, v, gate, gng_ref[...], ssave_ref[0].astype(F32))
        dq1, dq2, dk1, dk2, dv, dgate, dgng, ds = vjp((dos, ds_scr[...]))
        ds_scr[...] = ds
        dgng_ref[...] += dgng
        dq_ref[...] = jnp.concatenate([t[i] for i in range(hb) for t in (dq1, dq2)], axis=-1).astype(dq_ref.dtype)
        dk_ref[...] = jnp.concatenate([t[i] for i in range(hb) for t in (dk1, dk2)], axis=-1).astype(dk_ref.dtype)
        dv_ref[...] = jnp.concatenate([dv[i] for i in range(hb)], axis=-1).astype(dv_ref.dtype)
        dgate_ref[...] = jnp.concatenate([dgate[i] for i in range(hb)], axis=-1).astype(dgate_ref.dtype)

    qw, vw = hb * RET_DK, hb * RET_DV
    return pl.pallas_call(
        body, name="ret_bwd",
        grid=(RET_HEADS // hb, nc),
        in_specs=tok + const + [state, pl.BlockSpec((CHUNK, vw), lambda h, n: (cidx(n), h))],
        out_specs=[pl.BlockSpec((CHUNK, qw), lambda h, n: (cidx(n), h)),
                   pl.BlockSpec((CHUNK, qw), lambda h, n: (cidx(n), h)),
                   pl.BlockSpec((CHUNK, vw), lambda h, n: (cidx(n), h)),
                   pl.BlockSpec((CHUNK, vw), lambda h, n: (cidx(n), h)),
                   pl.BlockSpec((hb, 1, RET_DV), lambda h, n: (h, 0, 0))],
        out_shape=[jax.ShapeDtypeStruct((seq, RET_QK), BF16), jax.ShapeDtypeStruct((seq, RET_QK), BF16),
                   jax.ShapeDtypeStruct((seq, RET_VW), BF16), jax.ShapeDtypeStruct((seq, RET_VW), BF16),
                   jax.ShapeDtypeStruct((RET_HEADS, 1, RET_DV), F32)],
        scratch_shapes=[pltpu.VMEM((hb, RET_DK, RET_DV), F32)],
        compiler_params=_cparams(("parallel", "arbitrary")),
    )(p, p, p, p, cos, sin, *_ret_consts(), gn_g.reshape(RET_HEADS, 1, RET_DV), ssave, do)


def _gdn_common(qr, kr, gc_c, gc_r):
    qn = qr * lax.rsqrt(jnp.sum(qr * qr, -1, keepdims=True) + L2_EPS) * (GDN_DK ** -0.5)
    kn = kr * lax.rsqrt(jnp.sum(kr * kr, -1, keepdims=True) + L2_EPS)
    causal = _iota2((CHUNK, CHUNK), 0) >= _iota2((CHUNK, CHUNK), 1)
    decay = jnp.exp(jnp.where(causal, gc_c - gc_r, -1e30))
    return _rep2(qn), _rep2(kn), decay


def _gdn_a(qr, kr, beta_c, gc_c, gc_r):
    _, k, decay = _gdn_common(qr, kr, gc_c, gc_r)
    strict = _iota2((CHUNK, CHUNK), 0) > _iota2((CHUNK, CHUNK), 1)
    return jnp.where(strict, _nt(k * beta_c, k) * decay, 0.0)


def _gdn_main(qr, kr, v, z, beta_c, gc_c, gc_r, norm_g, t, s):
    q, k, decay = _gdn_common(qr, kr, gc_c, gc_r)
    eg = jnp.exp(gc_c)
    u = _nn(t, v * beta_c)
    w = _nn(t, k * (beta_c * eg))
    attn = _nt(q, k) * decay
    v_new = u - _nn(w, s)
    y = _nn(q * eg, s) + _nn(attn, v_new)
    last = _iota2((1, CHUNK, 1), 1) == CHUNK - 1
    gl = jnp.sum(jnp.where(last, gc_c, 0.0), axis=1, keepdims=True)
    s_new = s * jnp.exp(gl) + _tn(k * jnp.exp(gl - gc_c), v_new)
    yn = y * lax.rsqrt(jnp.mean(y * y, -1, keepdims=True) + RMS_EPS) * norm_g
    return yn * _silu(z), s_new


GDN_HK = 4


def _gdn_load(q_ref, k_ref, v_ref, z_ref):
    hk, hb, d = GDN_HK, 2 * GDN_HK, GDN_DK
    q, k, v, z = q_ref[...], k_ref[...], v_ref[...], z_ref[...]
    qs = jnp.stack([q[:, i * d:(i + 1) * d] for i in range(hk)])
    ks = jnp.stack([k[:, i * d:(i + 1) * d] for i in range(hk)])
    vs = jnp.stack([v[:, i * d:(i + 1) * d] for i in range(hb)])
    zs = jnp.stack([z[:, i * d:(i + 1) * d] for i in range(hb)])
    return qs, ks, vs, zs


def _gdn_specs(n_chunks, rev):
    hk, hb = GDN_HK, 2 * GDN_HK
    cidx = (lambda n: n_chunks - 1 - n) if rev else (lambda n: n)
    qw, vw = hk * GDN_DK, hb * GDN_DV
    tok = [
        pl.BlockSpec((CHUNK, qw), lambda h, n: (cidx(n), h)),
        pl.BlockSpec((CHUNK, qw), lambda h, n: (cidx(n), GDN_QK // qw + h)),
        pl.BlockSpec((CHUNK, vw), lambda h, n: (cidx(n), 2 * GDN_QK // vw + h)),
        pl.BlockSpec((CHUNK, vw), lambda h, n: (cidx(n), GDN_QKV // vw + h)),
        pl.BlockSpec((1, hb, CHUNK, 1), lambda h, n: (cidx(n), h, 0, 0)),
        pl.BlockSpec((1, hb, CHUNK, 1), lambda h, n: (cidx(n), h, 0, 0)),
        pl.BlockSpec((1, hb, 1, CHUNK), lambda h, n: (cidx(n), h, 0, 0)),
        pl.BlockSpec((1, GDN_DV), lambda h, n: (0, 0)),
    ]
    tsave = pl.BlockSpec((1, hb, CHUNK, CHUNK), lambda h, n: (cidx(n), h, 0, 0))
    ssave = pl.BlockSpec((1, hb, GDN_DK, GDN_DV), lambda h, n: (cidx(n), h, 0, 0))
    return tok, tsave, ssave, cidx


def _gdn_fwd(c, p, beta_c, gc_c, gc_r, norm_g):
    seq = c.shape[0]
    nc = seq // CHUNK
    hk, hb = GDN_HK, 2 * GDN_HK
    tok, tsave, ssave, _ = _gdn_specs(nc, False)

    def body(q_ref, k_ref, v_ref, z_ref, beta_ref, gcc_ref, gcr_ref, ng_ref, o_ref, tsave_ref, ssave_ref, s_scr):
        @pl.when(pl.program_id(1) == 0)
        def _():
            s_scr[...] = jnp.zeros_like(s_scr)

        qr, kr, v, z = _gdn_load(q_ref, k_ref, v_ref, z_ref)
        beta, gcc, gcr = beta_ref[0], gcc_ref[0], gcr_ref[0]
        s = s_scr[...]
        ssave_ref[0] = s.astype(BF16)
        t = _inv_unit_lower(_gdn_a(qr, kr, beta, gcc, gcr))
        tsave_ref[0] = t
        o, s_new = _gdn_main(qr, kr, v, z, beta, gcc, gcr, ng_ref[...], t, s)
        s_scr[...] = s_new
        o_ref[...] = jnp.concatenate([o[i] for i in range(hb)], axis=-1).astype(o_ref.dtype)

    return pl.pallas_call(
        body, name="gdn_fwd",
        grid=(GDN_K_HEADS // hk, nc),
        in_specs=tok,
        out_specs=[pl.BlockSpec((CHUNK, hb * GDN_DV), lambda h, n: (n, h)), tsave, ssave],
        out_shape=[jax.ShapeDtypeStruct((seq, GDN_VW), BF16),
                   jax.ShapeDtypeStruct((nc, GDN_V_HEADS, CHUNK, CHUNK), F32),
                   jax.ShapeDtypeStruct((nc, GDN_V_HEADS, GDN_DK, GDN_DV), BF16)],
        scratch_shapes=[pltpu.VMEM((hb, GDN_DK, GDN_DV), F32)],
        compiler_params=_cparams(("parallel", "arbitrary")),
    )(c, c, c, p, beta_c, gc_c, gc_r, norm_g.reshape(1, GDN_DV))


def _gdn_bwd(c, p, beta_c, gc_c, gc_r, norm_g, tsave, ssave, do):
    seq = c.shape[0]
    nc = seq // CHUNK
    hk, hb = GDN_HK, 2 * GDN_HK
    nhb = GDN_K_HEADS // hk
    tok, tsave_spec, ssave_spec, cidx = _gdn_specs(nc, True)

    def body(q_ref, k_ref, v_ref, z_ref, beta_ref, gcc_ref, gcr_ref, ng_ref, t_ref, s_ref, do_ref,
             dq_ref, dk_ref, dv_ref, dz_ref, dbeta_ref, dgcc_ref, dgcr_ref, dng_ref, ds_scr):
        @pl.when(pl.program_id(1) == 0)
        def _():
            ds_scr[...] = jnp.zeros_like(ds_scr)
            dng_ref[...] = jnp.zeros_like(dng_ref)

        qr, kr, v, z = _gdn_load(q_ref, k_ref, v_ref, z_ref)
        beta, gcc, gcr = beta_ref[0], gcc_ref[0], gcr_ref[0]
        t = t_ref[0]
        do = do_ref[...]
        dos = jnp.stack([do[:, i * GDN_DV:(i + 1) * GDN_DV] for i in range(hb)]).astype(F32)
        _, vjp_main = jax.vjp(_gdn_main, qr, kr, v, z, beta, gcc, gcr, ng_ref[...], t, s_ref[0].astype(F32))
        dqr, dkr, dv, dz, dbeta, dgcc, dgcr, dng, dt, ds = vjp_main((dos, ds_scr[...]))
        ds_scr[...] = ds
        da = -_mm3(_mm3(t, dt, _TN), t, _NT)
        _, vjp_a = jax.vjp(_gdn_a, qr, kr, beta, gcc, gcr)
        dqr2, dkr2, dbeta2, dgcc2, dgcr2 = vjp_a(da)
        dng_ref[...] += dng[None]
        dq_ref[...] = jnp.concatenate([(dqr + dqr2)[i] for i in range(hk)], axis=-1).astype(dq_ref.dtype)
        dk_ref[...] = jnp.concatenate([(dkr + dkr2)[i] for i in range(hk)], axis=-1).astype(dk_ref.dtype)
        dv_ref[...] = jnp.concatenate([dv[i] for i in range(hb)], axis=-1).astype(dv_ref.dtype)
        dz_ref[...] = jnp.concatenate([dz[i] for i in range(hb)], axis=-1).astype(dz_ref.dtype)
        dbeta_ref[0] = dbeta + dbeta2
        dgcc_ref[0] = dgcc + dgcc2
        dgcr_ref[0] = dgcr + dgcr2

    qw, vw = hk * GDN_DK, hb * GDN_DV
    col = pl.BlockSpec((1, hb, CHUNK, 1), lambda h, n: (cidx(n), h, 0, 0))
    row = pl.BlockSpec((1, hb, 1, CHUNK), lambda h, n: (cidx(n), h, 0, 0))
    return pl.pallas_call(
        body, name="gdn_bwd",
        grid=(nhb, nc),
        in_specs=tok + [tsave_spec, ssave_spec, pl.BlockSpec((CHUNK, vw), lambda h, n: (cidx(n), h))],
        out_specs=[pl.BlockSpec((CHUNK, qw), lambda h, n: (cidx(n), h)),
                   pl.BlockSpec((CHUNK, qw), lambda h, n: (cidx(n), h)),
                   pl.BlockSpec((CHUNK, vw), lambda h, n: (cidx(n), h)),
                   pl.BlockSpec((CHUNK, vw), lambda h, n: (cidx(n), h)),
                   col, col, row,
                   pl.BlockSpec((1, 1, GDN_DV), lambda h, n: (h, 0, 0))],
        out_shape=[jax.ShapeDtypeStruct((seq, GDN_QK), F32), jax.ShapeDtypeStruct((seq, GDN_QK), F32),
                   jax.ShapeDtypeStruct((seq, GDN_VW), F32), jax.ShapeDtypeStruct((seq, GDN_VW), BF16),
                   jax.ShapeDtypeStruct((nc, GDN_V_HEADS, CHUNK, 1), F32),
                   jax.ShapeDtypeStruct((nc, GDN_V_HEADS, CHUNK, 1), F32),
                   jax.ShapeDtypeStruct((nc, GDN_V_HEADS, 1, CHUNK), F32),
                   jax.ShapeDtypeStruct((nhb, 1, GDN_DV), F32)],
        scratch_shapes=[pltpu.VMEM((hb, GDN_DK, GDN_DV), F32)],
        compiler_params=_cparams(("parallel", "arbitrary")),
    )(c, c, c, p, beta_c, gc_c, gc_r, norm_g.reshape(1, GDN_DV), tsave, ssave, do)


CONV_TB = 512
CONV_CB = 1024
HALO = 8


def _conv_taps(ext, w):
    acc = w[GDN_CONV - 1:GDN_CONV] * ext
    for j in range(GDN_CONV - 1):
        acc = acc + w[j:j + 1] * pltpu.roll(ext, GDN_CONV - 1 - j, 0)
    return acc


def _conv_fwd(p, w):
    seq = p.shape[0]
    tb, cb = min(CONV_TB, seq), CONV_CB

    def body(prev_ref, cur_ref, w_ref, o_ref):
        first = pl.program_id(1) == 0
        prev = jnp.where(first, 0.0, prev_ref[...])
        ext = jnp.concatenate([prev, cur_ref[...]], axis=0)
        o_ref[...] = _silu(_conv_taps(ext, w_ref[...])[HALO:])

    return pl.pallas_call(
        body, name="conv_fwd",
        grid=(GDN_QKV // cb, seq // tb),
        in_specs=[pl.BlockSpec((HALO, cb), lambda j, i: (jnp.maximum(i * (tb // HALO) - 1, 0), j)),
                  pl.BlockSpec((tb, cb), lambda j, i: (i, j)),
                  pl.BlockSpec((GDN_CONV, cb), lambda j, i: (0, j))],
        out_specs=pl.BlockSpec((tb, cb), lambda j, i: (i, j)),
        out_shape=jax.ShapeDtypeStruct((seq, GDN_QKV), F32),
        compiler_params=_cparams(("parallel", "arbitrary")),
    )(p, p, w)


def _conv_bwd(p, dc, w):
    seq = p.shape[0]
    tb, cb = min(CONV_TB, seq), CONV_CB
    nt = seq // tb
    last_halo = seq // HALO - 1

    def body(prev_ref, cur_ref, next_ref, dcur_ref, dnext_ref, w_ref, du_ref, dw_ref):
        i = pl.program_id(1)

        @pl.when(i == 0)
        def _():
            dw_ref[...] = jnp.zeros_like(dw_ref)

        w = w_ref[...]
        prev = jnp.where(i == 0, 0.0, prev_ref[...])
        ext = jnp.concatenate([prev, cur_ref[...], next_ref[...]], axis=0)
        pre = _conv_taps(ext, w)
        dnext = jnp.where(i == nt - 1, 0.0, dnext_ref[...])
        dext = jnp.concatenate([jnp.zeros((HALO, cb), F32), dcur_ref[...], dnext], axis=0)
        sig = jax.nn.sigmoid(pre)
        dpre = dext * (sig * (1.0 + pre * (1.0 - sig)))
        rows = tb + 2 * HALO
        du = w[GDN_CONV - 1:GDN_CONV] * dpre
        for j in range(GDN_CONV - 1):
            du = du + w[j:j + 1] * pltpu.roll(dpre, rows - (GDN_CONV - 1 - j), 0)
        du_ref[...] = du[HALO:HALO + tb].astype(du_ref.dtype)
        dcore = dpre[HALO:HALO + tb]
        dws = []
        for j in range(GDN_CONV):
            sh = ext if j == GDN_CONV - 1 else pltpu.roll(ext, GDN_CONV - 1 - j, 0)
            dws.append(jnp.sum(dcore * sh[HALO:HALO + tb], axis=0, keepdims=True))
        dw_ref[...] += jnp.concatenate(dws, axis=0)

    hb = tb // HALO
    return pl.pallas_call(
        body, name="conv_bwd",
        grid=(GDN_QKV // cb, nt),
        in_specs=[pl.BlockSpec((HALO, cb), lambda j, i: (jnp.maximum(i * hb - 1, 0), j)),
                  pl.BlockSpec((tb, cb), lambda j, i: (i, j)),
                  pl.BlockSpec((HALO, cb), lambda j, i: (jnp.minimum((i + 1) * hb, last_halo), j)),
                  pl.BlockSpec((tb, cb), lambda j, i: (i, j)),
                  pl.BlockSpec((HALO, cb), lambda j, i: (jnp.minimum((i + 1) * hb, last_halo), j)),
                  pl.BlockSpec((GDN_CONV, cb), lambda j, i: (0, j))],
        out_specs=[pl.BlockSpec((tb, cb), lambda j, i: (i, j)),
                   pl.BlockSpec((GDN_CONV, cb), lambda j, i: (0, j))],
        out_shape=[jax.ShapeDtypeStruct((seq, GDN_QKV), BF16), jax.ShapeDtypeStruct((GDN_CONV, GDN_QKV), F32)],
        compiler_params=_cparams(("parallel", "arbitrary")),
    )(p, p, p, dc, dc, w)


GATE_TB = 512


def _split3(g):
    hi = g.astype(BF16)
    r = g - hi.astype(F32)
    mid = r.astype(BF16)
    lo = (r - mid.astype(F32)).astype(BF16)
    return hi, mid, lo


def _tri_chunks(n, upper):
    i, j = _iota2((n, n), 0), _iota2((n, n), 1)
    tri = (i <= j) if upper else (i >= j)
    return jnp.where(tri & ((i // CHUNK) == (j // CHUNK)), 1.0, 0.0).astype(BF16)


def _tri_apply(g, upper):
    tri = _tri_chunks(g.shape[0], upper)
    return sum(jnp.dot(tri, part, preferred_element_type=F32) for part in _split3(g))


@jax.custom_vjp
def _chunk_cumsum(g):
    return _tri_apply(g, False)


_chunk_cumsum.defvjp(lambda g: (_tri_apply(g, False), None), lambda _, d: (_tri_apply(d, True),))


def _gates(b, a, a_log, dt_bias):
    z = a + dt_bias
    softplus = jnp.maximum(z, 0.0) + jnp.log1p(jnp.exp(-jnp.abs(z)))
    g = -jnp.exp(a_log) * softplus
    return jax.nn.sigmoid(b), _chunk_cumsum(g)


def _gates_fwd(b, a, a_log, dt_bias):
    seq, nh = b.shape
    tb = min(GATE_TB, seq)

    def body(b_ref, a_ref, al_ref, dt_ref, beta_ref, gc_ref):
        beta, gc = _gates(b_ref[...], a_ref[...], al_ref[...], dt_ref[...])
        beta_ref[...] = beta
        gc_ref[...] = gc

    tok = pl.BlockSpec((tb, nh), lambda i: (i, 0))
    vec = pl.BlockSpec((1, nh), lambda i: (0, 0))
    return pl.pallas_call(
        body, name="gates_fwd", grid=(seq // tb,),
        in_specs=[tok, tok, vec, vec], out_specs=[tok, tok],
        out_shape=[jax.ShapeDtypeStruct((seq, nh), F32)] * 2,
        compiler_params=_cparams(("parallel",)),
    )(b, a, a_log, dt_bias)


def _gates_bwd(b, a, a_log, dt_bias, dbeta, dgc):
    seq, nh = b.shape
    tb = min(GATE_TB, seq)

    def body(b_ref, a_ref, al_ref, dt_ref, dbeta_ref, dgc_ref, db_ref, da_ref, dal_ref, ddt_ref):
        @pl.when(pl.program_id(0) == 0)
        def _():
            dal_ref[...] = jnp.zeros_like(dal_ref)
            ddt_ref[...] = jnp.zeros_like(ddt_ref)

        _, vjp = jax.vjp(_gates, b_ref[...], a_ref[...], al_ref[...], dt_ref[...])
        db, da, dal, ddt = vjp((dbeta_ref[...], dgc_ref[...]))
        db_ref[...] = db
        da_ref[...] = da
        dal_ref[...] += dal
        ddt_ref[...] += ddt

    tok = pl.BlockSpec((tb, nh), lambda i: (i, 0))
    vec = pl.BlockSpec((1, nh), lambda i: (0, 0))
    return pl.pallas_call(
        body, name="gates_bwd", grid=(seq // tb,),
        in_specs=[tok, tok, vec, vec, tok, tok], out_specs=[tok, tok, vec, vec],
        out_shape=[jax.ShapeDtypeStruct((seq, nh), F32)] * 2 + [jax.ShapeDtypeStruct((1, nh), F32)] * 2,
        compiler_params=_cparams(("arbitrary",)),
    )(b, a, a_log, dt_bias, dbeta, dgc)


LN_TR = 256


def _ln_stats(x, s):
    z = DN_ALPHA * x + s
    mu = jnp.mean(z, -1, keepdims=True)
    zc = z - mu
    var = jnp.mean(zc * zc, -1, keepdims=True)
    rstd = lax.rsqrt(var + LN_EPS)
    return zc * rstd, rstd


def _ln_fwd(x, s, g, b):
    seq, d = x.shape
    tr = min(LN_TR, seq)

    def body(x_ref, s_ref, g_ref, b_ref, o_ref):
        xhat, _ = _ln_stats(x_ref[...], s_ref[...])
        o_ref[...] = xhat * g_ref[...] + b_ref[...]

    tok = pl.BlockSpec((tr, d), lambda i: (i, 0))
    vec = pl.BlockSpec((1, d), lambda i: (0, 0))
    return pl.pallas_call(
        body, name="ln_fwd", grid=(seq // tr,),
        in_specs=[tok, tok, vec, vec], out_specs=tok,
        out_shape=jax.ShapeDtypeStruct((seq, d), F32),
        compiler_params=_cparams(("parallel",)),
    )(x, s, g.reshape(1, d), b.reshape(1, d))


def _ln_bwd(dy, x, s, g):
    seq, d = x.shape
    tr = min(LN_TR, seq)

    def body(dy_ref, x_ref, s_ref, g_ref, dz_ref, dg_ref, db_ref):
        @pl.when(pl.program_id(0) == 0)
        def _():
            dg_ref[...] = jnp.zeros_like(dg_ref)
            db_ref[...] = jnp.zeros_like(db_ref)

        dy = dy_ref[...]
        xhat, rstd = _ln_stats(x_ref[...], s_ref[...])
        dyg = dy * g_ref[...]
        m1 = jnp.mean(dyg, -1, keepdims=True)
        m2 = jnp.mean(dyg * xhat, -1, keepdims=True)
        dz_ref[...] = rstd * (dyg - m1 - xhat * m2)
        dg_ref[...] += jnp.sum(dy * xhat, axis=0, keepdims=True)
        db_ref[...] += jnp.sum(dy, axis=0, keepdims=True)

    tok = pl.BlockSpec((tr, d), lambda i: (i, 0))
    vec = pl.BlockSpec((1, d), lambda i: (0, 0))
    return pl.pallas_call(
        body, name="ln_bwd", grid=(seq // tr,),
        in_specs=[tok, tok, tok, vec], out_specs=[tok, vec, vec],
        out_shape=[jax.ShapeDtypeStruct((seq, d), F32), jax.ShapeDtypeStruct((1, d), F32),
                   jax.ShapeDtypeStruct((1, d), F32)],
        compiler_params=_cparams(("arbitrary",)),
    )(dy, x, s, g.reshape(1, d))


def _loss_head(y, target):
    seq, d = y.shape
    tr = min(LN_TR, seq)

    def body(y_ref, t_ref, loss_ref, dy_ref):
        @pl.when(pl.program_id(0) == 0)
        def _():
            loss_ref[...] = jnp.zeros_like(loss_ref)

        err = y_ref[...] - t_ref[...]
        dy_ref[...] = err * (1.0 / d)
        part = jnp.sum(jnp.sum(err * err, axis=0, keepdims=True), axis=1, keepdims=True)
        loss_ref[...] += part * (0.5 / d)

    tok = pl.BlockSpec((tr, d), lambda i: (i, 0))
    return pl.pallas_call(
        body, name="loss_head", grid=(seq // tr,),
        in_specs=[tok, tok], out_specs=[pl.BlockSpec((8, 128), lambda i: (0, 0)), tok],
        out_shape=[jax.ShapeDtypeStruct((8, 128), F32), jax.ShapeDtypeStruct((seq, d), F32)],
        compiler_params=_cparams(("arbitrary",)),
    )(y, target)


COMM_MID = 0.8


def _matmul(a, b, *, ta=False, tb=False, b_sharded=False, out_sharded=False, out_dtypes=(F32,), epilogue=None,
            extras=(), tm=1024, tn=512, tk=2048, name="matmul", comm=None):
    m, k = (a.shape[1], a.shape[0]) if ta else a.shape
    if b_sharded:
        bk, bn = b.shape[1], N_DEV * b.shape[2]
        shard_w = b.shape[2]
    else:
        bk, bn = b.shape
    n = bk if tb else bn
    assert k == (bn if tb else bk), (a.shape, b.shape)
    tm, tn, tk = min(tm, m), min(tn, n), min(tk, k)
    if b_sharded:
        if tb:
            tk = math.gcd(tk, shard_w)
        else:
            tn = math.gcd(tn, shard_w)
    if out_sharded:
        tn = math.gcd(tn, n // N_DEV)
    assert m % tm == 0 and n % tn == 0 and k % tk == 0, (m, n, k, tm, tn, tk)
    ni, nj, nk = m // tm, n // tn, k // tk
    dims = (((0 if ta else 1,), (1 if tb else 0,)), ((), ()))
    n_ex, n_out = len(extras), len(out_dtypes)
    n_ci = len(comm.ins) if comm else 0
    n_co = len(comm.out_shapes) if comm else 0
    total = ni * nj * nk
    mid_step = min(int(COMM_MID * total), total - 1)

    def body(*refs):
        a_ref, b_ref = refs[0], refs[1]
        ex_refs = refs[2:2 + n_ex]
        ci_refs = refs[2 + n_ex:2 + n_ex + n_ci]
        out_refs = refs[2 + n_ex + n_ci:2 + n_ex + n_ci + n_out]
        co_refs = refs[2 + n_ex + n_ci + n_out:2 + n_ex + n_ci + n_out + n_co]
        acc = refs[2 + n_ex + n_ci + n_out + n_co]
        sems = refs[3 + n_ex + n_ci + n_out + n_co:]
        kk = pl.program_id(2)
        step = (pl.program_id(0) * nj + pl.program_id(1)) * nk + kk

        if comm:
            @pl.when(step == 0)
            def _():
                comm.start(ci_refs, co_refs, *sems)

        @pl.when(kk == 0)
        def _():
            acc[...] = jnp.zeros_like(acc)

        acc[...] += lax.dot_general(a_ref[...].astype(BF16), b_ref[...].astype(BF16), dims,
                                    preferred_element_type=F32)

        @pl.when(kk == nk - 1)
        def _():
            res = acc[...]
            outs = (res,) if epilogue is None else epilogue(res, *[r[...] for r in ex_refs])
            for o_ref, val in zip(out_refs, outs, strict=True):
                o_ref[...] = val.astype(o_ref.dtype)

        if comm:
            @pl.when(step == mid_step)
            def _():
                comm.mid(ci_refs, co_refs, *sems)

            @pl.when(step == total - 1)
            def _():
                comm.finish(ci_refs, co_refs, *sems)

    a_spec = pl.BlockSpec((tk, tm), lambda i, j, kk: (kk, i)) if ta else pl.BlockSpec((tm, tk), lambda i, j, kk: (i, kk))
    if b_sharded and tb:
        per = shard_w // tk
        b_spec = pl.BlockSpec((None, tn, tk), lambda i, j, kk: (kk // per, j, kk % per))
    elif b_sharded:
        per = shard_w // tn
        b_spec = pl.BlockSpec((None, tk, tn), lambda i, j, kk: (j // per, kk, j % per))
    elif tb:
        b_spec = pl.BlockSpec((tn, tk), lambda i, j, kk: (j, kk))
    else:
        b_spec = pl.BlockSpec((tk, tn), lambda i, j, kk: (kk, j))
    ex_spec = pl.BlockSpec((tm, tn), lambda i, j, kk: (i, j))
    if out_sharded:
        per_o = n // N_DEV // tn
        o_spec = pl.BlockSpec((None, tm, tn), lambda i, j, kk: (j // per_o, i, j % per_o))
        o_shape = (N_DEV, m, n // N_DEV)
    else:
        o_spec, o_shape = ex_spec, (m, n)
    hbm = pl.BlockSpec(memory_space=pl.ANY)
    outs = pl.pallas_call(
        body, name=name, grid=(ni, nj, nk),
        in_specs=[a_spec, b_spec] + [ex_spec] * n_ex + [hbm] * n_ci,
        out_specs=[o_spec] * n_out + [hbm] * n_co,
        out_shape=[jax.ShapeDtypeStruct(o_shape, dt) for dt in out_dtypes] + (list(comm.out_shapes) if comm else []),
        scratch_shapes=[pltpu.VMEM((tm, tn), F32)] + (list(comm.scratch) if comm else []),
        compiler_params=_cparams(("arbitrary",) * 3 if comm else ("parallel", "parallel", "arbitrary")),
    )(a, b, *extras, *(comm.ins if comm else ()))
    return outs[0] if len(outs) == 1 else outs


def _epi_relu2(acc):
    r = jnp.maximum(acc, 0.0)
    return acc, r * r


def _epi_drelu2(acc, pre):
    return (acc * (2.0 * jnp.maximum(pre, 0.0)),)


def _epi_add(scale):
    return lambda acc, other: (acc + scale * other,)


def _adamw(parts, w, m, v, *, rows_per_step, name):
    n_parts, rows, cols = parts.shape
    tr = min(rows_per_step, rows)
    assert rows % tr == 0

    def body(p_ref, w_ref, m_ref, v_ref, g_ref, d_ref, mo_ref, vo_ref):
        g = p_ref[0].astype(F32)
        for i in range(1, n_parts):
            g = g + p_ref[i].astype(F32)
        m_new = ADAM_B1 * m_ref[...] + (1.0 - ADAM_B1) * g
        v_new = ADAM_B2 * v_ref[...] + (1.0 - ADAM_B2) * (g * g)
        m_hat = m_new / (1.0 - ADAM_B1 ** ADAM_STEP)
        v_hat = v_new / (1.0 - ADAM_B2 ** ADAM_STEP)
        g_ref[...] = g
        d_ref[...] = -ADAM_LR * (m_hat / (jnp.sqrt(v_hat) + ADAM_EPS) + ADAM_WD * w_ref[...])
        mo_ref[...] = m_new
        vo_ref[...] = v_new

    blk = pl.BlockSpec((tr, cols), lambda i: (i, 0))
    return pl.pallas_call(
        body, name=name, grid=(rows // tr,),
        in_specs=[pl.BlockSpec((n_parts, tr, cols), lambda i: (0, i, 0)), blk, blk, blk],
        out_specs=[blk] * 4,
        out_shape=[jax.ShapeDtypeStruct((rows, cols), F32)] * 4,
        compiler_params=_cparams(("parallel",)),
    )(parts, w, m, v)


def _position():
    return lax.axis_index("x"), lax.axis_index("y"), lax.axis_index("c")


def _comm_scratch(n):
    return [pltpu.SemaphoreType.DMA((7 * n,)), pltpu.SemaphoreType.DMA((7 * n,)), pltpu.SemaphoreType.DMA((n,))]


class _Gather:
    def __init__(self, blocks):
        self.ins = list(blocks)
        self.out_shapes = [jax.ShapeDtypeStruct((N_DEV,) + b.shape, b.dtype) for b in blocks]
        self.scratch = _comm_scratch(len(blocks))

    def _plan(self, n, ins, outs, send_sems, recv_sems, local_sems):
        x, y, c = _position()
        me, sibling = (x, y, c), (x, y, 1 - c)
        chips = [(1 - x, y), (x, 1 - y), (1 - x, 1 - y)]
        x_ref, out_ref = ins[n], outs[n]

        def slot(px, py, pc):
            return out_ref.at[4 * px + 2 * py + pc]

        def copy(k, blk, to, src=None):
            return pltpu.make_async_remote_copy(
                src_ref=slot(*blk) if src is None else src, dst_ref=slot(*blk),
                send_sem=send_sems.at[7 * n + k], recv_sem=recv_sems.at[7 * n + k], device_id=to, device_id_type=MESH)

        mine = lambda: pltpu.make_async_copy(x_ref, slot(*me), local_sems.at[n])
        first = lambda: [copy(0, me, sibling, src=x_ref)] + [copy(1 + j, me, (*chip, c), src=x_ref)
                                                             for j, chip in enumerate(chips)]
        passed = lambda j: copy(4 + j, (*chips[j], c), sibling)
        landed = lambda j: copy(1 + j, (*chips[j], c), me)
        from_sibling = lambda: [copy(0, sibling, me)] + [copy(4 + j, (*chip, 1 - c), me) for j, chip in enumerate(chips)]
        return mine, first, passed, landed, from_sibling

    def start(self, ins, outs, *sems):
        for n in range(len(self.ins)):
            mine, first, _, _, _ = self._plan(n, ins, outs, *sems)
            mine().start()
            for cp in first():
                cp.start()

    def mid(self, ins, outs, *sems):
        plans = [self._plan(n, ins, outs, *sems) for n in range(len(self.ins))]
        for j in range(3):
            for _, _, passed, landed, _ in plans:
                landed(j).wait_recv()
                passed(j).start()

    def finish(self, ins, outs, *sems):
        for n in range(len(self.ins)):
            mine, first, passed, _, from_sibling = self._plan(n, ins, outs, *sems)
            for cp in from_sibling():
                cp.wait_recv()
            for cp in first() + [passed(j) for j in range(3)]:
                cp.wait_send()
            mine().wait()


class _Exchange:
    def __init__(self, parts):
        self.ins = list(parts)
        self.out_shapes = [jax.ShapeDtypeStruct(p.shape, p.dtype) for p in parts]
        self.scratch = _comm_scratch(len(parts))

    def _plan(self, n, ins, outs, send_sems, recv_sems, local_sems):
        x, y, c = _position()
        me = 4 * x + 2 * y + c
        p_ref, out_ref = ins[n], outs[n]
        mine = lambda: pltpu.make_async_copy(p_ref.at[me], out_ref.at[me], local_sems.at[n])

        def copies(landing):
            out = []
            for k in range(1, N_DEV):
                px = 1 - x if k & 4 else x
                py = 1 - y if k & 2 else y
                pc = 1 - c if k & 1 else c
                peer_slot = 4 * px + 2 * py + pc
                out.append(pltpu.make_async_remote_copy(
                    src_ref=p_ref.at[peer_slot], dst_ref=out_ref.at[peer_slot if landing else me],
                    send_sem=send_sems.at[7 * n + k - 1], recv_sem=recv_sems.at[7 * n + k - 1],
                    device_id=(px, py, pc), device_id_type=MESH))
            return out

        return mine, copies

    def start(self, ins, outs, *sems):
        for n in range(len(self.ins)):
            mine, copies = self._plan(n, ins, outs, *sems)
            mine().start()
            for cp in copies(False):
                cp.start()

    def mid(self, ins, outs, *sems):
        pass

    def finish(self, ins, outs, *sems):
        for n in range(len(self.ins)):
            mine, copies = self._plan(n, ins, outs, *sems)
            for cp in copies(True):
                cp.wait_recv()
            for cp in copies(False):
                cp.wait_send()
            mine().wait()


def _comm_alone(comm, name):
    def body(*refs):
        n_i, n_o = len(comm.ins), len(comm.out_shapes)
        ins, outs, sems = refs[:n_i], refs[n_i:n_i + n_o], refs[n_i + n_o:]
        comm.start(ins, outs, *sems)
        comm.mid(ins, outs, *sems)
        comm.finish(ins, outs, *sems)

    hbm = pl.BlockSpec(memory_space=pl.ANY)
    return pl.pallas_call(
        body, name=name, out_shape=list(comm.out_shapes),
        in_specs=[hbm] * len(comm.ins), out_specs=[hbm] * len(comm.out_shapes),
        scratch_shapes=list(comm.scratch),
    )(*comm.ins)


def _all_gather_old(block, name):
    rows, cols = block.shape

    def body(x_ref, out_ref, send_sems, recv_sems, local_sem):
        x, y, c = _position()
        me, sibling = (x, y, c), (x, y, 1 - c)
        chips = [(1 - x, y), (x, 1 - y), (1 - x, 1 - y)]

        def slot(px, py, pc):
            return out_ref.at[4 * px + 2 * py + pc]

        def copy(k, blk, to, src=None):
            return pltpu.make_async_remote_copy(
                src_ref=slot(*blk) if src is None else src, dst_ref=slot(*blk),
                send_sem=send_sems.at[k], recv_sem=recv_sems.at[k], device_id=to, device_id_type=MESH)

        mine = pltpu.make_async_copy(x_ref, slot(*me), local_sem)
        mine.start()
        first = [copy(0, me, sibling, src=x_ref)]
        first += [copy(1 + j, me, (*chip, c), src=x_ref) for j, chip in enumerate(chips)]
        for cp in first:
            cp.start()
        passed = [copy(4 + j, (*chip, c), sibling) for j, chip in enumerate(chips)]
        for j, chip in enumerate(chips):
            copy(1 + j, (*chip, c), me).wait_recv()
            passed[j].start()
        copy(0, sibling, me).wait_recv()
        for j, chip in enumerate(chips):
            copy(4 + j, (*chip, 1 - c), me).wait_recv()
        for cp in first + passed:
            cp.wait_send()
        mine.wait()

    return pl.pallas_call(
        body, name=name,
        out_shape=jax.ShapeDtypeStruct((N_DEV, rows, cols), block.dtype),
        in_specs=[pl.BlockSpec(memory_space=pl.ANY)],
        out_specs=pl.BlockSpec(memory_space=pl.ANY),
        scratch_shapes=[pltpu.SemaphoreType.DMA((7,)), pltpu.SemaphoreType.DMA((7,)), pltpu.SemaphoreType.DMA],
    )(block)


def _exchange(parts, name):
    _, rows, cols = parts.shape

    def body(p_ref, out_ref, send_sems, recv_sems, local_sem):
        x, y, c = _position()
        me = 4 * x + 2 * y + c
        mine = pltpu.make_async_copy(p_ref.at[me], out_ref.at[me], local_sem)
        mine.start()
        peers = []
        for k in range(1, N_DEV):
            px = 1 - x if k & 4 else x
            py = 1 - y if k & 2 else y
            pc = 1 - c if k & 1 else c
            peers.append((k - 1, (px, py, pc), 4 * px + 2 * py + pc))

        def copy(k, peer, peer_slot):
            return pltpu.make_async_remote_copy(
                src_ref=p_ref.at[peer_slot], dst_ref=out_ref.at[me],
                send_sem=send_sems.at[k], recv_sem=recv_sems.at[k], device_id=peer, device_id_type=MESH)

        def arrival(k, peer, peer_slot):
            return pltpu.make_async_remote_copy(
                src_ref=p_ref.at[peer_slot], dst_ref=out_ref.at[peer_slot],
                send_sem=send_sems.at[k], recv_sem=recv_sems.at[k], device_id=peer, device_id_type=MESH)

        sends = [copy(*p) for p in peers]
        for cp in sends:
            cp.start()
        for p in peers:
            arrival(*p).wait_recv()
        for cp in sends:
            cp.wait_send()
        mine.wait()

    return pl.pallas_call(
        body, name=name,
        out_shape=jax.ShapeDtypeStruct(parts.shape, parts.dtype),
        in_specs=[pl.BlockSpec(memory_space=pl.ANY)],
        out_specs=pl.BlockSpec(memory_space=pl.ANY),
        scratch_shapes=[pltpu.SemaphoreType.DMA((7,)), pltpu.SemaphoreType.DMA((7,)), pltpu.SemaphoreType.DMA],
    )(parts)


RET_IN_W = 2 * RET_QK + 2 * RET_VW
GDN_IN_W = GDN_QKV + GDN_VW + 2 * GDN_V_HEADS
GDN_TAIL = 2 * GDN_V_HEADS
TAIL_PAD = 128


def _pack(ret_in, ret_out, gdn_in, gdn_out, w1, w2):
    return jnp.concatenate([t.reshape(-1, PACK_W) for t in (ret_in, ret_out, gdn_in, gdn_out, w1, w2)], axis=0)


def _pack_rows():
    shard = lambda n: n // N_DEV // PACK_W
    sizes = [D_MODEL * RET_IN_W, RET_VW * D_MODEL, D_MODEL * GDN_IN_W, GDN_VW * D_MODEL,
             DEPTH * D_MODEL * D_FF, DEPTH * D_FF * D_MODEL]
    rows = [shard(s) for s in sizes]
    return rows, np.concatenate([[0], np.cumsum(rows)])


def _unpack_shards(flat):
    _, off = _pack_rows()
    seg = lambda i: flat[off[i]:off[i + 1]]
    return (seg(0).reshape(1, D_MODEL, RET_IN_W // N_DEV), seg(1).reshape(1, RET_VW // N_DEV, D_MODEL),
            seg(2).reshape(1, D_MODEL, GDN_IN_W // N_DEV), seg(3).reshape(1, GDN_VW // N_DEV, D_MODEL),
            seg(4).reshape(DEPTH, D_MODEL, D_FF // N_DEV), seg(5).reshape(DEPTH, D_FF // N_DEV, D_MODEL))


def _unpack_full(allw):
    _, off = _pack_rows()
    seg = lambda i: allw[:, off[i]:off[i + 1]]
    cols = lambda t, n: t.reshape(N_DEV, D_MODEL, n // N_DEV).transpose(1, 0, 2).reshape(D_MODEL, n)
    ret_in = cols(seg(0), RET_IN_W)
    ret_out = seg(1).reshape(RET_VW, D_MODEL)
    gdn_in = cols(seg(2), GDN_IN_W)
    gdn_out = seg(3).reshape(GDN_VW, D_MODEL)
    w1 = seg(4).reshape(N_DEV, DEPTH, D_MODEL, D_FF // N_DEV).transpose(1, 2, 0, 3).reshape(DEPTH, D_MODEL, D_FF)
    w2 = seg(5).reshape(N_DEV, DEPTH, D_FF // N_DEV, D_MODEL).transpose(1, 0, 2, 3).reshape(DEPTH, D_FF, D_MODEL)
    return ret_in, ret_out, gdn_in, gdn_out, w1, w2


def _pack_parts(ret_in, ret_out, gdn_in, gdn_out, w1, w2):
    cols = lambda t, n: t.reshape(D_MODEL, N_DEV, n // N_DEV).transpose(1, 0, 2).reshape(N_DEV, -1, PACK_W)
    return jnp.concatenate([
        cols(ret_in, RET_IN_W), ret_out.reshape(N_DEV, -1, PACK_W),
        cols(gdn_in, GDN_IN_W), gdn_out.reshape(N_DEV, -1, PACK_W),
        w1.reshape(DEPTH, D_MODEL, N_DEV, D_FF // N_DEV).transpose(2, 0, 1, 3).reshape(N_DEV, -1, PACK_W),
        w2.reshape(DEPTH, N_DEV, D_FF // N_DEV, D_MODEL).transpose(1, 0, 2, 3).reshape(N_DEV, -1, PACK_W)], axis=1)


SMALL_SIZES = (RET_VW, GDN_V_HEADS, GDN_V_HEADS, GDN_DV, DEPTH * D_MODEL, DEPTH * D_MODEL, DEPTH * D_MODEL,
               DEPTH * D_MODEL, GDN_CONV * GDN_QKV)
SMALL_LANES = 128
SMALL_ROWS = -(-sum(SMALL_SIZES) // (8 * SMALL_LANES)) * 8


def _pack_small(*vecs):
    flat = jnp.concatenate([v.reshape(-1).astype(F32) for v in vecs])
    return jnp.pad(flat, (0, SMALL_ROWS * SMALL_LANES - flat.shape[0])).reshape(SMALL_ROWS, SMALL_LANES)


def _unpack_small(buf, shapes):
    flat, out, at = buf.reshape(-1), [], 0
    for shp in shapes:
        n = int(np.prod(shp))
        out.append(flat[at:at + n].reshape(shp))
        at += n
    return out


def _mlp_bwd(dz, h, a, r, w1, w2, name):
    da = _matmul(dz, w2, tb=True, out_dtypes=(BF16,), epilogue=_epi_drelu2, extras=(a,), name=name + "_da")
    dw2 = _matmul(r, dz, ta=True, out_dtypes=(BF16,), name=name + "_dw2").reshape(N_DEV, -1, D_MODEL)
    dw1, x_w2 = _matmul(h, da, ta=True, out_sharded=True, out_dtypes=(BF16,), name=name + "_dw1",
                        comm=_Exchange([dw2]))
    dh, x_w1 = _matmul(da, w1, tb=True, b_sharded=True, epilogue=_epi_add(DN_ALPHA), extras=(dz,), name=name + "_dh",
                       comm=_Exchange([dw1]))
    return dh, x_w1, x_w2


def _chunk_cols(t):
    seq, nh = t.shape
    t = t.reshape(seq // CHUNK, CHUNK, nh).transpose(0, 2, 1)
    return t[..., None], t[:, :, None, :]


def _unchunk(col, row=None):
    t = col[..., 0] if row is None else col[..., 0] + row[:, :, 0, :]
    nc, nh, _ = t.shape
    return t.transpose(0, 2, 1).reshape(nc * CHUNK, nh)


def kernel(x, ret_w_in, ret_gn_g, ret_w_out, gdn_w_in, gdn_conv_w, gdn_a_log, gdn_dt_bias, gdn_norm_g, gdn_w_out, ln_mix_g, ln_mix_b, mlp_w1, mlp_w2, ln_ffn_g, ln_ffn_b, loss_target, m_ret_w_in, m_ret_gn_g, m_ret_w_out, m_gdn_w_in, m_gdn_conv_w, m_gdn_a_log, m_gdn_dt_bias, m_gdn_norm_g, m_gdn_w_out, m_ln_mix_g, m_ln_mix_b, m_mlp_w1, m_mlp_w2, m_ln_ffn_g, m_ln_ffn_b, v_ret_w_in, v_ret_gn_g, v_ret_w_out, v_gdn_w_in, v_gdn_conv_w, v_gdn_a_log, v_gdn_dt_bias, v_gdn_norm_g, v_gdn_w_out, v_ln_mix_g, v_ln_mix_b, v_mlp_w1, v_mlp_w2, v_ln_ffn_g, v_ln_ffn_b):
    xt, target = x[0], loss_target[0]
    seq = xt.shape[0]
    me = 4 * lax.axis_index("x") + 2 * lax.axis_index("y") + lax.axis_index("c")

    bf = lambda t: t.astype(BF16)
    cos, sin = _rope_tables(seq)
    w_ret_in, = _comm_alone(_Gather([bf(ret_w_in[0])]), "gather_ret_in")
    conv_blk = jnp.pad(gdn_conv_w[0], ((0, HALO - GDN_CONV), (0, 0)))
    p0, w_ret_out, w1_0, w2_0, conv_all = _matmul(
        xt, w_ret_in, b_sharded=True, name="ret_in",
        comm=_Gather([bf(ret_w_out[0]), bf(mlp_w1[0]), bf(mlp_w2[0]), conv_blk]))
    w_ret_out, w2_0 = w_ret_out.reshape(RET_VW, D_MODEL), w2_0.reshape(D_FF, D_MODEL)
    conv_w = conv_all[:, :GDN_CONV].transpose(1, 0, 2).reshape(GDN_CONV, GDN_QKV)
    o0, s0 = _ret_fwd(p0, cos, sin, ret_gn_g[0])
    mix0 = _matmul(o0, w_ret_out, name="ret_out")
    h1 = _ln_fwd(xt, mix0, ln_mix_g[0], ln_mix_b[0])
    a0, r0, gdn_in_all = _matmul(h1, w1_0, b_sharded=True, out_dtypes=(F32, BF16), epilogue=_epi_relu2,
                                 name="mlp0_up", comm=_Gather([bf(gdn_w_in[0])]))
    m0, w_gdn_out, w1_1 = _matmul(r0, w2_0, name="mlp0_down", comm=_Gather([bf(gdn_w_out[0]), bf(mlp_w1[1])]))
    w_gdn_out = w_gdn_out.reshape(GDN_VW, D_MODEL)
    h2 = _ln_fwd(h1, m0, ln_ffn_g[0], ln_ffn_b[0])

    w_gdn_in = gdn_in_all.transpose(1, 0, 2).reshape(D_MODEL, GDN_IN_W)
    w_gdn_main = w_gdn_in[:, :GDN_IN_W - GDN_TAIL]
    w_gdn_tail = jnp.pad(w_gdn_in[:, GDN_IN_W - GDN_TAIL:], ((0, 0), (0, TAIL_PAD - GDN_TAIL)))
    p1, w2_1 = _matmul(h2, w_gdn_main, name="gdn_in", comm=_Gather([bf(mlp_w2[1])]))
    w2_1 = w2_1.reshape(D_FF, D_MODEL)
    pt = _matmul(h2, w_gdn_tail, name="gdn_in_tail")
    c1 = _conv_fwd(p1, conv_w)
    b_in, a_in = pt[:, :GDN_V_HEADS], pt[:, GDN_V_HEADS:GDN_TAIL]
    beta, gc = _gates_fwd(b_in, a_in, gdn_a_log, gdn_dt_bias)
    beta_c, _ = _chunk_cols(beta)
    gc_c, gc_r = _chunk_cols(gc)
    o1, t1, s1 = _gdn_fwd(c1, p1, beta_c, gc_c, gc_r, gdn_norm_g[0])
    mix1 = _matmul(o1, w_gdn_out, name="gdn_out")
    h3 = _ln_fwd(h2, mix1, ln_mix_g[1], ln_mix_b[1])
    a1, r1 = _matmul(h3, w1_1, b_sharded=True, out_dtypes=(F32, BF16), epilogue=_epi_relu2, name="mlp1_up")
    m1 = _matmul(r1, w2_1, name="mlp1_down")
    h4 = _ln_fwd(h3, m1, ln_ffn_g[1], ln_ffn_b[1])
    loss_blk, dh4 = _loss_head(h4, target)
    loss = lax.psum(loss_blk[0, 0], ("x", "y", "c"))

    dz, dg_ffn1, db_ffn1 = _ln_bwd(dh4, h3, m1, ln_ffn_g[1])
    dh3, x_w1_1, x_w2_1 = _mlp_bwd(dz, h3, a1, r1, w1_1, w2_1, "mlp1")
    dz, dg_mix1, db_mix1 = _ln_bwd(dh3, h2, mix1, ln_mix_g[1])
    do1 = _matmul(dz, w_gdn_out, tb=True, name="gdn_out_do")
    dw_gdn_out = _matmul(o1, dz, ta=True, out_dtypes=(BF16,), name="gdn_out_dw").reshape(N_DEV, -1, D_MODEL)
    dq, dk, dv, dzg, dbeta_c, dgc_c, dgc_r, dng = _gdn_bwd(c1, p1, beta_c, gc_c, gc_r, gdn_norm_g[0], t1, s1, do1)
    du, dconv = _conv_bwd(p1, jnp.concatenate([dq, dk, dv], axis=-1), conv_w)
    db_in, da_in, dalog, ddt = _gates_bwd(b_in, a_in, gdn_a_log, gdn_dt_bias, _unchunk(dbeta_c), _unchunk(dgc_c, dgc_r))
    dpt = jnp.concatenate([db_in, da_in, jnp.zeros((seq, TAIL_PAD - GDN_TAIL), F32)], axis=-1)
    dp1 = jnp.concatenate([du, dzg], axis=-1)
    dw_gdn_main, x_gdn_out = _matmul(h2, dp1, ta=True, out_dtypes=(BF16,), name="gdn_in_dw",
                                     comm=_Exchange([dw_gdn_out]))
    dw_gdn_tail = _matmul(h2, dpt, ta=True, out_dtypes=(BF16,), name="gdn_in_tail_dw")
    dw_gdn_in = jnp.concatenate([dw_gdn_main, dw_gdn_tail[:, :GDN_TAIL]], axis=-1)
    dw_gdn_in = dw_gdn_in.reshape(D_MODEL, N_DEV, GDN_IN_W // N_DEV).transpose(1, 0, 2)
    dh2 = _matmul(dpt, w_gdn_tail, tb=True, epilogue=_epi_add(DN_ALPHA), extras=(dz,), name="gdn_in_tail_dh")
    dh2, x_gdn_in = _matmul(dp1, w_gdn_main, tb=True, epilogue=_epi_add(1.0), extras=(dh2,), name="gdn_in_dh",
                            comm=_Exchange([dw_gdn_in]))

    dz, dg_ffn0, db_ffn0 = _ln_bwd(dh2, h1, m0, ln_ffn_g[0])
    dh1, x_w1_0, x_w2_0 = _mlp_bwd(dz, h1, a0, r0, w1_0, w2_0, "mlp0")
    dz, dg_mix0, db_mix0 = _ln_bwd(dh1, xt, mix0, ln_mix_g[0])
    do0 = _matmul(dz, w_ret_out, tb=True, name="ret_out_do")
    dw_ret_out = _matmul(o0, dz, ta=True, out_dtypes=(BF16,), name="ret_out_dw").reshape(N_DEV, -1, D_MODEL)
    dq, dk, dv, dgate, dgng = _ret_bwd(p0, cos, sin, ret_gn_g[0], s0, do0)
    dp0 = jnp.concatenate([dq, dk, dv, dgate], axis=-1)
    dw_ret_in, x_ret_out = _matmul(xt, dp0, ta=True, out_sharded=True, out_dtypes=(BF16,), name="ret_in_dw",
                                   comm=_Exchange([dw_ret_out]))
    dx, x_ret_in = _matmul(dp0, w_ret_in, tb=True, b_sharded=True, epilogue=_epi_add(DN_ALPHA), extras=(dz,),
                           name="ret_in_dx", comm=_Exchange([dw_ret_in]))

    def update(parts, w, m, v, name):
        outs = _adamw(parts, w.reshape(parts.shape[1:]), m.reshape(parts.shape[1:]), v.reshape(parts.shape[1:]),
                      rows_per_step=128, name=name)
        return [t.reshape(w.shape) for t in outs]

    u_w1 = [update(p, mlp_w1[l], m_mlp_w1[l], v_mlp_w1[l], f"adamw_w1_{l}") for l, p in enumerate((x_w1_0, x_w1_1))]
    u_w2 = [update(p, mlp_w2[l], m_mlp_w2[l], v_mlp_w2[l], f"adamw_w2_{l}") for l, p in enumerate((x_w2_0, x_w2_1))]
    big_out = list(zip(
        update(x_ret_in, ret_w_in, m_ret_w_in, v_ret_w_in, "adamw_ret_in"),
        update(x_ret_out, ret_w_out, m_ret_w_out, v_ret_w_out, "adamw_ret_out"),
        update(x_gdn_in, gdn_w_in, m_gdn_w_in, v_gdn_w_in, "adamw_gdn_in"),
        update(x_gdn_out, gdn_w_out, m_gdn_w_out, v_gdn_w_out, "adamw_gdn_out"),
        [jnp.stack([a, b]) for a, b in zip(*u_w1)],
        [jnp.stack([a, b]) for a, b in zip(*u_w2)]))

    small_w = (ret_gn_g, gdn_a_log, gdn_dt_bias, gdn_norm_g, ln_mix_g, ln_mix_b, ln_ffn_g, ln_ffn_b)
    small_m = (m_ret_gn_g, m_gdn_a_log, m_gdn_dt_bias, m_gdn_norm_g, m_ln_mix_g, m_ln_mix_b, m_ln_ffn_g, m_ln_ffn_b)
    small_v = (v_ret_gn_g, v_gdn_a_log, v_gdn_dt_bias, v_gdn_norm_g, v_ln_mix_g, v_ln_mix_b, v_ln_ffn_g, v_ln_ffn_b)
    small_g = (dgng, dalog, ddt, jnp.sum(dng, axis=0),
               jnp.concatenate([dg_mix0, dg_mix1]), jnp.concatenate([db_mix0, db_mix1]),
               jnp.concatenate([dg_ffn0, dg_ffn1]), jnp.concatenate([db_ffn0, db_ffn1]), dconv)
    small_parts, = _comm_alone(_Gather([_pack_small(*small_g)]), "gather_small_grads")
    zero_conv = jnp.zeros((GDN_CONV, GDN_QKV), F32)
    small_out = _adamw(small_parts, _pack_small(*small_w, zero_conv), _pack_small(*small_m, zero_conv),
                       _pack_small(*small_v, zero_conv), rows_per_step=SMALL_ROWS, name="adamw_small")
    shapes = [t.shape for t in small_w] + [(GDN_CONV, GDN_QKV)]
    small_out = [_unpack_small(t, shapes) for t in small_out]
    conv_g = lax.dynamic_slice(small_out[0][-1], (0, me * (GDN_QKV // N_DEV)), (GDN_CONV, GDN_QKV // N_DEV))
    conv_out = _adamw(conv_g[None], gdn_conv_w[0], m_gdn_conv_w[0], v_gdn_conv_w[0],
                      rows_per_step=GDN_CONV, name="adamw_conv")

    def ordered(kind):
        b, s, cv = big_out[kind], small_out[kind], conv_out[kind][None]
        return [b[0], s[0], b[1], b[2], cv, s[1], s[2], s[3], b[3], s[4], s[5], b[4], b[5], s[6], s[7]]

    return (loss, dx[None], *ordered(0), *ordered(1), *ordered(2), *ordered(3))
```

```python
import functools
import math

import jax
import jax.numpy as jnp
import numpy as np
from jax import lax
from jax.experimental import pallas as pl
from jax.experimental.pallas import tpu as pltpu

F32 = jnp.float32
BF16 = jnp.bfloat16

N_DEV = 8
D_MODEL = 2048
CHUNK = 64
RET_HEADS = 8
RET_DK = 256
RET_DV = 512
RET_QK = RET_HEADS * RET_DK
RET_VW = RET_HEADS * RET_DV
ROPE_BASE = 10000.0
GN_EPS = 1e-6
GDN_K_HEADS = 16
GDN_V_HEADS = 32
GDN_DK = 128
GDN_DV = 128
GDN_QK = GDN_K_HEADS * GDN_DK
GDN_VW = GDN_V_HEADS * GDN_DV
GDN_QKV = 2 * GDN_QK + GDN_VW
GDN_CONV = 4
RMS_EPS = 1e-6
L2_EPS = 1e-6
D_FF = 4 * D_MODEL
DEPTH = 2
DN_ALPHA = (2.0 * DEPTH) ** 0.25
LN_EPS = 1e-5
ADAM_LR = 0.001
ADAM_B1 = 0.9
ADAM_B2 = 0.999
ADAM_EPS = 1e-08
ADAM_WD = 0.01
ADAM_STEP = 10

VMEM_LIMIT = 56 * 1024 * 1024
MESH = pl.DeviceIdType.MESH


def _cparams(sem=None):
    return pltpu.CompilerParams(dimension_semantics=sem, vmem_limit_bytes=VMEM_LIMIT)


_NT = (((2,), (2,)), ((0,), (0,)))
_NN = (((2,), (1,)), ((0,), (0,)))
_TN = (((1,), (1,)), ((0,), (0,)))


def _dg(a, b, dims):
    return lax.dot_general(a.astype(BF16), b.astype(BF16), dims, preferred_element_type=F32)


@jax.custom_vjp
def _nt(a, b):
    return _dg(a, b, _NT)


@jax.custom_vjp
def _nn(a, b):
    return _dg(a, b, _NN)


@jax.custom_vjp
def _tn(a, b):
    return _dg(a, b, _TN)


_nt.defvjp(lambda a, b: (_dg(a, b, _NT), (a, b)), lambda r, g: (_nn(g, r[1]), _tn(g, r[0])))
_nn.defvjp(lambda a, b: (_dg(a, b, _NN), (a, b)), lambda r, g: (_nt(g, r[1]), _tn(r[0], g)))
_tn.defvjp(lambda a, b: (_dg(a, b, _TN), (a, b)), lambda r, g: (_nt(r[1], g), _nn(r[0], g)))


def _iota2(shape, dim):
    return lax.broadcasted_iota(jnp.int32, shape, dim)


def _inv_unit_lower(a):
    c = a.shape[-1]
    eye = (_iota2((c, c), 0) == _iota2((c, c), 1)).astype(F32)
    m = -a
    p = eye + m
    for _ in range(int(math.log2(c)) - 1):
        m = _dg(m, m, _NN)
        p = p + _dg(p, m, _NN)
    return p


def _silu(x):
    return x * jax.nn.sigmoid(x)


def _rep2(t):
    h = t.shape[0]
    return jnp.broadcast_to(t[:, None], (h, 2) + t.shape[1:]).reshape((2 * h,) + t.shape[1:])


def _ret_chunk(q1, q2, k1, k2, v, gate, gn_g, s, cos, sin, intra, qdec, kdec, cdec):
    q = jnp.concatenate([q1 * cos - q2 * sin, q1 * sin + q2 * cos], axis=-1)
    k = jnp.concatenate([k1 * cos - k2 * sin, k1 * sin + k2 * cos], axis=-1) * (RET_DK ** -0.5)
    scores = _nt(q, k) * intra
    y = _nn(scores, v) + _nn(q * qdec, s)
    s_new = s * cdec + _tn(k * kdec, v)
    mu = jnp.mean(y, -1, keepdims=True)
    yc = y - mu
    var = jnp.mean(yc * yc, -1, keepdims=True)
    o = _silu(gate) * (yc * lax.rsqrt(var + GN_EPS) * gn_g)
    return o, s_new


def _ret_consts():
    log_gamma = np.log1p(-np.exp2(-5.0 - np.arange(RET_HEADS, dtype=np.float64)))
    idx = np.arange(CHUNK, dtype=np.float64)
    lg = log_gamma[:, None]
    intra = np.exp(lg[..., None] * np.abs(idx[:, None] - idx[None, :]))
    qdec = np.exp(lg * (idx + 1.0))[..., None]
    kdec = np.exp(lg * (CHUNK - 1.0 - idx))[..., None]
    cdec = np.exp(log_gamma * CHUNK)[:, None, None]
    return [jnp.asarray(t, F32) for t in (intra, qdec, kdec, cdec)]


def _rope_tables(seq):
    half = RET_DK // 2
    inv = ROPE_BASE ** (-jnp.arange(half, dtype=F32) / half)
    ang = jnp.arange(seq).astype(F32)[:, None] * inv[None, :]
    return jnp.cos(ang), jnp.sin(ang)


RET_HB = 2


def _ret_load(q_ref, k_ref, v_ref, gate_ref):
    hb, dk, dv, h = RET_HB, RET_DK, RET_DV, RET_DK // 2
    q, k, v, gate = q_ref[...], k_ref[...], v_ref[...], gate_ref[...]
    q1 = jnp.stack([q[:, i * dk:i * dk + h] for i in range(hb)])
    q2 = jnp.stack([q[:, i * dk + h:(i + 1) * dk] for i in range(hb)])
    k1 = jnp.stack([k[:, i * dk:i * dk + h] for i in range(hb)])
    k2 = jnp.stack([k[:, i * dk + h:(i + 1) * dk] for i in range(hb)])
    vs = jnp.stack([v[:, i * dv:(i + 1) * dv] for i in range(hb)])
    gs = jnp.stack([gate[:, i * dv:(i + 1) * dv] for i in range(hb)])
    return q1, q2, k1, k2, vs, gs


def _ret_specs(n_chunks, rev):
    hb = RET_HB
    cidx = (lambda n: n_chunks - 1 - n) if rev else (lambda n: n)
    qw, vw = hb * RET_DK, hb * RET_DV
    tok = [
        pl.BlockSpec((CHUNK, qw), lambda h, n: (cidx(n), h)),
        pl.BlockSpec((CHUNK, qw), lambda h, n: (cidx(n), RET_QK // qw + h)),
        pl.BlockSpec((CHUNK, vw), lambda h, n: (cidx(n), 2 * RET_QK // vw + h)),
        pl.BlockSpec((CHUNK, vw), lambda h, n: (cidx(n), (2 * RET_QK + RET_VW) // vw + h)),
        pl.BlockSpec((CHUNK, RET_DK // 2), lambda h, n: (cidx(n), 0)),
        pl.BlockSpec((CHUNK, RET_DK // 2), lambda h, n: (cidx(n), 0)),
    ]
    const = [
        pl.BlockSpec((hb, CHUNK, CHUNK), lambda h, n: (h, 0, 0)),
        pl.BlockSpec((hb, CHUNK, 1), lambda h, n: (h, 0, 0)),
        pl.BlockSpec((hb, CHUNK, 1), lambda h, n: (h, 0, 0)),
        pl.BlockSpec((hb, 1, 1), lambda h, n: (h, 0, 0)),
        pl.BlockSpec((hb, 1, RET_DV), lambda h, n: (h, 0, 0)),
    ]
    state = pl.BlockSpec((1, hb, RET_DK, RET_DV), lambda h, n: (cidx(n), h, 0, 0))
    return tok, const, state, cidx


def _ret_fwd(p, cos, sin, gn_g):
    seq = p.shape[0]
    nc = seq // CHUNK
    hb = RET_HB
    tok, const, state, _ = _ret_specs(nc, False)

    def body(q_ref, k_ref, v_ref, gate_ref, cos_ref, sin_ref, intra_ref, qdec_ref, kdec_ref, cdec_ref, gng_ref,
             o_ref, ssave_ref, s_scr):
        @pl.when(pl.program_id(1) == 0)
        def _():
            s_scr[...] = jnp.zeros_like(s_scr)

        q1, q2, k1, k2, v, gate = _ret_load(q_ref, k_ref, v_ref, gate_ref)
        s = s_scr[...]
        ssave_ref[0] = s.astype(BF16)
        o, s_new = _ret_chunk(q1, q2, k1, k2, v, gate, gng_ref[...], s, cos_ref[...], sin_ref[...],
                              intra_ref[...], qdec_ref[...], kdec_ref[...], cdec_ref[...])
        s_scr[...] = s_new
        o_ref[...] = jnp.concatenate([o[i] for i in range(hb)], axis=-1).astype(o_ref.dtype)

    return pl.pallas_call(
        body, name="ret_fwd",
        grid=(RET_HEADS // hb, nc),
        in_specs=tok + const,
        out_specs=[pl.BlockSpec((CHUNK, hb * RET_DV), lambda h, n: (n, h)), state],
        out_shape=[jax.ShapeDtypeStruct((seq, RET_VW), BF16),
                   jax.ShapeDtypeStruct((nc, RET_HEADS, RET_DK, RET_DV), BF16)],
        scratch_shapes=[pltpu.VMEM((hb, RET_DK, RET_DV), F32)],
        compiler_params=_cparams(("parallel", "arbitrary")),
    )(p, p, p, p, cos, sin, *_ret_consts(), gn_g.reshape(RET_HEADS, 1, RET_DV))


def _ret_bwd(p, cos, sin, gn_g, ssave, do):
    seq = p.shape[0]
    nc = seq // CHUNK
    hb = RET_HB
    tok, const, state, cidx = _ret_specs(nc, True)

    def body(q_ref, k_ref, v_ref, gate_ref, cos_ref, sin_ref, intra_ref, qdec_ref, kdec_ref, cdec_ref, gng_ref,
             ssave_ref, do_ref, dq_ref, dk_ref, dv_ref, dgate_ref, dgng_ref, ds_scr):
        @pl.when(pl.program_id(1) == 0)
        def _():
            ds_scr[...] = jnp.zeros_like(ds_scr)
            dgng_ref[...] = jnp.zeros_like(dgng_ref)

        q1, q2, k1, k2, v, gate = _ret_load(q_ref, k_ref, v_ref, gate_ref)
        do = do_ref[...]
        dos = jnp.stack([do[:, i * RET_DV:(i + 1) * RET_DV] for i in range(hb)]).astype(F32)
        fn = functools.partial(_ret_chunk, cos=cos_ref[...], sin=sin_ref[...], intra=intra_ref[...],
                               qdec=qdec_ref[...], kdec=kdec_ref[...], cdec=cdec_ref[...])
        _, vjp = jax.vjp(fn, q1, q2, k1, k2, v, gate, gng_ref[...], ssave_ref[0].astype(F32))
        dq1, dq2, dk1, dk2, dv, dgate, dgng, ds = vjp((dos, ds_scr[...]))
        ds_scr[...] = ds
        dgng_ref[...] += dgng
        dq_ref[...] = jnp.concatenate([t[i] for i in range(hb) for t in (dq1, dq2)], axis=-1).astype(dq_ref.dtype)
        dk_ref[...] = jnp.concatenate([t[i] for i in range(hb) for t in (dk1, dk2)], axis=-1).astype(dk_ref.dtype)
        dv_ref[...] = jnp.concatenate([dv[i] for i in range(hb)], axis=-1).astype(dv_ref.dtype)
        dgate_ref[...] = jnp.concatenate([dgate[i] for i in range(hb)], axis=-1).astype(dgate_ref.dtype)

    qw, vw = hb * RET_DK, hb * RET_DV
    return pl.pallas_call(
        body, name="ret_bwd",
        grid=(RET_HEADS // hb, nc),
        in_specs=tok + const + [state, pl.BlockSpec((CHUNK, vw), lambda h, n: (cidx(n), h))],
        out_specs=[pl.BlockSpec((CHUNK, qw), lambda h, n: (cidx(n), h)),
                   pl.BlockSpec((CHUNK, qw), lambda h, n: (cidx(n), h)),
                   pl.BlockSpec((CHUNK, vw), lambda h, n: (cidx(n), h)),
                   pl.BlockSpec((CHUNK, vw), lambda h, n: (cidx(n), h)),
                   pl.BlockSpec((hb, 1, RET_DV), lambda h, n: (h, 0, 0))],
        out_shape=[jax.ShapeDtypeStruct((seq, RET_QK), BF16), jax.ShapeDtypeStruct((seq, RET_QK), BF16),
                   jax.ShapeDtypeStruct((seq, RET_VW), BF16), jax.ShapeDtypeStruct((seq, RET_VW), BF16),
                   jax.ShapeDtypeStruct((RET_HEADS, 1, RET_DV), F32)],
        scratch_shapes=[pltpu.VMEM((hb, RET_DK, RET_DV), F32)],
        compiler_params=_cparams(("parallel", "arbitrary")),
    )(p, p, p, p, cos, sin, *_ret_consts(), gn_g.reshape(RET_HEADS, 1, RET_DV), ssave, do)


def _gdn_common(qr, kr, gc_c, gc_r):
    qn = qr * lax.rsqrt(jnp.sum(qr * qr, -1, keepdims=True) + L2_EPS) * (GDN_DK ** -0.5)
    kn = kr * lax.rsqrt(jnp.sum(kr * kr, -1, keepdims=True) + L2_EPS)
    causal = _iota2((CHUNK, CHUNK), 0) >= _iota2((CHUNK, CHUNK), 1)
    decay = jnp.exp(jnp.where(causal, gc_c - gc_r, -1e30))
    return _rep2(qn), _rep2(kn), decay


def _gdn_a(qr, kr, beta_c, gc_c, gc_r):
    _, k, decay = _gdn_common(qr, kr, gc_c, gc_r)
    strict = _iota2((CHUNK, CHUNK), 0) > _iota2((CHUNK, CHUNK), 1)
    return jnp.where(strict, _nt(k * beta_c, k) * decay, 0.0)


def _gdn_main(qr, kr, v, z, beta_c, gc_c, gc_r, norm_g, t, s):
    q, k, decay = _gdn_common(qr, kr, gc_c, gc_r)
    eg = jnp.exp(gc_c)
    u = _nn(t, v * beta_c)
    w = _nn(t, k * (beta_c * eg))
    attn = _nt(q, k) * decay
    v_new = u - _nn(w, s)
    y = _nn(q * eg, s) + _nn(attn, v_new)
    last = _iota2((1, CHUNK, 1), 1) == CHUNK - 1
    gl = jnp.sum(jnp.where(last, gc_c, 0.0), axis=1, keepdims=True)
    s_new = s * jnp.exp(gl) + _tn(k * jnp.exp(gl - gc_c), v_new)
    yn = y * lax.rsqrt(jnp.mean(y * y, -1, keepdims=True) + RMS_EPS) * norm_g
    return yn * _silu(z), s_new


GDN_HK = 4


def _gdn_load(q_ref, k_ref, v_ref, z_ref):
    hk, hb, d = GDN_HK, 2 * GDN_HK, GDN_DK
    q, k, v, z = q_ref[...], k_ref[...], v_ref[...], z_ref[...]
    qs = jnp.stack([q[:, i * d:(i + 1) * d] for i in range(hk)])
    ks = jnp.stack([k[:, i * d:(i + 1) * d] for i in range(hk)])
    vs = jnp.stack([v[:, i * d:(i + 1) * d] for i in range(hb)])
    zs = jnp.stack([z[:, i * d:(i + 1) * d] for i in range(hb)])
    return qs, ks, vs, zs


def _gdn_specs(n_chunks, rev):
    hk, hb = GDN_HK, 2 * GDN_HK
    cidx = (lambda n: n_chunks - 1 - n) if rev else (lambda n: n)
    qw, vw = hk * GDN_DK, hb * GDN_DV
    tok = [
        pl.BlockSpec((CHUNK, qw), lambda h, n: (cidx(n), h)),
        pl.BlockSpec((CHUNK, qw), lambda h, n: (cidx(n), GDN_QK // qw + h)),
        pl.BlockSpec((CHUNK, vw), lambda h, n: (cidx(n), 2 * GDN_QK // vw + h)),
        pl.BlockSpec((CHUNK, vw), lambda h, n: (cidx(n), GDN_QKV // vw + h)),
        pl.BlockSpec((1, hb, CHUNK, 1), lambda h, n: (cidx(n), h, 0, 0)),
        pl.BlockSpec((1, hb, CHUNK, 1), lambda h, n: (cidx(n), h, 0, 0)),
        pl.BlockSpec((1, hb, 1, CHUNK), lambda h, n: (cidx(n), h, 0, 0)),
        pl.BlockSpec((1, GDN_DV), lambda h, n: (0, 0)),
    ]
    tsave = pl.BlockSpec((1, hb, CHUNK, CHUNK), lambda h, n: (cidx(n), h, 0, 0))
    ssave = pl.BlockSpec((1, hb, GDN_DK, GDN_DV), lambda h, n: (cidx(n), h, 0, 0))
    return tok, tsave, ssave, cidx


def _gdn_fwd(c, p, beta_c, gc_c, gc_r, norm_g):
    seq = c.shape[0]
    nc = seq // CHUNK
    hk, hb = GDN_HK, 2 * GDN_HK
    tok, tsave, ssave, _ = _gdn_specs(nc, False)

    def body(q_ref, k_ref, v_ref, z_ref, beta_ref, gcc_ref, gcr_ref, ng_ref, o_ref, tsave_ref, ssave_ref, s_scr):
        @pl.when(pl.program_id(1) == 0)
        def _():
            s_scr[...] = jnp.zeros_like(s_scr)

        qr, kr, v, z = _gdn_load(q_ref, k_ref, v_ref, z_ref)
        beta, gcc, gcr = beta_ref[0], gcc_ref[0], gcr_ref[0]
        s = s_scr[...]
        ssave_ref[0] = s.astype(BF16)
        t = _inv_unit_lower(_gdn_a(qr, kr, beta, gcc, gcr))
        tsave_ref[0] = t.astype(BF16)
        o, s_new = _gdn_main(qr, kr, v, z, beta, gcc, gcr, ng_ref[...], t, s)
        s_scr[...] = s_new
        o_ref[...] = jnp.concatenate([o[i] for i in range(hb)], axis=-1).astype(o_ref.dtype)

    return pl.pallas_call(
        body, name="gdn_fwd",
        grid=(GDN_K_HEADS // hk, nc),
        in_specs=tok,
        out_specs=[pl.BlockSpec((CHUNK, hb * GDN_DV), lambda h, n: (n, h)), tsave, ssave],
        out_shape=[jax.ShapeDtypeStruct((seq, GDN_VW), BF16),
                   jax.ShapeDtypeStruct((nc, GDN_V_HEADS, CHUNK, CHUNK), BF16),
                   jax.ShapeDtypeStruct((nc, GDN_V_HEADS, GDN_DK, GDN_DV), BF16)],
        scratch_shapes=[pltpu.VMEM((hb, GDN_DK, GDN_DV), F32)],
        compiler_params=_cparams(("parallel", "arbitrary")),
    )(c, c, c, p, beta_c, gc_c, gc_r, norm_g.reshape(1, GDN_DV))


def _gdn_bwd(c, p, beta_c, gc_c, gc_r, norm_g, tsave, ssave, do):
    seq = c.shape[0]
    nc = seq // CHUNK
    hk, hb = GDN_HK, 2 * GDN_HK
    nhb = GDN_K_HEADS // hk
    tok, tsave_spec, ssave_spec, cidx = _gdn_specs(nc, True)

    def body(q_ref, k_ref, v_ref, z_ref, beta_ref, gcc_ref, gcr_ref, ng_ref, t_ref, s_ref, do_ref,
             dq_ref, dk_ref, dv_ref, dz_ref, dbeta_ref, dgcc_ref, dgcr_ref, dng_ref, ds_scr):
        @pl.when(pl.program_id(1) == 0)
        def _():
            ds_scr[...] = jnp.zeros_like(ds_scr)
            dng_ref[...] = jnp.zeros_like(dng_ref)

        qr, kr, v, z = _gdn_load(q_ref, k_ref, v_ref, z_ref)
        beta, gcc, gcr = beta_ref[0], gcc_ref[0], gcr_ref[0]
        t = t_ref[0].astype(F32)
        do = do_ref[...]
        dos = jnp.stack([do[:, i * GDN_DV:(i + 1) * GDN_DV] for i in range(hb)]).astype(F32)
        _, vjp_main = jax.vjp(_gdn_main, qr, kr, v, z, beta, gcc, gcr, ng_ref[...], t, s_ref[0].astype(F32))
        dqr, dkr, dv, dz, dbeta, dgcc, dgcr, dng, dt, ds = vjp_main((dos, ds_scr[...]))
        ds_scr[...] = ds
        da = -_dg(_dg(t, dt, _TN), t, _NT)
        _, vjp_a = jax.vjp(_gdn_a, qr, kr, beta, gcc, gcr)
        dqr2, dkr2, dbeta2, dgcc2, dgcr2 = vjp_a(da)
        dng_ref[...] += dng[None]
        dq_ref[...] = jnp.concatenate([(dqr + dqr2)[i] for i in range(hk)], axis=-1).astype(dq_ref.dtype)
        dk_ref[...] = jnp.concatenate([(dkr + dkr2)[i] for i in range(hk)], axis=-1).astype(dk_ref.dtype)
        dv_ref[...] = jnp.concatenate([dv[i] for i in range(hb)], axis=-1).astype(dv_ref.dtype)
        dz_ref[...] = jnp.concatenate([dz[i] for i in range(hb)], axis=-1).astype(dz_ref.dtype)
        dbeta_ref[0] = dbeta + dbeta2
        dgcc_ref[0] = dgcc + dgcc2
        dgcr_ref[0] = dgcr + dgcr2

    qw, vw = hk * GDN_DK, hb * GDN_DV
    col = pl.BlockSpec((1, hb, CHUNK, 1), lambda h, n: (cidx(n), h, 0, 0))
    row = pl.BlockSpec((1, hb, 1, CHUNK), lambda h, n: (cidx(n), h, 0, 0))
    return pl.pallas_call(
        body, name="gdn_bwd",
        grid=(nhb, nc),
        in_specs=tok + [tsave_spec, ssave_spec, pl.BlockSpec((CHUNK, vw), lambda h, n: (cidx(n), h))],
        out_specs=[pl.BlockSpec((CHUNK, qw), lambda h, n: (cidx(n), h)),
                   pl.BlockSpec((CHUNK, qw), lambda h, n: (cidx(n), h)),
                   pl.BlockSpec((CHUNK, vw), lambda h, n: (cidx(n), h)),
                   pl.BlockSpec((CHUNK, vw), lambda h, n: (cidx(n), h)),
                   col, col, row,
                   pl.BlockSpec((1, 1, GDN_DV), lambda h, n: (h, 0, 0))],
        out_shape=[jax.ShapeDtypeStruct((seq, GDN_QK), F32), jax.ShapeDtypeStruct((seq, GDN_QK), F32),
                   jax.ShapeDtypeStruct((seq, GDN_VW), F32), jax.ShapeDtypeStruct((seq, GDN_VW), BF16),
                   jax.ShapeDtypeStruct((nc, GDN_V_HEADS, CHUNK, 1), F32),
                   jax.ShapeDtypeStruct((nc, GDN_V_HEADS, CHUNK, 1), F32),
                   jax.ShapeDtypeStruct((nc, GDN_V_HEADS, 1, CHUNK), F32),
                   jax.ShapeDtypeStruct((nhb, 1, GDN_DV), F32)],
        scratch_shapes=[pltpu.VMEM((hb, GDN_DK, GDN_DV), F32)],
        compiler_params=_cparams(("parallel", "arbitrary")),
    )(c, c, c, p, beta_c, gc_c, gc_r, norm_g.reshape(1, GDN_DV), tsave, ssave, do)


CONV_TB = 512
CONV_CB = 1024
HALO = 8


def _conv_taps(ext, w):
    acc = w[GDN_CONV - 1:GDN_CONV] * ext
    for j in range(GDN_CONV - 1):
        acc = acc + w[j:j + 1] * pltpu.roll(ext, GDN_CONV - 1 - j, 0)
    return acc


def _conv_fwd(p, w):
    seq = p.shape[0]
    tb, cb = min(CONV_TB, seq), CONV_CB

    def body(prev_ref, cur_ref, w_ref, o_ref):
        first = pl.program_id(1) == 0
        prev = jnp.where(first, 0.0, prev_ref[...])
        ext = jnp.concatenate([prev, cur_ref[...]], axis=0)
        o_ref[...] = _silu(_conv_taps(ext, w_ref[...])[HALO:])

    return pl.pallas_call(
        body, name="conv_fwd",
        grid=(GDN_QKV // cb, seq // tb),
        in_specs=[pl.BlockSpec((HALO, cb), lambda j, i: (jnp.maximum(i * (tb // HALO) - 1, 0), j)),
                  pl.BlockSpec((tb, cb), lambda j, i: (i, j)),
                  pl.BlockSpec((GDN_CONV, cb), lambda j, i: (0, j))],
        out_specs=pl.BlockSpec((tb, cb), lambda j, i: (i, j)),
        out_shape=jax.ShapeDtypeStruct((seq, GDN_QKV), F32),
        compiler_params=_cparams(("parallel", "arbitrary")),
    )(p, p, w)


def _conv_bwd(p, dc, w):
    seq = p.shape[0]
    tb, cb = min(CONV_TB, seq), CONV_CB
    nt = seq // tb
    last_halo = seq // HALO - 1

    def body(prev_ref, cur_ref, next_ref, dcur_ref, dnext_ref, w_ref, du_ref, dw_ref):
        i = pl.program_id(1)

        @pl.when(i == 0)
        def _():
            dw_ref[...] = jnp.zeros_like(dw_ref)

        w = w_ref[...]
        prev = jnp.where(i == 0, 0.0, prev_ref[...])
        ext = jnp.concatenate([prev, cur_ref[...], next_ref[...]], axis=0)
        pre = _conv_taps(ext, w)
        dnext = jnp.where(i == nt - 1, 0.0, dnext_ref[...])
        dext = jnp.concatenate([jnp.zeros((HALO, cb), F32), dcur_ref[...], dnext], axis=0)
        sig = jax.nn.sigmoid(pre)
        dpre = dext * (sig * (1.0 + pre * (1.0 - sig)))
        rows = tb + 2 * HALO
        du = w[GDN_CONV - 1:GDN_CONV] * dpre
        for j in range(GDN_CONV - 1):
            du = du + w[j:j + 1] * pltpu.roll(dpre, rows - (GDN_CONV - 1 - j), 0)
        du_ref[...] = du[HALO:HALO + tb].astype(du_ref.dtype)
        dcore = dpre[HALO:HALO + tb]
        dws = []
        for j in range(GDN_CONV):
            sh = ext if j == GDN_CONV - 1 else pltpu.roll(ext, GDN_CONV - 1 - j, 0)
            dws.append(jnp.sum(dcore * sh[HALO:HALO + tb], axis=0, keepdims=True))
        dw_ref[...] += jnp.concatenate(dws, axis=0)

    hb = tb // HALO
    return pl.pallas_call(
        body, name="conv_bwd",
        grid=(GDN_QKV // cb, nt),
        in_specs=[pl.BlockSpec((HALO, cb), lambda j, i: (jnp.maximum(i * hb - 1, 0), j)),
                  pl.BlockSpec((tb, cb), lambda j, i: (i, j)),
                  pl.BlockSpec((HALO, cb), lambda j, i: (jnp.minimum((i + 1) * hb, last_halo), j)),
                  pl.BlockSpec((tb, cb), lambda j, i: (i, j)),
                  pl.BlockSpec((HALO, cb), lambda j, i: (jnp.minimum((i + 1) * hb, last_halo), j)),
                  pl.BlockSpec((GDN_CONV, cb), lambda j, i: (0, j))],
        out_specs=[pl.BlockSpec((tb, cb), lambda j, i: (i, j)),
                   pl.BlockSpec((GDN_CONV, cb), lambda j, i: (0, j))],
        out_shape=[jax.ShapeDtypeStruct((seq, GDN_QKV), BF16), jax.ShapeDtypeStruct((GDN_CONV, GDN_QKV), F32)],
        compiler_params=_cparams(("parallel", "arbitrary")),
    )(p, p, p, dc, dc, w)


GATE_TB = 512


def _split3(g):
    hi = g.astype(BF16)
    r = g - hi.astype(F32)
    mid = r.astype(BF16)
    lo = (r - mid.astype(F32)).astype(BF16)
    return hi, mid, lo


def _tri_chunks(n, upper):
    i, j = _iota2((n, n), 0), _iota2((n, n), 1)
    tri = (i <= j) if upper else (i >= j)
    return jnp.where(tri & ((i // CHUNK) == (j // CHUNK)), 1.0, 0.0).astype(BF16)


def _tri_apply(g, upper):
    tri = _tri_chunks(g.shape[0], upper)
    return sum(jnp.dot(tri, part, preferred_element_type=F32) for part in _split3(g))


@jax.custom_vjp
def _chunk_cumsum(g):
    return _tri_apply(g, False)


_chunk_cumsum.defvjp(lambda g: (_tri_apply(g, False), None), lambda _, d: (_tri_apply(d, True),))


def _gates(b, a, a_log, dt_bias):
    z = a + dt_bias
    softplus = jnp.maximum(z, 0.0) + jnp.log1p(jnp.exp(-jnp.abs(z)))
    g = -jnp.exp(a_log) * softplus
    return jax.nn.sigmoid(b), _chunk_cumsum(g)


def _gates_fwd(b, a, a_log, dt_bias):
    seq, nh = b.shape
    tb = min(GATE_TB, seq)

    def body(b_ref, a_ref, al_ref, dt_ref, beta_ref, gc_ref):
        beta, gc = _gates(b_ref[...], a_ref[...], al_ref[...], dt_ref[...])
        beta_ref[...] = beta
        gc_ref[...] = gc

    tok = pl.BlockSpec((tb, nh), lambda i: (i, 0))
    vec = pl.BlockSpec((1, nh), lambda i: (0, 0))
    return pl.pallas_call(
        body, name="gates_fwd", grid=(seq // tb,),
        in_specs=[tok, tok, vec, vec], out_specs=[tok, tok],
        out_shape=[jax.ShapeDtypeStruct((seq, nh), F32)] * 2,
        compiler_params=_cparams(("parallel",)),
    )(b, a, a_log, dt_bias)


def _gates_bwd(b, a, a_log, dt_bias, dbeta, dgc):
    seq, nh = b.shape
    tb = min(GATE_TB, seq)

    def body(b_ref, a_ref, al_ref, dt_ref, dbeta_ref, dgc_ref, db_ref, da_ref, dal_ref, ddt_ref):
        @pl.when(pl.program_id(0) == 0)
        def _():
            dal_ref[...] = jnp.zeros_like(dal_ref)
            ddt_ref[...] = jnp.zeros_like(ddt_ref)

        _, vjp = jax.vjp(_gates, b_ref[...], a_ref[...], al_ref[...], dt_ref[...])
        db, da, dal, ddt = vjp((dbeta_ref[...], dgc_ref[...]))
        db_ref[...] = db
        da_ref[...] = da
        dal_ref[...] += dal
        ddt_ref[...] += ddt

    tok = pl.BlockSpec((tb, nh), lambda i: (i, 0))
    vec = pl.BlockSpec((1, nh), lambda i: (0, 0))
    return pl.pallas_call(
        body, name="gates_bwd", grid=(seq // tb,),
        in_specs=[tok, tok, vec, vec, tok, tok], out_specs=[tok, tok, vec, vec],
        out_shape=[jax.ShapeDtypeStruct((seq, nh), F32)] * 2 + [jax.ShapeDtypeStruct((1, nh), F32)] * 2,
        compiler_params=_cparams(("arbitrary",)),
    )(b, a, a_log, dt_bias, dbeta, dgc)


LN_TR = 256


def _ln_stats(x, s):
    z = DN_ALPHA * x + s
    mu = jnp.mean(z, -1, keepdims=True)
    zc = z - mu
    var = jnp.mean(zc * zc, -1, keepdims=True)
    rstd = lax.rsqrt(var + LN_EPS)
    return zc * rstd, rstd


def _ln_fwd(x, s, g, b):
    seq, d = x.shape
    tr = min(LN_TR, seq)

    def body(x_ref, s_ref, g_ref, b_ref, o_ref, ob_ref):
        xhat, _ = _ln_stats(x_ref[...], s_ref[...])
        y = xhat * g_ref[...] + b_ref[...]
        o_ref[...] = y
        ob_ref[...] = y.astype(BF16)

    tok = pl.BlockSpec((tr, d), lambda i: (i, 0))
    vec = pl.BlockSpec((1, d), lambda i: (0, 0))
    return pl.pallas_call(
        body, name="ln_fwd", grid=(seq // tr,),
        in_specs=[tok, tok, vec, vec], out_specs=[tok, tok],
        out_shape=[jax.ShapeDtypeStruct((seq, d), F32), jax.ShapeDtypeStruct((seq, d), BF16)],
        compiler_params=_cparams(("parallel",)),
    )(x, s, g.reshape(1, d), b.reshape(1, d))


def _ln_bwd(dy, x, s, g):
    seq, d = x.shape
    tr = min(LN_TR, seq)

    def body(dy_ref, x_ref, s_ref, g_ref, dz_ref, dzb_ref, dg_ref, db_ref):
        @pl.when(pl.program_id(0) == 0)
        def _():
            dg_ref[...] = jnp.zeros_like(dg_ref)
            db_ref[...] = jnp.zeros_like(db_ref)

        dy = dy_ref[...]
        xhat, rstd = _ln_stats(x_ref[...], s_ref[...])
        dyg = dy * g_ref[...]
        m1 = jnp.mean(dyg, -1, keepdims=True)
        m2 = jnp.mean(dyg * xhat, -1, keepdims=True)
        dz = rstd * (dyg - m1 - xhat * m2)
        dz_ref[...] = dz
        dzb_ref[...] = dz.astype(BF16)
        dg_ref[...] += jnp.sum(dy * xhat, axis=0, keepdims=True)
        db_ref[...] += jnp.sum(dy, axis=0, keepdims=True)

    tok = pl.BlockSpec((tr, d), lambda i: (i, 0))
    vec = pl.BlockSpec((1, d), lambda i: (0, 0))
    return pl.pallas_call(
        body, name="ln_bwd", grid=(seq // tr,),
        in_specs=[tok, tok, tok, vec], out_specs=[tok, tok, vec, vec],
        out_shape=[jax.ShapeDtypeStruct((seq, d), F32), jax.ShapeDtypeStruct((seq, d), BF16),
                   jax.ShapeDtypeStruct((1, d), F32), jax.ShapeDtypeStruct((1, d), F32)],
        compiler_params=_cparams(("arbitrary",)),
    )(dy, x, s, g.reshape(1, d))


def _loss_head(y, target):
    seq, d = y.shape
    tr = min(LN_TR, seq)

    def body(y_ref, t_ref, loss_ref, dy_ref):
        @pl.when(pl.program_id(0) == 0)
        def _():
            loss_ref[...] = jnp.zeros_like(loss_ref)

        err = y_ref[...] - t_ref[...]
        dy_ref[...] = err * (1.0 / d)
        part = jnp.sum(jnp.sum(err * err, axis=0, keepdims=True), axis=1, keepdims=True)
        loss_ref[...] += part * (0.5 / d)

    tok = pl.BlockSpec((tr, d), lambda i: (i, 0))
    return pl.pallas_call(
        body, name="loss_head", grid=(seq // tr,),
        in_specs=[tok, tok], out_specs=[pl.BlockSpec((8, 128), lambda i: (0, 0)), tok],
        out_shape=[jax.ShapeDtypeStruct((8, 128), F32), jax.ShapeDtypeStruct((seq, d), F32)],
        compiler_params=_cparams(("arbitrary",)),
    )(y, target)


COMM_MID = 0.8


def _matmul(a, b, *, ta=False, tb=False, b_sharded=False, out_sharded=False, out_dtypes=(F32,), epilogue=None,
            extras=(), tm=1024, tn=1024, tk=2048, name="matmul", comm=None):
    m, k = (a.shape[1], a.shape[0]) if ta else a.shape
    if b_sharded:
        bk, bn = b.shape[1], N_DEV * b.shape[2]
        shard_w = b.shape[2]
    else:
        bk, bn = b.shape
    n = bk if tb else bn
    assert k == (bn if tb else bk), (a.shape, b.shape)
    tm, tn, tk = min(tm, m), min(tn, n), min(tk, k)
    if b_sharded:
        if tb:
            tk = math.gcd(tk, shard_w)
        else:
            tn = math.gcd(tn, shard_w)
    if out_sharded:
        tn = math.gcd(tn, n // N_DEV)
    assert m % tm == 0 and n % tn == 0 and k % tk == 0, (m, n, k, tm, tn, tk)
    ni, nj, nk = m // tm, n // tn, k // tk
    dims = (((0 if ta else 1,), (1 if tb else 0,)), ((), ()))
    n_ex, n_out = len(extras), len(out_dtypes)
    n_ci = len(comm.ins) if comm else 0
    n_co = len(comm.out_shapes) if comm else 0
    total = ni * nj * nk
    mid_step = min(int(COMM_MID * total), total - 1)

    def body(*refs):
        a_ref, b_ref = refs[0], refs[1]
        ex_refs = refs[2:2 + n_ex]
        ci_refs = refs[2 + n_ex:2 + n_ex + n_ci]
        out_refs = refs[2 + n_ex + n_ci:2 + n_ex + n_ci + n_out]
        co_refs = refs[2 + n_ex + n_ci + n_out:2 + n_ex + n_ci + n_out + n_co]
        scratch = refs[2 + n_ex + n_ci + n_out + n_co:]
        acc, sems = (None, scratch) if nk == 1 else (scratch[0], scratch[1:])
        kk = pl.program_id(2)
        step = (pl.program_id(0) * nj + pl.program_id(1)) * nk + kk

        if comm:
            @pl.when(step == 0)
            def _():
                comm.start(ci_refs, co_refs, *sems)

        prod = lax.dot_general(a_ref[...].astype(BF16), b_ref[...].astype(BF16), dims, preferred_element_type=F32)

        def write(res):
            outs = (res,) if epilogue is None else epilogue(res, *[r[...] for r in ex_refs])
            for o_ref, val in zip(out_refs, outs, strict=True):
                o_ref[...] = val.astype(o_ref.dtype)

        if nk == 1:
            write(prod)
        else:
            @pl.when(kk == 0)
            def _():
                acc[...] = prod

            @pl.when(kk > 0)
            def _():
                acc[...] += prod

            @pl.when(kk == nk - 1)
            def _():
                write(acc[...])

        if comm:
            @pl.when(step == mid_step)
            def _():
                comm.mid(ci_refs, co_refs, *sems)

            @pl.when(step == total - 1)
            def _():
                comm.finish(ci_refs, co_refs, *sems)

    a_spec = pl.BlockSpec((tk, tm), lambda i, j, kk: (kk, i)) if ta else pl.BlockSpec((tm, tk), lambda i, j, kk: (i, kk))
    if b_sharded and tb:
        per = shard_w // tk
        b_spec = pl.BlockSpec((None, tn, tk), lambda i, j, kk: (kk // per, j, kk % per))
    elif b_sharded:
        per = shard_w // tn
        b_spec = pl.BlockSpec((None, tk, tn), lambda i, j, kk: (j // per, kk, j % per))
    elif tb:
        b_spec = pl.BlockSpec((tn, tk), lambda i, j, kk: (j, kk))
    else:
        b_spec = pl.BlockSpec((tk, tn), lambda i, j, kk: (kk, j))
    ex_spec = pl.BlockSpec((tm, tn), lambda i, j, kk: (i, j))
    if out_sharded:
        per_o = n // N_DEV // tn
        o_spec = pl.BlockSpec((None, tm, tn), lambda i, j, kk: (j // per_o, i, j % per_o))
        o_shape = (N_DEV, m, n // N_DEV)
    else:
        o_spec, o_shape = ex_spec, (m, n)
    hbm = pl.BlockSpec(memory_space=pl.ANY)
    outs = pl.pallas_call(
        body, name=name, grid=(ni, nj, nk),
        in_specs=[a_spec, b_spec] + [ex_spec] * n_ex + [hbm] * n_ci,
        out_specs=[o_spec] * n_out + [hbm] * n_co,
        out_shape=[jax.ShapeDtypeStruct(o_shape, dt) for dt in out_dtypes] + (list(comm.out_shapes) if comm else []),
        scratch_shapes=([] if nk == 1 else [pltpu.VMEM((tm, tn), F32)]) + (list(comm.scratch) if comm else []),
        compiler_params=_cparams(("arbitrary",) * 3 if comm else ("parallel", "parallel", "arbitrary")),
    )(a, b, *extras, *(comm.ins if comm else ()))
    return outs[0] if len(outs) == 1 else outs


def _epi_relu2(acc):
    r = jnp.maximum(acc, 0.0)
    return acc, r * r


def _epi_drelu2(acc, pre):
    return (acc * (2.0 * jnp.maximum(pre, 0.0)),)


def _epi_add(scale):
    return lambda acc, other: (acc + scale * other,)


def _adamw(parts, w, m, v, *, rows_per_step, name):
    n_parts, rows, cols = parts.shape
    tr = min(rows_per_step, rows)
    assert rows % tr == 0

    def body(p_ref, w_ref, m_ref, v_ref, g_ref, d_ref, mo_ref, vo_ref):
        g = p_ref[0].astype(F32)
        for i in range(1, n_parts):
            g = g + p_ref[i].astype(F32)
        m_new = ADAM_B1 * m_ref[...] + (1.0 - ADAM_B1) * g
        v_new = ADAM_B2 * v_ref[...] + (1.0 - ADAM_B2) * (g * g)
        m_hat = m_new / (1.0 - ADAM_B1 ** ADAM_STEP)
        v_hat = v_new / (1.0 - ADAM_B2 ** ADAM_STEP)
        g_ref[...] = g
        d_ref[...] = -ADAM_LR * (m_hat / (jnp.sqrt(v_hat) + ADAM_EPS) + ADAM_WD * w_ref[...])
        mo_ref[...] = m_new
        vo_ref[...] = v_new

    blk = pl.BlockSpec((tr, cols), lambda i: (i, 0))
    return pl.pallas_call(
        body, name=name, grid=(rows // tr,),
        in_specs=[pl.BlockSpec((n_parts, tr, cols), lambda i: (0, i, 0)), blk, blk, blk],
        out_specs=[blk] * 4,
        out_shape=[jax.ShapeDtypeStruct((rows, cols), F32)] * 4,
        compiler_params=_cparams(("parallel",)),
    )(parts, w, m, v)


def _position():
    return lax.axis_index("x"), lax.axis_index("y"), lax.axis_index("c")


def _comm_scratch(n):
    return [pltpu.SemaphoreType.DMA((7 * n,)), pltpu.SemaphoreType.DMA((7 * n,)), pltpu.SemaphoreType.DMA((n,))]


class _Gather:
    def __init__(self, blocks):
        self.ins = list(blocks)
        self.out_shapes = [jax.ShapeDtypeStruct((N_DEV,) + b.shape, b.dtype) for b in blocks]
        self.scratch = _comm_scratch(len(blocks))

    def _plan(self, n, ins, outs, send_sems, recv_sems, local_sems):
        x, y, c = _position()
        me, sibling = (x, y, c), (x, y, 1 - c)
        chips = [(1 - x, y), (x, 1 - y), (1 - x, 1 - y)]
        x_ref, out_ref = ins[n], outs[n]

        def slot(px, py, pc):
            return out_ref.at[4 * px + 2 * py + pc]

        def copy(k, blk, to, src=None):
            return pltpu.make_async_remote_copy(
                src_ref=slot(*blk) if src is None else src, dst_ref=slot(*blk),
                send_sem=send_sems.at[7 * n + k], recv_sem=recv_sems.at[7 * n + k], device_id=to, device_id_type=MESH)

        mine = lambda: pltpu.make_async_copy(x_ref, slot(*me), local_sems.at[n])
        first = lambda: [copy(0, me, sibling, src=x_ref)] + [copy(1 + j, me, (*chip, c), src=x_ref)
                                                             for j, chip in enumerate(chips)]
        passed = lambda j: copy(4 + j, (*chips[j], c), sibling)
        landed = lambda j: copy(1 + j, (*chips[j], c), me)
        from_sibling = lambda: [copy(0, sibling, me)] + [copy(4 + j, (*chip, 1 - c), me) for j, chip in enumerate(chips)]
        return mine, first, passed, landed, from_sibling

    def start(self, ins, outs, *sems):
        for n in range(len(self.ins)):
            mine, first, _, _, _ = self._plan(n, ins, outs, *sems)
            mine().start()
            for cp in first():
                cp.start()

    def mid(self, ins, outs, *sems):
        plans = [self._plan(n, ins, outs, *sems) for n in range(len(self.ins))]
        for j in range(3):
            for _, _, passed, landed, _ in plans:
                landed(j).wait_recv()
                passed(j).start()

    def finish(self, ins, outs, *sems):
        for n in range(len(self.ins)):
            mine, first, passed, _, from_sibling = self._plan(n, ins, outs, *sems)
            for cp in from_sibling():
                cp.wait_recv()
            for cp in first() + [passed(j) for j in range(3)]:
                cp.wait_send()
            mine().wait()


class _Exchange:
    def __init__(self, parts):
        self.ins = list(parts)
        self.out_shapes = [jax.ShapeDtypeStruct(p.shape, p.dtype) for p in parts]
        self.scratch = _comm_scratch(len(parts))

    def _plan(self, n, ins, outs, send_sems, recv_sems, local_sems):
        x, y, c = _position()
        me = 4 * x + 2 * y + c
        p_ref, out_ref = ins[n], outs[n]
        mine = lambda: pltpu.make_async_copy(p_ref.at[me], out_ref.at[me], local_sems.at[n])

        def copies(landing):
            out = []
            for k in range(1, N_DEV):
                px = 1 - x if k & 4 else x
                py = 1 - y if k & 2 else y
                pc = 1 - c if k & 1 else c
                peer_slot = 4 * px + 2 * py + pc
                out.append(pltpu.make_async_remote_copy(
                    src_ref=p_ref.at[peer_slot], dst_ref=out_ref.at[peer_slot if landing else me],
                    send_sem=send_sems.at[7 * n + k - 1], recv_sem=recv_sems.at[7 * n + k - 1],
                    device_id=(px, py, pc), device_id_type=MESH))
            return out

        return mine, copies

    def start(self, ins, outs, *sems):
        for n in range(len(self.ins)):
            mine, copies = self._plan(n, ins, outs, *sems)
            mine().start()
            for cp in copies(False):
                cp.start()

    def mid(self, ins, outs, *sems):
        pass

    def finish(self, ins, outs, *sems):
        for n in range(len(self.ins)):
            mine, copies = self._plan(n, ins, outs, *sems)
            for cp in copies(True):
                cp.wait_recv()
            for cp in copies(False):
                cp.wait_send()
            mine().wait()


def _comm_alone(comm, name):
    def body(*refs):
        n_i, n_o = len(comm.ins), len(comm.out_shapes)
        ins, outs, sems = refs[:n_i], refs[n_i:n_i + n_o], refs[n_i + n_o:]
        comm.start(ins, outs, *sems)
        comm.mid(ins, outs, *sems)
        comm.finish(ins, outs, *sems)

    hbm = pl.BlockSpec(memory_space=pl.ANY)
    return pl.pallas_call(
        body, name=name, out_shape=list(comm.out_shapes),
        in_specs=[hbm] * len(comm.ins), out_specs=[hbm] * len(comm.out_shapes),
        scratch_shapes=list(comm.scratch),
    )(*comm.ins)


RET_IN_W = 2 * RET_QK + 2 * RET_VW
GDN_IN_W = GDN_QKV + GDN_VW + 2 * GDN_V_HEADS
GDN_TAIL = 2 * GDN_V_HEADS
TAIL_PAD = 128

SMALL_SIZES = (RET_VW, GDN_V_HEADS, GDN_V_HEADS, GDN_DV, DEPTH * D_MODEL, DEPTH * D_MODEL, DEPTH * D_MODEL,
               DEPTH * D_MODEL, GDN_CONV * GDN_QKV)
SMALL_LANES = 128
SMALL_ROWS = -(-sum(SMALL_SIZES) // (8 * SMALL_LANES)) * 8


def _pack_small(*vecs):
    flat = jnp.concatenate([v.reshape(-1).astype(F32) for v in vecs])
    return jnp.pad(flat, (0, SMALL_ROWS * SMALL_LANES - flat.shape[0])).reshape(SMALL_ROWS, SMALL_LANES)


def _unpack_small(buf, shapes):
    flat, out, at = buf.reshape(-1), [], 0
    for shp in shapes:
        n = int(np.prod(shp))
        out.append(flat[at:at + n].reshape(shp))
        at += n
    return out


def _mlp_bwd(dz, h, a, r, w1, w2, name):
    dz, dzb = dz
    da = _matmul(dzb, w2, tb=True, out_dtypes=(BF16,), epilogue=_epi_drelu2, extras=(a,), name=name + "_da")
    dw2 = _matmul(r, dzb, ta=True, out_dtypes=(BF16,), name=name + "_dw2").reshape(N_DEV, -1, D_MODEL)
    dw1, x_w2 = _matmul(h, da, ta=True, out_sharded=True, out_dtypes=(BF16,), name=name + "_dw1",
                        comm=_Exchange([dw2]))
    dh, x_w1 = _matmul(da, w1, tb=True, b_sharded=True, epilogue=_epi_add(DN_ALPHA), extras=(dz,), name=name + "_dh",
                       comm=_Exchange([dw1]))
    return dh, x_w1, x_w2


def _chunk_cols(t):
    seq, nh = t.shape
    t = t.reshape(seq // CHUNK, CHUNK, nh).transpose(0, 2, 1)
    return t[..., None], t[:, :, None, :]


def _unchunk(col, row=None):
    t = col[..., 0] if row is None else col[..., 0] + row[:, :, 0, :]
    nc, nh, _ = t.shape
    return t.transpose(0, 2, 1).reshape(nc * CHUNK, nh)


def kernel(x, ret_w_in, ret_gn_g, ret_w_out, gdn_w_in, gdn_conv_w, gdn_a_log, gdn_dt_bias, gdn_norm_g, gdn_w_out, ln_mix_g, ln_mix_b, mlp_w1, mlp_w2, ln_ffn_g, ln_ffn_b, loss_target, m_ret_w_in, m_ret_gn_g, m_ret_w_out, m_gdn_w_in, m_gdn_conv_w, m_gdn_a_log, m_gdn_dt_bias, m_gdn_norm_g, m_gdn_w_out, m_ln_mix_g, m_ln_mix_b, m_mlp_w1, m_mlp_w2, m_ln_ffn_g, m_ln_ffn_b, v_ret_w_in, v_ret_gn_g, v_ret_w_out, v_gdn_w_in, v_gdn_conv_w, v_gdn_a_log, v_gdn_dt_bias, v_gdn_norm_g, v_gdn_w_out, v_ln_mix_g, v_ln_mix_b, v_mlp_w1, v_mlp_w2, v_ln_ffn_g, v_ln_ffn_b):
    xt, target = x[0], loss_target[0]
    seq = xt.shape[0]
    me = 4 * lax.axis_index("x") + 2 * lax.axis_index("y") + lax.axis_index("c")

    bf = lambda t: t.astype(BF16)
    cos, sin = _rope_tables(seq)
    w_ret_in, = _comm_alone(_Gather([bf(ret_w_in[0])]), "gather_ret_in")
    conv_blk = jnp.pad(gdn_conv_w[0], ((0, HALO - GDN_CONV), (0, 0)))
    shard_in = RET_IN_W // N_DEV
    xb = bf(xt)
    p0, w_ret_out, w1_0, w2_0, conv_all = _matmul(
        xb, w_ret_in, b_sharded=True, tn=shard_in, name="ret_in",
        comm=_Gather([bf(ret_w_out[0]), bf(mlp_w1[0]), bf(mlp_w2[0]), conv_blk]))
    w_ret_out, w2_0 = w_ret_out.reshape(RET_VW, D_MODEL), w2_0.reshape(D_FF, D_MODEL)
    conv_w = conv_all[:, :GDN_CONV].transpose(1, 0, 2).reshape(GDN_CONV, GDN_QKV)
    o0, s0 = _ret_fwd(p0, cos, sin, ret_gn_g[0])
    mix0 = _matmul(o0, w_ret_out, name="ret_out")
    h1, h1b = _ln_fwd(xt, mix0, ln_mix_g[0], ln_mix_b[0])
    a0, r0, gdn_in_all = _matmul(h1b, w1_0, b_sharded=True, out_dtypes=(F32, BF16), epilogue=_epi_relu2,
                                 name="mlp0_up", comm=_Gather([bf(gdn_w_in[0])]))
    m0, w_gdn_out, w1_1 = _matmul(r0, w2_0, name="mlp0_down", comm=_Gather([bf(gdn_w_out[0]), bf(mlp_w1[1])]))
    w_gdn_out = w_gdn_out.reshape(GDN_VW, D_MODEL)
    h2, h2b = _ln_fwd(h1, m0, ln_ffn_g[0], ln_ffn_b[0])

    w_gdn_in = gdn_in_all.transpose(1, 0, 2).reshape(D_MODEL, GDN_IN_W)
    w_gdn_main = w_gdn_in[:, :GDN_IN_W - GDN_TAIL]
    w_gdn_tail = jnp.pad(w_gdn_in[:, GDN_IN_W - GDN_TAIL:], ((0, 0), (0, TAIL_PAD - GDN_TAIL)))
    p1, w2_1 = _matmul(h2b, w_gdn_main, name="gdn_in", comm=_Gather([bf(mlp_w2[1])]))
    w2_1 = w2_1.reshape(D_FF, D_MODEL)
    pt = _matmul(h2b, w_gdn_tail, name="gdn_in_tail")
    c1 = _conv_fwd(p1, conv_w)
    b_in, a_in = pt[:, :GDN_V_HEADS], pt[:, GDN_V_HEADS:GDN_TAIL]
    beta, gc = _gates_fwd(b_in, a_in, gdn_a_log, gdn_dt_bias)
    beta_c, _ = _chunk_cols(beta)
    gc_c, gc_r = _chunk_cols(gc)
    o1, t1, s1 = _gdn_fwd(c1, p1, beta_c, gc_c, gc_r, gdn_norm_g[0])
    mix1 = _matmul(o1, w_gdn_out, name="gdn_out")
    h3, h3b = _ln_fwd(h2, mix1, ln_mix_g[1], ln_mix_b[1])
    a1, r1 = _matmul(h3b, w1_1, b_sharded=True, out_dtypes=(F32, BF16), epilogue=_epi_relu2, name="mlp1_up")
    m1 = _matmul(r1, w2_1, name="mlp1_down")
    h4, _ = _ln_fwd(h3, m1, ln_ffn_g[1], ln_ffn_b[1])
    loss_blk, dh4 = _loss_head(h4, target)
    loss = lax.psum(loss_blk[0, 0], ("x", "y", "c"))

    dz, dzb, dg_ffn1, db_ffn1 = _ln_bwd(dh4, h3, m1, ln_ffn_g[1])
    dh3, x_w1_1, x_w2_1 = _mlp_bwd((dz, dzb), h3b, a1, r1, w1_1, w2_1, "mlp1")
    dz, dzb, dg_mix1, db_mix1 = _ln_bwd(dh3, h2, mix1, ln_mix_g[1])
    do1 = _matmul(dzb, w_gdn_out, tb=True, name="gdn_out_do")
    dw_gdn_out = _matmul(o1, dzb, ta=True, out_dtypes=(BF16,), name="gdn_out_dw").reshape(N_DEV, -1, D_MODEL)
    dq, dk, dv, dzg, dbeta_c, dgc_c, dgc_r, dng = _gdn_bwd(c1, p1, beta_c, gc_c, gc_r, gdn_norm_g[0], t1, s1, do1)
    du, dconv = _conv_bwd(p1, jnp.concatenate([dq, dk, dv], axis=-1), conv_w)
    db_in, da_in, dalog, ddt = _gates_bwd(b_in, a_in, gdn_a_log, gdn_dt_bias, _unchunk(dbeta_c), _unchunk(dgc_c, dgc_r))
    dpt = jnp.concatenate([db_in, da_in, jnp.zeros((seq, TAIL_PAD - GDN_TAIL), F32)], axis=-1)
    dp1 = jnp.concatenate([du, dzg], axis=-1)
    dw_gdn_main, x_gdn_out = _matmul(h2b, dp1, ta=True, out_dtypes=(BF16,), name="gdn_in_dw",
                                     comm=_Exchange([dw_gdn_out]))
    dw_gdn_tail = _matmul(h2b, dpt, ta=True, out_dtypes=(BF16,), name="gdn_in_tail_dw")
    dw_gdn_in = jnp.concatenate([dw_gdn_main, dw_gdn_tail[:, :GDN_TAIL]], axis=-1)
    dw_gdn_in = dw_gdn_in.reshape(D_MODEL, N_DEV, GDN_IN_W // N_DEV).transpose(1, 0, 2)
    dh2 = _matmul(dpt, w_gdn_tail, tb=True, epilogue=_epi_add(DN_ALPHA), extras=(dz,), name="gdn_in_tail_dh")
    dh2, x_gdn_in = _matmul(dp1, w_gdn_main, tb=True, epilogue=_epi_add(1.0), extras=(dh2,), name="gdn_in_dh",
                            comm=_Exchange([dw_gdn_in]))

    dz, dzb, dg_ffn0, db_ffn0 = _ln_bwd(dh2, h1, m0, ln_ffn_g[0])
    dh1, x_w1_0, x_w2_0 = _mlp_bwd((dz, dzb), h1b, a0, r0, w1_0, w2_0, "mlp0")
    dz, dzb, dg_mix0, db_mix0 = _ln_bwd(dh1, xt, mix0, ln_mix_g[0])
    do0 = _matmul(dzb, w_ret_out, tb=True, name="ret_out_do")
    dw_ret_out = _matmul(o0, dzb, ta=True, out_dtypes=(BF16,), name="ret_out_dw").reshape(N_DEV, -1, D_MODEL)
    dq, dk, dv, dgate, dgng = _ret_bwd(p0, cos, sin, ret_gn_g[0], s0, do0)
    dp0 = jnp.concatenate([dq, dk, dv, dgate], axis=-1)
    dw_ret_in, x_ret_out = _matmul(xb, dp0, ta=True, out_sharded=True, out_dtypes=(BF16,), tn=shard_in,
                                   name="ret_in_dw", comm=_Exchange([dw_ret_out]))
    dx, x_ret_in = _matmul(dp0, w_ret_in, tb=True, b_sharded=True, epilogue=_epi_add(DN_ALPHA), extras=(dz,),
                           tk=shard_in, name="ret_in_dx", comm=_Exchange([dw_ret_in]))

    def update(parts, w, m, v, name):
        outs = _adamw(parts, w.reshape(parts.shape[1:]), m.reshape(parts.shape[1:]), v.reshape(parts.shape[1:]),
                      rows_per_step=128, name=name)
        return [t.reshape(w.shape) for t in outs]

    u_w1 = [update(p, mlp_w1[l], m_mlp_w1[l], v_mlp_w1[l], f"adamw_w1_{l}") for l, p in enumerate((x_w1_0, x_w1_1))]
    u_w2 = [update(p, mlp_w2[l], m_mlp_w2[l], v_mlp_w2[l], f"adamw_w2_{l}") for l, p in enumerate((x_w2_0, x_w2_1))]
    big_out = list(zip(
        update(x_ret_in, ret_w_in, m_ret_w_in, v_ret_w_in, "adamw_ret_in"),
        update(x_ret_out, ret_w_out, m_ret_w_out, v_ret_w_out, "adamw_ret_out"),
        update(x_gdn_in, gdn_w_in, m_gdn_w_in, v_gdn_w_in, "adamw_gdn_in"),
        update(x_gdn_out, gdn_w_out, m_gdn_w_out, v_gdn_w_out, "adamw_gdn_out"),
        [jnp.stack([a, b]) for a, b in zip(*u_w1)],
        [jnp.stack([a, b]) for a, b in zip(*u_w2)]))

    small_w = (ret_gn_g, gdn_a_log, gdn_dt_bias, gdn_norm_g, ln_mix_g, ln_mix_b, ln_ffn_g, ln_ffn_b)
    small_m = (m_ret_gn_g, m_gdn_a_log, m_gdn_dt_bias, m_gdn_norm_g, m_ln_mix_g, m_ln_mix_b, m_ln_ffn_g, m_ln_ffn_b)
    small_v = (v_ret_gn_g, v_gdn_a_log, v_gdn_dt_bias, v_gdn_norm_g, v_ln_mix_g, v_ln_mix_b, v_ln_ffn_g, v_ln_ffn_b)
    small_g = (dgng, dalog, ddt, jnp.sum(dng, axis=0),
               jnp.concatenate([dg_mix0, dg_mix1]), jnp.concatenate([db_mix0, db_mix1]),
               jnp.concatenate([dg_ffn0, dg_ffn1]), jnp.concatenate([db_ffn0, db_ffn1]), dconv)
    small_parts, = _comm_alone(_Gather([_pack_small(*small_g)]), "gather_small_grads")
    zero_conv = jnp.zeros((GDN_CONV, GDN_QKV), F32)
    small_out = _adamw(small_parts, _pack_small(*small_w, zero_conv), _pack_small(*small_m, zero_conv),
                       _pack_small(*small_v, zero_conv), rows_per_step=SMALL_ROWS, name="adamw_small")
    shapes = [t.shape for t in small_w] + [(GDN_CONV, GDN_QKV)]
    small_out = [_unpack_small(t, shapes) for t in small_out]
    conv_g = lax.dynamic_slice(small_out[0][-1], (0, me * (GDN_QKV // N_DEV)), (GDN_CONV, GDN_QKV // N_DEV))
    conv_out = _adamw(conv_g[None], gdn_conv_w[0], m_gdn_conv_w[0], v_gdn_conv_w[0],
                      rows_per_step=GDN_CONV, name="adamw_conv")

    def ordered(kind):
        b, s, cv = big_out[kind], small_out[kind], conv_out[kind][None]
        return [b[0], s[0], b[1], b[2], cv, s[1], s[2], s[3], b[3], s[4], s[5], b[4], b[5], s[6], s[7]]

    return (loss, dx[None], *ordered(0), *ordered(1), *ordered(2), *ordered(3))
```

```python
import functools
import math

import jax
import jax.numpy as jnp
import numpy as np
from jax import lax
from jax.experimental import pallas as pl
from jax.experimental.pallas import tpu as pltpu

F32 = jnp.float32
BF16 = jnp.bfloat16

N_DEV = 8
D_MODEL = 2048
CHUNK = 64
RET_HEADS = 8
RET_DK = 256
RET_DV = 512
RET_QK = RET_HEADS * RET_DK
RET_VW = RET_HEADS * RET_DV
ROPE_BASE = 10000.0
GN_EPS = 1e-6
GDN_K_HEADS = 16
GDN_V_HEADS = 32
GDN_DK = 128
GDN_DV = 128
GDN_QK = GDN_K_HEADS * GDN_DK
GDN_VW = GDN_V_HEADS * GDN_DV
GDN_QKV = 2 * GDN_QK + GDN_VW
GDN_CONV = 4
RMS_EPS = 1e-6
L2_EPS = 1e-6
D_FF = 4 * D_MODEL
DEPTH = 2
DN_ALPHA = (2.0 * DEPTH) ** 0.25
LN_EPS = 1e-5
ADAM_LR = 0.001
ADAM_B1 = 0.9
ADAM_B2 = 0.999
ADAM_EPS = 1e-08
ADAM_WD = 0.01
ADAM_STEP = 10

VMEM_LIMIT = 56 * 1024 * 1024
MESH = pl.DeviceIdType.MESH


def _cparams(sem=None):
    return pltpu.CompilerParams(dimension_semantics=sem, vmem_limit_bytes=VMEM_LIMIT)


_NT = (((2,), (2,)), ((0,), (0,)))
_NN = (((2,), (1,)), ((0,), (0,)))
_TN = (((1,), (1,)), ((0,), (0,)))


def _dg(a, b, dims):
    return lax.dot_general(a.astype(BF16), b.astype(BF16), dims, preferred_element_type=F32)


@jax.custom_vjp
def _nt(a, b):
    return _dg(a, b, _NT)


@jax.custom_vjp
def _nn(a, b):
    return _dg(a, b, _NN)


@jax.custom_vjp
def _tn(a, b):
    return _dg(a, b, _TN)


_nt.defvjp(lambda a, b: (_dg(a, b, _NT), (a, b)), lambda r, g: (_nn(g, r[1]), _tn(g, r[0])))
_nn.defvjp(lambda a, b: (_dg(a, b, _NN), (a, b)), lambda r, g: (_nt(g, r[1]), _tn(r[0], g)))
_tn.defvjp(lambda a, b: (_dg(a, b, _TN), (a, b)), lambda r, g: (_nt(r[1], g), _nn(r[0], g)))


def _iota2(shape, dim):
    return lax.broadcasted_iota(jnp.int32, shape, dim)


def _inv_unit_lower(a):
    c = a.shape[-1]
    eye = (_iota2((c, c), 0) == _iota2((c, c), 1)).astype(F32)
    m = -a
    p = eye + m
    for _ in range(int(math.log2(c)) - 1):
        m = _dg(m, m, _NN)
        p = p + _dg(p, m, _NN)
    return p


def _silu(x):
    return x * jax.nn.sigmoid(x)


def _rep2(t):
    h = t.shape[0]
    return jnp.broadcast_to(t[:, None], (h, 2) + t.shape[1:]).reshape((2 * h,) + t.shape[1:])


def _ret_chunk(q1, q2, k1, k2, v, gate, gn_g, s, cos, sin, intra, qdec, kdec, cdec):
    q = jnp.concatenate([q1 * cos - q2 * sin, q1 * sin + q2 * cos], axis=-1)
    k = jnp.concatenate([k1 * cos - k2 * sin, k1 * sin + k2 * cos], axis=-1) * (RET_DK ** -0.5)
    scores = _nt(q, k) * intra
    y = _nn(scores, v) + _nn(q * qdec, s)
    s_new = s * cdec + _tn(k * kdec, v)
    mu = jnp.mean(y, -1, keepdims=True)
    yc = y - mu
    var = jnp.mean(yc * yc, -1, keepdims=True)
    o = _silu(gate) * (yc * lax.rsqrt(var + GN_EPS) * gn_g)
    return o, s_new


def _ret_consts():
    log_gamma = np.log1p(-np.exp2(-5.0 - np.arange(RET_HEADS, dtype=np.float64)))
    idx = np.arange(CHUNK, dtype=np.float64)
    lg = log_gamma[:, None]
    intra = np.exp(lg[..., None] * np.abs(idx[:, None] - idx[None, :]))
    qdec = np.exp(lg * (idx + 1.0))[..., None]
    kdec = np.exp(lg * (CHUNK - 1.0 - idx))[..., None]
    cdec = np.exp(log_gamma * CHUNK)[:, None, None]
    return [jnp.asarray(t, F32) for t in (intra, qdec, kdec, cdec)]


def _rope_tables(seq):
    half = RET_DK // 2
    inv = ROPE_BASE ** (-jnp.arange(half, dtype=F32) / half)
    ang = jnp.arange(seq).astype(F32)[:, None] * inv[None, :]
    return jnp.cos(ang), jnp.sin(ang)


RET_HB = 2


def _ret_load(q_ref, k_ref, v_ref, gate_ref):
    hb, dk, dv, h = RET_HB, RET_DK, RET_DV, RET_DK // 2
    q, k, v, gate = q_ref[...], k_ref[...], v_ref[...], gate_ref[...]
    q1 = jnp.stack([q[:, i * dk:i * dk + h] for i in range(hb)])
    q2 = jnp.stack([q[:, i * dk + h:(i + 1) * dk] for i in range(hb)])
    k1 = jnp.stack([k[:, i * dk:i * dk + h] for i in range(hb)])
    k2 = jnp.stack([k[:, i * dk + h:(i + 1) * dk] for i in range(hb)])
    vs = jnp.stack([v[:, i * dv:(i + 1) * dv] for i in range(hb)])
    gs = jnp.stack([gate[:, i * dv:(i + 1) * dv] for i in range(hb)])
    return q1, q2, k1, k2, vs, gs


def _ret_specs(n_chunks, rev):
    hb = RET_HB
    cidx = (lambda n: n_chunks - 1 - n) if rev else (lambda n: n)
    qw, vw = hb * RET_DK, hb * RET_DV
    tok = [
        pl.BlockSpec((CHUNK, qw), lambda h, n: (cidx(n), h)),
        pl.BlockSpec((CHUNK, qw), lambda h, n: (cidx(n), RET_QK // qw + h)),
        pl.BlockSpec((CHUNK, vw), lambda h, n: (cidx(n), 2 * RET_QK // vw + h)),
        pl.BlockSpec((CHUNK, vw), lambda h, n: (cidx(n), (2 * RET_QK + RET_VW) // vw + h)),
        pl.BlockSpec((CHUNK, RET_DK // 2), lambda h, n: (cidx(n), 0)),
        pl.BlockSpec((CHUNK, RET_DK // 2), lambda h, n: (cidx(n), 0)),
    ]
    const = [
        pl.BlockSpec((hb, CHUNK, CHUNK), lambda h, n: (h, 0, 0)),
        pl.BlockSpec((hb, CHUNK, 1), lambda h, n: (h, 0, 0)),
        pl.BlockSpec((hb, CHUNK, 1), lambda h, n: (h, 0, 0)),
        pl.BlockSpec((hb, 1, 1), lambda h, n: (h, 0, 0)),
        pl.BlockSpec((hb, 1, RET_DV), lambda h, n: (h, 0, 0)),
    ]
    state = pl.BlockSpec((1, hb, RET_DK, RET_DV), lambda h, n: (cidx(n), h, 0, 0))
    return tok, const, state, cidx


def _grid_call(body, *, name, grid, in_specs, out_specs, out_shape, scratch_shapes, args, comm=None):
    if comm is None:
        return pl.pallas_call(body, name=name, grid=grid, in_specs=in_specs, out_specs=out_specs, out_shape=out_shape,
                              scratch_shapes=scratch_shapes,
                              compiler_params=_cparams(("parallel", "arbitrary")))(*args)
    n_in, n_out, n_scr = len(in_specs), len(out_specs), len(scratch_shapes)
    n_ci, n_co = len(comm.ins), len(comm.out_shapes)
    total = grid[0] * grid[1]
    mid_step = min(int(COMM_MID * total), total - 1)

    def carrying(*refs):
        ins, ci = refs[:n_in], refs[n_in:n_in + n_ci]
        at = n_in + n_ci
        outs, co = refs[at:at + n_out], refs[at + n_out:at + n_out + n_co]
        at += n_out + n_co
        scr, sems = refs[at:at + n_scr], refs[at + n_scr:]
        step = pl.program_id(0) * grid[1] + pl.program_id(1)
        pl.when(step == 0)(lambda: comm.start(ci, co, *sems))
        body(*ins, *outs, *scr)
        pl.when(step == mid_step)(lambda: comm.mid(ci, co, *sems))
        pl.when(step == total - 1)(lambda: comm.finish(ci, co, *sems))

    hbm = pl.BlockSpec(memory_space=pl.ANY)
    return pl.pallas_call(
        carrying, name=name, grid=grid,
        in_specs=list(in_specs) + [hbm] * n_ci, out_specs=list(out_specs) + [hbm] * n_co,
        out_shape=list(out_shape) + list(comm.out_shapes),
        scratch_shapes=list(scratch_shapes) + list(comm.scratch),
        compiler_params=_cparams(("arbitrary", "arbitrary")))(*args, *comm.ins)


def _ret_fwd(p, cos, sin, gn_g, comm=None):
    seq = p.shape[0]
    nc = seq // CHUNK
    hb = RET_HB
    tok, const, state, _ = _ret_specs(nc, False)

    def body(q_ref, k_ref, v_ref, gate_ref, cos_ref, sin_ref, intra_ref, qdec_ref, kdec_ref, cdec_ref, gng_ref,
             o_ref, ssave_ref, s_scr):
        @pl.when(pl.program_id(1) == 0)
        def _():
            s_scr[...] = jnp.zeros_like(s_scr)

        q1, q2, k1, k2, v, gate = _ret_load(q_ref, k_ref, v_ref, gate_ref)
        s = s_scr[...]
        ssave_ref[0] = s.astype(BF16)
        o, s_new = _ret_chunk(q1, q2, k1, k2, v, gate, gng_ref[...], s, cos_ref[...], sin_ref[...],
                              intra_ref[...], qdec_ref[...], kdec_ref[...], cdec_ref[...])
        s_scr[...] = s_new
        o_ref[...] = jnp.concatenate([o[i] for i in range(hb)], axis=-1).astype(o_ref.dtype)

    return _grid_call(
        body, name="ret_fwd", comm=comm,
        grid=(RET_HEADS // hb, nc),
        in_specs=tok + const,
        out_specs=[pl.BlockSpec((CHUNK, hb * RET_DV), lambda h, n: (n, h)), state],
        out_shape=[jax.ShapeDtypeStruct((seq, RET_VW), BF16),
                   jax.ShapeDtypeStruct((nc, RET_HEADS, RET_DK, RET_DV), BF16)],
        scratch_shapes=[pltpu.VMEM((hb, RET_DK, RET_DV), F32)],
        args=(p, p, p, p, cos, sin, *_ret_consts(), gn_g.reshape(RET_HEADS, 1, RET_DV)))


def _ret_bwd(p, cos, sin, gn_g, ssave, do, comm=None):
    seq = p.shape[0]
    nc = seq // CHUNK
    hb = RET_HB
    tok, const, state, cidx = _ret_specs(nc, True)

    def body(q_ref, k_ref, v_ref, gate_ref, cos_ref, sin_ref, intra_ref, qdec_ref, kdec_ref, cdec_ref, gng_ref,
             ssave_ref, do_ref, dq_ref, dk_ref, dv_ref, dgate_ref, dgng_ref, ds_scr):
        @pl.when(pl.program_id(1) == 0)
        def _():
            ds_scr[...] = jnp.zeros_like(ds_scr)
            dgng_ref[...] = jnp.zeros_like(dgng_ref)

        q1, q2, k1, k2, v, gate = _ret_load(q_ref, k_ref, v_ref, gate_ref)
        do = do_ref[...]
        dos = jnp.stack([do[:, i * RET_DV:(i + 1) * RET_DV] for i in range(hb)]).astype(F32)
        fn = functools.partial(_ret_chunk, cos=cos_ref[...], sin=sin_ref[...], intra=intra_ref[...],
                               qdec=qdec_ref[...], kdec=kdec_ref[...], cdec=cdec_ref[...])
        _, vjp = jax.vjp(fn, q1, q2, k1, k2, v, gate, gng_ref[...], ssave_ref[0].astype(F32))
        dq1, dq2, dk1, dk2, dv, dgate, dgng, ds = vjp((dos, ds_scr[...]))
        ds_scr[...] = ds
        dgng_ref[...] += dgng
        dq_ref[...] = jnp.concatenate([t[i] for i in range(hb) for t in (dq1, dq2)], axis=-1).astype(dq_ref.dtype)
        dk_ref[...] = jnp.concatenate([t[i] for i in range(hb) for t in (dk1, dk2)], axis=-1).astype(dk_ref.dtype)
        dv_ref[...] = jnp.concatenate([dv[i] for i in range(hb)], axis=-1).astype(dv_ref.dtype)
        dgate_ref[...] = jnp.concatenate([dgate[i] for i in range(hb)], axis=-1).astype(dgate_ref.dtype)

    qw, vw = hb * RET_DK, hb * RET_DV
    return _grid_call(
        body, name="ret_bwd", comm=comm,
        grid=(RET_HEADS // hb, nc),
        in_specs=tok + const + [state, pl.BlockSpec((CHUNK, vw), lambda h, n: (cidx(n), h))],
        out_specs=[pl.BlockSpec((CHUNK, qw), lambda h, n: (cidx(n), h)),
                   pl.BlockSpec((CHUNK, qw), lambda h, n: (cidx(n), h)),
                   pl.BlockSpec((CHUNK, vw), lambda h, n: (cidx(n), h)),
                   pl.BlockSpec((CHUNK, vw), lambda h, n: (cidx(n), h)),
                   pl.BlockSpec((hb, 1, RET_DV), lambda h, n: (h, 0, 0))],
        out_shape=[jax.ShapeDtypeStruct((seq, RET_QK), BF16), jax.ShapeDtypeStruct((seq, RET_QK), BF16),
                   jax.ShapeDtypeStruct((seq, RET_VW), BF16), jax.ShapeDtypeStruct((seq, RET_VW), BF16),
                   jax.ShapeDtypeStruct((RET_HEADS, 1, RET_DV), F32)],
        scratch_shapes=[pltpu.VMEM((hb, RET_DK, RET_DV), F32)],
        args=(p, p, p, p, cos, sin, *_ret_consts(), gn_g.reshape(RET_HEADS, 1, RET_DV), ssave, do))


def _gdn_common(qr, kr, gc_c, gc_r):
    qn = qr * lax.rsqrt(jnp.sum(qr * qr, -1, keepdims=True) + L2_EPS) * (GDN_DK ** -0.5)
    kn = kr * lax.rsqrt(jnp.sum(kr * kr, -1, keepdims=True) + L2_EPS)
    causal = _iota2((CHUNK, CHUNK), 0) >= _iota2((CHUNK, CHUNK), 1)
    decay = jnp.exp(jnp.where(causal, gc_c - gc_r, -1e30))
    return _rep2(qn), _rep2(kn), decay


def _gdn_a(qr, kr, beta_c, gc_c, gc_r):
    _, k, decay = _gdn_common(qr, kr, gc_c, gc_r)
    strict = _iota2((CHUNK, CHUNK), 0) > _iota2((CHUNK, CHUNK), 1)
    return jnp.where(strict, _nt(k * beta_c, k) * decay, 0.0)


def _gdn_main(qr, kr, v, z, beta_c, gc_c, gc_r, norm_g, t, s):
    q, k, decay = _gdn_common(qr, kr, gc_c, gc_r)
    eg = jnp.exp(gc_c)
    u = _nn(t, v * beta_c)
    w = _nn(t, k * (beta_c * eg))
    attn = _nt(q, k) * decay
    v_new = u - _nn(w, s)
    y = _nn(q * eg, s) + _nn(attn, v_new)
    last = _iota2((1, CHUNK, 1), 1) == CHUNK - 1
    gl = jnp.sum(jnp.where(last, gc_c, 0.0), axis=1, keepdims=True)
    s_new = s * jnp.exp(gl) + _tn(k * jnp.exp(gl - gc_c), v_new)
    yn = y * lax.rsqrt(jnp.mean(y * y, -1, keepdims=True) + RMS_EPS) * norm_g
    return yn * _silu(z), s_new


GDN_HK = 4


def _gdn_load(q_ref, k_ref, v_ref, z_ref):
    hk, hb, d = GDN_HK, 2 * GDN_HK, GDN_DK
    q, k, v, z = q_ref[...], k_ref[...], v_ref[...], z_ref[...]
    qs = jnp.stack([q[:, i * d:(i + 1) * d] for i in range(hk)])
    ks = jnp.stack([k[:, i * d:(i + 1) * d] for i in range(hk)])
    vs = jnp.stack([v[:, i * d:(i + 1) * d] for i in range(hb)])
    zs = jnp.stack([z[:, i * d:(i + 1) * d] for i in range(hb)])
    return qs, ks, vs, zs


def _gdn_specs(n_chunks, rev):
    hk, hb = GDN_HK, 2 * GDN_HK
    cidx = (lambda n: n_chunks - 1 - n) if rev else (lambda n: n)
    qw, vw = hk * GDN_DK, hb * GDN_DV
    tok = [
        pl.BlockSpec((CHUNK, qw), lambda h, n: (cidx(n), h)),
        pl.BlockSpec((CHUNK, qw), lambda h, n: (cidx(n), GDN_QK // qw + h)),
        pl.BlockSpec((CHUNK, vw), lambda h, n: (cidx(n), 2 * GDN_QK // vw + h)),
        pl.BlockSpec((CHUNK, vw), lambda h, n: (cidx(n), GDN_QKV // vw + h)),
        pl.BlockSpec((1, hb, CHUNK, 1), lambda h, n: (cidx(n), h, 0, 0)),
        pl.BlockSpec((1, hb, CHUNK, 1), lambda h, n: (cidx(n), h, 0, 0)),
        pl.BlockSpec((1, hb, 1, CHUNK), lambda h, n: (cidx(n), h, 0, 0)),
        pl.BlockSpec((1, GDN_DV), lambda h, n: (0, 0)),
    ]
    tsave = pl.BlockSpec((1, hb, CHUNK, CHUNK), lambda h, n: (cidx(n), h, 0, 0))
    ssave = pl.BlockSpec((1, hb, GDN_DK, GDN_DV), lambda h, n: (cidx(n), h, 0, 0))
    return tok, tsave, ssave, cidx


def _gdn_fwd(c, p, beta_c, gc_c, gc_r, norm_g):
    seq = c.shape[0]
    nc = seq // CHUNK
    hk, hb = GDN_HK, 2 * GDN_HK
    tok, tsave, ssave, _ = _gdn_specs(nc, False)

    def body(q_ref, k_ref, v_ref, z_ref, beta_ref, gcc_ref, gcr_ref, ng_ref, o_ref, tsave_ref, ssave_ref, s_scr):
        @pl.when(pl.program_id(1) == 0)
        def _():
            s_scr[...] = jnp.zeros_like(s_scr)

        qr, kr, v, z = _gdn_load(q_ref, k_ref, v_ref, z_ref)
        beta, gcc, gcr = beta_ref[0], gcc_ref[0], gcr_ref[0]
        s = s_scr[...]
        ssave_ref[0] = s.astype(BF16)
        t = _inv_unit_lower(_gdn_a(qr, kr, beta, gcc, gcr))
        tsave_ref[0] = t.astype(BF16)
        o, s_new = _gdn_main(qr, kr, v, z, beta, gcc, gcr, ng_ref[...], t, s)
        s_scr[...] = s_new
        o_ref[...] = jnp.concatenate([o[i] for i in range(hb)], axis=-1).astype(o_ref.dtype)

    return pl.pallas_call(
        body, name="gdn_fwd",
        grid=(GDN_K_HEADS // hk, nc),
        in_specs=tok,
        out_specs=[pl.BlockSpec((CHUNK, hb * GDN_DV), lambda h, n: (n, h)), tsave, ssave],
        out_shape=[jax.ShapeDtypeStruct((seq, GDN_VW), BF16),
                   jax.ShapeDtypeStruct((nc, GDN_V_HEADS, CHUNK, CHUNK), BF16),
                   jax.ShapeDtypeStruct((nc, GDN_V_HEADS, GDN_DK, GDN_DV), BF16)],
        scratch_shapes=[pltpu.VMEM((hb, GDN_DK, GDN_DV), F32)],
        compiler_params=_cparams(("parallel", "arbitrary")),
    )(c, c, c, p, beta_c, gc_c, gc_r, norm_g.reshape(1, GDN_DV))


def _gdn_bwd(c, p, beta_c, gc_c, gc_r, norm_g, tsave, ssave, do, comm=None):
    seq = c.shape[0]
    nc = seq // CHUNK
    hk, hb = GDN_HK, 2 * GDN_HK
    nhb = GDN_K_HEADS // hk
    tok, tsave_spec, ssave_spec, cidx = _gdn_specs(nc, True)

    def body(q_ref, k_ref, v_ref, z_ref, beta_ref, gcc_ref, gcr_ref, ng_ref, t_ref, s_ref, do_ref,
             dq_ref, dk_ref, dv_ref, dz_ref, dbeta_ref, dgcc_ref, dgcr_ref, dng_ref, ds_scr):
        @pl.when(pl.program_id(1) == 0)
        def _():
            ds_scr[...] = jnp.zeros_like(ds_scr)
            dng_ref[...] = jnp.zeros_like(dng_ref)

        qr, kr, v, z = _gdn_load(q_ref, k_ref, v_ref, z_ref)
        beta, gcc, gcr = beta_ref[0], gcc_ref[0], gcr_ref[0]
        t = t_ref[0].astype(F32)
        do = do_ref[...]
        dos = jnp.stack([do[:, i * GDN_DV:(i + 1) * GDN_DV] for i in range(hb)]).astype(F32)
        _, vjp_main = jax.vjp(_gdn_main, qr, kr, v, z, beta, gcc, gcr, ng_ref[...], t, s_ref[0].astype(F32))
        dqr, dkr, dv, dz, dbeta, dgcc, dgcr, dng, dt, ds = vjp_main((dos, ds_scr[...]))
        ds_scr[...] = ds
        da = -_dg(_dg(t, dt, _TN), t, _NT)
        _, vjp_a = jax.vjp(_gdn_a, qr, kr, beta, gcc, gcr)
        dqr2, dkr2, dbeta2, dgcc2, dgcr2 = vjp_a(da)
        dng_ref[...] += dng[None]
        dq_ref[...] = jnp.concatenate([(dqr + dqr2)[i] for i in range(hk)], axis=-1).astype(dq_ref.dtype)
        dk_ref[...] = jnp.concatenate([(dkr + dkr2)[i] for i in range(hk)], axis=-1).astype(dk_ref.dtype)
        dv_ref[...] = jnp.concatenate([dv[i] for i in range(hb)], axis=-1).astype(dv_ref.dtype)
        dz_ref[...] = jnp.concatenate([dz[i] for i in range(hb)], axis=-1).astype(dz_ref.dtype)
        dbeta_ref[0] = dbeta + dbeta2
        dgcc_ref[0] = dgcc + dgcc2
        dgcr_ref[0] = dgcr + dgcr2

    qw, vw = hk * GDN_DK, hb * GDN_DV
    col = pl.BlockSpec((1, hb, CHUNK, 1), lambda h, n: (cidx(n), h, 0, 0))
    row = pl.BlockSpec((1, hb, 1, CHUNK), lambda h, n: (cidx(n), h, 0, 0))
    return _grid_call(
        body, name="gdn_bwd", comm=comm,
        grid=(nhb, nc),
        in_specs=tok + [tsave_spec, ssave_spec, pl.BlockSpec((CHUNK, vw), lambda h, n: (cidx(n), h))],
        out_specs=[pl.BlockSpec((CHUNK, qw), lambda h, n: (cidx(n), h)),
                   pl.BlockSpec((CHUNK, qw), lambda h, n: (cidx(n), h)),
                   pl.BlockSpec((CHUNK, vw), lambda h, n: (cidx(n), h)),
                   pl.BlockSpec((CHUNK, vw), lambda h, n: (cidx(n), h)),
                   col, col, row,
                   pl.BlockSpec((1, 1, GDN_DV), lambda h, n: (h, 0, 0))],
        out_shape=[jax.ShapeDtypeStruct((seq, GDN_QK), F32), jax.ShapeDtypeStruct((seq, GDN_QK), F32),
                   jax.ShapeDtypeStruct((seq, GDN_VW), F32), jax.ShapeDtypeStruct((seq, GDN_VW), BF16),
                   jax.ShapeDtypeStruct((nc, GDN_V_HEADS, CHUNK, 1), F32),
                   jax.ShapeDtypeStruct((nc, GDN_V_HEADS, CHUNK, 1), F32),
                   jax.ShapeDtypeStruct((nc, GDN_V_HEADS, 1, CHUNK), F32),
                   jax.ShapeDtypeStruct((nhb, 1, GDN_DV), F32)],
        scratch_shapes=[pltpu.VMEM((hb, GDN_DK, GDN_DV), F32)],
        args=(c, c, c, p, beta_c, gc_c, gc_r, norm_g.reshape(1, GDN_DV), tsave, ssave, do))


CONV_TB = 512
CONV_CB = 1024
HALO = 8


def _conv_taps(ext, w):
    acc = w[GDN_CONV - 1:GDN_CONV] * ext
    for j in range(GDN_CONV - 1):
        acc = acc + w[j:j + 1] * pltpu.roll(ext, GDN_CONV - 1 - j, 0)
    return acc


def _conv_fwd(p, w):
    seq = p.shape[0]
    tb, cb = min(CONV_TB, seq), CONV_CB

    def body(prev_ref, cur_ref, w_ref, o_ref):
        first = pl.program_id(1) == 0
        prev = jnp.where(first, 0.0, prev_ref[...])
        ext = jnp.concatenate([prev, cur_ref[...]], axis=0)
        o_ref[...] = _silu(_conv_taps(ext, w_ref[...])[HALO:])

    return pl.pallas_call(
        body, name="conv_fwd",
        grid=(GDN_QKV // cb, seq // tb),
        in_specs=[pl.BlockSpec((HALO, cb), lambda j, i: (jnp.maximum(i * (tb // HALO) - 1, 0), j)),
                  pl.BlockSpec((tb, cb), lambda j, i: (i, j)),
                  pl.BlockSpec((GDN_CONV, cb), lambda j, i: (0, j))],
        out_specs=pl.BlockSpec((tb, cb), lambda j, i: (i, j)),
        out_shape=jax.ShapeDtypeStruct((seq, GDN_QKV), F32),
        compiler_params=_cparams(("parallel", "arbitrary")),
    )(p, p, w)


def _conv_bwd(p, dc, w):
    seq = p.shape[0]
    tb, cb = min(CONV_TB, seq), CONV_CB
    nt = seq // tb
    last_halo = seq // HALO - 1

    def body(prev_ref, cur_ref, next_ref, dcur_ref, dnext_ref, w_ref, du_ref, dw_ref):
        i = pl.program_id(1)

        @pl.when(i == 0)
        def _():
            dw_ref[...] = jnp.zeros_like(dw_ref)

        w = w_ref[...]
        prev = jnp.where(i == 0, 0.0, prev_ref[...])
        ext = jnp.concatenate([prev, cur_ref[...], next_ref[...]], axis=0)
        pre = _conv_taps(ext, w)
        dnext = jnp.where(i == nt - 1, 0.0, dnext_ref[...])
        dext = jnp.concatenate([jnp.zeros((HALO, cb), F32), dcur_ref[...], dnext], axis=0)
        sig = jax.nn.sigmoid(pre)
        dpre = dext * (sig * (1.0 + pre * (1.0 - sig)))
        rows = tb + 2 * HALO
        du = w[GDN_CONV - 1:GDN_CONV] * dpre
        for j in range(GDN_CONV - 1):
            du = du + w[j:j + 1] * pltpu.roll(dpre, rows - (GDN_CONV - 1 - j), 0)
        du_ref[...] = du[HALO:HALO + tb].astype(du_ref.dtype)
        dcore = dpre[HALO:HALO + tb]
        dws = []
        for j in range(GDN_CONV):
            sh = ext if j == GDN_CONV - 1 else pltpu.roll(ext, GDN_CONV - 1 - j, 0)
            dws.append(jnp.sum(dcore * sh[HALO:HALO + tb], axis=0, keepdims=True))
        dw_ref[...] += jnp.concatenate(dws, axis=0)

    hb = tb // HALO
    return pl.pallas_call(
        body, name="conv_bwd",
        grid=(GDN_QKV // cb, nt),
        in_specs=[pl.BlockSpec((HALO, cb), lambda j, i: (jnp.maximum(i * hb - 1, 0), j)),
                  pl.BlockSpec((tb, cb), lambda j, i: (i, j)),
                  pl.BlockSpec((HALO, cb), lambda j, i: (jnp.minimum((i + 1) * hb, last_halo), j)),
                  pl.BlockSpec((tb, cb), lambda j, i: (i, j)),
                  pl.BlockSpec((HALO, cb), lambda j, i: (jnp.minimum((i + 1) * hb, last_halo), j)),
                  pl.BlockSpec((GDN_CONV, cb), lambda j, i: (0, j))],
        out_specs=[pl.BlockSpec((tb, cb), lambda j, i: (i, j)),
                   pl.BlockSpec((GDN_CONV, cb), lambda j, i: (0, j))],
        out_shape=[jax.ShapeDtypeStruct((seq, GDN_QKV), BF16), jax.ShapeDtypeStruct((GDN_CONV, GDN_QKV), F32)],
        compiler_params=_cparams(("parallel", "arbitrary")),
    )(p, p, p, dc, dc, w)


GATE_TB = 512


def _split3(g):
    hi = g.astype(BF16)
    r = g - hi.astype(F32)
    mid = r.astype(BF16)
    lo = (r - mid.astype(F32)).astype(BF16)
    return hi, mid, lo


def _tri_chunks(n, upper):
    i, j = _iota2((n, n), 0), _iota2((n, n), 1)
    tri = (i <= j) if upper else (i >= j)
    return jnp.where(tri & ((i // CHUNK) == (j // CHUNK)), 1.0, 0.0).astype(BF16)


def _tri_apply(g, upper):
    tri = _tri_chunks(g.shape[0], upper)
    return sum(jnp.dot(tri, part, preferred_element_type=F32) for part in _split3(g))


@jax.custom_vjp
def _chunk_cumsum(g):
    return _tri_apply(g, False)


_chunk_cumsum.defvjp(lambda g: (_tri_apply(g, False), None), lambda _, d: (_tri_apply(d, True),))


def _gates(b, a, a_log, dt_bias):
    z = a + dt_bias
    softplus = jnp.maximum(z, 0.0) + jnp.log1p(jnp.exp(-jnp.abs(z)))
    g = -jnp.exp(a_log) * softplus
    return jax.nn.sigmoid(b), _chunk_cumsum(g)


def _gates_fwd(b, a, a_log, dt_bias):
    seq, nh = b.shape
    tb = min(GATE_TB, seq)

    def body(b_ref, a_ref, al_ref, dt_ref, beta_ref, gc_ref):
        beta, gc = _gates(b_ref[...], a_ref[...], al_ref[...], dt_ref[...])
        beta_ref[...] = beta
        gc_ref[...] = gc

    tok = pl.BlockSpec((tb, nh), lambda i: (i, 0))
    vec = pl.BlockSpec((1, nh), lambda i: (0, 0))
    return pl.pallas_call(
        body, name="gates_fwd", grid=(seq // tb,),
        in_specs=[tok, tok, vec, vec], out_specs=[tok, tok],
        out_shape=[jax.ShapeDtypeStruct((seq, nh), F32)] * 2,
        compiler_params=_cparams(("parallel",)),
    )(b, a, a_log, dt_bias)


def _gates_bwd(b, a, a_log, dt_bias, dbeta, dgc):
    seq, nh = b.shape
    tb = min(GATE_TB, seq)

    def body(b_ref, a_ref, al_ref, dt_ref, dbeta_ref, dgc_ref, db_ref, da_ref, dal_ref, ddt_ref):
        @pl.when(pl.program_id(0) == 0)
        def _():
            dal_ref[...] = jnp.zeros_like(dal_ref)
            ddt_ref[...] = jnp.zeros_like(ddt_ref)

        _, vjp = jax.vjp(_gates, b_ref[...], a_ref[...], al_ref[...], dt_ref[...])
        db, da, dal, ddt = vjp((dbeta_ref[...], dgc_ref[...]))
        db_ref[...] = db
        da_ref[...] = da
        dal_ref[...] += dal
        ddt_ref[...] += ddt

    tok = pl.BlockSpec((tb, nh), lambda i: (i, 0))
    vec = pl.BlockSpec((1, nh), lambda i: (0, 0))
    return pl.pallas_call(
        body, name="gates_bwd", grid=(seq // tb,),
        in_specs=[tok, tok, vec, vec, tok, tok], out_specs=[tok, tok, vec, vec],
        out_shape=[jax.ShapeDtypeStruct((seq, nh), F32)] * 2 + [jax.ShapeDtypeStruct((1, nh), F32)] * 2,
        compiler_params=_cparams(("arbitrary",)),
    )(b, a, a_log, dt_bias, dbeta, dgc)


LN_TR = 256


def _ln_stats(x, s):
    z = DN_ALPHA * x + s
    mu = jnp.mean(z, -1, keepdims=True)
    zc = z - mu
    var = jnp.mean(zc * zc, -1, keepdims=True)
    rstd = lax.rsqrt(var + LN_EPS)
    return zc * rstd, rstd


def _ln_fwd(x, s, g, b):
    seq, d = x.shape
    tr = min(LN_TR, seq)

    def body(x_ref, s_ref, g_ref, b_ref, o_ref, ob_ref):
        xhat, _ = _ln_stats(x_ref[...], s_ref[...])
        y = xhat * g_ref[...] + b_ref[...]
        o_ref[...] = y
        ob_ref[...] = y.astype(BF16)

    tok = pl.BlockSpec((tr, d), lambda i: (i, 0))
    vec = pl.BlockSpec((1, d), lambda i: (0, 0))
    return pl.pallas_call(
        body, name="ln_fwd", grid=(seq // tr,),
        in_specs=[tok, tok, vec, vec], out_specs=[tok, tok],
        out_shape=[jax.ShapeDtypeStruct((seq, d), F32), jax.ShapeDtypeStruct((seq, d), BF16)],
        compiler_params=_cparams(("parallel",)),
    )(x, s, g.reshape(1, d), b.reshape(1, d))


def _ln_bwd(dy, x, s, g):
    seq, d = x.shape
    tr = min(LN_TR, seq)

    def body(dy_ref, x_ref, s_ref, g_ref, dz_ref, dzb_ref, dg_ref, db_ref):
        @pl.when(pl.program_id(0) == 0)
        def _():
            dg_ref[...] = jnp.zeros_like(dg_ref)
            db_ref[...] = jnp.zeros_like(db_ref)

        dy = dy_ref[...]
        xhat, rstd = _ln_stats(x_ref[...], s_ref[...])
        dyg = dy * g_ref[...]
        m1 = jnp.mean(dyg, -1, keepdims=True)
        m2 = jnp.mean(dyg * xhat, -1, keepdims=True)
        dz = rstd * (dyg - m1 - xhat * m2)
        dz_ref[...] = dz
        dzb_ref[...] = dz.astype(BF16)
        dg_ref[...] += jnp.sum(dy * xhat, axis=0, keepdims=True)
        db_ref[...] += jnp.sum(dy, axis=0, keepdims=True)

    tok = pl.BlockSpec((tr, d), lambda i: (i, 0))
    vec = pl.BlockSpec((1, d), lambda i: (0, 0))
    return pl.pallas_call(
        body, name="ln_bwd", grid=(seq // tr,),
        in_specs=[tok, tok, tok, vec], out_specs=[tok, tok, vec, vec],
        out_shape=[jax.ShapeDtypeStruct((seq, d), F32), jax.ShapeDtypeStruct((seq, d), BF16),
                   jax.ShapeDtypeStruct((1, d), F32), jax.ShapeDtypeStruct((1, d), F32)],
        compiler_params=_cparams(("arbitrary",)),
    )(dy, x, s, g.reshape(1, d))


def _loss_head(y, target):
    seq, d = y.shape
    tr = min(LN_TR, seq)

    def body(y_ref, t_ref, loss_ref, dy_ref):
        @pl.when(pl.program_id(0) == 0)
        def _():
            loss_ref[...] = jnp.zeros_like(loss_ref)

        err = y_ref[...] - t_ref[...]
        dy_ref[...] = err * (1.0 / d)
        part = jnp.sum(jnp.sum(err * err, axis=0, keepdims=True), axis=1, keepdims=True)
        loss_ref[...] += part * (0.5 / d)

    tok = pl.BlockSpec((tr, d), lambda i: (i, 0))
    return pl.pallas_call(
        body, name="loss_head", grid=(seq // tr,),
        in_specs=[tok, tok], out_specs=[pl.BlockSpec((8, 128), lambda i: (0, 0)), tok],
        out_shape=[jax.ShapeDtypeStruct((8, 128), F32), jax.ShapeDtypeStruct((seq, d), F32)],
        compiler_params=_cparams(("arbitrary",)),
    )(y, target)


COMM_MID = 0.8


def _matmul(a, b, *, ta=False, tb=False, b_sharded=False, out_sharded=False, out_dtypes=(F32,), epilogue=None,
            extras=(), tm=1024, tn=1024, tk=2048, name="matmul", comm=None):
    m, k = (a.shape[1], a.shape[0]) if ta else a.shape
    if b_sharded:
        bk, bn = b.shape[1], N_DEV * b.shape[2]
        shard_w = b.shape[2]
    else:
        bk, bn = b.shape
    n = bk if tb else bn
    assert k == (bn if tb else bk), (a.shape, b.shape)
    tm, tn, tk = min(tm, m), min(tn, n), min(tk, k)
    if b_sharded:
        if tb:
            tk = math.gcd(tk, shard_w)
        else:
            tn = math.gcd(tn, shard_w)
    if out_sharded:
        tn = math.gcd(tn, n // N_DEV)
    assert m % tm == 0 and n % tn == 0 and k % tk == 0, (m, n, k, tm, tn, tk)
    ni, nj, nk = m // tm, n // tn, k // tk
    dims = (((0 if ta else 1,), (1 if tb else 0,)), ((), ()))
    n_ex, n_out = len(extras), len(out_dtypes)
    n_ci = len(comm.ins) if comm else 0
    n_co = len(comm.out_shapes) if comm else 0
    total = ni * nj * nk
    mid_step = min(int(COMM_MID * total), total - 1)

    def body(*refs):
        a_ref, b_ref = refs[0], refs[1]
        ex_refs = refs[2:2 + n_ex]
        ci_refs = refs[2 + n_ex:2 + n_ex + n_ci]
        out_refs = refs[2 + n_ex + n_ci:2 + n_ex + n_ci + n_out]
        co_refs = refs[2 + n_ex + n_ci + n_out:2 + n_ex + n_ci + n_out + n_co]
        scratch = refs[2 + n_ex + n_ci + n_out + n_co:]
        acc, sems = (None, scratch) if nk == 1 else (scratch[0], scratch[1:])
        kk = pl.program_id(2)
        step = (pl.program_id(0) * nj + pl.program_id(1)) * nk + kk

        if comm:
            @pl.when(step == 0)
            def _():
                comm.start(ci_refs, co_refs, *sems)

        prod = lax.dot_general(a_ref[...].astype(BF16), b_ref[...].astype(BF16), dims, preferred_element_type=F32)

        def write(res):
            outs = (res,) if epilogue is None else epilogue(res, *[r[...] for r in ex_refs])
            for o_ref, val in zip(out_refs, outs, strict=True):
                o_ref[...] = val.astype(o_ref.dtype)

        if nk == 1:
            write(prod)
        else:
            @pl.when(kk == 0)
            def _():
                acc[...] = prod

            @pl.when(kk > 0)
            def _():
                acc[...] += prod

            @pl.when(kk == nk - 1)
            def _():
                write(acc[...])

        if comm:
            @pl.when(step == mid_step)
            def _():
                comm.mid(ci_refs, co_refs, *sems)

            @pl.when(step == total - 1)
            def _():
                comm.finish(ci_refs, co_refs, *sems)

    a_spec = pl.BlockSpec((tk, tm), lambda i, j, kk: (kk, i)) if ta else pl.BlockSpec((tm, tk), lambda i, j, kk: (i, kk))
    if b_sharded and tb:
        per = shard_w // tk
        b_spec = pl.BlockSpec((None, tn, tk), lambda i, j, kk: (kk // per, j, kk % per))
    elif b_sharded:
        per = shard_w // tn
        b_spec = pl.BlockSpec((None, tk, tn), lambda i, j, kk: (j // per, kk, j % per))
    elif tb:
        b_spec = pl.BlockSpec((tn, tk), lambda i, j, kk: (j, kk))
    else:
        b_spec = pl.BlockSpec((tk, tn), lambda i, j, kk: (kk, j))
    ex_spec = pl.BlockSpec((tm, tn), lambda i, j, kk: (i, j))
    if out_sharded:
        per_o = n // N_DEV // tn
        o_spec = pl.BlockSpec((None, tm, tn), lambda i, j, kk: (j // per_o, i, j % per_o))
        o_shape = (N_DEV, m, n // N_DEV)
    else:
        o_spec, o_shape = ex_spec, (m, n)
    hbm = pl.BlockSpec(memory_space=pl.ANY)
    outs = pl.pallas_call(
        body, name=name, grid=(ni, nj, nk),
        in_specs=[a_spec, b_spec] + [ex_spec] * n_ex + [hbm] * n_ci,
        out_specs=[o_spec] * n_out + [hbm] * n_co,
        out_shape=[jax.ShapeDtypeStruct(o_shape, dt) for dt in out_dtypes] + (list(comm.out_shapes) if comm else []),
        scratch_shapes=([] if nk == 1 else [pltpu.VMEM((tm, tn), F32)]) + (list(comm.scratch) if comm else []),
        compiler_params=_cparams(("arbitrary",) * 3 if comm else ("parallel", "parallel", "arbitrary")),
    )(a, b, *extras, *(comm.ins if comm else ()))
    return outs[0] if len(outs) == 1 else outs


def _epi_relu2(acc):
    r = jnp.maximum(acc, 0.0)
    return acc, r * r


def _epi_drelu2(acc, pre):
    return (acc * (2.0 * jnp.maximum(pre, 0.0)),)


def _epi_add(scale):
    return lambda acc, other: (acc + scale * other,)


def _adamw(parts, w, m, v, *, rows_per_step, name):
    n_parts, rows, cols = parts.shape
    tr = min(rows_per_step, rows)
    assert rows % tr == 0

    def body(p_ref, w_ref, m_ref, v_ref, g_ref, d_ref, mo_ref, vo_ref):
        g = p_ref[0].astype(F32)
        for i in range(1, n_parts):
            g = g + p_ref[i].astype(F32)
        m_new = ADAM_B1 * m_ref[...] + (1.0 - ADAM_B1) * g
        v_new = ADAM_B2 * v_ref[...] + (1.0 - ADAM_B2) * (g * g)
        m_hat = m_new / (1.0 - ADAM_B1 ** ADAM_STEP)
        v_hat = v_new / (1.0 - ADAM_B2 ** ADAM_STEP)
        g_ref[...] = g
        d_ref[...] = -ADAM_LR * (m_hat / (jnp.sqrt(v_hat) + ADAM_EPS) + ADAM_WD * w_ref[...])
        mo_ref[...] = m_new
        vo_ref[...] = v_new

    blk = pl.BlockSpec((tr, cols), lambda i: (i, 0))
    return pl.pallas_call(
        body, name=name, grid=(rows // tr,),
        in_specs=[pl.BlockSpec((n_parts, tr, cols), lambda i: (0, i, 0)), blk, blk, blk],
        out_specs=[blk] * 4,
        out_shape=[jax.ShapeDtypeStruct((rows, cols), F32)] * 4,
        compiler_params=_cparams(("parallel",)),
    )(parts, w, m, v)


def _position():
    return lax.axis_index("x"), lax.axis_index("y"), lax.axis_index("c")


def _comm_scratch(n):
    return [pltpu.SemaphoreType.DMA((7 * n,)), pltpu.SemaphoreType.DMA((7 * n,)), pltpu.SemaphoreType.DMA((n,))]


class _Gather:
    def __init__(self, blocks):
        self.ins = list(blocks)
        self.out_shapes = [jax.ShapeDtypeStruct((N_DEV,) + b.shape, b.dtype) for b in blocks]
        self.scratch = _comm_scratch(len(blocks))

    def _plan(self, n, ins, outs, send_sems, recv_sems, local_sems):
        x, y, c = _position()
        me, sibling = (x, y, c), (x, y, 1 - c)
        chips = [(1 - x, y), (x, 1 - y), (1 - x, 1 - y)]
        x_ref, out_ref = ins[n], outs[n]

        def slot(px, py, pc):
            return out_ref.at[4 * px + 2 * py + pc]

        def copy(k, blk, to, src=None):
            return pltpu.make_async_remote_copy(
                src_ref=slot(*blk) if src is None else src, dst_ref=slot(*blk),
                send_sem=send_sems.at[7 * n + k], recv_sem=recv_sems.at[7 * n + k], device_id=to, device_id_type=MESH)

        mine = lambda: pltpu.make_async_copy(x_ref, slot(*me), local_sems.at[n])
        first = lambda: [copy(0, me, sibling, src=x_ref)] + [copy(1 + j, me, (*chip, c), src=x_ref)
                                                             for j, chip in enumerate(chips)]
        passed = lambda j: copy(4 + j, (*chips[j], c), sibling)
        landed = lambda j: copy(1 + j, (*chips[j], c), me)
        from_sibling = lambda: [copy(0, sibling, me)] + [copy(4 + j, (*chip, 1 - c), me) for j, chip in enumerate(chips)]
        return mine, first, passed, landed, from_sibling

    def start(self, ins, outs, *sems):
        for n in range(len(self.ins)):
            mine, first, _, _, _ = self._plan(n, ins, outs, *sems)
            mine().start()
            for cp in first():
                cp.start()

    def mid(self, ins, outs, *sems):
        plans = [self._plan(n, ins, outs, *sems) for n in range(len(self.ins))]
        for j in range(3):
            for _, _, passed, landed, _ in plans:
                landed(j).wait_recv()
                passed(j).start()

    def finish(self, ins, outs, *sems):
        for n in range(len(self.ins)):
            mine, first, passed, _, from_sibling = self._plan(n, ins, outs, *sems)
            for cp in from_sibling():
                cp.wait_recv()
            for cp in first() + [passed(j) for j in range(3)]:
                cp.wait_send()
            mine().wait()


class _Exchange:
    def __init__(self, parts):
        self.ins = list(parts)
        self.out_shapes = [jax.ShapeDtypeStruct(p.shape, p.dtype) for p in parts]
        self.scratch = _comm_scratch(len(parts))

    def _plan(self, n, ins, outs, send_sems, recv_sems, local_sems):
        x, y, c = _position()
        me = 4 * x + 2 * y + c
        p_ref, out_ref = ins[n], outs[n]
        mine = lambda: pltpu.make_async_copy(p_ref.at[me], out_ref.at[me], local_sems.at[n])

        def copies(landing):
            out = []
            for k in range(1, N_DEV):
                px = 1 - x if k & 4 else x
                py = 1 - y if k & 2 else y
                pc = 1 - c if k & 1 else c
                peer_slot = 4 * px + 2 * py + pc
                out.append(pltpu.make_async_remote_copy(
                    src_ref=p_ref.at[peer_slot], dst_ref=out_ref.at[peer_slot if landing else me],
                    send_sem=send_sems.at[7 * n + k - 1], recv_sem=recv_sems.at[7 * n + k - 1],
                    device_id=(px, py, pc), device_id_type=MESH))
            return out

        return mine, copies

    def start(self, ins, outs, *sems):
        for n in range(len(self.ins)):
            mine, copies = self._plan(n, ins, outs, *sems)
            mine().start()
            for cp in copies(False):
                cp.start()

    def mid(self, ins, outs, *sems):
        pass

    def finish(self, ins, outs, *sems):
        for n in range(len(self.ins)):
            mine, copies = self._plan(n, ins, outs, *sems)
            for cp in copies(True):
                cp.wait_recv()
            for cp in copies(False):
                cp.wait_send()
            mine().wait()


def _comm_alone(comm, name):
    def body(*refs):
        n_i, n_o = len(comm.ins), len(comm.out_shapes)
        ins, outs, sems = refs[:n_i], refs[n_i:n_i + n_o], refs[n_i + n_o:]
        comm.start(ins, outs, *sems)
        comm.mid(ins, outs, *sems)
        comm.finish(ins, outs, *sems)

    hbm = pl.BlockSpec(memory_space=pl.ANY)
    return pl.pallas_call(
        body, name=name, out_shape=list(comm.out_shapes),
        in_specs=[hbm] * len(comm.ins), out_specs=[hbm] * len(comm.out_shapes),
        scratch_shapes=list(comm.scratch),
    )(*comm.ins)


RET_IN_W = 2 * RET_QK + 2 * RET_VW
GDN_IN_W = GDN_QKV + GDN_VW + 2 * GDN_V_HEADS
GDN_TAIL = 2 * GDN_V_HEADS
TAIL_PAD = 128

SMALL_SIZES = (RET_VW, GDN_V_HEADS, GDN_V_HEADS, GDN_DV, DEPTH * D_MODEL, DEPTH * D_MODEL, DEPTH * D_MODEL,
               DEPTH * D_MODEL, GDN_CONV * GDN_QKV)
SMALL_LANES = 128
SMALL_ROWS = -(-sum(SMALL_SIZES) // (8 * SMALL_LANES)) * 8


def _pack_small(*vecs):
    flat = jnp.concatenate([v.reshape(-1).astype(F32) for v in vecs])
    return jnp.pad(flat, (0, SMALL_ROWS * SMALL_LANES - flat.shape[0])).reshape(SMALL_ROWS, SMALL_LANES)


def _unpack_small(buf, shapes):
    flat, out, at = buf.reshape(-1), [], 0
    for shp in shapes:
        n = int(np.prod(shp))
        out.append(flat[at:at + n].reshape(shp))
        at += n
    return out


def _mlp_bwd(dz, h, a, r, w1, w2, name):
    dz, dzb = dz
    da = _matmul(dzb, w2, tb=True, out_dtypes=(BF16,), epilogue=_epi_drelu2, extras=(a,), name=name + "_da")
    dw2 = _matmul(r, dzb, ta=True, out_dtypes=(BF16,), name=name + "_dw2").reshape(N_DEV, -1, D_MODEL)
    dw1 = _matmul(h, da, ta=True, out_sharded=True, out_dtypes=(BF16,), name=name + "_dw1")
    dh = _matmul(da, w1, tb=True, b_sharded=True, epilogue=_epi_add(DN_ALPHA), extras=(dz,), name=name + "_dh")
    return dh, dw1, dw2


def _chunk_cols(t):
    seq, nh = t.shape
    t = t.reshape(seq // CHUNK, CHUNK, nh).transpose(0, 2, 1)
    return t[..., None], t[:, :, None, :]


def _unchunk(col, row=None):
    t = col[..., 0] if row is None else col[..., 0] + row[:, :, 0, :]
    nc, nh, _ = t.shape
    return t.transpose(0, 2, 1).reshape(nc * CHUNK, nh)


def kernel(x, ret_w_in, ret_gn_g, ret_w_out, gdn_w_in, gdn_conv_w, gdn_a_log, gdn_dt_bias, gdn_norm_g, gdn_w_out, ln_mix_g, ln_mix_b, mlp_w1, mlp_w2, ln_ffn_g, ln_ffn_b, loss_target, m_ret_w_in, m_ret_gn_g, m_ret_w_out, m_gdn_w_in, m_gdn_conv_w, m_gdn_a_log, m_gdn_dt_bias, m_gdn_norm_g, m_gdn_w_out, m_ln_mix_g, m_ln_mix_b, m_mlp_w1, m_mlp_w2, m_ln_ffn_g, m_ln_ffn_b, v_ret_w_in, v_ret_gn_g, v_ret_w_out, v_gdn_w_in, v_gdn_conv_w, v_gdn_a_log, v_gdn_dt_bias, v_gdn_norm_g, v_gdn_w_out, v_ln_mix_g, v_ln_mix_b, v_mlp_w1, v_mlp_w2, v_ln_ffn_g, v_ln_ffn_b):
    xt, target = x[0], loss_target[0]
    seq = xt.shape[0]
    me = 4 * lax.axis_index("x") + 2 * lax.axis_index("y") + lax.axis_index("c")

    bf = lambda t: t.astype(BF16)
    cos, sin = _rope_tables(seq)
    w_ret_in, = _comm_alone(_Gather([bf(ret_w_in[0])]), "gather_ret_in")
    conv_blk = jnp.pad(gdn_conv_w[0], ((0, HALO - GDN_CONV), (0, 0)))
    shard_in = RET_IN_W // N_DEV
    xb = bf(xt)
    p0, w_ret_out, w1_0, conv_all = _matmul(
        xb, w_ret_in, b_sharded=True, tn=shard_in, name="ret_in",
        comm=_Gather([bf(ret_w_out[0]), bf(mlp_w1[0]), conv_blk]))
    w_ret_out = w_ret_out.reshape(RET_VW, D_MODEL)
    conv_w = conv_all[:, :GDN_CONV].transpose(1, 0, 2).reshape(GDN_CONV, GDN_QKV)
    o0, s0, w2_0, gdn_in_all = _ret_fwd(p0, cos, sin, ret_gn_g[0], comm=_Gather([bf(mlp_w2[0]), bf(gdn_w_in[0])]))
    w2_0 = w2_0.reshape(D_FF, D_MODEL)
    mix0 = _matmul(o0, w_ret_out, name="ret_out")
    h1, h1b = _ln_fwd(xt, mix0, ln_mix_g[0], ln_mix_b[0])
    a0, r0, w_gdn_out, w1_1 = _matmul(h1b, w1_0, b_sharded=True, out_dtypes=(F32, BF16), epilogue=_epi_relu2,
                                      name="mlp0_up", comm=_Gather([bf(gdn_w_out[0]), bf(mlp_w1[1])]))
    w_gdn_out = w_gdn_out.reshape(GDN_VW, D_MODEL)
    m0, w2_1 = _matmul(r0, w2_0, name="mlp0_down", comm=_Gather([bf(mlp_w2[1])]))
    w2_1 = w2_1.reshape(D_FF, D_MODEL)
    h2, h2b = _ln_fwd(h1, m0, ln_ffn_g[0], ln_ffn_b[0])

    w_gdn_in = gdn_in_all.transpose(1, 0, 2).reshape(D_MODEL, GDN_IN_W)
    w_gdn_main = w_gdn_in[:, :GDN_IN_W - GDN_TAIL]
    w_gdn_tail = jnp.pad(w_gdn_in[:, GDN_IN_W - GDN_TAIL:], ((0, 0), (0, TAIL_PAD - GDN_TAIL)))
    p1 = _matmul(h2b, w_gdn_main, name="gdn_in")
    pt = _matmul(h2b, w_gdn_tail, name="gdn_in_tail")
    c1 = _conv_fwd(p1, conv_w)
    b_in, a_in = pt[:, :GDN_V_HEADS], pt[:, GDN_V_HEADS:GDN_TAIL]
    beta, gc = _gates_fwd(b_in, a_in, gdn_a_log, gdn_dt_bias)
    beta_c, _ = _chunk_cols(beta)
    gc_c, gc_r = _chunk_cols(gc)
    o1, t1, s1 = _gdn_fwd(c1, p1, beta_c, gc_c, gc_r, gdn_norm_g[0])
    mix1 = _matmul(o1, w_gdn_out, name="gdn_out")
    h3, h3b = _ln_fwd(h2, mix1, ln_mix_g[1], ln_mix_b[1])
    a1, r1 = _matmul(h3b, w1_1, b_sharded=True, out_dtypes=(F32, BF16), epilogue=_epi_relu2, name="mlp1_up")
    m1 = _matmul(r1, w2_1, name="mlp1_down")
    h4, _ = _ln_fwd(h3, m1, ln_ffn_g[1], ln_ffn_b[1])
    loss_blk, dh4 = _loss_head(h4, target)
    loss = lax.psum(loss_blk[0, 0], ("x", "y", "c"))

    dz, dzb, dg_ffn1, db_ffn1 = _ln_bwd(dh4, h3, m1, ln_ffn_g[1])
    dh3, dw1_1, dw2_1 = _mlp_bwd((dz, dzb), h3b, a1, r1, w1_1, w2_1, "mlp1")
    dz, dzb, dg_mix1, db_mix1 = _ln_bwd(dh3, h2, mix1, ln_mix_g[1])
    do1 = _matmul(dzb, w_gdn_out, tb=True, name="gdn_out_do")
    dw_gdn_out = _matmul(o1, dzb, ta=True, out_dtypes=(BF16,), name="gdn_out_dw").reshape(N_DEV, -1, D_MODEL)
    dq, dk, dv, dzg, dbeta_c, dgc_c, dgc_r, dng, x_w1_1, x_w2_1, x_gdn_out = _gdn_bwd(
        c1, p1, beta_c, gc_c, gc_r, gdn_norm_g[0], t1, s1, do1, comm=_Exchange([dw1_1, dw2_1, dw_gdn_out]))
    du, dconv = _conv_bwd(p1, jnp.concatenate([dq, dk, dv], axis=-1), conv_w)
    db_in, da_in, dalog, ddt = _gates_bwd(b_in, a_in, gdn_a_log, gdn_dt_bias, _unchunk(dbeta_c), _unchunk(dgc_c, dgc_r))
    dpt = jnp.concatenate([db_in, da_in, jnp.zeros((seq, TAIL_PAD - GDN_TAIL), F32)], axis=-1)
    dp1 = jnp.concatenate([du, dzg], axis=-1)
    dw_gdn_main = _matmul(h2b, dp1, ta=True, out_dtypes=(BF16,), name="gdn_in_dw")
    dw_gdn_tail = _matmul(h2b, dpt, ta=True, out_dtypes=(BF16,), name="gdn_in_tail_dw")
    dw_gdn_in = jnp.concatenate([dw_gdn_main, dw_gdn_tail[:, :GDN_TAIL]], axis=-1)
    dw_gdn_in = dw_gdn_in.reshape(D_MODEL, N_DEV, GDN_IN_W // N_DEV).transpose(1, 0, 2)
    dh2 = _matmul(dpt, w_gdn_tail, tb=True, epilogue=_epi_add(DN_ALPHA), extras=(dz,), name="gdn_in_tail_dh")
    dh2, x_gdn_in = _matmul(dp1, w_gdn_main, tb=True, epilogue=_epi_add(1.0), extras=(dh2,), name="gdn_in_dh",
                            comm=_Exchange([dw_gdn_in]))

    dz, dzb, dg_ffn0, db_ffn0 = _ln_bwd(dh2, h1, m0, ln_ffn_g[0])
    dh1, dw1_0, dw2_0 = _mlp_bwd((dz, dzb), h1b, a0, r0, w1_0, w2_0, "mlp0")
    dz, dzb, dg_mix0, db_mix0 = _ln_bwd(dh1, xt, mix0, ln_mix_g[0])
    do0 = _matmul(dzb, w_ret_out, tb=True, name="ret_out_do")
    dw_ret_out = _matmul(o0, dzb, ta=True, out_dtypes=(BF16,), name="ret_out_dw").reshape(N_DEV, -1, D_MODEL)
    dq, dk, dv, dgate, dgng, x_w1_0, x_w2_0 = _ret_bwd(p0, cos, sin, ret_gn_g[0], s0, do0,
                                                      comm=_Exchange([dw1_0, dw2_0]))
    dp0 = jnp.concatenate([dq, dk, dv, dgate], axis=-1)
    dw_ret_in, x_ret_out = _matmul(xb, dp0, ta=True, out_sharded=True, out_dtypes=(BF16,), tn=shard_in,
                                   name="ret_in_dw", comm=_Exchange([dw_ret_out]))
    dx, x_ret_in = _matmul(dp0, w_ret_in, tb=True, b_sharded=True, epilogue=_epi_add(DN_ALPHA), extras=(dz,),
                           tk=shard_in, name="ret_in_dx", comm=_Exchange([dw_ret_in]))

    def update(parts, w, m, v, name):
        outs = _adamw(parts, w.reshape(parts.shape[1:]), m.reshape(parts.shape[1:]), v.reshape(parts.shape[1:]),
                      rows_per_step=128, name=name)
        return [t.reshape(w.shape) for t in outs]

    u_w1 = [update(p, mlp_w1[l], m_mlp_w1[l], v_mlp_w1[l], f"adamw_w1_{l}") for l, p in enumerate((x_w1_0, x_w1_1))]
    u_w2 = [update(p, mlp_w2[l], m_mlp_w2[l], v_mlp_w2[l], f"adamw_w2_{l}") for l, p in enumerate((x_w2_0, x_w2_1))]
    big_out = list(zip(
        update(x_ret_in, ret_w_in, m_ret_w_in, v_ret_w_in, "adamw_ret_in"),
        update(x_ret_out, ret_w_out, m_ret_w_out, v_ret_w_out, "adamw_ret_out"),
        update(x_gdn_in, gdn_w_in, m_gdn_w_in, v_gdn_w_in, "adamw_gdn_in"),
        update(x_gdn_out, gdn_w_out, m_gdn_w_out, v_gdn_w_out, "adamw_gdn_out"),
        [jnp.stack([a, b]) for a, b in zip(*u_w1)],
        [jnp.stack([a, b]) for a, b in zip(*u_w2)]))

    small_w = (ret_gn_g, gdn_a_log, gdn_dt_bias, gdn_norm_g, ln_mix_g, ln_mix_b, ln_ffn_g, ln_ffn_b)
    small_m = (m_ret_gn_g, m_gdn_a_log, m_gdn_dt_bias, m_gdn_norm_g, m_ln_mix_g, m_ln_mix_b, m_ln_ffn_g, m_ln_ffn_b)
    small_v = (v_ret_gn_g, v_gdn_a_log, v_gdn_dt_bias, v_gdn_norm_g, v_ln_mix_g, v_ln_mix_b, v_ln_ffn_g, v_ln_ffn_b)
    small_g = (dgng, dalog, ddt, jnp.sum(dng, axis=0),
               jnp.concatenate([dg_mix0, dg_mix1]), jnp.concatenate([db_mix0, db_mix1]),
               jnp.concatenate([dg_ffn0, dg_ffn1]), jnp.concatenate([db_ffn0, db_ffn1]), dconv)
    small_parts, = _comm_alone(_Gather([_pack_small(*small_g)]), "gather_small_grads")
    zero_conv = jnp.zeros((GDN_CONV, GDN_QKV), F32)
    small_out = _adamw(small_parts, _pack_small(*small_w, zero_conv), _pack_small(*small_m, zero_conv),
                       _pack_small(*small_v, zero_conv), rows_per_step=SMALL_ROWS, name="adamw_small")
    shapes = [t.shape for t in small_w] + [(GDN_CONV, GDN_QKV)]
    small_out = [_unpack_small(t, shapes) for t in small_out]
    conv_g = lax.dynamic_slice(small_out[0][-1], (0, me * (GDN_QKV // N_DEV)), (GDN_CONV, GDN_QKV // N_DEV))
    conv_out = _adamw(conv_g[None], gdn_conv_w[0], m_gdn_conv_w[0], v_gdn_conv_w[0],
                      rows_per_step=GDN_CONV, name="adamw_conv")

    def ordered(kind):
        b, s, cv = big_out[kind], small_out[kind], conv_out[kind][None]
        return [b[0], s[0], b[1], b[2], cv, s[1], s[2], s[3], b[3], s[4], s[5], b[4], b[5], s[6], s[7]]

    return (loss, dx[None], *ordered(0), *ordered(1), *ordered(2), *ordered(3))
```

```python
import functools
import math

import jax
import jax.numpy as jnp
import numpy as np
from jax import lax
from jax.experimental import pallas as pl
from jax.experimental.pallas import tpu as pltpu

F32 = jnp.float32
BF16 = jnp.bfloat16

N_DEV = 8
D_MODEL = 2048
CHUNK = 64
RET_HEADS = 8
RET_DK = 256
RET_DV = 512
RET_QK = RET_HEADS * RET_DK
RET_VW = RET_HEADS * RET_DV
ROPE_BASE = 10000.0
GN_EPS = 1e-6
GDN_K_HEADS = 16
GDN_V_HEADS = 32
GDN_DK = 128
GDN_DV = 128
GDN_QK = GDN_K_HEADS * GDN_DK
GDN_VW = GDN_V_HEADS * GDN_DV
GDN_QKV = 2 * GDN_QK + GDN_VW
GDN_CONV = 4
RMS_EPS = 1e-6
L2_EPS = 1e-6
D_FF = 4 * D_MODEL
DEPTH = 2
DN_ALPHA = (2.0 * DEPTH) ** 0.25
LN_EPS = 1e-5
ADAM_LR = 0.001
ADAM_B1 = 0.9
ADAM_B2 = 0.999
ADAM_EPS = 1e-08
ADAM_WD = 0.01
ADAM_STEP = 10

VMEM_LIMIT = 56 * 1024 * 1024
MESH = pl.DeviceIdType.MESH


def _cparams(sem=None):
    return pltpu.CompilerParams(dimension_semantics=sem, vmem_limit_bytes=VMEM_LIMIT)


_NT = (((2,), (2,)), ((0,), (0,)))
_NN = (((2,), (1,)), ((0,), (0,)))
_TN = (((1,), (1,)), ((0,), (0,)))


def _dg(a, b, dims):
    return lax.dot_general(a.astype(BF16), b.astype(BF16), dims, preferred_element_type=F32)


@jax.custom_vjp
def _nt(a, b):
    return _dg(a, b, _NT)


@jax.custom_vjp
def _nn(a, b):
    return _dg(a, b, _NN)


@jax.custom_vjp
def _tn(a, b):
    return _dg(a, b, _TN)


_nt.defvjp(lambda a, b: (_dg(a, b, _NT), (a, b)), lambda r, g: (_nn(g, r[1]), _tn(g, r[0])))
_nn.defvjp(lambda a, b: (_dg(a, b, _NN), (a, b)), lambda r, g: (_nt(g, r[1]), _tn(r[0], g)))
_tn.defvjp(lambda a, b: (_dg(a, b, _TN), (a, b)), lambda r, g: (_nt(r[1], g), _nn(r[0], g)))


def _iota2(shape, dim):
    return lax.broadcasted_iota(jnp.int32, shape, dim)


def _inv_unit_lower(a):
    c = a.shape[-1]
    eye = (_iota2((c, c), 0) == _iota2((c, c), 1)).astype(F32)
    m = -a
    p = eye + m
    for _ in range(int(math.log2(c)) - 1):
        m = _dg(m, m, _NN)
        p = p + _dg(p, m, _NN)
    return p


def _silu(x):
    return x * jax.nn.sigmoid(x)


def _rep2(t):
    h = t.shape[0]
    return jnp.broadcast_to(t[:, None], (h, 2) + t.shape[1:]).reshape((2 * h,) + t.shape[1:])


def _ret_chunk(q1, q2, k1, k2, v, gate, gn_g, s, cos, sin, intra, qdec, kdec, cdec):
    q = jnp.concatenate([q1 * cos - q2 * sin, q1 * sin + q2 * cos], axis=-1)
    k = jnp.concatenate([k1 * cos - k2 * sin, k1 * sin + k2 * cos], axis=-1) * (RET_DK ** -0.5)
    scores = _nt(q, k) * intra
    y = _nn(scores, v) + _nn(q * qdec, s)
    s_new = s * cdec + _tn(k * kdec, v)
    mu = jnp.mean(y, -1, keepdims=True)
    yc = y - mu
    var = jnp.mean(yc * yc, -1, keepdims=True)
    o = _silu(gate) * (yc * lax.rsqrt(var + GN_EPS) * gn_g)
    return o, s_new


def _ret_consts():
    log_gamma = np.log1p(-np.exp2(-5.0 - np.arange(RET_HEADS, dtype=np.float64)))
    idx = np.arange(CHUNK, dtype=np.float64)
    lg = log_gamma[:, None]
    intra = np.exp(lg[..., None] * np.abs(idx[:, None] - idx[None, :]))
    qdec = np.exp(lg * (idx + 1.0))[..., None]
    kdec = np.exp(lg * (CHUNK - 1.0 - idx))[..., None]
    cdec = np.exp(log_gamma * CHUNK)[:, None, None]
    return [jnp.asarray(t, F32) for t in (intra, qdec, kdec, cdec)]


def _rope_tables(seq):
    half = RET_DK // 2
    inv = ROPE_BASE ** (-jnp.arange(half, dtype=F32) / half)
    ang = jnp.arange(seq).astype(F32)[:, None] * inv[None, :]
    return jnp.cos(ang), jnp.sin(ang)


RET_HB = 2


def _ret_load(q_ref, k_ref, v_ref, gate_ref):
    hb, dk, dv, h = RET_HB, RET_DK, RET_DV, RET_DK // 2
    q, k, v, gate = q_ref[...], k_ref[...], v_ref[...], gate_ref[...]
    q1 = jnp.stack([q[:, i * dk:i * dk + h] for i in range(hb)])
    q2 = jnp.stack([q[:, i * dk + h:(i + 1) * dk] for i in range(hb)])
    k1 = jnp.stack([k[:, i * dk:i * dk + h] for i in range(hb)])
    k2 = jnp.stack([k[:, i * dk + h:(i + 1) * dk] for i in range(hb)])
    vs = jnp.stack([v[:, i * dv:(i + 1) * dv] for i in range(hb)])
    gs = jnp.stack([gate[:, i * dv:(i + 1) * dv] for i in range(hb)])
    return q1, q2, k1, k2, vs, gs


def _ret_specs(n_chunks, rev):
    hb = RET_HB
    cidx = (lambda n: n_chunks - 1 - n) if rev else (lambda n: n)
    qw, vw = hb * RET_DK, hb * RET_DV
    tok = [
        pl.BlockSpec((CHUNK, qw), lambda h, n: (cidx(n), h)),
        pl.BlockSpec((CHUNK, qw), lambda h, n: (cidx(n), RET_QK // qw + h)),
        pl.BlockSpec((CHUNK, vw), lambda h, n: (cidx(n), 2 * RET_QK // vw + h)),
        pl.BlockSpec((CHUNK, vw), lambda h, n: (cidx(n), (2 * RET_QK + RET_VW) // vw + h)),
        pl.BlockSpec((CHUNK, RET_DK // 2), lambda h, n: (cidx(n), 0)),
        pl.BlockSpec((CHUNK, RET_DK // 2), lambda h, n: (cidx(n), 0)),
    ]
    const = [
        pl.BlockSpec((hb, CHUNK, CHUNK), lambda h, n: (h, 0, 0)),
        pl.BlockSpec((hb, CHUNK, 1), lambda h, n: (h, 0, 0)),
        pl.BlockSpec((hb, CHUNK, 1), lambda h, n: (h, 0, 0)),
        pl.BlockSpec((hb, 1, 1), lambda h, n: (h, 0, 0)),
        pl.BlockSpec((hb, 1, RET_DV), lambda h, n: (h, 0, 0)),
    ]
    state = pl.BlockSpec((1, hb, RET_DK, RET_DV), lambda h, n: (cidx(n), h, 0, 0))
    return tok, const, state, cidx


def _grid_call(body, *, name, grid, in_specs, out_specs, out_shape, scratch_shapes, args, comm=None):
    if comm is None:
        return pl.pallas_call(body, name=name, grid=grid, in_specs=in_specs, out_specs=out_specs, out_shape=out_shape,
                              scratch_shapes=scratch_shapes,
                              compiler_params=_cparams(("parallel", "arbitrary")))(*args)
    n_in, n_out, n_scr = len(in_specs), len(out_specs), len(scratch_shapes)
    n_ci, n_co = len(comm.ins), len(comm.out_shapes)
    total = grid[0] * grid[1]
    mid_step = min(int(COMM_MID * total), total - 1)

    def carrying(*refs):
        ins, ci = refs[:n_in], refs[n_in:n_in + n_ci]
        at = n_in + n_ci
        outs, co = refs[at:at + n_out], refs[at + n_out:at + n_out + n_co]
        at += n_out + n_co
        scr, sems = refs[at:at + n_scr], refs[at + n_scr:]
        step = pl.program_id(0) * grid[1] + pl.program_id(1)
        pl.when(step == 0)(lambda: comm.start(ci, co, *sems))
        body(*ins, *outs, *scr)
        pl.when(step == mid_step)(lambda: comm.mid(ci, co, *sems))
        pl.when(step == total - 1)(lambda: comm.finish(ci, co, *sems))

    hbm = pl.BlockSpec(memory_space=pl.ANY)
    return pl.pallas_call(
        carrying, name=name, grid=grid,
        in_specs=list(in_specs) + [hbm] * n_ci, out_specs=list(out_specs) + [hbm] * n_co,
        out_shape=list(out_shape) + list(comm.out_shapes),
        scratch_shapes=list(scratch_shapes) + list(comm.scratch),
        compiler_params=_cparams(("arbitrary", "arbitrary")))(*args, *comm.ins)


def _ret_fwd(p, cos, sin, gn_g, comm=None):
    seq = p.shape[0]
    nc = seq // CHUNK
    hb = RET_HB
    tok, const, state, _ = _ret_specs(nc, False)

    def body(q_ref, k_ref, v_ref, gate_ref, cos_ref, sin_ref, intra_ref, qdec_ref, kdec_ref, cdec_ref, gng_ref,
             o_ref, ssave_ref, s_scr):
        @pl.when(pl.program_id(1) == 0)
        def _():
            s_scr[...] = jnp.zeros_like(s_scr)

        q1, q2, k1, k2, v, gate = _ret_load(q_ref, k_ref, v_ref, gate_ref)
        s = s_scr[...]
        ssave_ref[0] = s.astype(BF16)
        o, s_new = _ret_chunk(q1, q2, k1, k2, v, gate, gng_ref[...], s, cos_ref[...], sin_ref[...],
                              intra_ref[...], qdec_ref[...], kdec_ref[...], cdec_ref[...])
        s_scr[...] = s_new
        o_ref[...] = jnp.concatenate([o[i] for i in range(hb)], axis=-1).astype(o_ref.dtype)

    return _grid_call(
        body, name="ret_fwd", comm=comm,
        grid=(RET_HEADS // hb, nc),
        in_specs=tok + const,
        out_specs=[pl.BlockSpec((CHUNK, hb * RET_DV), lambda h, n: (n, h)), state],
        out_shape=[jax.ShapeDtypeStruct((seq, RET_VW), BF16),
                   jax.ShapeDtypeStruct((nc, RET_HEADS, RET_DK, RET_DV), BF16)],
        scratch_shapes=[pltpu.VMEM((hb, RET_DK, RET_DV), F32)],
        args=(p, p, p, p, cos, sin, *_ret_consts(), gn_g.reshape(RET_HEADS, 1, RET_DV)))


def _ret_bwd(p, cos, sin, gn_g, ssave, do, comm=None):
    seq = p.shape[0]
    nc = seq // CHUNK
    hb = RET_HB
    tok, const, state, cidx = _ret_specs(nc, True)

    def body(q_ref, k_ref, v_ref, gate_ref, cos_ref, sin_ref, intra_ref, qdec_ref, kdec_ref, cdec_ref, gng_ref,
             ssave_ref, do_ref, dq_ref, dk_ref, dv_ref, dgate_ref, dgng_ref, ds_scr):
        @pl.when(pl.program_id(1) == 0)
        def _():
            ds_scr[...] = jnp.zeros_like(ds_scr)
            dgng_ref[...] = jnp.zeros_like(dgng_ref)

        q1, q2, k1, k2, v, gate = _ret_load(q_ref, k_ref, v_ref, gate_ref)
        do = do_ref[...]
        dos = jnp.stack([do[:, i * RET_DV:(i + 1) * RET_DV] for i in range(hb)]).astype(F32)
        fn = functools.partial(_ret_chunk, cos=cos_ref[...], sin=sin_ref[...], intra=intra_ref[...],
                               qdec=qdec_ref[...], kdec=kdec_ref[...], cdec=cdec_ref[...])
        _, vjp = jax.vjp(fn, q1, q2, k1, k2, v, gate, gng_ref[...], ssave_ref[0].astype(F32))
        dq1, dq2, dk1, dk2, dv, dgate, dgng, ds = vjp((dos, ds_scr[...]))
        ds_scr[...] = ds
        dgng_ref[...] += dgng
        dq_ref[...] = jnp.concatenate([t[i] for i in range(hb) for t in (dq1, dq2)], axis=-1).astype(dq_ref.dtype)
        dk_ref[...] = jnp.concatenate([t[i] for i in range(hb) for t in (dk1, dk2)], axis=-1).astype(dk_ref.dtype)
        dv_ref[...] = jnp.concatenate([dv[i] for i in range(hb)], axis=-1).astype(dv_ref.dtype)
        dgate_ref[...] = jnp.concatenate([dgate[i] for i in range(hb)], axis=-1).astype(dgate_ref.dtype)

    qw, vw = hb * RET_DK, hb * RET_DV
    return _grid_call(
        body, name="ret_bwd", comm=comm,
        grid=(RET_HEADS // hb, nc),
        in_specs=tok + const + [state, pl.BlockSpec((CHUNK, vw), lambda h, n: (cidx(n), h))],
        out_specs=[pl.BlockSpec((CHUNK, qw), lambda h, n: (cidx(n), h)),
                   pl.BlockSpec((CHUNK, qw), lambda h, n: (cidx(n), h)),
                   pl.BlockSpec((CHUNK, vw), lambda h, n: (cidx(n), h)),
                   pl.BlockSpec((CHUNK, vw), lambda h, n: (cidx(n), h)),
                   pl.BlockSpec((hb, 1, RET_DV), lambda h, n: (h, 0, 0))],
        out_shape=[jax.ShapeDtypeStruct((seq, RET_QK), BF16), jax.ShapeDtypeStruct((seq, RET_QK), BF16),
                   jax.ShapeDtypeStruct((seq, RET_VW), BF16), jax.ShapeDtypeStruct((seq, RET_VW), BF16),
                   jax.ShapeDtypeStruct((RET_HEADS, 1, RET_DV), F32)],
        scratch_shapes=[pltpu.VMEM((hb, RET_DK, RET_DV), F32)],
        args=(p, p, p, p, cos, sin, *_ret_consts(), gn_g.reshape(RET_HEADS, 1, RET_DV), ssave, do))


def _gdn_common(qr, kr, gc_c, gc_r):
    qn = qr * lax.rsqrt(jnp.sum(qr * qr, -1, keepdims=True) + L2_EPS) * (GDN_DK ** -0.5)
    kn = kr * lax.rsqrt(jnp.sum(kr * kr, -1, keepdims=True) + L2_EPS)
    causal = _iota2((CHUNK, CHUNK), 0) >= _iota2((CHUNK, CHUNK), 1)
    decay = jnp.exp(jnp.where(causal, gc_c - gc_r, -1e30))
    return _rep2(qn), _rep2(kn), decay


def _gdn_a(k, decay, beta_c):
    strict = _iota2((CHUNK, CHUNK), 0) > _iota2((CHUNK, CHUNK), 1)
    return jnp.where(strict, _nt(k * beta_c, k) * decay, 0.0)


@jax.custom_vjp
def _inv_saved(a, t):
    return t


_inv_saved.defvjp(lambda a, t: (t, t),
                  lambda t, dt: (-_dg(_dg(t, dt, _TN), t, _NT), jnp.zeros_like(t)))


def _gdn_chunk(qr, kr, v, z, beta_c, gc_c, gc_r, norm_g, t_saved, s):
    q, k, decay = _gdn_common(qr, kr, gc_c, gc_r)
    t = _inv_saved(_gdn_a(k, decay, beta_c), t_saved)
    return _gdn_rest(q, k, decay, v, z, beta_c, gc_c, norm_g, t, s)


def _gdn_rest(q, k, decay, v, z, beta_c, gc_c, norm_g, t, s):
    eg = jnp.exp(gc_c)
    u = _nn(t, v * beta_c)
    w = _nn(t, k * (beta_c * eg))
    attn = _nt(q, k) * decay
    v_new = u - _nn(w, s)
    y = _nn(q * eg, s) + _nn(attn, v_new)
    last = _iota2((1, CHUNK, 1), 1) == CHUNK - 1
    gl = jnp.sum(jnp.where(last, gc_c, 0.0), axis=1, keepdims=True)
    s_new = s * jnp.exp(gl) + _tn(k * jnp.exp(gl - gc_c), v_new)
    yn = y * lax.rsqrt(jnp.mean(y * y, -1, keepdims=True) + RMS_EPS) * norm_g
    return yn * _silu(z), s_new


GDN_HK = 4


def _gdn_load(q_ref, k_ref, v_ref, z_ref):
    hk, hb, d = GDN_HK, 2 * GDN_HK, GDN_DK
    q, k, v, z = q_ref[...], k_ref[...], v_ref[...], z_ref[...]
    qs = jnp.stack([q[:, i * d:(i + 1) * d] for i in range(hk)])
    ks = jnp.stack([k[:, i * d:(i + 1) * d] for i in range(hk)])
    vs = jnp.stack([v[:, i * d:(i + 1) * d] for i in range(hb)])
    zs = jnp.stack([z[:, i * d:(i + 1) * d] for i in range(hb)])
    return qs, ks, vs, zs


def _head_cols(blk):
    return jnp.stack([blk[:, i:i + 1] for i in range(blk.shape[1])])


def _cols_block(cols):
    return jnp.concatenate([cols[i] for i in range(cols.shape[0])], axis=-1)


def _gdn_specs(n_chunks, rev):
    hk, hb = GDN_HK, 2 * GDN_HK
    cidx = (lambda n: n_chunks - 1 - n) if rev else (lambda n: n)
    qw, vw = hk * GDN_DK, hb * GDN_DV
    tok = [
        pl.BlockSpec((CHUNK, qw), lambda h, n: (cidx(n), h)),
        pl.BlockSpec((CHUNK, qw), lambda h, n: (cidx(n), GDN_QK // qw + h)),
        pl.BlockSpec((CHUNK, vw), lambda h, n: (cidx(n), 2 * GDN_QK // vw + h)),
        pl.BlockSpec((CHUNK, vw), lambda h, n: (cidx(n), GDN_QKV // vw + h)),
        pl.BlockSpec((None, CHUNK, hb), lambda h, n: (h, cidx(n), 0)),
        pl.BlockSpec((None, CHUNK, hb), lambda h, n: (h, cidx(n), 0)),
        pl.BlockSpec((1, hb, 1, CHUNK), lambda h, n: (cidx(n), h, 0, 0)),
        pl.BlockSpec((1, GDN_DV), lambda h, n: (0, 0)),
    ]
    tsave = pl.BlockSpec((1, hb, CHUNK, CHUNK), lambda h, n: (cidx(n), h, 0, 0))
    ssave = pl.BlockSpec((1, hb, GDN_DK, GDN_DV), lambda h, n: (cidx(n), h, 0, 0))
    return tok, tsave, ssave, cidx


def _gdn_fwd(c, p, beta_c, gc_c, gc_r, norm_g):
    seq = c.shape[0]
    nc = seq // CHUNK
    hk, hb = GDN_HK, 2 * GDN_HK
    tok, tsave, ssave, _ = _gdn_specs(nc, False)

    def body(q_ref, k_ref, v_ref, z_ref, beta_ref, gcc_ref, gcr_ref, ng_ref, o_ref, tsave_ref, ssave_ref, s_scr):
        @pl.when(pl.program_id(1) == 0)
        def _():
            s_scr[...] = jnp.zeros_like(s_scr)

        qr, kr, v, z = _gdn_load(q_ref, k_ref, v_ref, z_ref)
        beta, gcc, gcr = _head_cols(beta_ref[...]), _head_cols(gcc_ref[...]), gcr_ref[0]
        s = s_scr[...]
        ssave_ref[0] = s.astype(BF16)
        q, k, decay = _gdn_common(qr, kr, gcc, gcr)
        t = _inv_unit_lower(_gdn_a(k, decay, beta))
        tsave_ref[0] = t.astype(BF16)
        o, s_new = _gdn_rest(q, k, decay, v, z, beta, gcc, ng_ref[...], t, s)
        s_scr[...] = s_new
        o_ref[...] = jnp.concatenate([o[i] for i in range(hb)], axis=-1).astype(o_ref.dtype)

    return pl.pallas_call(
        body, name="gdn_fwd",
        grid=(GDN_K_HEADS // hk, nc),
        in_specs=tok,
        out_specs=[pl.BlockSpec((CHUNK, hb * GDN_DV), lambda h, n: (n, h)), tsave, ssave],
        out_shape=[jax.ShapeDtypeStruct((seq, GDN_VW), BF16),
                   jax.ShapeDtypeStruct((nc, GDN_V_HEADS, CHUNK, CHUNK), BF16),
                   jax.ShapeDtypeStruct((nc, GDN_V_HEADS, GDN_DK, GDN_DV), BF16)],
        scratch_shapes=[pltpu.VMEM((hb, GDN_DK, GDN_DV), F32)],
        compiler_params=_cparams(("parallel", "arbitrary")),
    )(c, c, c, p, beta_c, gc_c, gc_r, norm_g.reshape(1, GDN_DV))


def _gdn_bwd(c, p, beta_c, gc_c, gc_r, norm_g, tsave, ssave, do, comm=None):
    seq = c.shape[0]
    nc = seq // CHUNK
    hk, hb = GDN_HK, 2 * GDN_HK
    nhb = GDN_K_HEADS // hk
    tok, tsave_spec, ssave_spec, cidx = _gdn_specs(nc, True)

    def body(q_ref, k_ref, v_ref, z_ref, beta_ref, gcc_ref, gcr_ref, ng_ref, t_ref, s_ref, do_ref,
             dq_ref, dk_ref, dv_ref, dz_ref, dbeta_ref, dgcc_ref, dgcr_ref, dng_ref, ds_scr):
        @pl.when(pl.program_id(1) == 0)
        def _():
            ds_scr[...] = jnp.zeros_like(ds_scr)
            dng_ref[...] = jnp.zeros_like(dng_ref)

        qr, kr, v, z = _gdn_load(q_ref, k_ref, v_ref, z_ref)
        beta, gcc, gcr = _head_cols(beta_ref[...]), _head_cols(gcc_ref[...]), gcr_ref[0]
        do = do_ref[...]
        dos = jnp.stack([do[:, i * GDN_DV:(i + 1) * GDN_DV] for i in range(hb)]).astype(F32)
        _, vjp = jax.vjp(_gdn_chunk, qr, kr, v, z, beta, gcc, gcr, ng_ref[...], t_ref[0].astype(F32),
                         s_ref[0].astype(F32))
        dqr, dkr, dv, dz, dbeta, dgcc, dgcr, dng, _, ds = vjp((dos, ds_scr[...]))
        ds_scr[...] = ds
        dng_ref[...] += dng[None]
        dq_ref[...] = jnp.concatenate([dqr[i] for i in range(hk)], axis=-1).astype(dq_ref.dtype)
        dk_ref[...] = jnp.concatenate([dkr[i] for i in range(hk)], axis=-1).astype(dk_ref.dtype)
        dv_ref[...] = jnp.concatenate([dv[i] for i in range(hb)], axis=-1).astype(dv_ref.dtype)
        dz_ref[...] = jnp.concatenate([dz[i] for i in range(hb)], axis=-1).astype(dz_ref.dtype)
        dbeta_ref[...] = _cols_block(dbeta)
        dgcc_ref[...] = _cols_block(dgcc)
        dgcr_ref[0] = dgcr

    qw, vw = hk * GDN_DK, hb * GDN_DV
    col = pl.BlockSpec((None, CHUNK, hb), lambda h, n: (h, cidx(n), 0))
    row = pl.BlockSpec((1, hb, 1, CHUNK), lambda h, n: (cidx(n), h, 0, 0))
    return _grid_call(
        body, name="gdn_bwd", comm=comm,
        grid=(nhb, nc),
        in_specs=tok + [tsave_spec, ssave_spec, pl.BlockSpec((CHUNK, vw), lambda h, n: (cidx(n), h))],
        out_specs=[pl.BlockSpec((CHUNK, qw), lambda h, n: (cidx(n), h)),
                   pl.BlockSpec((CHUNK, qw), lambda h, n: (cidx(n), h)),
                   pl.BlockSpec((CHUNK, vw), lambda h, n: (cidx(n), h)),
                   pl.BlockSpec((CHUNK, vw), lambda h, n: (cidx(n), h)),
                   col, col, row,
                   pl.BlockSpec((1, 1, GDN_DV), lambda h, n: (h, 0, 0))],
        out_shape=[jax.ShapeDtypeStruct((seq, GDN_QK), F32), jax.ShapeDtypeStruct((seq, GDN_QK), F32),
                   jax.ShapeDtypeStruct((seq, GDN_VW), F32), jax.ShapeDtypeStruct((seq, GDN_VW), BF16),
                   jax.ShapeDtypeStruct((nhb, seq, hb), F32),
                   jax.ShapeDtypeStruct((nhb, seq, hb), F32),
                   jax.ShapeDtypeStruct((nc, GDN_V_HEADS, 1, CHUNK), F32),
                   jax.ShapeDtypeStruct((nhb, 1, GDN_DV), F32)],
        scratch_shapes=[pltpu.VMEM((hb, GDN_DK, GDN_DV), F32)],
        args=(c, c, c, p, beta_c, gc_c, gc_r, norm_g.reshape(1, GDN_DV), tsave, ssave, do))


CONV_TB = 512
CONV_CB = 1024
HALO = 8


def _conv_taps(ext, w):
    acc = w[GDN_CONV - 1:GDN_CONV] * ext
    for j in range(GDN_CONV - 1):
        acc = acc + w[j:j + 1] * pltpu.roll(ext, GDN_CONV - 1 - j, 0)
    return acc


def _conv_fwd(p, w):
    seq = p.shape[0]
    tb, cb = min(CONV_TB, seq), CONV_CB

    def body(prev_ref, cur_ref, w_ref, o_ref):
        first = pl.program_id(1) == 0
        prev = jnp.where(first, 0.0, prev_ref[...])
        ext = jnp.concatenate([prev, cur_ref[...]], axis=0)
        o_ref[...] = _silu(_conv_taps(ext, w_ref[...])[HALO:])

    return pl.pallas_call(
        body, name="conv_fwd",
        grid=(GDN_QKV // cb, seq // tb),
        in_specs=[pl.BlockSpec((HALO, cb), lambda j, i: (jnp.maximum(i * (tb // HALO) - 1, 0), j)),
                  pl.BlockSpec((tb, cb), lambda j, i: (i, j)),
                  pl.BlockSpec((GDN_CONV, cb), lambda j, i: (0, j))],
        out_specs=pl.BlockSpec((tb, cb), lambda j, i: (i, j)),
        out_shape=jax.ShapeDtypeStruct((seq, GDN_QKV), F32),
        compiler_params=_cparams(("parallel", "arbitrary")),
    )(p, p, w)


def _conv_bwd(p, dq, dk, dv, w):
    seq = p.shape[0]
    tb, cb = min(CONV_TB, seq), CONV_CB
    nt = seq // tb
    last_halo = seq // HALO - 1
    first_col = (0, GDN_QK // cb, 2 * GDN_QK // cb, GDN_QKV // cb)

    def body(prev_ref, cur_ref, next_ref, dq_ref, dqn_ref, dk_ref, dkn_ref, dv_ref, dvn_ref, w_ref, du_ref, dw_ref):
        j, i = pl.program_id(0), pl.program_id(1)

        @pl.when(i == 0)
        def _():
            dw_ref[...] = jnp.zeros_like(dw_ref)

        def pick(q_ref, k_ref, v_ref):
            return jnp.where(j < first_col[1], q_ref[...], jnp.where(j < first_col[2], k_ref[...], v_ref[...]))

        w = w_ref[...]
        prev = jnp.where(i == 0, 0.0, prev_ref[...])
        ext = jnp.concatenate([prev, cur_ref[...], next_ref[...]], axis=0)
        pre = _conv_taps(ext, w)
        dnext = jnp.where(i == nt - 1, 0.0, pick(dqn_ref, dkn_ref, dvn_ref))
        dext = jnp.concatenate([jnp.zeros((HALO, cb), F32), pick(dq_ref, dk_ref, dv_ref), dnext], axis=0)
        sig = jax.nn.sigmoid(pre)
        dpre = dext * (sig * (1.0 + pre * (1.0 - sig)))
        rows = tb + 2 * HALO
        du = w[GDN_CONV - 1:GDN_CONV] * dpre
        for j in range(GDN_CONV - 1):
            du = du + w[j:j + 1] * pltpu.roll(dpre, rows - (GDN_CONV - 1 - j), 0)
        du_ref[...] = du[HALO:HALO + tb].astype(du_ref.dtype)
        dcore = dpre[HALO:HALO + tb]
        dws = []
        for j in range(GDN_CONV):
            sh = ext if j == GDN_CONV - 1 else pltpu.roll(ext, GDN_CONV - 1 - j, 0)
            dws.append(jnp.sum(dcore * sh[HALO:HALO + tb], axis=0, keepdims=True))
        dw_ref[...] += jnp.concatenate(dws, axis=0)

    hb = tb // HALO

    def part(k):
        lo, hi = first_col[k], first_col[k + 1]
        here = lambda j: (j >= lo) & (j < hi)
        col = lambda j: jnp.clip(j - lo, 0, hi - lo - 1)
        return [pl.BlockSpec((tb, cb), lambda j, i: (jnp.where(here(j), i, 0), col(j))),
                pl.BlockSpec((HALO, cb),
                             lambda j, i: (jnp.where(here(j), jnp.minimum((i + 1) * hb, last_halo), 0), col(j)))]

    return pl.pallas_call(
        body, name="conv_bwd",
        grid=(GDN_QKV // cb, nt),
        in_specs=[pl.BlockSpec((HALO, cb), lambda j, i: (jnp.maximum(i * hb - 1, 0), j)),
                  pl.BlockSpec((tb, cb), lambda j, i: (i, j)),
                  pl.BlockSpec((HALO, cb), lambda j, i: (jnp.minimum((i + 1) * hb, last_halo), j))]
        + part(0) + part(1) + part(2) + [pl.BlockSpec((GDN_CONV, cb), lambda j, i: (0, j))],
        out_specs=[pl.BlockSpec((tb, cb), lambda j, i: (i, j)),
                   pl.BlockSpec((GDN_CONV, cb), lambda j, i: (0, j))],
        out_shape=[jax.ShapeDtypeStruct((seq, GDN_QKV), BF16), jax.ShapeDtypeStruct((GDN_CONV, GDN_QKV), F32)],
        compiler_params=_cparams(("parallel", "arbitrary")),
    )(p, p, p, dq, dq, dk, dk, dv, dv, w)


GATE_TB = 512


def _split3(g):
    hi = g.astype(BF16)
    r = g - hi.astype(F32)
    mid = r.astype(BF16)
    lo = (r - mid.astype(F32)).astype(BF16)
    return hi, mid, lo


def _tri_chunks(n, upper):
    i, j = _iota2((n, n), 0), _iota2((n, n), 1)
    tri = (i <= j) if upper else (i >= j)
    return jnp.where(tri & ((i // CHUNK) == (j // CHUNK)), 1.0, 0.0).astype(BF16)


def _tri_apply(g, upper):
    tri = _tri_chunks(g.shape[0], upper)
    return sum(jnp.dot(tri, part, preferred_element_type=F32) for part in _split3(g))


@jax.custom_vjp
def _chunk_cumsum(g):
    return _tri_apply(g, False)


_chunk_cumsum.defvjp(lambda g: (_tri_apply(g, False), None), lambda _, d: (_tri_apply(d, True),))


def _gates(b, a, a_log, dt_bias):
    z = a + dt_bias
    softplus = jnp.maximum(z, 0.0) + jnp.log1p(jnp.exp(-jnp.abs(z)))
    g = -jnp.exp(a_log) * softplus
    return jax.nn.sigmoid(b), _chunk_cumsum(g)


def _gates_fwd(b, a, a_log, dt_bias):
    seq, nh = b.shape
    tb = min(GATE_TB, seq)

    def body(b_ref, a_ref, al_ref, dt_ref, beta_ref, gc_ref):
        beta, gc = _gates(b_ref[...], a_ref[...], al_ref[...], dt_ref[...])
        beta_ref[...] = beta
        gc_ref[...] = gc

    tok = pl.BlockSpec((tb, nh), lambda i: (i, 0))
    vec = pl.BlockSpec((1, nh), lambda i: (0, 0))
    return pl.pallas_call(
        body, name="gates_fwd", grid=(seq // tb,),
        in_specs=[tok, tok, vec, vec], out_specs=[tok, tok],
        out_shape=[jax.ShapeDtypeStruct((seq, nh), F32)] * 2,
        compiler_params=_cparams(("parallel",)),
    )(b, a, a_log, dt_bias)


def _gates_bwd(b, a, a_log, dt_bias, dbeta, dgc):
    seq, nh = b.shape
    tb = min(GATE_TB, seq)

    def body(b_ref, a_ref, al_ref, dt_ref, dbeta_ref, dgc_ref, db_ref, da_ref, dal_ref, ddt_ref):
        @pl.when(pl.program_id(0) == 0)
        def _():
            dal_ref[...] = jnp.zeros_like(dal_ref)
            ddt_ref[...] = jnp.zeros_like(ddt_ref)

        _, vjp = jax.vjp(_gates, b_ref[...], a_ref[...], al_ref[...], dt_ref[...])
        db, da, dal, ddt = vjp((dbeta_ref[...], dgc_ref[...]))
        db_ref[...] = db
        da_ref[...] = da
        dal_ref[...] += dal
        ddt_ref[...] += ddt

    tok = pl.BlockSpec((tb, nh), lambda i: (i, 0))
    vec = pl.BlockSpec((1, nh), lambda i: (0, 0))
    return pl.pallas_call(
        body, name="gates_bwd", grid=(seq // tb,),
        in_specs=[tok, tok, vec, vec, tok, tok], out_specs=[tok, tok, vec, vec],
        out_shape=[jax.ShapeDtypeStruct((seq, nh), F32)] * 2 + [jax.ShapeDtypeStruct((1, nh), F32)] * 2,
        compiler_params=_cparams(("arbitrary",)),
    )(b, a, a_log, dt_bias, dbeta, dgc)


LN_TR = 256


def _ln_stats(x, s):
    z = DN_ALPHA * x + s
    mu = jnp.mean(z, -1, keepdims=True)
    zc = z - mu
    var = jnp.mean(zc * zc, -1, keepdims=True)
    rstd = lax.rsqrt(var + LN_EPS)
    return zc * rstd, rstd


def _ln_fwd(x, s, g, b):
    seq, d = x.shape
    tr = min(LN_TR, seq)

    def body(x_ref, s_ref, g_ref, b_ref, o_ref, ob_ref):
        xhat, _ = _ln_stats(x_ref[...], s_ref[...])
        y = xhat * g_ref[...] + b_ref[...]
        o_ref[...] = y
        ob_ref[...] = y.astype(BF16)

    tok = pl.BlockSpec((tr, d), lambda i: (i, 0))
    vec = pl.BlockSpec((1, d), lambda i: (0, 0))
    return pl.pallas_call(
        body, name="ln_fwd", grid=(seq // tr,),
        in_specs=[tok, tok, vec, vec], out_specs=[tok, tok],
        out_shape=[jax.ShapeDtypeStruct((seq, d), F32), jax.ShapeDtypeStruct((seq, d), BF16)],
        compiler_params=_cparams(("parallel",)),
    )(x, s, g.reshape(1, d), b.reshape(1, d))


def _ln_bwd(dy, x, s, g):
    seq, d = x.shape
    tr = min(LN_TR, seq)

    def body(dy_ref, x_ref, s_ref, g_ref, dz_ref, dzb_ref, dg_ref, db_ref):
        @pl.when(pl.program_id(0) == 0)
        def _():
            dg_ref[...] = jnp.zeros_like(dg_ref)
            db_ref[...] = jnp.zeros_like(db_ref)

        dy = dy_ref[...]
        xhat, rstd = _ln_stats(x_ref[...], s_ref[...])
        dyg = dy * g_ref[...]
        m1 = jnp.mean(dyg, -1, keepdims=True)
        m2 = jnp.mean(dyg * xhat, -1, keepdims=True)
        dz = rstd * (dyg - m1 - xhat * m2)
        dz_ref[...] = dz
        dzb_ref[...] = dz.astype(BF16)
        dg_ref[...] += jnp.sum(dy * xhat, axis=0, keepdims=True)
        db_ref[...] += jnp.sum(dy, axis=0, keepdims=True)

    tok = pl.BlockSpec((tr, d), lambda i: (i, 0))
    vec = pl.BlockSpec((1, d), lambda i: (0, 0))
    return pl.pallas_call(
        body, name="ln_bwd", grid=(seq // tr,),
        in_specs=[tok, tok, tok, vec], out_specs=[tok, tok, vec, vec],
        out_shape=[jax.ShapeDtypeStruct((seq, d), F32), jax.ShapeDtypeStruct((seq, d), BF16),
                   jax.ShapeDtypeStruct((1, d), F32), jax.ShapeDtypeStruct((1, d), F32)],
        compiler_params=_cparams(("arbitrary",)),
    )(dy, x, s, g.reshape(1, d))


def _loss_head(y, target):
    seq, d = y.shape
    tr = min(LN_TR, seq)

    def body(y_ref, t_ref, loss_ref, dy_ref):
        @pl.when(pl.program_id(0) == 0)
        def _():
            loss_ref[...] = jnp.zeros_like(loss_ref)

        err = y_ref[...] - t_ref[...]
        dy_ref[...] = err * (1.0 / d)
        part = jnp.sum(jnp.sum(err * err, axis=0, keepdims=True), axis=1, keepdims=True)
        loss_ref[...] += part * (0.5 / d)

    tok = pl.BlockSpec((tr, d), lambda i: (i, 0))
    return pl.pallas_call(
        body, name="loss_head", grid=(seq // tr,),
        in_specs=[tok, tok], out_specs=[pl.BlockSpec((8, 128), lambda i: (0, 0)), tok],
        out_shape=[jax.ShapeDtypeStruct((8, 128), F32), jax.ShapeDtypeStruct((seq, d), F32)],
        compiler_params=_cparams(("arbitrary",)),
    )(y, target)


COMM_MID = 0.8


def _matmul(a, b, *, ta=False, tb=False, b_sharded=False, out_sharded=False, out_dtypes=(F32,), epilogue=None,
            extras=(), tm=1024, tn=1024, tk=2048, name="matmul", comm=None, b_cols=None):
    m, k = (a.shape[1], a.shape[0]) if ta else a.shape
    if b_sharded:
        bk, bn = b.shape[1], N_DEV * b.shape[2]
        shard_w = b.shape[2]
    else:
        bk, bn = b.shape[0], b_cols or b.shape[1]
    n = bk if tb else bn
    assert k == (bn if tb else bk), (a.shape, b.shape)
    tm, tn, tk = min(tm, m), min(tn, n), min(tk, k)
    if b_sharded:
        if tb:
            tk = math.gcd(tk, shard_w)
        else:
            tn = math.gcd(tn, shard_w)
    if out_sharded:
        tn = math.gcd(tn, n // N_DEV)
    assert m % tm == 0 and n % tn == 0 and k % tk == 0, (m, n, k, tm, tn, tk)
    ni, nj, nk = m // tm, n // tn, k // tk
    dims = (((0 if ta else 1,), (1 if tb else 0,)), ((), ()))
    n_ex, n_out = len(extras), len(out_dtypes)
    n_ci = len(comm.ins) if comm else 0
    n_co = len(comm.out_shapes) if comm else 0
    total = ni * nj * nk
    mid_step = min(int(COMM_MID * total), total - 1)

    def body(*refs):
        a_ref, b_ref = refs[0], refs[1]
        ex_refs = refs[2:2 + n_ex]
        ci_refs = refs[2 + n_ex:2 + n_ex + n_ci]
        out_refs = refs[2 + n_ex + n_ci:2 + n_ex + n_ci + n_out]
        co_refs = refs[2 + n_ex + n_ci + n_out:2 + n_ex + n_ci + n_out + n_co]
        scratch = refs[2 + n_ex + n_ci + n_out + n_co:]
        acc, sems = (None, scratch) if nk == 1 else (scratch[0], scratch[1:])
        kk = pl.program_id(2)
        step = (pl.program_id(0) * nj + pl.program_id(1)) * nk + kk

        if comm:
            @pl.when(step == 0)
            def _():
                comm.start(ci_refs, co_refs, *sems)

        prod = lax.dot_general(a_ref[...].astype(BF16), b_ref[...].astype(BF16), dims, preferred_element_type=F32)

        def write(res):
            outs = (res,) if epilogue is None else epilogue(res, *[r[...] for r in ex_refs])
            for o_ref, val in zip(out_refs, outs, strict=True):
                o_ref[...] = val.astype(o_ref.dtype)

        if nk == 1:
            write(prod)
        else:
            @pl.when(kk == 0)
            def _():
                acc[...] = prod

            @pl.when(kk > 0)
            def _():
                acc[...] += prod

            @pl.when(kk == nk - 1)
            def _():
                write(acc[...])

        if comm:
            @pl.when(step == mid_step)
            def _():
                comm.mid(ci_refs, co_refs, *sems)

            @pl.when(step == total - 1)
            def _():
                comm.finish(ci_refs, co_refs, *sems)

    a_spec = pl.BlockSpec((tk, tm), lambda i, j, kk: (kk, i)) if ta else pl.BlockSpec((tm, tk), lambda i, j, kk: (i, kk))
    if b_sharded and tb:
        per = shard_w // tk
        b_spec = pl.BlockSpec((None, tn, tk), lambda i, j, kk: (kk // per, j, kk % per))
    elif b_sharded:
        per = shard_w // tn
        b_spec = pl.BlockSpec((None, tk, tn), lambda i, j, kk: (j // per, kk, j % per))
    elif tb:
        b_spec = pl.BlockSpec((tn, tk), lambda i, j, kk: (j, kk))
    else:
        b_spec = pl.BlockSpec((tk, tn), lambda i, j, kk: (kk, j))
    ex_spec = pl.BlockSpec((tm, tn), lambda i, j, kk: (i, j))
    if out_sharded:
        per_o = n // N_DEV // tn
        o_spec = pl.BlockSpec((None, tm, tn), lambda i, j, kk: (j // per_o, i, j % per_o))
        o_shape = (N_DEV, m, n // N_DEV)
    else:
        o_spec, o_shape = ex_spec, (m, n)
    hbm = pl.BlockSpec(memory_space=pl.ANY)
    outs = pl.pallas_call(
        body, name=name, grid=(ni, nj, nk),
        in_specs=[a_spec, b_spec] + [ex_spec] * n_ex + [hbm] * n_ci,
        out_specs=[o_spec] * n_out + [hbm] * n_co,
        out_shape=[jax.ShapeDtypeStruct(o_shape, dt) for dt in out_dtypes] + (list(comm.out_shapes) if comm else []),
        scratch_shapes=([] if nk == 1 else [pltpu.VMEM((tm, tn), F32)]) + (list(comm.scratch) if comm else []),
        compiler_params=_cparams(("arbitrary",) * 3 if comm else ("parallel", "parallel", "arbitrary")),
    )(a, b, *extras, *(comm.ins if comm else ()))
    return outs[0] if len(outs) == 1 else outs


def _epi_relu2(acc):
    r = jnp.maximum(acc, 0.0)
    return acc, r * r


def _epi_drelu2(acc, pre):
    return (acc * (2.0 * jnp.maximum(pre, 0.0)),)


def _epi_add(scale):
    return lambda acc, other: (acc + scale * other,)


def _adamw(parts, w, m, v, *, rows_per_step, name, layer=None, n_layers=None, into=None):
    n_parts, rows, cols = parts.shape
    tr = min(rows_per_step, rows)
    assert rows % tr == 0

    def body(p_ref, w_ref, m_ref, v_ref, *rest):
        g_ref, d_ref, mo_ref, vo_ref = rest[-4:]
        g = p_ref[0].astype(F32)
        for i in range(1, n_parts):
            g = g + p_ref[i].astype(F32)
        m_new = ADAM_B1 * m_ref[...] + (1.0 - ADAM_B1) * g
        v_new = ADAM_B2 * v_ref[...] + (1.0 - ADAM_B2) * (g * g)
        m_hat = m_new / (1.0 - ADAM_B1 ** ADAM_STEP)
        v_hat = v_new / (1.0 - ADAM_B2 ** ADAM_STEP)
        g_ref[...] = g
        d_ref[...] = -ADAM_LR * (m_hat / (jnp.sqrt(v_hat) + ADAM_EPS) + ADAM_WD * w_ref[...])
        mo_ref[...] = m_new
        vo_ref[...] = v_new

    blk = pl.BlockSpec((tr, cols), lambda i: (i, 0))
    if layer is None:
        out_blk, out_shape = blk, (rows, cols)
    else:
        out_blk, out_shape = pl.BlockSpec((None, tr, cols), lambda i: (layer, i, 0)), (n_layers, rows, cols)
    into = list(into or ())
    return pl.pallas_call(
        body, name=name, grid=(rows // tr,),
        in_specs=[pl.BlockSpec((n_parts, tr, cols), lambda i: (0, i, 0)), blk, blk, blk]
        + [pl.BlockSpec(memory_space=pl.ANY)] * len(into),
        out_specs=[out_blk] * 4,
        out_shape=[jax.ShapeDtypeStruct(out_shape, F32)] * 4,
        input_output_aliases={4 + k: k for k in range(len(into))},
        compiler_params=_cparams(("parallel",)),
    )(parts, w, m, v, *into)


def _position():
    return lax.axis_index("x"), lax.axis_index("y"), lax.axis_index("c")


def _comm_scratch(n):
    return [pltpu.SemaphoreType.DMA((7 * n,)), pltpu.SemaphoreType.DMA((7 * n,)), pltpu.SemaphoreType.DMA((n,))]


class _Gather:
    def __init__(self, blocks):
        self.ins = list(blocks)
        self.out_shapes = [jax.ShapeDtypeStruct((N_DEV,) + b.shape, b.dtype) for b in blocks]
        self.scratch = _comm_scratch(len(blocks))

    def _plan(self, n, ins, outs, send_sems, recv_sems, local_sems):
        x, y, c = _position()
        me, sibling = (x, y, c), (x, y, 1 - c)
        chips = [(1 - x, y), (x, 1 - y), (1 - x, 1 - y)]
        x_ref, out_ref = ins[n], outs[n]

        def slot(px, py, pc):
            return out_ref.at[4 * px + 2 * py + pc]

        def copy(k, blk, to, src=None):
            return pltpu.make_async_remote_copy(
                src_ref=slot(*blk) if src is None else src, dst_ref=slot(*blk),
                send_sem=send_sems.at[7 * n + k], recv_sem=recv_sems.at[7 * n + k], device_id=to, device_id_type=MESH)

        mine = lambda: pltpu.make_async_copy(x_ref, slot(*me), local_sems.at[n])
        first = lambda: [copy(0, me, sibling, src=x_ref)] + [copy(1 + j, me, (*chip, c), src=x_ref)
                                                             for j, chip in enumerate(chips)]
        passed = lambda j: copy(4 + j, (*chips[j], c), sibling)
        landed = lambda j: copy(1 + j, (*chips[j], c), me)
        from_sibling = lambda: [copy(0, sibling, me)] + [copy(4 + j, (*chip, 1 - c), me) for j, chip in enumerate(chips)]
        return mine, first, passed, landed, from_sibling

    def start(self, ins, outs, *sems):
        for n in range(len(self.ins)):
            mine, first, _, _, _ = self._plan(n, ins, outs, *sems)
            mine().start()
            for cp in first():
                cp.start()

    def mid(self, ins, outs, *sems):
        plans = [self._plan(n, ins, outs, *sems) for n in range(len(self.ins))]
        for j in range(3):
            for _, _, passed, landed, _ in plans:
                landed(j).wait_recv()
                passed(j).start()

    def finish(self, ins, outs, *sems):
        for n in range(len(self.ins)):
            mine, first, passed, _, from_sibling = self._plan(n, ins, outs, *sems)
            for cp in from_sibling():
                cp.wait_recv()
            for cp in first() + [passed(j) for j in range(3)]:
                cp.wait_send()
            mine().wait()


class _Exchange:
    def __init__(self, parts):
        self.ins = list(parts)
        self.out_shapes = [jax.ShapeDtypeStruct(p.shape, p.dtype) for p in parts]
        self.scratch = _comm_scratch(len(parts))

    def _plan(self, n, ins, outs, send_sems, recv_sems, local_sems):
        x, y, c = _position()
        me = 4 * x + 2 * y + c
        p_ref, out_ref = ins[n], outs[n]
        mine = lambda: pltpu.make_async_copy(p_ref.at[me], out_ref.at[me], local_sems.at[n])

        def copies(landing):
            out = []
            for k in range(1, N_DEV):
                px = 1 - x if k & 4 else x
                py = 1 - y if k & 2 else y
                pc = 1 - c if k & 1 else c
                peer_slot = 4 * px + 2 * py + pc
                out.append(pltpu.make_async_remote_copy(
                    src_ref=p_ref.at[peer_slot], dst_ref=out_ref.at[peer_slot if landing else me],
                    send_sem=send_sems.at[7 * n + k - 1], recv_sem=recv_sems.at[7 * n + k - 1],
                    device_id=(px, py, pc), device_id_type=MESH))
            return out

        return mine, copies

    def start(self, ins, outs, *sems):
        for n in range(len(self.ins)):
            mine, copies = self._plan(n, ins, outs, *sems)
            mine().start()
            for cp in copies(False):
                cp.start()

    def mid(self, ins, outs, *sems):
        pass

    def finish(self, ins, outs, *sems):
        for n in range(len(self.ins)):
            mine, copies = self._plan(n, ins, outs, *sems)
            for cp in copies(True):
                cp.wait_recv()
            for cp in copies(False):
                cp.wait_send()
            mine().wait()


def _comm_alone(comm, name):
    def body(*refs):
        n_i, n_o = len(comm.ins), len(comm.out_shapes)
        ins, outs, sems = refs[:n_i], refs[n_i:n_i + n_o], refs[n_i + n_o:]
        comm.start(ins, outs, *sems)
        comm.mid(ins, outs, *sems)
        comm.finish(ins, outs, *sems)

    hbm = pl.BlockSpec(memory_space=pl.ANY)
    return pl.pallas_call(
        body, name=name, out_shape=list(comm.out_shapes),
        in_specs=[hbm] * len(comm.ins), out_specs=[hbm] * len(comm.out_shapes),
        scratch_shapes=list(comm.scratch),
    )(*comm.ins)


RET_IN_W = 2 * RET_QK + 2 * RET_VW
GDN_IN_W = GDN_QKV + GDN_VW + 2 * GDN_V_HEADS
GDN_TAIL = 2 * GDN_V_HEADS
TAIL_PAD = 128

SMALL_SIZES = (RET_VW, GDN_V_HEADS, GDN_V_HEADS, GDN_DV, DEPTH * D_MODEL, DEPTH * D_MODEL, DEPTH * D_MODEL,
               DEPTH * D_MODEL, GDN_CONV * GDN_QKV)
SMALL_LANES = 128
SMALL_ROWS = -(-sum(SMALL_SIZES) // (8 * SMALL_LANES)) * 8


def _pack_small(*vecs):
    flat = jnp.concatenate([v.reshape(-1).astype(F32) for v in vecs])
    return jnp.pad(flat, (0, SMALL_ROWS * SMALL_LANES - flat.shape[0])).reshape(SMALL_ROWS, SMALL_LANES)


def _unpack_small(buf, shapes):
    flat, out, at = buf.reshape(-1), [], 0
    for shp in shapes:
        n = int(np.prod(shp))
        out.append(flat[at:at + n].reshape(shp))
        at += n
    return out


def _mlp_bwd(dz, h, a, r, w1, w2, name):
    dz, dzb = dz
    da = _matmul(dzb, w2, tb=True, out_dtypes=(BF16,), epilogue=_epi_drelu2, extras=(a,), name=name + "_da")
    dw2 = _matmul(r, dzb, ta=True, out_dtypes=(BF16,), name=name + "_dw2").reshape(N_DEV, -1, D_MODEL)
    dw1 = _matmul(h, da, ta=True, out_sharded=True, out_dtypes=(BF16,), name=name + "_dw1")
    dh = _matmul(da, w1, tb=True, b_sharded=True, epilogue=_epi_add(DN_ALPHA), extras=(dz,), name=name + "_dh")
    return dh, dw1, dw2


def _head_blocks(t):
    seq, nh = t.shape
    hb = 2 * GDN_HK
    return t.reshape(seq, nh // hb, hb).transpose(1, 0, 2)


def _from_head_blocks(t):
    return t.transpose(1, 0, 2).reshape(t.shape[1], -1)


def _chunk_rows(t):
    seq, nh = t.shape
    return t.reshape(seq // CHUNK, CHUNK, nh).transpose(0, 2, 1)[:, :, None, :]


def _from_chunk_rows(t):
    nc, nh = t.shape[:2]
    return t[:, :, 0, :].transpose(0, 2, 1).reshape(nc * CHUNK, nh)


def kernel(x, ret_w_in, ret_gn_g, ret_w_out, gdn_w_in, gdn_conv_w, gdn_a_log, gdn_dt_bias, gdn_norm_g, gdn_w_out, ln_mix_g, ln_mix_b, mlp_w1, mlp_w2, ln_ffn_g, ln_ffn_b, loss_target, m_ret_w_in, m_ret_gn_g, m_ret_w_out, m_gdn_w_in, m_gdn_conv_w, m_gdn_a_log, m_gdn_dt_bias, m_gdn_norm_g, m_gdn_w_out, m_ln_mix_g, m_ln_mix_b, m_mlp_w1, m_mlp_w2, m_ln_ffn_g, m_ln_ffn_b, v_ret_w_in, v_ret_gn_g, v_ret_w_out, v_gdn_w_in, v_gdn_conv_w, v_gdn_a_log, v_gdn_dt_bias, v_gdn_norm_g, v_gdn_w_out, v_ln_mix_g, v_ln_mix_b, v_mlp_w1, v_mlp_w2, v_ln_ffn_g, v_ln_ffn_b):
    xt, target = x[0], loss_target[0]
    seq = xt.shape[0]
    me = 4 * lax.axis_index("x") + 2 * lax.axis_index("y") + lax.axis_index("c")

    bf = lambda t: t.astype(BF16)
    cos, sin = _rope_tables(seq)
    w_ret_in, = _comm_alone(_Gather([bf(ret_w_in[0])]), "gather_ret_in")
    conv_blk = jnp.pad(gdn_conv_w[0], ((0, HALO - GDN_CONV), (0, 0)))
    shard_in = RET_IN_W // N_DEV
    xb = bf(xt)
    p0, w_ret_out, w1_0, conv_all = _matmul(
        xb, w_ret_in, b_sharded=True, tn=shard_in, name="ret_in",
        comm=_Gather([bf(ret_w_out[0]), bf(mlp_w1[0]), conv_blk]))
    w_ret_out = w_ret_out.reshape(RET_VW, D_MODEL)
    conv_w = conv_all[:, :GDN_CONV].transpose(1, 0, 2).reshape(GDN_CONV, GDN_QKV)
    o0, s0, w2_0, gdn_in_all = _ret_fwd(p0, cos, sin, ret_gn_g[0], comm=_Gather([bf(mlp_w2[0]), bf(gdn_w_in[0])]))
    w2_0 = w2_0.reshape(D_FF, D_MODEL)
    mix0 = _matmul(o0, w_ret_out, name="ret_out")
    h1, h1b = _ln_fwd(xt, mix0, ln_mix_g[0], ln_mix_b[0])
    a0, r0, w_gdn_out, w1_1 = _matmul(h1b, w1_0, b_sharded=True, out_dtypes=(F32, BF16), epilogue=_epi_relu2,
                                      name="mlp0_up", comm=_Gather([bf(gdn_w_out[0]), bf(mlp_w1[1])]))
    w_gdn_out = w_gdn_out.reshape(GDN_VW, D_MODEL)
    m0, w2_1 = _matmul(r0, w2_0, name="mlp0_down", comm=_Gather([bf(mlp_w2[1])]))
    w2_1 = w2_1.reshape(D_FF, D_MODEL)
    h2, h2b = _ln_fwd(h1, m0, ln_ffn_g[0], ln_ffn_b[0])

    w_gdn_in = gdn_in_all.transpose(1, 0, 2).reshape(D_MODEL, GDN_IN_W)
    main_w = GDN_IN_W - GDN_TAIL
    w_gdn_tail = jnp.pad(w_gdn_in[:, main_w:], ((0, 0), (0, TAIL_PAD - GDN_TAIL)))
    p1 = _matmul(h2b, w_gdn_in, b_cols=main_w, name="gdn_in")
    pt = _matmul(h2b, w_gdn_tail, name="gdn_in_tail")
    c1 = _conv_fwd(p1, conv_w)
    b_in, a_in = pt[:, :GDN_V_HEADS], pt[:, GDN_V_HEADS:GDN_TAIL]
    beta, gc = _gates_fwd(b_in, a_in, gdn_a_log, gdn_dt_bias)
    beta_c, gc_c, gc_r = _head_blocks(beta), _head_blocks(gc), _chunk_rows(gc)
    o1, t1, s1 = _gdn_fwd(c1, p1, beta_c, gc_c, gc_r, gdn_norm_g[0])
    mix1 = _matmul(o1, w_gdn_out, name="gdn_out")
    h3, h3b = _ln_fwd(h2, mix1, ln_mix_g[1], ln_mix_b[1])
    a1, r1 = _matmul(h3b, w1_1, b_sharded=True, out_dtypes=(F32, BF16), epilogue=_epi_relu2, name="mlp1_up")
    m1 = _matmul(r1, w2_1, name="mlp1_down")
    h4, _ = _ln_fwd(h3, m1, ln_ffn_g[1], ln_ffn_b[1])
    loss_blk, dh4 = _loss_head(h4, target)
    loss = lax.psum(loss_blk[0, 0], ("x", "y", "c"))

    dz, dzb, dg_ffn1, db_ffn1 = _ln_bwd(dh4, h3, m1, ln_ffn_g[1])
    dh3, dw1_1, dw2_1 = _mlp_bwd((dz, dzb), h3b, a1, r1, w1_1, w2_1, "mlp1")
    dz, dzb, dg_mix1, db_mix1 = _ln_bwd(dh3, h2, mix1, ln_mix_g[1])
    do1 = _matmul(dzb, w_gdn_out, tb=True, name="gdn_out_do")
    dw_gdn_out = _matmul(o1, dzb, ta=True, out_dtypes=(BF16,), name="gdn_out_dw").reshape(N_DEV, -1, D_MODEL)
    dq, dk, dv, dzg, dbeta_c, dgc_c, dgc_r, dng, x_w1_1, x_w2_1, x_gdn_out = _gdn_bwd(
        c1, p1, beta_c, gc_c, gc_r, gdn_norm_g[0], t1, s1, do1, comm=_Exchange([dw1_1, dw2_1, dw_gdn_out]))
    du, dconv = _conv_bwd(p1, dq, dk, dv, conv_w)
    db_in, da_in, dalog, ddt = _gates_bwd(b_in, a_in, gdn_a_log, gdn_dt_bias, _from_head_blocks(dbeta_c),
                                          _from_head_blocks(dgc_c) + _from_chunk_rows(dgc_r))
    dpt = jnp.concatenate([db_in, da_in, jnp.zeros((seq, TAIL_PAD - GDN_TAIL), F32)], axis=-1)
    dp1 = jnp.concatenate([du, dzg], axis=-1)
    dw_gdn_main = _matmul(h2b, dp1, ta=True, out_dtypes=(BF16,), name="gdn_in_dw")
    dw_gdn_tail = _matmul(h2b, dpt, ta=True, out_dtypes=(BF16,), name="gdn_in_tail_dw")
    dw_gdn_in = jnp.concatenate([dw_gdn_main, dw_gdn_tail[:, :GDN_TAIL]], axis=-1)
    dw_gdn_in = dw_gdn_in.reshape(D_MODEL, N_DEV, GDN_IN_W // N_DEV).transpose(1, 0, 2)
    dh2 = _matmul(dpt, w_gdn_tail, tb=True, epilogue=_epi_add(DN_ALPHA), extras=(dz,), name="gdn_in_tail_dh")
    dh2, x_gdn_in = _matmul(dp1, w_gdn_in, tb=True, b_cols=main_w, epilogue=_epi_add(1.0), extras=(dh2,),
                            name="gdn_in_dh", comm=_Exchange([dw_gdn_in]))

    dz, dzb, dg_ffn0, db_ffn0 = _ln_bwd(dh2, h1, m0, ln_ffn_g[0])
    dh1, dw1_0, dw2_0 = _mlp_bwd((dz, dzb), h1b, a0, r0, w1_0, w2_0, "mlp0")
    dz, dzb, dg_mix0, db_mix0 = _ln_bwd(dh1, xt, mix0, ln_mix_g[0])
    do0 = _matmul(dzb, w_ret_out, tb=True, name="ret_out_do")
    dw_ret_out = _matmul(o0, dzb, ta=True, out_dtypes=(BF16,), name="ret_out_dw").reshape(N_DEV, -1, D_MODEL)
    dq, dk, dv, dgate, dgng, x_w1_0, x_w2_0 = _ret_bwd(p0, cos, sin, ret_gn_g[0], s0, do0,
                                                      comm=_Exchange([dw1_0, dw2_0]))
    dp0 = jnp.concatenate([dq, dk, dv, dgate], axis=-1)
    dw_ret_in, x_ret_out = _matmul(xb, dp0, ta=True, out_sharded=True, out_dtypes=(BF16,), tn=shard_in,
                                   name="ret_in_dw", comm=_Exchange([dw_ret_out]))
    dx, x_ret_in = _matmul(dp0, w_ret_in, tb=True, b_sharded=True, epilogue=_epi_add(DN_ALPHA), extras=(dz,),
                           tk=shard_in, name="ret_in_dx", comm=_Exchange([dw_ret_in]))

    def update(parts, w, m, v, name, **slab):
        shape = parts.shape[1:]
        outs = _adamw(parts, w.reshape(shape), m.reshape(shape), v.reshape(shape), rows_per_step=128, name=name,
                      **slab)
        return outs if slab else [t.reshape(w.shape) for t in outs]

    u_w1 = update(x_w1_1, mlp_w1[1], m_mlp_w1[1], v_mlp_w1[1], "adamw_w1_1", layer=1, n_layers=DEPTH)
    u_w1 = update(x_w1_0, mlp_w1[0], m_mlp_w1[0], v_mlp_w1[0], "adamw_w1_0", layer=0, n_layers=DEPTH, into=u_w1)
    u_w2 = update(x_w2_1, mlp_w2[1], m_mlp_w2[1], v_mlp_w2[1], "adamw_w2_1", layer=1, n_layers=DEPTH)
    u_w2 = update(x_w2_0, mlp_w2[0], m_mlp_w2[0], v_mlp_w2[0], "adamw_w2_0", layer=0, n_layers=DEPTH, into=u_w2)
    big_out = list(zip(
        update(x_ret_in, ret_w_in, m_ret_w_in, v_ret_w_in, "adamw_ret_in"),
        update(x_ret_out, ret_w_out, m_ret_w_out, v_ret_w_out, "adamw_ret_out"),
        update(x_gdn_in, gdn_w_in, m_gdn_w_in, v_gdn_w_in, "adamw_gdn_in"),
        update(x_gdn_out, gdn_w_out, m_gdn_w_out, v_gdn_w_out, "adamw_gdn_out"),
        u_w1, u_w2))

    small_w = (ret_gn_g, gdn_a_log, gdn_dt_bias, gdn_norm_g, ln_mix_g, ln_mix_b, ln_ffn_g, ln_ffn_b)
    small_m = (m_ret_gn_g, m_gdn_a_log, m_gdn_dt_bias, m_gdn_norm_g, m_ln_mix_g, m_ln_mix_b, m_ln_ffn_g, m_ln_ffn_b)
    small_v = (v_ret_gn_g, v_gdn_a_log, v_gdn_dt_bias, v_gdn_norm_g, v_ln_mix_g, v_ln_mix_b, v_ln_ffn_g, v_ln_ffn_b)
    small_g = (dgng, dalog, ddt, jnp.sum(dng, axis=0),
               jnp.concatenate([dg_mix0, dg_mix1]), jnp.concatenate([db_mix0, db_mix1]),
               jnp.concatenate([dg_ffn0, dg_ffn1]), jnp.concatenate([db_ffn0, db_ffn1]), dconv)
    small_parts, = _comm_alone(_Gather([_pack_small(*small_g)]), "gather_small_grads")
    zero_conv = jnp.zeros((GDN_CONV, GDN_QKV), F32)
    small_out = _adamw(small_parts, _pack_small(*small_w, zero_conv), _pack_small(*small_m, zero_conv),
                       _pack_small(*small_v, zero_conv), rows_per_step=SMALL_ROWS, name="adamw_small")
    shapes = [t.shape for t in small_w] + [(GDN_CONV, GDN_QKV)]
    small_out = [_unpack_small(t, shapes) for t in small_out]
    conv_g = lax.dynamic_slice(small_out[0][-1], (0, me * (GDN_QKV // N_DEV)), (GDN_CONV, GDN_QKV // N_DEV))
    conv_out = _adamw(conv_g[None], gdn_conv_w[0], m_gdn_conv_w[0], v_gdn_conv_w[0],
                      rows_per_step=GDN_CONV, name="adamw_conv")

    def ordered(kind):
        b, s, cv = big_out[kind], small_out[kind], conv_out[kind][None]
        return [b[0], s[0], b[1], b[2], cv, s[1], s[2], s[3], b[3], s[4], s[5], b[4], b[5], s[6], s[7]]

    return (loss, dx[None], *ordered(0), *ordered(1), *ordered(2), *ordered(3))
```

```python
import functools
import math

import jax
import jax.numpy as jnp
import numpy as np
from jax import lax
from jax.experimental import pallas as pl
from jax.experimental.pallas import tpu as pltpu

F32 = jnp.float32
BF16 = jnp.bfloat16

N_DEV = 8
D_MODEL = 2048
CHUNK = 64
RET_HEADS = 8
RET_DK = 256
RET_DV = 512
RET_QK = RET_HEADS * RET_DK
RET_VW = RET_HEADS * RET_DV
ROPE_BASE = 10000.0
GN_EPS = 1e-6
GDN_K_HEADS = 16
GDN_V_HEADS = 32
GDN_DK = 128
GDN_DV = 128
GDN_QK = GDN_K_HEADS * GDN_DK
GDN_VW = GDN_V_HEADS * GDN_DV
GDN_QKV = 2 * GDN_QK + GDN_VW
GDN_CONV = 4
RMS_EPS = 1e-6
L2_EPS = 1e-6
D_FF = 4 * D_MODEL
DEPTH = 2
DN_ALPHA = (2.0 * DEPTH) ** 0.25
LN_EPS = 1e-5
ADAM_LR = 0.001
ADAM_B1 = 0.9
ADAM_B2 = 0.999
ADAM_EPS = 1e-08
ADAM_WD = 0.01
ADAM_STEP = 10

VMEM_LIMIT = 56 * 1024 * 1024
MESH = pl.DeviceIdType.MESH


def _cparams(sem=None):
    return pltpu.CompilerParams(dimension_semantics=sem, vmem_limit_bytes=VMEM_LIMIT)


_NT = (((2,), (2,)), ((0,), (0,)))
_NN = (((2,), (1,)), ((0,), (0,)))
_TN = (((1,), (1,)), ((0,), (0,)))


def _dg(a, b, dims):
    return lax.dot_general(a.astype(BF16), b.astype(BF16), dims, preferred_element_type=F32)


@jax.custom_vjp
def _nt(a, b):
    return _dg(a, b, _NT)


@jax.custom_vjp
def _nn(a, b):
    return _dg(a, b, _NN)


@jax.custom_vjp
def _tn(a, b):
    return _dg(a, b, _TN)


_nt.defvjp(lambda a, b: (_dg(a, b, _NT), (a, b)), lambda r, g: (_nn(g, r[1]), _tn(g, r[0])))
_nn.defvjp(lambda a, b: (_dg(a, b, _NN), (a, b)), lambda r, g: (_nt(g, r[1]), _tn(r[0], g)))
_tn.defvjp(lambda a, b: (_dg(a, b, _TN), (a, b)), lambda r, g: (_nt(r[1], g), _nn(r[0], g)))


def _iota2(shape, dim):
    return lax.broadcasted_iota(jnp.int32, shape, dim)


def _inv_unit_lower(a):
    c = a.shape[-1]
    eye = (_iota2((c, c), 0) == _iota2((c, c), 1)).astype(F32)
    m = -a
    p = eye + m
    for _ in range(int(math.log2(c)) - 1):
        m = _dg(m, m, _NN)
        p = p + _dg(p, m, _NN)
    return p


def _silu(x):
    return x * jax.nn.sigmoid(x)


def _rep2(t):
    h = t.shape[0]
    return jnp.broadcast_to(t[:, None], (h, 2) + t.shape[1:]).reshape((2 * h,) + t.shape[1:])


def _ret_chunk(q1, q2, k1, k2, v, gate, gn_g, s, cos, sin, intra, qdec, kdec, cdec):
    q = jnp.concatenate([q1 * cos - q2 * sin, q1 * sin + q2 * cos], axis=-1)
    k = jnp.concatenate([k1 * cos - k2 * sin, k1 * sin + k2 * cos], axis=-1) * (RET_DK ** -0.5)
    scores = _nt(q, k) * intra
    y = _nn(scores, v) + _nn(q * qdec, s)
    s_new = s * cdec + _tn(k * kdec, v)
    mu = jnp.mean(y, -1, keepdims=True)
    yc = y - mu
    var = jnp.mean(yc * yc, -1, keepdims=True)
    o = _silu(gate) * (yc * lax.rsqrt(var + GN_EPS) * gn_g)
    return o, s_new


def _ret_consts():
    log_gamma = np.log1p(-np.exp2(-5.0 - np.arange(RET_HEADS, dtype=np.float64)))
    idx = np.arange(CHUNK, dtype=np.float64)
    lg = log_gamma[:, None]
    intra = np.exp(lg[..., None] * np.abs(idx[:, None] - idx[None, :]))
    qdec = np.exp(lg * (idx + 1.0))[..., None]
    kdec = np.exp(lg * (CHUNK - 1.0 - idx))[..., None]
    cdec = np.exp(log_gamma * CHUNK)[:, None, None]
    return [jnp.asarray(t, F32) for t in (intra, qdec, kdec, cdec)]


def _rope_tables(seq):
    half = RET_DK // 2
    inv = ROPE_BASE ** (-jnp.arange(half, dtype=F32) / half)
    ang = jnp.arange(seq).astype(F32)[:, None] * inv[None, :]
    return jnp.cos(ang), jnp.sin(ang)


RET_HB = 8


def _ret_load(q_ref, k_ref, v_ref, gate_ref):
    hb, dk, dv, h = RET_HB, RET_DK, RET_DV, RET_DK // 2
    q, k, v, gate = q_ref[...], k_ref[...], v_ref[...], gate_ref[...]
    q1 = jnp.stack([q[:, i * dk:i * dk + h] for i in range(hb)])
    q2 = jnp.stack([q[:, i * dk + h:(i + 1) * dk] for i in range(hb)])
    k1 = jnp.stack([k[:, i * dk:i * dk + h] for i in range(hb)])
    k2 = jnp.stack([k[:, i * dk + h:(i + 1) * dk] for i in range(hb)])
    vs = jnp.stack([v[:, i * dv:(i + 1) * dv] for i in range(hb)])
    gs = jnp.stack([gate[:, i * dv:(i + 1) * dv] for i in range(hb)])
    return q1, q2, k1, k2, vs, gs


def _ret_specs(n_chunks, rev):
    hb = RET_HB
    cidx = (lambda n: n_chunks - 1 - n) if rev else (lambda n: n)
    qw, vw = hb * RET_DK, hb * RET_DV
    tok = [
        pl.BlockSpec((CHUNK, qw), lambda h, n: (cidx(n), h)),
        pl.BlockSpec((CHUNK, qw), lambda h, n: (cidx(n), RET_QK // qw + h)),
        pl.BlockSpec((CHUNK, vw), lambda h, n: (cidx(n), 2 * RET_QK // vw + h)),
        pl.BlockSpec((CHUNK, vw), lambda h, n: (cidx(n), (2 * RET_QK + RET_VW) // vw + h)),
        pl.BlockSpec((CHUNK, RET_DK // 2), lambda h, n: (cidx(n), 0)),
        pl.BlockSpec((CHUNK, RET_DK // 2), lambda h, n: (cidx(n), 0)),
    ]
    const = [
        pl.BlockSpec((hb, CHUNK, CHUNK), lambda h, n: (h, 0, 0)),
        pl.BlockSpec((hb, CHUNK, 1), lambda h, n: (h, 0, 0)),
        pl.BlockSpec((hb, CHUNK, 1), lambda h, n: (h, 0, 0)),
        pl.BlockSpec((hb, 1, 1), lambda h, n: (h, 0, 0)),
        pl.BlockSpec((hb, 1, RET_DV), lambda h, n: (h, 0, 0)),
    ]
    state = pl.BlockSpec((1, hb, RET_DK, RET_DV), lambda h, n: (cidx(n), h, 0, 0))
    return tok, const, state, cidx


def _grid_call(body, *, name, grid, in_specs, out_specs, out_shape, scratch_shapes, args, comm=None):
    if comm is None:
        return pl.pallas_call(body, name=name, grid=grid, in_specs=in_specs, out_specs=out_specs, out_shape=out_shape,
                              scratch_shapes=scratch_shapes,
                              compiler_params=_cparams(("parallel", "arbitrary")))(*args)
    n_in, n_out, n_scr = len(in_specs), len(out_specs), len(scratch_shapes)
    n_ci, n_co = len(comm.ins), len(comm.out_shapes)
    total = grid[0] * grid[1]
    mid_step = min(int(COMM_MID * total), total - 1)

    def carrying(*refs):
        ins, ci = refs[:n_in], refs[n_in:n_in + n_ci]
        at = n_in + n_ci
        outs, co = refs[at:at + n_out], refs[at + n_out:at + n_out + n_co]
        at += n_out + n_co
        scr, sems = refs[at:at + n_scr], refs[at + n_scr:]
        step = pl.program_id(0) * grid[1] + pl.program_id(1)
        pl.when(step == 0)(lambda: comm.start(ci, co, *sems))
        body(*ins, *outs, *scr)
        pl.when(step == mid_step)(lambda: comm.mid(ci, co, *sems))
        pl.when(step == total - 1)(lambda: comm.finish(ci, co, *sems))

    hbm = pl.BlockSpec(memory_space=pl.ANY)
    return pl.pallas_call(
        carrying, name=name, grid=grid,
        in_specs=list(in_specs) + [hbm] * n_ci, out_specs=list(out_specs) + [hbm] * n_co,
        out_shape=list(out_shape) + list(comm.out_shapes),
        scratch_shapes=list(scratch_shapes) + list(comm.scratch),
        compiler_params=_cparams(("arbitrary", "arbitrary")))(*args, *comm.ins)


def _ret_fwd(p, cos, sin, gn_g, comm=None):
    seq = p.shape[0]
    nc = seq // CHUNK
    hb = RET_HB
    tok, const, state, _ = _ret_specs(nc, False)

    def body(q_ref, k_ref, v_ref, gate_ref, cos_ref, sin_ref, intra_ref, qdec_ref, kdec_ref, cdec_ref, gng_ref,
             o_ref, ssave_ref, s_scr):
        @pl.when(pl.program_id(1) == 0)
        def _():
            s_scr[...] = jnp.zeros_like(s_scr)

        q1, q2, k1, k2, v, gate = _ret_load(q_ref, k_ref, v_ref, gate_ref)
        s = s_scr[...]
        ssave_ref[0] = s.astype(BF16)
        o, s_new = _ret_chunk(q1, q2, k1, k2, v, gate, gng_ref[...], s, cos_ref[...], sin_ref[...],
                              intra_ref[...], qdec_ref[...], kdec_ref[...], cdec_ref[...])
        s_scr[...] = s_new
        o_ref[...] = jnp.concatenate([o[i] for i in range(hb)], axis=-1).astype(o_ref.dtype)

    return _grid_call(
        body, name="ret_fwd", comm=comm,
        grid=(RET_HEADS // hb, nc),
        in_specs=tok + const,
        out_specs=[pl.BlockSpec((CHUNK, hb * RET_DV), lambda h, n: (n, h)), state],
        out_shape=[jax.ShapeDtypeStruct((seq, RET_VW), BF16),
                   jax.ShapeDtypeStruct((nc, RET_HEADS, RET_DK, RET_DV), BF16)],
        scratch_shapes=[pltpu.VMEM((hb, RET_DK, RET_DV), F32)],
        args=(p, p, p, p, cos, sin, *_ret_consts(), gn_g.reshape(RET_HEADS, 1, RET_DV)))


def _ret_bwd(p, cos, sin, gn_g, ssave, do, comm=None):
    seq = p.shape[0]
    nc = seq // CHUNK
    hb = RET_HB
    tok, const, state, cidx = _ret_specs(nc, True)

    def body(q_ref, k_ref, v_ref, gate_ref, cos_ref, sin_ref, intra_ref, qdec_ref, kdec_ref, cdec_ref, gng_ref,
             ssave_ref, do_ref, dq_ref, dk_ref, dv_ref, dgate_ref, dgng_ref, ds_scr):
        @pl.when(pl.program_id(1) == 0)
        def _():
            ds_scr[...] = jnp.zeros_like(ds_scr)
            dgng_ref[...] = jnp.zeros_like(dgng_ref)

        q1, q2, k1, k2, v, gate = _ret_load(q_ref, k_ref, v_ref, gate_ref)
        do = do_ref[...]
        dos = jnp.stack([do[:, i * RET_DV:(i + 1) * RET_DV] for i in range(hb)]).astype(F32)
        fn = functools.partial(_ret_chunk, cos=cos_ref[...], sin=sin_ref[...], intra=intra_ref[...],
                               qdec=qdec_ref[...], kdec=kdec_ref[...], cdec=cdec_ref[...])
        _, vjp = jax.vjp(fn, q1, q2, k1, k2, v, gate, gng_ref[...], ssave_ref[0].astype(F32))
        dq1, dq2, dk1, dk2, dv, dgate, dgng, ds = vjp((dos, ds_scr[...]))
        ds_scr[...] = ds
        dgng_ref[...] += dgng
        dq_ref[...] = jnp.concatenate([t[i] for i in range(hb) for t in (dq1, dq2)], axis=-1).astype(dq_ref.dtype)
        dk_ref[...] = jnp.concatenate([t[i] for i in range(hb) for t in (dk1, dk2)], axis=-1).astype(dk_ref.dtype)
        dv_ref[...] = jnp.concatenate([dv[i] for i in range(hb)], axis=-1).astype(dv_ref.dtype)
        dgate_ref[...] = jnp.concatenate([dgate[i] for i in range(hb)], axis=-1).astype(dgate_ref.dtype)

    qw, vw = hb * RET_DK, hb * RET_DV
    return _grid_call(
        body, name="ret_bwd", comm=comm,
        grid=(RET_HEADS // hb, nc),
        in_specs=tok + const + [state, pl.BlockSpec((CHUNK, vw), lambda h, n: (cidx(n), h))],
        out_specs=[pl.BlockSpec((CHUNK, qw), lambda h, n: (cidx(n), h)),
                   pl.BlockSpec((CHUNK, qw), lambda h, n: (cidx(n), h)),
                   pl.BlockSpec((CHUNK, vw), lambda h, n: (cidx(n), h)),
                   pl.BlockSpec((CHUNK, vw), lambda h, n: (cidx(n), h)),
                   pl.BlockSpec((hb, 1, RET_DV), lambda h, n: (h, 0, 0))],
        out_shape=[jax.ShapeDtypeStruct((seq, RET_QK), BF16), jax.ShapeDtypeStruct((seq, RET_QK), BF16),
                   jax.ShapeDtypeStruct((seq, RET_VW), BF16), jax.ShapeDtypeStruct((seq, RET_VW), BF16),
                   jax.ShapeDtypeStruct((RET_HEADS, 1, RET_DV), F32)],
        scratch_shapes=[pltpu.VMEM((hb, RET_DK, RET_DV), F32)],
        args=(p, p, p, p, cos, sin, *_ret_consts(), gn_g.reshape(RET_HEADS, 1, RET_DV), ssave, do))


def _gdn_common(qr, kr, gc_c, gc_r):
    qn = qr * lax.rsqrt(jnp.sum(qr * qr, -1, keepdims=True) + L2_EPS) * (GDN_DK ** -0.5)
    kn = kr * lax.rsqrt(jnp.sum(kr * kr, -1, keepdims=True) + L2_EPS)
    causal = _iota2((CHUNK, CHUNK), 0) >= _iota2((CHUNK, CHUNK), 1)
    decay = jnp.exp(jnp.where(causal, gc_c - gc_r, -1e30))
    return _rep2(qn), _rep2(kn), decay


def _gdn_a(k, decay, beta_c):
    strict = _iota2((CHUNK, CHUNK), 0) > _iota2((CHUNK, CHUNK), 1)
    return jnp.where(strict, _nt(k * beta_c, k) * decay, 0.0)


@jax.custom_vjp
def _inv_saved(a, t):
    return t


_inv_saved.defvjp(lambda a, t: (t, t),
                  lambda t, dt: (-_dg(_dg(t, dt, _TN), t, _NT), jnp.zeros_like(t)))


def _gdn_chunk(qr, kr, v, z, beta_c, gc_c, gc_r, norm_g, t_saved, s):
    q, k, decay = _gdn_common(qr, kr, gc_c, gc_r)
    t = _inv_saved(_gdn_a(k, decay, beta_c), t_saved)
    return _gdn_rest(q, k, decay, v, z, beta_c, gc_c, norm_g, t, s)


def _gdn_rest(q, k, decay, v, z, beta_c, gc_c, norm_g, t, s):
    eg = jnp.exp(gc_c)
    u = _nn(t, v * beta_c)
    w = _nn(t, k * (beta_c * eg))
    attn = _nt(q, k) * decay
    v_new = u - _nn(w, s)
    y = _nn(q * eg, s) + _nn(attn, v_new)
    last = _iota2((1, CHUNK, 1), 1) == CHUNK - 1
    gl = jnp.sum(jnp.where(last, gc_c, 0.0), axis=1, keepdims=True)
    s_new = s * jnp.exp(gl) + _tn(k * jnp.exp(gl - gc_c), v_new)
    yn = y * lax.rsqrt(jnp.mean(y * y, -1, keepdims=True) + RMS_EPS) * norm_g
    return yn * _silu(z), s_new


GDN_HK = 16


def _gdn_load(q_ref, k_ref, v_ref, z_ref):
    hk, hb, d = GDN_HK, 2 * GDN_HK, GDN_DK
    q, k, v, z = q_ref[...], k_ref[...], v_ref[...], z_ref[...]
    qs = jnp.stack([q[:, i * d:(i + 1) * d] for i in range(hk)])
    ks = jnp.stack([k[:, i * d:(i + 1) * d] for i in range(hk)])
    vs = jnp.stack([v[:, i * d:(i + 1) * d] for i in range(hb)])
    zs = jnp.stack([z[:, i * d:(i + 1) * d] for i in range(hb)])
    return qs, ks, vs, zs


def _gdn_specs(n_chunks, rev):
    hk, hb = GDN_HK, 2 * GDN_HK
    cidx = (lambda n: n_chunks - 1 - n) if rev else (lambda n: n)
    qw, vw = hk * GDN_DK, hb * GDN_DV
    tok = [
        pl.BlockSpec((CHUNK, qw), lambda h, n: (cidx(n), h)),
        pl.BlockSpec((CHUNK, qw), lambda h, n: (cidx(n), GDN_QK // qw + h)),
        pl.BlockSpec((CHUNK, vw), lambda h, n: (cidx(n), 2 * GDN_QK // vw + h)),
        pl.BlockSpec((CHUNK, vw), lambda h, n: (cidx(n), GDN_QKV // vw + h)),
        pl.BlockSpec((1, hb, CHUNK, 1), lambda h, n: (cidx(n), h, 0, 0)),
        pl.BlockSpec((1, hb, CHUNK, 1), lambda h, n: (cidx(n), h, 0, 0)),
        pl.BlockSpec((1, hb, 1, CHUNK), lambda h, n: (cidx(n), h, 0, 0)),
        pl.BlockSpec((1, GDN_DV), lambda h, n: (0, 0)),
    ]
    tsave = pl.BlockSpec((1, hb, CHUNK, CHUNK), lambda h, n: (cidx(n), h, 0, 0))
    ssave = pl.BlockSpec((1, hb, GDN_DK, GDN_DV), lambda h, n: (cidx(n), h, 0, 0))
    return tok, tsave, ssave, cidx


def _gdn_fwd(c, p, beta_c, gc_c, gc_r, norm_g):
    seq = c.shape[0]
    nc = seq // CHUNK
    hk, hb = GDN_HK, 2 * GDN_HK
    tok, tsave, ssave, _ = _gdn_specs(nc, False)

    def body(q_ref, k_ref, v_ref, z_ref, beta_ref, gcc_ref, gcr_ref, ng_ref, o_ref, tsave_ref, ssave_ref, s_scr):
        @pl.when(pl.program_id(1) == 0)
        def _():
            s_scr[...] = jnp.zeros_like(s_scr)

        qr, kr, v, z = _gdn_load(q_ref, k_ref, v_ref, z_ref)
        beta, gcc, gcr = beta_ref[0], gcc_ref[0], gcr_ref[0]
        s = s_scr[...]
        ssave_ref[0] = s.astype(BF16)
        q, k, decay = _gdn_common(qr, kr, gcc, gcr)
        t = _inv_unit_lower(_gdn_a(k, decay, beta))
        tsave_ref[0] = t.astype(BF16)
        o, s_new = _gdn_rest(q, k, decay, v, z, beta, gcc, ng_ref[...], t, s)
        s_scr[...] = s_new
        o_ref[...] = jnp.concatenate([o[i] for i in range(hb)], axis=-1).astype(o_ref.dtype)

    return pl.pallas_call(
        body, name="gdn_fwd",
        grid=(GDN_K_HEADS // hk, nc),
        in_specs=tok,
        out_specs=[pl.BlockSpec((CHUNK, hb * GDN_DV), lambda h, n: (n, h)), tsave, ssave],
        out_shape=[jax.ShapeDtypeStruct((seq, GDN_VW), BF16),
                   jax.ShapeDtypeStruct((nc, GDN_V_HEADS, CHUNK, CHUNK), BF16),
                   jax.ShapeDtypeStruct((nc, GDN_V_HEADS, GDN_DK, GDN_DV), BF16)],
        scratch_shapes=[pltpu.VMEM((hb, GDN_DK, GDN_DV), F32)],
        compiler_params=_cparams(("parallel", "arbitrary")),
    )(c, c, c, p, beta_c, gc_c, gc_r, norm_g.reshape(1, GDN_DV))


def _gdn_bwd(c, p, beta_c, gc_c, gc_r, norm_g, tsave, ssave, do, comm=None):
    seq = c.shape[0]
    nc = seq // CHUNK
    hk, hb = GDN_HK, 2 * GDN_HK
    nhb = GDN_K_HEADS // hk
    tok, tsave_spec, ssave_spec, cidx = _gdn_specs(nc, True)

    def body(q_ref, k_ref, v_ref, z_ref, beta_ref, gcc_ref, gcr_ref, ng_ref, t_ref, s_ref, do_ref,
             dq_ref, dk_ref, dv_ref, dz_ref, dbeta_ref, dgcc_ref, dgcr_ref, dng_ref, ds_scr):
        @pl.when(pl.program_id(1) == 0)
        def _():
            ds_scr[...] = jnp.zeros_like(ds_scr)
            dng_ref[...] = jnp.zeros_like(dng_ref)

        qr, kr, v, z = _gdn_load(q_ref, k_ref, v_ref, z_ref)
        beta, gcc, gcr = beta_ref[0], gcc_ref[0], gcr_ref[0]
        do = do_ref[...]
        dos = jnp.stack([do[:, i * GDN_DV:(i + 1) * GDN_DV] for i in range(hb)]).astype(F32)
        _, vjp = jax.vjp(_gdn_chunk, qr, kr, v, z, beta, gcc, gcr, ng_ref[...], t_ref[0].astype(F32),
                         s_ref[0].astype(F32))
        dqr, dkr, dv, dz, dbeta, dgcc, dgcr, dng, _, ds = vjp((dos, ds_scr[...]))
        ds_scr[...] = ds
        dng_ref[...] += dng[None]
        dq_ref[...] = jnp.concatenate([dqr[i] for i in range(hk)], axis=-1).astype(dq_ref.dtype)
        dk_ref[...] = jnp.concatenate([dkr[i] for i in range(hk)], axis=-1).astype(dk_ref.dtype)
        dv_ref[...] = jnp.concatenate([dv[i] for i in range(hb)], axis=-1).astype(dv_ref.dtype)
        dz_ref[...] = jnp.concatenate([dz[i] for i in range(hb)], axis=-1).astype(dz_ref.dtype)
        dbeta_ref[0] = dbeta
        dgcc_ref[0] = dgcc
        dgcr_ref[0] = dgcr

    qw, vw = hk * GDN_DK, hb * GDN_DV
    col = pl.BlockSpec((1, hb, CHUNK, 1), lambda h, n: (cidx(n), h, 0, 0))
    row = pl.BlockSpec((1, hb, 1, CHUNK), lambda h, n: (cidx(n), h, 0, 0))
    return _grid_call(
        body, name="gdn_bwd", comm=comm,
        grid=(nhb, nc),
        in_specs=tok + [tsave_spec, ssave_spec, pl.BlockSpec((CHUNK, vw), lambda h, n: (cidx(n), h))],
        out_specs=[pl.BlockSpec((CHUNK, qw), lambda h, n: (cidx(n), h)),
                   pl.BlockSpec((CHUNK, qw), lambda h, n: (cidx(n), h)),
                   pl.BlockSpec((CHUNK, vw), lambda h, n: (cidx(n), h)),
                   pl.BlockSpec((CHUNK, vw), lambda h, n: (cidx(n), h)),
                   col, col, row,
                   pl.BlockSpec((1, 1, GDN_DV), lambda h, n: (h, 0, 0))],
        out_shape=[jax.ShapeDtypeStruct((seq, GDN_QK), F32), jax.ShapeDtypeStruct((seq, GDN_QK), F32),
                   jax.ShapeDtypeStruct((seq, GDN_VW), F32), jax.ShapeDtypeStruct((seq, GDN_VW), BF16),
                   jax.ShapeDtypeStruct((nc, GDN_V_HEADS, CHUNK, 1), F32),
                   jax.ShapeDtypeStruct((nc, GDN_V_HEADS, CHUNK, 1), F32),
                   jax.ShapeDtypeStruct((nc, GDN_V_HEADS, 1, CHUNK), F32),
                   jax.ShapeDtypeStruct((nhb, 1, GDN_DV), F32)],
        scratch_shapes=[pltpu.VMEM((hb, GDN_DK, GDN_DV), F32)],
        args=(c, c, c, p, beta_c, gc_c, gc_r, norm_g.reshape(1, GDN_DV), tsave, ssave, do))


CONV_TB = 512
CONV_CB = 1024
HALO = 8


def _conv_taps(ext, w):
    acc = w[GDN_CONV - 1:GDN_CONV] * ext
    for j in range(GDN_CONV - 1):
        acc = acc + w[j:j + 1] * pltpu.roll(ext, GDN_CONV - 1 - j, 0)
    return acc


def _conv_fwd(p, w):
    seq = p.shape[0]
    tb, cb = min(CONV_TB, seq), CONV_CB

    def body(prev_ref, cur_ref, w_ref, o_ref):
        first = pl.program_id(1) == 0
        prev = jnp.where(first, 0.0, prev_ref[...])
        ext = jnp.concatenate([prev, cur_ref[...]], axis=0)
        o_ref[...] = _silu(_conv_taps(ext, w_ref[...])[HALO:])

    return pl.pallas_call(
        body, name="conv_fwd",
        grid=(GDN_QKV // cb, seq // tb),
        in_specs=[pl.BlockSpec((HALO, cb), lambda j, i: (jnp.maximum(i * (tb // HALO) - 1, 0), j)),
                  pl.BlockSpec((tb, cb), lambda j, i: (i, j)),
                  pl.BlockSpec((GDN_CONV, cb), lambda j, i: (0, j))],
        out_specs=pl.BlockSpec((tb, cb), lambda j, i: (i, j)),
        out_shape=jax.ShapeDtypeStruct((seq, GDN_QKV), F32),
        compiler_params=_cparams(("parallel", "arbitrary")),
    )(p, p, w)


def _conv_bwd(p, dq, dk, dv, w):
    seq = p.shape[0]
    tb, cb = min(CONV_TB, seq), CONV_CB
    nt = seq // tb
    last_halo = seq // HALO - 1
    first_col = (0, GDN_QK // cb, 2 * GDN_QK // cb, GDN_QKV // cb)

    def body(prev_ref, cur_ref, next_ref, dq_ref, dqn_ref, dk_ref, dkn_ref, dv_ref, dvn_ref, w_ref, du_ref, dw_ref):
        j, i = pl.program_id(0), pl.program_id(1)

        @pl.when(i == 0)
        def _():
            dw_ref[...] = jnp.zeros_like(dw_ref)

        def pick(q_ref, k_ref, v_ref):
            return jnp.where(j < first_col[1], q_ref[...], jnp.where(j < first_col[2], k_ref[...], v_ref[...]))

        w = w_ref[...]
        prev = jnp.where(i == 0, 0.0, prev_ref[...])
        ext = jnp.concatenate([prev, cur_ref[...], next_ref[...]], axis=0)
        pre = _conv_taps(ext, w)
        dnext = jnp.where(i == nt - 1, 0.0, pick(dqn_ref, dkn_ref, dvn_ref))
        dext = jnp.concatenate([jnp.zeros((HALO, cb), F32), pick(dq_ref, dk_ref, dv_ref), dnext], axis=0)
        sig = jax.nn.sigmoid(pre)
        dpre = dext * (sig * (1.0 + pre * (1.0 - sig)))
        rows = tb + 2 * HALO
        du = w[GDN_CONV - 1:GDN_CONV] * dpre
        for j in range(GDN_CONV - 1):
            du = du + w[j:j + 1] * pltpu.roll(dpre, rows - (GDN_CONV - 1 - j), 0)
        du_ref[...] = du[HALO:HALO + tb].astype(du_ref.dtype)
        dcore = dpre[HALO:HALO + tb]
        dws = []
        for j in range(GDN_CONV):
            sh = ext if j == GDN_CONV - 1 else pltpu.roll(ext, GDN_CONV - 1 - j, 0)
            dws.append(jnp.sum(dcore * sh[HALO:HALO + tb], axis=0, keepdims=True))
        dw_ref[...] += jnp.concatenate(dws, axis=0)

    hb = tb // HALO

    def part(k):
        lo, hi = first_col[k], first_col[k + 1]
        here = lambda j: (j >= lo) & (j < hi)
        col = lambda j: jnp.clip(j - lo, 0, hi - lo - 1)
        return [pl.BlockSpec((tb, cb), lambda j, i: (jnp.where(here(j), i, 0), col(j))),
                pl.BlockSpec((HALO, cb),
                             lambda j, i: (jnp.where(here(j), jnp.minimum((i + 1) * hb, last_halo), 0), col(j)))]

    return pl.pallas_call(
        body, name="conv_bwd",
        grid=(GDN_QKV // cb, nt),
        in_specs=[pl.BlockSpec((HALO, cb), lambda j, i: (jnp.maximum(i * hb - 1, 0), j)),
                  pl.BlockSpec((tb, cb), lambda j, i: (i, j)),
                  pl.BlockSpec((HALO, cb), lambda j, i: (jnp.minimum((i + 1) * hb, last_halo), j))]
        + part(0) + part(1) + part(2) + [pl.BlockSpec((GDN_CONV, cb), lambda j, i: (0, j))],
        out_specs=[pl.BlockSpec((tb, cb), lambda j, i: (i, j)),
                   pl.BlockSpec((GDN_CONV, cb), lambda j, i: (0, j))],
        out_shape=[jax.ShapeDtypeStruct((seq, GDN_QKV), BF16), jax.ShapeDtypeStruct((GDN_CONV, GDN_QKV), F32)],
        compiler_params=_cparams(("parallel", "arbitrary")),
    )(p, p, p, dq, dq, dk, dk, dv, dv, w)


GATE_TB = 512


def _split3(g):
    hi = g.astype(BF16)
    r = g - hi.astype(F32)
    mid = r.astype(BF16)
    lo = (r - mid.astype(F32)).astype(BF16)
    return hi, mid, lo


def _tri_chunks(n, upper):
    i, j = _iota2((n, n), 0), _iota2((n, n), 1)
    tri = (i <= j) if upper else (i >= j)
    return jnp.where(tri & ((i // CHUNK) == (j // CHUNK)), 1.0, 0.0).astype(BF16)


def _tri_apply(g, upper):
    tri = _tri_chunks(g.shape[0], upper)
    return sum(jnp.dot(tri, part, preferred_element_type=F32) for part in _split3(g))


@jax.custom_vjp
def _chunk_cumsum(g):
    return _tri_apply(g, False)


_chunk_cumsum.defvjp(lambda g: (_tri_apply(g, False), None), lambda _, d: (_tri_apply(d, True),))


def _gates(b, a, a_log, dt_bias):
    z = a + dt_bias
    softplus = jnp.maximum(z, 0.0) + jnp.log1p(jnp.exp(-jnp.abs(z)))
    g = -jnp.exp(a_log) * softplus
    return jax.nn.sigmoid(b), _chunk_cumsum(g)


def _gates_fwd(b, a, a_log, dt_bias):
    seq, nh = b.shape
    tb = min(GATE_TB, seq)

    def body(b_ref, a_ref, al_ref, dt_ref, beta_ref, gc_ref):
        beta, gc = _gates(b_ref[...], a_ref[...], al_ref[...], dt_ref[...])
        beta_ref[...] = beta
        gc_ref[...] = gc

    tok = pl.BlockSpec((tb, nh), lambda i: (i, 0))
    vec = pl.BlockSpec((1, nh), lambda i: (0, 0))
    return pl.pallas_call(
        body, name="gates_fwd", grid=(seq // tb,),
        in_specs=[tok, tok, vec, vec], out_specs=[tok, tok],
        out_shape=[jax.ShapeDtypeStruct((seq, nh), F32)] * 2,
        compiler_params=_cparams(("parallel",)),
    )(b, a, a_log, dt_bias)


def _gates_bwd(b, a, a_log, dt_bias, dbeta, dgc):
    seq, nh = b.shape
    tb = min(GATE_TB, seq)

    def body(b_ref, a_ref, al_ref, dt_ref, dbeta_ref, dgc_ref, db_ref, da_ref, dal_ref, ddt_ref):
        @pl.when(pl.program_id(0) == 0)
        def _():
            dal_ref[...] = jnp.zeros_like(dal_ref)
            ddt_ref[...] = jnp.zeros_like(ddt_ref)

        _, vjp = jax.vjp(_gates, b_ref[...], a_ref[...], al_ref[...], dt_ref[...])
        db, da, dal, ddt = vjp((dbeta_ref[...], dgc_ref[...]))
        db_ref[...] = db
        da_ref[...] = da
        dal_ref[...] += dal
        ddt_ref[...] += ddt

    tok = pl.BlockSpec((tb, nh), lambda i: (i, 0))
    vec = pl.BlockSpec((1, nh), lambda i: (0, 0))
    return pl.pallas_call(
        body, name="gates_bwd", grid=(seq // tb,),
        in_specs=[tok, tok, vec, vec, tok, tok], out_specs=[tok, tok, vec, vec],
        out_shape=[jax.ShapeDtypeStruct((seq, nh), F32)] * 2 + [jax.ShapeDtypeStruct((1, nh), F32)] * 2,
        compiler_params=_cparams(("arbitrary",)),
    )(b, a, a_log, dt_bias, dbeta, dgc)


LN_TR = 256


def _ln_stats(x, s):
    z = DN_ALPHA * x + s
    mu = jnp.mean(z, -1, keepdims=True)
    zc = z - mu
    var = jnp.mean(zc * zc, -1, keepdims=True)
    rstd = lax.rsqrt(var + LN_EPS)
    return zc * rstd, rstd


def _ln_fwd(x, s, g, b):
    seq, d = x.shape
    tr = min(LN_TR, seq)

    def body(x_ref, s_ref, g_ref, b_ref, o_ref, ob_ref):
        xhat, _ = _ln_stats(x_ref[...], s_ref[...])
        y = xhat * g_ref[...] + b_ref[...]
        o_ref[...] = y
        ob_ref[...] = y.astype(BF16)

    tok = pl.BlockSpec((tr, d), lambda i: (i, 0))
    vec = pl.BlockSpec((1, d), lambda i: (0, 0))
    return pl.pallas_call(
        body, name="ln_fwd", grid=(seq // tr,),
        in_specs=[tok, tok, vec, vec], out_specs=[tok, tok],
        out_shape=[jax.ShapeDtypeStruct((seq, d), F32), jax.ShapeDtypeStruct((seq, d), BF16)],
        compiler_params=_cparams(("parallel",)),
    )(x, s, g.reshape(1, d), b.reshape(1, d))


def _ln_bwd(dy, x, s, g):
    seq, d = x.shape
    tr = min(LN_TR, seq)

    def body(dy_ref, x_ref, s_ref, g_ref, dz_ref, dzb_ref, dg_ref, db_ref):
        @pl.when(pl.program_id(0) == 0)
        def _():
            dg_ref[...] = jnp.zeros_like(dg_ref)
            db_ref[...] = jnp.zeros_like(db_ref)

        dy = dy_ref[...]
        xhat, rstd = _ln_stats(x_ref[...], s_ref[...])
        dyg = dy * g_ref[...]
        m1 = jnp.mean(dyg, -1, keepdims=True)
        m2 = jnp.mean(dyg * xhat, -1, keepdims=True)
        dz = rstd * (dyg - m1 - xhat * m2)
        dz_ref[...] = dz
        dzb_ref[...] = dz.astype(BF16)
        dg_ref[...] += jnp.sum(dy * xhat, axis=0, keepdims=True)
        db_ref[...] += jnp.sum(dy, axis=0, keepdims=True)

    tok = pl.BlockSpec((tr, d), lambda i: (i, 0))
    vec = pl.BlockSpec((1, d), lambda i: (0, 0))
    return pl.pallas_call(
        body, name="ln_bwd", grid=(seq // tr,),
        in_specs=[tok, tok, tok, vec], out_specs=[tok, tok, vec, vec],
        out_shape=[jax.ShapeDtypeStruct((seq, d), F32), jax.ShapeDtypeStruct((seq, d), BF16),
                   jax.ShapeDtypeStruct((1, d), F32), jax.ShapeDtypeStruct((1, d), F32)],
        compiler_params=_cparams(("arbitrary",)),
    )(dy, x, s, g.reshape(1, d))


def _loss_head(y, target):
    seq, d = y.shape
    tr = min(LN_TR, seq)

    def body(y_ref, t_ref, loss_ref, dy_ref):
        @pl.when(pl.program_id(0) == 0)
        def _():
            loss_ref[...] = jnp.zeros_like(loss_ref)

        err = y_ref[...] - t_ref[...]
        dy_ref[...] = err * (1.0 / d)
        part = jnp.sum(jnp.sum(err * err, axis=0, keepdims=True), axis=1, keepdims=True)
        loss_ref[...] += part * (0.5 / d)

    tok = pl.BlockSpec((tr, d), lambda i: (i, 0))
    return pl.pallas_call(
        body, name="loss_head", grid=(seq // tr,),
        in_specs=[tok, tok], out_specs=[pl.BlockSpec((8, 128), lambda i: (0, 0)), tok],
        out_shape=[jax.ShapeDtypeStruct((8, 128), F32), jax.ShapeDtypeStruct((seq, d), F32)],
        compiler_params=_cparams(("arbitrary",)),
    )(y, target)


COMM_MID = 0.8


def _matmul(a, b, *, ta=False, tb=False, b_sharded=False, out_sharded=False, out_dtypes=(F32,), epilogue=None,
            extras=(), tm=1024, tn=1024, tk=2048, name="matmul", comm=None, b_cols=None):
    m, k = (a.shape[1], a.shape[0]) if ta else a.shape
    if b_sharded:
        bk, bn = b.shape[1], N_DEV * b.shape[2]
        shard_w = b.shape[2]
    else:
        bk, bn = b.shape[0], b_cols or b.shape[1]
    n = bk if tb else bn
    assert k == (bn if tb else bk), (a.shape, b.shape)
    tm, tn, tk = min(tm, m), min(tn, n), min(tk, k)
    if b_sharded:
        if tb:
            tk = math.gcd(tk, shard_w)
        else:
            tn = math.gcd(tn, shard_w)
    if out_sharded:
        tn = math.gcd(tn, n // N_DEV)
    assert m % tm == 0 and n % tn == 0 and k % tk == 0, (m, n, k, tm, tn, tk)
    ni, nj, nk = m // tm, n // tn, k // tk
    dims = (((0 if ta else 1,), (1 if tb else 0,)), ((), ()))
    n_ex, n_out = len(extras), len(out_dtypes)
    n_ci = len(comm.ins) if comm else 0
    n_co = len(comm.out_shapes) if comm else 0
    total = ni * nj * nk
    mid_step = min(int(COMM_MID * total), total - 1)

    def body(*refs):
        a_ref, b_ref = refs[0], refs[1]
        ex_refs = refs[2:2 + n_ex]
        ci_refs = refs[2 + n_ex:2 + n_ex + n_ci]
        out_refs = refs[2 + n_ex + n_ci:2 + n_ex + n_ci + n_out]
        co_refs = refs[2 + n_ex + n_ci + n_out:2 + n_ex + n_ci + n_out + n_co]
        scratch = refs[2 + n_ex + n_ci + n_out + n_co:]
        acc, sems = (None, scratch) if nk == 1 else (scratch[0], scratch[1:])
        kk = pl.program_id(2)
        step = (pl.program_id(0) * nj + pl.program_id(1)) * nk + kk

        if comm:
            @pl.when(step == 0)
            def _():
                comm.start(ci_refs, co_refs, *sems)

        prod = lax.dot_general(a_ref[...].astype(BF16), b_ref[...].astype(BF16), dims, preferred_element_type=F32)

        def write(res):
            outs = (res,) if epilogue is None else epilogue(res, *[r[...] for r in ex_refs])
            for o_ref, val in zip(out_refs, outs, strict=True):
                o_ref[...] = val.astype(o_ref.dtype)

        if nk == 1:
            write(prod)
        else:
            @pl.when(kk == 0)
            def _():
                acc[...] = prod

            @pl.when(kk > 0)
            def _():
                acc[...] += prod

            @pl.when(kk == nk - 1)
            def _():
                write(acc[...])

        if comm:
            @pl.when(step == mid_step)
            def _():
                comm.mid(ci_refs, co_refs, *sems)

            @pl.when(step == total - 1)
            def _():
                comm.finish(ci_refs, co_refs, *sems)

    a_spec = pl.BlockSpec((tk, tm), lambda i, j, kk: (kk, i)) if ta else pl.BlockSpec((tm, tk), lambda i, j, kk: (i, kk))
    if b_sharded and tb:
        per = shard_w // tk
        b_spec = pl.BlockSpec((None, tn, tk), lambda i, j, kk: (kk // per, j, kk % per))
    elif b_sharded:
        per = shard_w // tn
        b_spec = pl.BlockSpec((None, tk, tn), lambda i, j, kk: (j // per, kk, j % per))
    elif tb:
        b_spec = pl.BlockSpec((tn, tk), lambda i, j, kk: (j, kk))
    else:
        b_spec = pl.BlockSpec((tk, tn), lambda i, j, kk: (kk, j))
    ex_spec = pl.BlockSpec((tm, tn), lambda i, j, kk: (i, j))
    if out_sharded:
        per_o = n // N_DEV // tn
        o_spec = pl.BlockSpec((None, tm, tn), lambda i, j, kk: (j // per_o, i, j % per_o))
        o_shape = (N_DEV, m, n // N_DEV)
    else:
        o_spec, o_shape = ex_spec, (m, n)
    hbm = pl.BlockSpec(memory_space=pl.ANY)
    outs = pl.pallas_call(
        body, name=name, grid=(ni, nj, nk),
        in_specs=[a_spec, b_spec] + [ex_spec] * n_ex + [hbm] * n_ci,
        out_specs=[o_spec] * n_out + [hbm] * n_co,
        out_shape=[jax.ShapeDtypeStruct(o_shape, dt) for dt in out_dtypes] + (list(comm.out_shapes) if comm else []),
        scratch_shapes=([] if nk == 1 else [pltpu.VMEM((tm, tn), F32)]) + (list(comm.scratch) if comm else []),
        compiler_params=_cparams(("arbitrary",) * 3 if comm else ("parallel", "parallel", "arbitrary")),
    )(a, b, *extras, *(comm.ins if comm else ()))
    return outs[0] if len(outs) == 1 else outs


def _epi_relu2(acc):
    r = jnp.maximum(acc, 0.0)
    return acc, r * r


def _epi_drelu2(acc, pre):
    return (acc * (2.0 * jnp.maximum(pre, 0.0)),)


def _epi_add(scale):
    return lambda acc, other: (acc + scale * other,)


def _adamw(parts, w, m, v, *, rows_per_step, name, layer=None, n_layers=None, into=None):
    n_parts, rows, cols = parts.shape
    tr = min(rows_per_step, rows)
    assert rows % tr == 0

    def body(p_ref, w_ref, m_ref, v_ref, *rest):
        g_ref, d_ref, mo_ref, vo_ref = rest[-4:]
        g = p_ref[0].astype(F32)
        for i in range(1, n_parts):
            g = g + p_ref[i].astype(F32)
        m_new = ADAM_B1 * m_ref[...] + (1.0 - ADAM_B1) * g
        v_new = ADAM_B2 * v_ref[...] + (1.0 - ADAM_B2) * (g * g)
        m_hat = m_new / (1.0 - ADAM_B1 ** ADAM_STEP)
        v_hat = v_new / (1.0 - ADAM_B2 ** ADAM_STEP)
        g_ref[...] = g
        d_ref[...] = -ADAM_LR * (m_hat / (jnp.sqrt(v_hat) + ADAM_EPS) + ADAM_WD * w_ref[...])
        mo_ref[...] = m_new
        vo_ref[...] = v_new

    blk = pl.BlockSpec((tr, cols), lambda i: (i, 0))
    if layer is None:
        out_blk, out_shape = blk, (rows, cols)
    else:
        out_blk, out_shape = pl.BlockSpec((None, tr, cols), lambda i: (layer, i, 0)), (n_layers, rows, cols)
    into = list(into or ())
    return pl.pallas_call(
        body, name=name, grid=(rows // tr,),
        in_specs=[pl.BlockSpec((n_parts, tr, cols), lambda i: (0, i, 0)), blk, blk, blk]
        + [pl.BlockSpec(memory_space=pl.ANY)] * len(into),
        out_specs=[out_blk] * 4,
        out_shape=[jax.ShapeDtypeStruct(out_shape, F32)] * 4,
        input_output_aliases={4 + k: k for k in range(len(into))},
        compiler_params=_cparams(("parallel",)),
    )(parts, w, m, v, *into)


def _position():
    return lax.axis_index("x"), lax.axis_index("y"), lax.axis_index("c")


def _comm_scratch(n):
    return [pltpu.SemaphoreType.DMA((7 * n,)), pltpu.SemaphoreType.DMA((7 * n,)), pltpu.SemaphoreType.DMA((n,))]


class _Gather:
    def __init__(self, blocks):
        self.ins = list(blocks)
        self.out_shapes = [jax.ShapeDtypeStruct((N_DEV,) + b.shape, b.dtype) for b in blocks]
        self.scratch = _comm_scratch(len(blocks))

    def _plan(self, n, ins, outs, send_sems, recv_sems, local_sems):
        x, y, c = _position()
        me, sibling = (x, y, c), (x, y, 1 - c)
        chips = [(1 - x, y), (x, 1 - y), (1 - x, 1 - y)]
        x_ref, out_ref = ins[n], outs[n]

        def slot(px, py, pc):
            return out_ref.at[4 * px + 2 * py + pc]

        def copy(k, blk, to, src=None):
            return pltpu.make_async_remote_copy(
                src_ref=slot(*blk) if src is None else src, dst_ref=slot(*blk),
                send_sem=send_sems.at[7 * n + k], recv_sem=recv_sems.at[7 * n + k], device_id=to, device_id_type=MESH)

        mine = lambda: pltpu.make_async_copy(x_ref, slot(*me), local_sems.at[n])
        first = lambda: [copy(0, me, sibling, src=x_ref)] + [copy(1 + j, me, (*chip, c), src=x_ref)
                                                             for j, chip in enumerate(chips)]
        passed = lambda j: copy(4 + j, (*chips[j], c), sibling)
        landed = lambda j: copy(1 + j, (*chips[j], c), me)
        from_sibling = lambda: [copy(0, sibling, me)] + [copy(4 + j, (*chip, 1 - c), me) for j, chip in enumerate(chips)]
        return mine, first, passed, landed, from_sibling

    def start(self, ins, outs, *sems):
        for n in range(len(self.ins)):
            mine, first, _, _, _ = self._plan(n, ins, outs, *sems)
            mine().start()
            for cp in first():
                cp.start()

    def mid(self, ins, outs, *sems):
        plans = [self._plan(n, ins, outs, *sems) for n in range(len(self.ins))]
        for j in range(3):
            for _, _, passed, landed, _ in plans:
                landed(j).wait_recv()
                passed(j).start()

    def finish(self, ins, outs, *sems):
        for n in range(len(self.ins)):
            mine, first, passed, _, from_sibling = self._plan(n, ins, outs, *sems)
            for cp in from_sibling():
                cp.wait_recv()
            for cp in first() + [passed(j) for j in range(3)]:
                cp.wait_send()
            mine().wait()


class _Exchange:
    def __init__(self, parts):
        self.ins = list(parts)
        self.out_shapes = [jax.ShapeDtypeStruct(p.shape, p.dtype) for p in parts]
        self.scratch = _comm_scratch(len(parts))

    def _plan(self, n, ins, outs, send_sems, recv_sems, local_sems):
        x, y, c = _position()
        me = 4 * x + 2 * y + c
        p_ref, out_ref = ins[n], outs[n]
        mine = lambda: pltpu.make_async_copy(p_ref.at[me], out_ref.at[me], local_sems.at[n])

        def copies(landing):
            out = []
            for k in range(1, N_DEV):
                px = 1 - x if k & 4 else x
                py = 1 - y if k & 2 else y
                pc = 1 - c if k & 1 else c
                peer_slot = 4 * px + 2 * py + pc
                out.append(pltpu.make_async_remote_copy(
                    src_ref=p_ref.at[peer_slot], dst_ref=out_ref.at[peer_slot if landing else me],
                    send_sem=send_sems.at[7 * n + k - 1], recv_sem=recv_sems.at[7 * n + k - 1],
                    device_id=(px, py, pc), device_id_type=MESH))
            return out

        return mine, copies

    def start(self, ins, outs, *sems):
        for n in range(len(self.ins)):
            mine, copies = self._plan(n, ins, outs, *sems)
            mine().start()
            for cp in copies(False):
                cp.start()

    def mid(self, ins, outs, *sems):
        pass

    def finish(self, ins, outs, *sems):
        for n in range(len(self.ins)):
            mine, copies = self._plan(n, ins, outs, *sems)
            for cp in copies(True):
                cp.wait_recv()
            for cp in copies(False):
                cp.wait_send()
            mine().wait()


def _comm_alone(comm, name):
    def body(*refs):
        n_i, n_o = len(comm.ins), len(comm.out_shapes)
        ins, outs, sems = refs[:n_i], refs[n_i:n_i + n_o], refs[n_i + n_o:]
        comm.start(ins, outs, *sems)
        comm.mid(ins, outs, *sems)
        comm.finish(ins, outs, *sems)

    hbm = pl.BlockSpec(memory_space=pl.ANY)
    return pl.pallas_call(
        body, name=name, out_shape=list(comm.out_shapes),
        in_specs=[hbm] * len(comm.ins), out_specs=[hbm] * len(comm.out_shapes),
        scratch_shapes=list(comm.scratch),
    )(*comm.ins)


RET_IN_W = 2 * RET_QK + 2 * RET_VW
GDN_IN_W = GDN_QKV + GDN_VW + 2 * GDN_V_HEADS
GDN_TAIL = 2 * GDN_V_HEADS
TAIL_PAD = 128

SMALL_SIZES = (RET_VW, GDN_V_HEADS, GDN_V_HEADS, GDN_DV, DEPTH * D_MODEL, DEPTH * D_MODEL, DEPTH * D_MODEL,
               DEPTH * D_MODEL, GDN_CONV * GDN_QKV)
SMALL_LANES = 128
SMALL_ROWS = -(-sum(SMALL_SIZES) // (8 * SMALL_LANES)) * 8


def _pack_small(*vecs):
    flat = jnp.concatenate([v.reshape(-1).astype(F32) for v in vecs])
    return jnp.pad(flat, (0, SMALL_ROWS * SMALL_LANES - flat.shape[0])).reshape(SMALL_ROWS, SMALL_LANES)


def _unpack_small(buf, shapes):
    flat, out, at = buf.reshape(-1), [], 0
    for shp in shapes:
        n = int(np.prod(shp))
        out.append(flat[at:at + n].reshape(shp))
        at += n
    return out


def _mlp_bwd(dz, h, a, r, w1, w2, name):
    dz, dzb = dz
    da = _matmul(dzb, w2, tb=True, out_dtypes=(BF16,), epilogue=_epi_drelu2, extras=(a,), name=name + "_da")
    dw2 = _matmul(r, dzb, ta=True, out_dtypes=(BF16,), name=name + "_dw2").reshape(N_DEV, -1, D_MODEL)
    dw1 = _matmul(h, da, ta=True, out_sharded=True, out_dtypes=(BF16,), name=name + "_dw1")
    dh = _matmul(da, w1, tb=True, b_sharded=True, epilogue=_epi_add(DN_ALPHA), extras=(dz,), name=name + "_dh")
    return dh, dw1, dw2


def _chunk_cols(t):
    seq, nh = t.shape
    return t.reshape(seq // CHUNK, CHUNK, nh).transpose(0, 2, 1)[..., None]


def _from_chunk_cols(t):
    nc, nh = t.shape[:2]
    return t[..., 0].transpose(0, 2, 1).reshape(nc * CHUNK, nh)


def _chunk_rows(t):
    seq, nh = t.shape
    return t.reshape(seq // CHUNK, CHUNK, nh).transpose(0, 2, 1)[:, :, None, :]


def _from_chunk_rows(t):
    nc, nh = t.shape[:2]
    return t[:, :, 0, :].transpose(0, 2, 1).reshape(nc * CHUNK, nh)


def kernel(x, ret_w_in, ret_gn_g, ret_w_out, gdn_w_in, gdn_conv_w, gdn_a_log, gdn_dt_bias, gdn_norm_g, gdn_w_out, ln_mix_g, ln_mix_b, mlp_w1, mlp_w2, ln_ffn_g, ln_ffn_b, loss_target, m_ret_w_in, m_ret_gn_g, m_ret_w_out, m_gdn_w_in, m_gdn_conv_w, m_gdn_a_log, m_gdn_dt_bias, m_gdn_norm_g, m_gdn_w_out, m_ln_mix_g, m_ln_mix_b, m_mlp_w1, m_mlp_w2, m_ln_ffn_g, m_ln_ffn_b, v_ret_w_in, v_ret_gn_g, v_ret_w_out, v_gdn_w_in, v_gdn_conv_w, v_gdn_a_log, v_gdn_dt_bias, v_gdn_norm_g, v_gdn_w_out, v_ln_mix_g, v_ln_mix_b, v_mlp_w1, v_mlp_w2, v_ln_ffn_g, v_ln_ffn_b):
    xt, target = x[0], loss_target[0]
    seq = xt.shape[0]
    me = 4 * lax.axis_index("x") + 2 * lax.axis_index("y") + lax.axis_index("c")

    bf = lambda t: t.astype(BF16)
    cos, sin = _rope_tables(seq)
    w_ret_in, = _comm_alone(_Gather([bf(ret_w_in[0])]), "gather_ret_in")
    conv_blk = jnp.pad(gdn_conv_w[0], ((0, HALO - GDN_CONV), (0, 0)))
    shard_in = RET_IN_W // N_DEV
    xb = bf(xt)
    p0, w_ret_out, w1_0, conv_all = _matmul(
        xb, w_ret_in, b_sharded=True, tn=shard_in, name="ret_in",
        comm=_Gather([bf(ret_w_out[0]), bf(mlp_w1[0]), conv_blk]))
    w_ret_out = w_ret_out.reshape(RET_VW, D_MODEL)
    conv_w = conv_all[:, :GDN_CONV].transpose(1, 0, 2).reshape(GDN_CONV, GDN_QKV)
    o0, s0, w2_0, gdn_in_all = _ret_fwd(p0, cos, sin, ret_gn_g[0], comm=_Gather([bf(mlp_w2[0]), bf(gdn_w_in[0])]))
    w2_0 = w2_0.reshape(D_FF, D_MODEL)
    mix0 = _matmul(o0, w_ret_out, name="ret_out")
    h1, h1b = _ln_fwd(xt, mix0, ln_mix_g[0], ln_mix_b[0])
    a0, r0, w_gdn_out, w1_1 = _matmul(h1b, w1_0, b_sharded=True, out_dtypes=(F32, BF16), epilogue=_epi_relu2,
                                      name="mlp0_up", comm=_Gather([bf(gdn_w_out[0]), bf(mlp_w1[1])]))
    w_gdn_out = w_gdn_out.reshape(GDN_VW, D_MODEL)
    m0, w2_1 = _matmul(r0, w2_0, name="mlp0_down", comm=_Gather([bf(mlp_w2[1])]))
    w2_1 = w2_1.reshape(D_FF, D_MODEL)
    h2, h2b = _ln_fwd(h1, m0, ln_ffn_g[0], ln_ffn_b[0])

    w_gdn_in = gdn_in_all.transpose(1, 0, 2).reshape(D_MODEL, GDN_IN_W)
    main_w = GDN_IN_W - GDN_TAIL
    w_gdn_tail = jnp.pad(w_gdn_in[:, main_w:], ((0, 0), (0, TAIL_PAD - GDN_TAIL)))
    p1 = _matmul(h2b, w_gdn_in, b_cols=main_w, name="gdn_in")
    pt = _matmul(h2b, w_gdn_tail, name="gdn_in_tail")
    c1 = _conv_fwd(p1, conv_w)
    b_in, a_in = pt[:, :GDN_V_HEADS], pt[:, GDN_V_HEADS:GDN_TAIL]
    beta, gc = _gates_fwd(b_in, a_in, gdn_a_log, gdn_dt_bias)
    beta_c, gc_c, gc_r = _chunk_cols(beta), _chunk_cols(gc), _chunk_rows(gc)
    o1, t1, s1 = _gdn_fwd(c1, p1, beta_c, gc_c, gc_r, gdn_norm_g[0])
    mix1 = _matmul(o1, w_gdn_out, name="gdn_out")
    h3, h3b = _ln_fwd(h2, mix1, ln_mix_g[1], ln_mix_b[1])
    a1, r1 = _matmul(h3b, w1_1, b_sharded=True, out_dtypes=(F32, BF16), epilogue=_epi_relu2, name="mlp1_up")
    m1 = _matmul(r1, w2_1, name="mlp1_down")
    h4, _ = _ln_fwd(h3, m1, ln_ffn_g[1], ln_ffn_b[1])
    loss_blk, dh4 = _loss_head(h4, target)
    loss = lax.psum(loss_blk[0, 0], ("x", "y", "c"))

    dz, dzb, dg_ffn1, db_ffn1 = _ln_bwd(dh4, h3, m1, ln_ffn_g[1])
    dh3, dw1_1, dw2_1 = _mlp_bwd((dz, dzb), h3b, a1, r1, w1_1, w2_1, "mlp1")
    dz, dzb, dg_mix1, db_mix1 = _ln_bwd(dh3, h2, mix1, ln_mix_g[1])
    do1 = _matmul(dzb, w_gdn_out, tb=True, name="gdn_out_do")
    dw_gdn_out = _matmul(o1, dzb, ta=True, out_dtypes=(BF16,), name="gdn_out_dw").reshape(N_DEV, -1, D_MODEL)
    dq, dk, dv, dzg, dbeta_c, dgc_c, dgc_r, dng, x_w1_1, x_w2_1, x_gdn_out = _gdn_bwd(
        c1, p1, beta_c, gc_c, gc_r, gdn_norm_g[0], t1, s1, do1, comm=_Exchange([dw1_1, dw2_1, dw_gdn_out]))
    du, dconv = _conv_bwd(p1, dq, dk, dv, conv_w)
    db_in, da_in, dalog, ddt = _gates_bwd(b_in, a_in, gdn_a_log, gdn_dt_bias, _from_chunk_cols(dbeta_c),
                                          _from_chunk_cols(dgc_c) + _from_chunk_rows(dgc_r))
    dpt = jnp.concatenate([db_in, da_in, jnp.zeros((seq, TAIL_PAD - GDN_TAIL), F32)], axis=-1)
    dp1 = jnp.concatenate([du, dzg], axis=-1)
    dw_gdn_main = _matmul(h2b, dp1, ta=True, out_dtypes=(BF16,), name="gdn_in_dw")
    dw_gdn_tail = _matmul(h2b, dpt, ta=True, out_dtypes=(BF16,), name="gdn_in_tail_dw")
    dw_gdn_in = jnp.concatenate([dw_gdn_main, dw_gdn_tail[:, :GDN_TAIL]], axis=-1)
    dw_gdn_in = dw_gdn_in.reshape(D_MODEL, N_DEV, GDN_IN_W // N_DEV).transpose(1, 0, 2)
    dh2 = _matmul(dpt, w_gdn_tail, tb=True, epilogue=_epi_add(DN_ALPHA), extras=(dz,), name="gdn_in_tail_dh")
    dh2, x_gdn_in = _matmul(dp1, w_gdn_in, tb=True, b_cols=main_w, epilogue=_epi_add(1.0), extras=(dh2,),
                            name="gdn_in_dh", comm=_Exchange([dw_gdn_in]))

    dz, dzb, dg_ffn0, db_ffn0 = _ln_bwd(dh2, h1, m0, ln_ffn_g[0])
    dh1, dw1_0, dw2_0 = _mlp_bwd((dz, dzb), h1b, a0, r0, w1_0, w2_0, "mlp0")
    dz, dzb, dg_mix0, db_mix0 = _ln_bwd(dh1, xt, mix0, ln_mix_g[0])
    do0 = _matmul(dzb, w_ret_out, tb=True, name="ret_out_do")
    dw_ret_out = _matmul(o0, dzb, ta=True, out_dtypes=(BF16,), name="ret_out_dw").reshape(N_DEV, -1, D_MODEL)
    dq, dk, dv, dgate, dgng, x_w1_0, x_w2_0 = _ret_bwd(p0, cos, sin, ret_gn_g[0], s0, do0,
                                                      comm=_Exchange([dw1_0, dw2_0]))
    dp0 = jnp.concatenate([dq, dk, dv, dgate], axis=-1)
    dw_ret_in, x_ret_out = _matmul(xb, dp0, ta=True, out_sharded=True, out_dtypes=(BF16,), tn=shard_in,
                                   name="ret_in_dw", comm=_Exchange([dw_ret_out]))
    dx, x_ret_in = _matmul(dp0, w_ret_in, tb=True, b_sharded=True, epilogue=_epi_add(DN_ALPHA), extras=(dz,),
                           tk=shard_in, name="ret_in_dx", comm=_Exchange([dw_ret_in]))

    def update(parts, w, m, v, name, **slab):
        shape = parts.shape[1:]
        outs = _adamw(parts, w.reshape(shape), m.reshape(shape), v.reshape(shape), rows_per_step=128, name=name,
                      **slab)
        return outs if slab else [t.reshape(w.shape) for t in outs]

    u_w1 = update(x_w1_1, mlp_w1[1], m_mlp_w1[1], v_mlp_w1[1], "adamw_w1_1", layer=1, n_layers=DEPTH)
    u_w1 = update(x_w1_0, mlp_w1[0], m_mlp_w1[0], v_mlp_w1[0], "adamw_w1_0", layer=0, n_layers=DEPTH, into=u_w1)
    u_w2 = update(x_w2_1, mlp_w2[1], m_mlp_w2[1], v_mlp_w2[1], "adamw_w2_1", layer=1, n_layers=DEPTH)
    u_w2 = update(x_w2_0, mlp_w2[0], m_mlp_w2[0], v_mlp_w2[0], "adamw_w2_0", layer=0, n_layers=DEPTH, into=u_w2)
    big_out = list(zip(
        update(x_ret_in, ret_w_in, m_ret_w_in, v_ret_w_in, "adamw_ret_in"),
        update(x_ret_out, ret_w_out, m_ret_w_out, v_ret_w_out, "adamw_ret_out"),
        update(x_gdn_in, gdn_w_in, m_gdn_w_in, v_gdn_w_in, "adamw_gdn_in"),
        update(x_gdn_out, gdn_w_out, m_gdn_w_out, v_gdn_w_out, "adamw_gdn_out"),
        u_w1, u_w2))

    small_w = (ret_gn_g, gdn_a_log, gdn_dt_bias, gdn_norm_g, ln_mix_g, ln_mix_b, ln_ffn_g, ln_ffn_b)
    small_m = (m_ret_gn_g, m_gdn_a_log, m_gdn_dt_bias, m_gdn_norm_g, m_ln_mix_g, m_ln_mix_b, m_ln_ffn_g, m_ln_ffn_b)
    small_v = (v_ret_gn_g, v_gdn_a_log, v_gdn_dt_bias, v_gdn_norm_g, v_ln_mix_g, v_ln_mix_b, v_ln_ffn_g, v_ln_ffn_b)
    small_g = (dgng, dalog, ddt, jnp.sum(dng, axis=0),
               jnp.concatenate([dg_mix0, dg_mix1]), jnp.concatenate([db_mix0, db_mix1]),
               jnp.concatenate([dg_ffn0, dg_ffn1]), jnp.concatenate([db_ffn0, db_ffn1]), dconv)
    small_parts, = _comm_alone(_Gather([_pack_small(*small_g)]), "gather_small_grads")
    zero_conv = jnp.zeros((GDN_CONV, GDN_QKV), F32)
    small_out = _adamw(small_parts, _pack_small(*small_w, zero_conv), _pack_small(*small_m, zero_conv),
                       _pack_small(*small_v, zero_conv), rows_per_step=SMALL_ROWS, name="adamw_small")
    shapes = [t.shape for t in small_w] + [(GDN_CONV, GDN_QKV)]
    small_out = [_unpack_small(t, shapes) for t in small_out]
    conv_g = lax.dynamic_slice(small_out[0][-1], (0, me * (GDN_QKV // N_DEV)), (GDN_CONV, GDN_QKV // N_DEV))
    conv_out = _adamw(conv_g[None], gdn_conv_w[0], m_gdn_conv_w[0], v_gdn_conv_w[0],
                      rows_per_step=GDN_CONV, name="adamw_conv")

    def ordered(kind):
        b, s, cv = big_out[kind], small_out[kind], conv_out[kind][None]
        return [b[0], s[0], b[1], b[2], cv, s[1], s[2], s[3], b[3], s[4], s[5], b[4], b[5], s[6], s[7]]

    return (loss, dx[None], *ordered(0), *ordered(1), *ordered(2), *ordered(3))
```

```python
import functools
import math

import jax
import jax.numpy as jnp
import numpy as np
from jax import lax
from jax.experimental import pallas as pl
from jax.experimental.pallas import tpu as pltpu

F32 = jnp.float32
BF16 = jnp.bfloat16

N_DEV = 8
D_MODEL = 2048
CHUNK = 64
RET_HEADS = 8
RET_DK = 256
RET_DV = 512
RET_QK = RET_HEADS * RET_DK
RET_VW = RET_HEADS * RET_DV
ROPE_BASE = 10000.0
GN_EPS = 1e-6
GDN_K_HEADS = 16
GDN_V_HEADS = 32
GDN_DK = 128
GDN_DV = 128
GDN_QK = GDN_K_HEADS * GDN_DK
GDN_VW = GDN_V_HEADS * GDN_DV
GDN_QKV = 2 * GDN_QK + GDN_VW
GDN_CONV = 4
RMS_EPS = 1e-6
L2_EPS = 1e-6
D_FF = 4 * D_MODEL
DEPTH = 2
DN_ALPHA = (2.0 * DEPTH) ** 0.25
LN_EPS = 1e-5
ADAM_LR = 0.001
ADAM_B1 = 0.9
ADAM_B2 = 0.999
ADAM_EPS = 1e-08
ADAM_WD = 0.01
ADAM_STEP = 10

VMEM_LIMIT = 56 * 1024 * 1024
MESH = pl.DeviceIdType.MESH


def _cparams(sem=None):
    return pltpu.CompilerParams(dimension_semantics=sem, vmem_limit_bytes=VMEM_LIMIT)


_NT = (((2,), (2,)), ((0,), (0,)))
_NN = (((2,), (1,)), ((0,), (0,)))
_TN = (((1,), (1,)), ((0,), (0,)))


def _dg(a, b, dims):
    return lax.dot_general(a.astype(BF16), b.astype(BF16), dims, preferred_element_type=F32)


@jax.custom_vjp
def _nt(a, b):
    return _dg(a, b, _NT)


@jax.custom_vjp
def _nn(a, b):
    return _dg(a, b, _NN)


@jax.custom_vjp
def _tn(a, b):
    return _dg(a, b, _TN)


_nt.defvjp(lambda a, b: (_dg(a, b, _NT), (a, b)), lambda r, g: (_nn(g, r[1]), _tn(g, r[0])))
_nn.defvjp(lambda a, b: (_dg(a, b, _NN), (a, b)), lambda r, g: (_nt(g, r[1]), _tn(r[0], g)))
_tn.defvjp(lambda a, b: (_dg(a, b, _TN), (a, b)), lambda r, g: (_nt(r[1], g), _nn(r[0], g)))


def _iota2(shape, dim):
    return lax.broadcasted_iota(jnp.int32, shape, dim)


def _inv_unit_lower(a):
    c = a.shape[-1]
    eye = (_iota2((c, c), 0) == _iota2((c, c), 1)).astype(F32)
    m = -a
    p = eye + m
    for _ in range(int(math.log2(c)) - 1):
        m = _dg(m, m, _NN)
        p = p + _dg(p, m, _NN)
    return p


def _silu(x):
    return x * jax.nn.sigmoid(x)


def _rep2(t):
    h = t.shape[0]
    return jnp.broadcast_to(t[:, None], (h, 2) + t.shape[1:]).reshape((2 * h,) + t.shape[1:])


def _ret_chunk(q1, q2, k1, k2, v, gate, gn_g, s, cos, sin, intra, qdec, kdec, cdec):
    q = jnp.concatenate([q1 * cos - q2 * sin, q1 * sin + q2 * cos], axis=-1)
    k = jnp.concatenate([k1 * cos - k2 * sin, k1 * sin + k2 * cos], axis=-1) * (RET_DK ** -0.5)
    scores = _nt(q, k) * intra
    y = _nn(scores, v) + _nn(q * qdec, s)
    s_new = s * cdec + _tn(k * kdec, v)
    mu = jnp.mean(y, -1, keepdims=True)
    yc = y - mu
    var = jnp.mean(yc * yc, -1, keepdims=True)
    o = _silu(gate) * (yc * lax.rsqrt(var + GN_EPS) * gn_g)
    return o, s_new


def _ret_consts():
    log_gamma = np.log1p(-np.exp2(-5.0 - np.arange(RET_HEADS, dtype=np.float64)))
    idx = np.arange(CHUNK, dtype=np.float64)
    lg = log_gamma[:, None]
    intra = np.exp(lg[..., None] * np.abs(idx[:, None] - idx[None, :]))
    qdec = np.exp(lg * (idx + 1.0))[..., None]
    kdec = np.exp(lg * (CHUNK - 1.0 - idx))[..., None]
    cdec = np.exp(log_gamma * CHUNK)[:, None, None]
    return [jnp.asarray(t, F32) for t in (intra, qdec, kdec, cdec)]


def _rope_tables(seq):
    half = RET_DK // 2
    inv = ROPE_BASE ** (-jnp.arange(half, dtype=F32) / half)
    ang = jnp.arange(seq).astype(F32)[:, None] * inv[None, :]
    return jnp.cos(ang), jnp.sin(ang)


RET_HB = 8


def _ret_load(q_ref, k_ref, v_ref, gate_ref):
    hb, dk, dv, h = RET_HB, RET_DK, RET_DV, RET_DK // 2
    q, k, v, gate = q_ref[...], k_ref[...], v_ref[...], gate_ref[...]
    q1 = jnp.stack([q[:, i * dk:i * dk + h] for i in range(hb)])
    q2 = jnp.stack([q[:, i * dk + h:(i + 1) * dk] for i in range(hb)])
    k1 = jnp.stack([k[:, i * dk:i * dk + h] for i in range(hb)])
    k2 = jnp.stack([k[:, i * dk + h:(i + 1) * dk] for i in range(hb)])
    vs = jnp.stack([v[:, i * dv:(i + 1) * dv] for i in range(hb)])
    gs = jnp.stack([gate[:, i * dv:(i + 1) * dv] for i in range(hb)])
    return q1, q2, k1, k2, vs, gs


def _ret_specs(n_chunks, rev):
    hb = RET_HB
    cidx = (lambda n: n_chunks - 1 - n) if rev else (lambda n: n)
    qw, vw = hb * RET_DK, hb * RET_DV
    tok = [
        pl.BlockSpec((CHUNK, qw), lambda h, n: (cidx(n), h)),
        pl.BlockSpec((CHUNK, qw), lambda h, n: (cidx(n), RET_QK // qw + h)),
        pl.BlockSpec((CHUNK, vw), lambda h, n: (cidx(n), 2 * RET_QK // vw + h)),
        pl.BlockSpec((CHUNK, vw), lambda h, n: (cidx(n), (2 * RET_QK + RET_VW) // vw + h)),
        pl.BlockSpec((CHUNK, RET_DK // 2), lambda h, n: (cidx(n), 0)),
        pl.BlockSpec((CHUNK, RET_DK // 2), lambda h, n: (cidx(n), 0)),
    ]
    const = [
        pl.BlockSpec((hb, CHUNK, CHUNK), lambda h, n: (h, 0, 0)),
        pl.BlockSpec((hb, CHUNK, 1), lambda h, n: (h, 0, 0)),
        pl.BlockSpec((hb, CHUNK, 1), lambda h, n: (h, 0, 0)),
        pl.BlockSpec((hb, 1, 1), lambda h, n: (h, 0, 0)),
        pl.BlockSpec((hb, 1, RET_DV), lambda h, n: (h, 0, 0)),
    ]
    state = pl.BlockSpec((1, hb, RET_DK, RET_DV), lambda h, n: (cidx(n), h, 0, 0))
    return tok, const, state, cidx


def _grid_call(body, *, name, grid, in_specs, out_specs, out_shape, scratch_shapes, args, comm=None):
    if comm is None:
        return pl.pallas_call(body, name=name, grid=grid, in_specs=in_specs, out_specs=out_specs, out_shape=out_shape,
                              scratch_shapes=scratch_shapes,
                              compiler_params=_cparams(("parallel", "arbitrary")))(*args)
    n_in, n_out, n_scr = len(in_specs), len(out_specs), len(scratch_shapes)
    n_ci, n_co = len(comm.ins), len(comm.out_shapes)
    total = grid[0] * grid[1]
    mid_step = min(int(COMM_MID * total), total - 1)

    def carrying(*refs):
        ins, ci = refs[:n_in], refs[n_in:n_in + n_ci]
        at = n_in + n_ci
        outs, co = refs[at:at + n_out], refs[at + n_out:at + n_out + n_co]
        at += n_out + n_co
        scr, sems = refs[at:at + n_scr], refs[at + n_scr:]
        step = pl.program_id(0) * grid[1] + pl.program_id(1)
        pl.when(step == 0)(lambda: comm.start(ci, co, *sems))
        body(*ins, *outs, *scr)
        pl.when(step == mid_step)(lambda: comm.mid(ci, co, *sems))
        pl.when(step == total - 1)(lambda: comm.finish(ci, co, *sems))

    hbm = pl.BlockSpec(memory_space=pl.ANY)
    return pl.pallas_call(
        carrying, name=name, grid=grid,
        in_specs=list(in_specs) + [hbm] * n_ci, out_specs=list(out_specs) + [hbm] * n_co,
        out_shape=list(out_shape) + list(comm.out_shapes),
        scratch_shapes=list(scratch_shapes) + list(comm.scratch),
        compiler_params=_cparams(("arbitrary", "arbitrary")))(*args, *comm.ins)


def _ret_fwd(p, cos, sin, gn_g, comm=None):
    seq = p.shape[0]
    nc = seq // CHUNK
    hb = RET_HB
    tok, const, state, _ = _ret_specs(nc, False)

    def body(q_ref, k_ref, v_ref, gate_ref, cos_ref, sin_ref, intra_ref, qdec_ref, kdec_ref, cdec_ref, gng_ref,
             o_ref, ssave_ref, s_scr):
        @pl.when(pl.program_id(1) == 0)
        def _():
            s_scr[...] = jnp.zeros_like(s_scr)

        q1, q2, k1, k2, v, gate = _ret_load(q_ref, k_ref, v_ref, gate_ref)
        s = s_scr[...]
        ssave_ref[0] = s.astype(BF16)
        o, s_new = _ret_chunk(q1, q2, k1, k2, v, gate, gng_ref[...], s, cos_ref[...], sin_ref[...],
                              intra_ref[...], qdec_ref[...], kdec_ref[...], cdec_ref[...])
        s_scr[...] = s_new
        o_ref[...] = jnp.concatenate([o[i] for i in range(hb)], axis=-1).astype(o_ref.dtype)

    return _grid_call(
        body, name="ret_fwd", comm=comm,
        grid=(RET_HEADS // hb, nc),
        in_specs=tok + const,
        out_specs=[pl.BlockSpec((CHUNK, hb * RET_DV), lambda h, n: (n, h)), state],
        out_shape=[jax.ShapeDtypeStruct((seq, RET_VW), BF16),
                   jax.ShapeDtypeStruct((nc, RET_HEADS, RET_DK, RET_DV), BF16)],
        scratch_shapes=[pltpu.VMEM((hb, RET_DK, RET_DV), F32)],
        args=(p, p, p, p, cos, sin, *_ret_consts(), gn_g.reshape(RET_HEADS, 1, RET_DV)))


def _ret_bwd(p, cos, sin, gn_g, ssave, do, comm=None):
    seq = p.shape[0]
    nc = seq // CHUNK
    hb = RET_HB
    tok, const, state, cidx = _ret_specs(nc, True)

    assert hb == RET_HEADS

    def body(q_ref, k_ref, v_ref, gate_ref, cos_ref, sin_ref, intra_ref, qdec_ref, kdec_ref, cdec_ref, gng_ref,
             ssave_ref, do_ref, dp_ref, dgng_ref, ds_scr):
        @pl.when(pl.program_id(1) == 0)
        def _():
            ds_scr[...] = jnp.zeros_like(ds_scr)
            dgng_ref[...] = jnp.zeros_like(dgng_ref)

        q1, q2, k1, k2, v, gate = _ret_load(q_ref, k_ref, v_ref, gate_ref)
        do = do_ref[...]
        dos = jnp.stack([do[:, i * RET_DV:(i + 1) * RET_DV] for i in range(hb)]).astype(F32)
        fn = functools.partial(_ret_chunk, cos=cos_ref[...], sin=sin_ref[...], intra=intra_ref[...],
                               qdec=qdec_ref[...], kdec=kdec_ref[...], cdec=cdec_ref[...])
        _, vjp = jax.vjp(fn, q1, q2, k1, k2, v, gate, gng_ref[...], ssave_ref[0].astype(F32))
        dq1, dq2, dk1, dk2, dv, dgate, dgng, ds = vjp((dos, ds_scr[...]))
        ds_scr[...] = ds
        dgng_ref[...] += dgng
        pieces = ([t[i] for i in range(hb) for t in (dq1, dq2)] + [t[i] for i in range(hb) for t in (dk1, dk2)]
                  + [dv[i] for i in range(hb)] + [dgate[i] for i in range(hb)])
        dp_ref[...] = jnp.concatenate([t.astype(dp_ref.dtype) for t in pieces], axis=-1)

    vw = hb * RET_DV
    width = 2 * RET_QK + 2 * RET_VW
    return _grid_call(
        body, name="ret_bwd", comm=comm,
        grid=(RET_HEADS // hb, nc),
        in_specs=tok + const + [state, pl.BlockSpec((CHUNK, vw), lambda h, n: (cidx(n), h))],
        out_specs=[pl.BlockSpec((CHUNK, width), lambda h, n: (cidx(n), 0)),
                   pl.BlockSpec((hb, 1, RET_DV), lambda h, n: (h, 0, 0))],
        out_shape=[jax.ShapeDtypeStruct((seq, width), BF16),
                   jax.ShapeDtypeStruct((RET_HEADS, 1, RET_DV), F32)],
        scratch_shapes=[pltpu.VMEM((hb, RET_DK, RET_DV), F32)],
        args=(p, p, p, p, cos, sin, *_ret_consts(), gn_g.reshape(RET_HEADS, 1, RET_DV), ssave, do))


def _gdn_common(qr, kr, gc_c, gc_r):
    qn = qr * lax.rsqrt(jnp.sum(qr * qr, -1, keepdims=True) + L2_EPS) * (GDN_DK ** -0.5)
    kn = kr * lax.rsqrt(jnp.sum(kr * kr, -1, keepdims=True) + L2_EPS)
    causal = _iota2((CHUNK, CHUNK), 0) >= _iota2((CHUNK, CHUNK), 1)
    decay = jnp.exp(jnp.where(causal, gc_c - gc_r, -1e30))
    return _rep2(qn), _rep2(kn), decay


def _gdn_a(k, decay, beta_c):
    strict = _iota2((CHUNK, CHUNK), 0) > _iota2((CHUNK, CHUNK), 1)
    return jnp.where(strict, _nt(k * beta_c, k) * decay, 0.0)


@jax.custom_vjp
def _inv_saved(a, t):
    return t


_inv_saved.defvjp(lambda a, t: (t, t),
                  lambda t, dt: (-_dg(_dg(t, dt, _TN), t, _NT), jnp.zeros_like(t)))


def _gdn_chunk(qr, kr, v, z, beta_c, gc_c, gc_r, norm_g, t_saved, s):
    q, k, decay = _gdn_common(qr, kr, gc_c, gc_r)
    t = _inv_saved(_gdn_a(k, decay, beta_c), t_saved)
    return _gdn_rest(q, k, decay, v, z, beta_c, gc_c, norm_g, t, s)


def _gdn_rest(q, k, decay, v, z, beta_c, gc_c, norm_g, t, s):
    eg = jnp.exp(gc_c)
    u = _nn(t, v * beta_c)
    w = _nn(t, k * (beta_c * eg))
    attn = _nt(q, k) * decay
    v_new = u - _nn(w, s)
    y = _nn(q * eg, s) + _nn(attn, v_new)
    last = _iota2((1, CHUNK, 1), 1) == CHUNK - 1
    gl = jnp.sum(jnp.where(last, gc_c, 0.0), axis=1, keepdims=True)
    s_new = s * jnp.exp(gl) + _tn(k * jnp.exp(gl - gc_c), v_new)
    yn = y * lax.rsqrt(jnp.mean(y * y, -1, keepdims=True) + RMS_EPS) * norm_g
    return yn * _silu(z), s_new


GDN_HK = 16


def _gdn_load(q_ref, k_ref, v_ref, z_ref):
    hk, hb, d = GDN_HK, 2 * GDN_HK, GDN_DK
    q, k, v, z = q_ref[...], k_ref[...], v_ref[...], z_ref[...]
    qs = jnp.stack([q[:, i * d:(i + 1) * d] for i in range(hk)])
    ks = jnp.stack([k[:, i * d:(i + 1) * d] for i in range(hk)])
    vs = jnp.stack([v[:, i * d:(i + 1) * d] for i in range(hb)])
    zs = jnp.stack([z[:, i * d:(i + 1) * d] for i in range(hb)])
    return qs, ks, vs, zs


def _gdn_specs(n_chunks, rev):
    hk, hb = GDN_HK, 2 * GDN_HK
    cidx = (lambda n: n_chunks - 1 - n) if rev else (lambda n: n)
    qw, vw = hk * GDN_DK, hb * GDN_DV
    tok = [
        pl.BlockSpec((CHUNK, qw), lambda h, n: (cidx(n), h)),
        pl.BlockSpec((CHUNK, qw), lambda h, n: (cidx(n), GDN_QK // qw + h)),
        pl.BlockSpec((CHUNK, vw), lambda h, n: (cidx(n), 2 * GDN_QK // vw + h)),
        pl.BlockSpec((CHUNK, vw), lambda h, n: (cidx(n), GDN_QKV // vw + h)),
        pl.BlockSpec((1, hb, CHUNK, 1), lambda h, n: (cidx(n), h, 0, 0)),
        pl.BlockSpec((1, hb, CHUNK, 1), lambda h, n: (cidx(n), h, 0, 0)),
        pl.BlockSpec((1, hb, 1, CHUNK), lambda h, n: (cidx(n), h, 0, 0)),
        pl.BlockSpec((1, GDN_DV), lambda h, n: (0, 0)),
    ]
    tsave = pl.BlockSpec((1, hb, CHUNK, CHUNK), lambda h, n: (cidx(n), h, 0, 0))
    ssave = pl.BlockSpec((1, hb, GDN_DK, GDN_DV), lambda h, n: (cidx(n), h, 0, 0))
    return tok, tsave, ssave, cidx


def _gdn_fwd(c, p, beta_c, gc_c, gc_r, norm_g):
    seq = c.shape[0]
    nc = seq // CHUNK
    hk, hb = GDN_HK, 2 * GDN_HK
    tok, tsave, ssave, _ = _gdn_specs(nc, False)

    def body(q_ref, k_ref, v_ref, z_ref, beta_ref, gcc_ref, gcr_ref, ng_ref, o_ref, tsave_ref, ssave_ref, s_scr):
        @pl.when(pl.program_id(1) == 0)
        def _():
            s_scr[...] = jnp.zeros_like(s_scr)

        qr, kr, v, z = _gdn_load(q_ref, k_ref, v_ref, z_ref)
        beta, gcc, gcr = beta_ref[0], gcc_ref[0], gcr_ref[0]
        s = s_scr[...]
        ssave_ref[0] = s.astype(BF16)
        q, k, decay = _gdn_common(qr, kr, gcc, gcr)
        t = _inv_unit_lower(_gdn_a(k, decay, beta))
        tsave_ref[0] = t.astype(BF16)
        o, s_new = _gdn_rest(q, k, decay, v, z, beta, gcc, ng_ref[...], t, s)
        s_scr[...] = s_new
        o_ref[...] = jnp.concatenate([o[i] for i in range(hb)], axis=-1).astype(o_ref.dtype)

    return pl.pallas_call(
        body, name="gdn_fwd",
        grid=(GDN_K_HEADS // hk, nc),
        in_specs=tok,
        out_specs=[pl.BlockSpec((CHUNK, hb * GDN_DV), lambda h, n: (n, h)), tsave, ssave],
        out_shape=[jax.ShapeDtypeStruct((seq, GDN_VW), BF16),
                   jax.ShapeDtypeStruct((nc, GDN_V_HEADS, CHUNK, CHUNK), BF16),
                   jax.ShapeDtypeStruct((nc, GDN_V_HEADS, GDN_DK, GDN_DV), BF16)],
        scratch_shapes=[pltpu.VMEM((hb, GDN_DK, GDN_DV), F32)],
        compiler_params=_cparams(("parallel", "arbitrary")),
    )(c, c, c, p, beta_c, gc_c, gc_r, norm_g.reshape(1, GDN_DV))


def _gdn_bwd(c, p, beta_c, gc_c, gc_r, norm_g, tsave, ssave, do, comm=None):
    seq = c.shape[0]
    nc = seq // CHUNK
    hk, hb = GDN_HK, 2 * GDN_HK
    nhb = GDN_K_HEADS // hk
    tok, tsave_spec, ssave_spec, cidx = _gdn_specs(nc, True)

    assert hk == GDN_K_HEADS

    def body(q_ref, k_ref, v_ref, z_ref, beta_ref, gcc_ref, gcr_ref, ng_ref, t_ref, s_ref, do_ref,
             dc_ref, dz_ref, dbeta_ref, dgcc_ref, dgcr_ref, dng_ref, ds_scr):
        @pl.when(pl.program_id(1) == 0)
        def _():
            ds_scr[...] = jnp.zeros_like(ds_scr)
            dng_ref[...] = jnp.zeros_like(dng_ref)

        qr, kr, v, z = _gdn_load(q_ref, k_ref, v_ref, z_ref)
        beta, gcc, gcr = beta_ref[0], gcc_ref[0], gcr_ref[0]
        do = do_ref[...]
        dos = jnp.stack([do[:, i * GDN_DV:(i + 1) * GDN_DV] for i in range(hb)]).astype(F32)
        _, vjp = jax.vjp(_gdn_chunk, qr, kr, v, z, beta, gcc, gcr, ng_ref[...], t_ref[0].astype(F32),
                         s_ref[0].astype(F32))
        dqr, dkr, dv, dz, dbeta, dgcc, dgcr, dng, _, ds = vjp((dos, ds_scr[...]))
        ds_scr[...] = ds
        dng_ref[...] += dng[None]
        dc_ref[...] = jnp.concatenate([dqr[i] for i in range(hk)] + [dkr[i] for i in range(hk)]
                                      + [dv[i] for i in range(hb)], axis=-1)
        dz_ref[...] = jnp.concatenate([dz[i] for i in range(hb)], axis=-1).astype(dz_ref.dtype)
        dbeta_ref[0] = dbeta
        dgcc_ref[0] = dgcc
        dgcr_ref[0] = dgcr

    qw, vw = hk * GDN_DK, hb * GDN_DV
    col = pl.BlockSpec((1, hb, CHUNK, 1), lambda h, n: (cidx(n), h, 0, 0))
    row = pl.BlockSpec((1, hb, 1, CHUNK), lambda h, n: (cidx(n), h, 0, 0))
    return _grid_call(
        body, name="gdn_bwd", comm=comm,
        grid=(nhb, nc),
        in_specs=tok + [tsave_spec, ssave_spec, pl.BlockSpec((CHUNK, vw), lambda h, n: (cidx(n), h))],
        out_specs=[pl.BlockSpec((CHUNK, GDN_QKV), lambda h, n: (cidx(n), 0)),
                   pl.BlockSpec((CHUNK, vw), lambda h, n: (cidx(n), GDN_QKV // vw)),
                   col, col, row,
                   pl.BlockSpec((1, 1, GDN_DV), lambda h, n: (h, 0, 0))],
        out_shape=[jax.ShapeDtypeStruct((seq, GDN_QKV), F32), jax.ShapeDtypeStruct((seq, GDN_QKV + GDN_VW), BF16),
                   jax.ShapeDtypeStruct((nc, GDN_V_HEADS, CHUNK, 1), F32),
                   jax.ShapeDtypeStruct((nc, GDN_V_HEADS, CHUNK, 1), F32),
                   jax.ShapeDtypeStruct((nc, GDN_V_HEADS, 1, CHUNK), F32),
                   jax.ShapeDtypeStruct((nhb, 1, GDN_DV), F32)],
        scratch_shapes=[pltpu.VMEM((hb, GDN_DK, GDN_DV), F32)],
        args=(c, c, c, p, beta_c, gc_c, gc_r, norm_g.reshape(1, GDN_DV), tsave, ssave, do))


CONV_TB = 512
CONV_CB = 1024
HALO = 8


def _conv_taps(ext, w):
    acc = w[GDN_CONV - 1:GDN_CONV] * ext
    for j in range(GDN_CONV - 1):
        acc = acc + w[j:j + 1] * pltpu.roll(ext, GDN_CONV - 1 - j, 0)
    return acc


def _conv_fwd(p, w):
    seq = p.shape[0]
    tb, cb = min(CONV_TB, seq), CONV_CB

    def body(prev_ref, cur_ref, w_ref, o_ref):
        first = pl.program_id(1) == 0
        prev = jnp.where(first, 0.0, prev_ref[...])
        ext = jnp.concatenate([prev, cur_ref[...]], axis=0)
        o_ref[...] = _silu(_conv_taps(ext, w_ref[...])[HALO:])

    return pl.pallas_call(
        body, name="conv_fwd",
        grid=(GDN_QKV // cb, seq // tb),
        in_specs=[pl.BlockSpec((HALO, cb), lambda j, i: (jnp.maximum(i * (tb // HALO) - 1, 0), j)),
                  pl.BlockSpec((tb, cb), lambda j, i: (i, j)),
                  pl.BlockSpec((GDN_CONV, cb), lambda j, i: (0, j))],
        out_specs=pl.BlockSpec((tb, cb), lambda j, i: (i, j)),
        out_shape=jax.ShapeDtypeStruct((seq, GDN_QKV), F32),
        compiler_params=_cparams(("parallel", "arbitrary")),
    )(p, p, w)


def _conv_bwd(p, dc, w, dp):
    seq = p.shape[0]
    tb, cb = min(CONV_TB, seq), CONV_CB
    nt = seq // tb
    last_halo = seq // HALO - 1

    def body(prev_ref, cur_ref, next_ref, dcur_ref, dnext_ref, w_ref, _, du_ref, dw_ref):
        i = pl.program_id(1)

        @pl.when(i == 0)
        def _():
            dw_ref[...] = jnp.zeros_like(dw_ref)

        w = w_ref[...]
        prev = jnp.where(i == 0, 0.0, prev_ref[...])
        ext = jnp.concatenate([prev, cur_ref[...], next_ref[...]], axis=0)
        pre = _conv_taps(ext, w)
        dnext = jnp.where(i == nt - 1, 0.0, dnext_ref[...])
        dext = jnp.concatenate([jnp.zeros((HALO, cb), F32), dcur_ref[...], dnext], axis=0)
        sig = jax.nn.sigmoid(pre)
        dpre = dext * (sig * (1.0 + pre * (1.0 - sig)))
        rows = tb + 2 * HALO
        du = w[GDN_CONV - 1:GDN_CONV] * dpre
        for j in range(GDN_CONV - 1):
            du = du + w[j:j + 1] * pltpu.roll(dpre, rows - (GDN_CONV - 1 - j), 0)
        du_ref[...] = du[HALO:HALO + tb].astype(du_ref.dtype)
        dcore = dpre[HALO:HALO + tb]
        dws = []
        for j in range(GDN_CONV):
            sh = ext if j == GDN_CONV - 1 else pltpu.roll(ext, GDN_CONV - 1 - j, 0)
            dws.append(jnp.sum(dcore * sh[HALO:HALO + tb], axis=0, keepdims=True))
        dw_ref[...] += jnp.concatenate(dws, axis=0)

    hb = tb // HALO
    cur = pl.BlockSpec((tb, cb), lambda j, i: (i, j))
    nxt = pl.BlockSpec((HALO, cb), lambda j, i: (jnp.minimum((i + 1) * hb, last_halo), j))
    return pl.pallas_call(
        body, name="conv_bwd",
        grid=(GDN_QKV // cb, nt),
        in_specs=[pl.BlockSpec((HALO, cb), lambda j, i: (jnp.maximum(i * hb - 1, 0), j)), cur, nxt, cur, nxt,
                  pl.BlockSpec((GDN_CONV, cb), lambda j, i: (0, j)), pl.BlockSpec(memory_space=pl.ANY)],
        out_specs=[cur, pl.BlockSpec((GDN_CONV, cb), lambda j, i: (0, j))],
        out_shape=[jax.ShapeDtypeStruct(dp.shape, dp.dtype), jax.ShapeDtypeStruct((GDN_CONV, GDN_QKV), F32)],
        input_output_aliases={6: 0},
        compiler_params=_cparams(("parallel", "arbitrary")),
    )(p, p, p, dc, dc, w, dp)


GATE_TB = 512


def _split3(g):
    hi = g.astype(BF16)
    r = g - hi.astype(F32)
    mid = r.astype(BF16)
    lo = (r - mid.astype(F32)).astype(BF16)
    return hi, mid, lo


def _tri_chunks(n, upper):
    i, j = _iota2((n, n), 0), _iota2((n, n), 1)
    tri = (i <= j) if upper else (i >= j)
    return jnp.where(tri & ((i // CHUNK) == (j // CHUNK)), 1.0, 0.0).astype(BF16)


def _tri_apply(g, upper):
    tri = _tri_chunks(g.shape[0], upper)
    return sum(jnp.dot(tri, part, preferred_element_type=F32) for part in _split3(g))


@jax.custom_vjp
def _chunk_cumsum(g):
    return _tri_apply(g, False)


_chunk_cumsum.defvjp(lambda g: (_tri_apply(g, False), None), lambda _, d: (_tri_apply(d, True),))


def _gates(b, a, a_log, dt_bias):
    z = a + dt_bias
    softplus = jnp.maximum(z, 0.0) + jnp.log1p(jnp.exp(-jnp.abs(z)))
    g = -jnp.exp(a_log) * softplus
    return jax.nn.sigmoid(b), _chunk_cumsum(g)


def _gates_fwd(b, a, a_log, dt_bias):
    seq, nh = b.shape
    tb = min(GATE_TB, seq)

    def body(b_ref, a_ref, al_ref, dt_ref, beta_ref, gc_ref):
        beta, gc = _gates(b_ref[...], a_ref[...], al_ref[...], dt_ref[...])
        beta_ref[...] = beta
        gc_ref[...] = gc

    tok = pl.BlockSpec((tb, nh), lambda i: (i, 0))
    vec = pl.BlockSpec((1, nh), lambda i: (0, 0))
    return pl.pallas_call(
        body, name="gates_fwd", grid=(seq // tb,),
        in_specs=[tok, tok, vec, vec], out_specs=[tok, tok],
        out_shape=[jax.ShapeDtypeStruct((seq, nh), F32)] * 2,
        compiler_params=_cparams(("parallel",)),
    )(b, a, a_log, dt_bias)


def _gates_bwd(b, a, a_log, dt_bias, dbeta, dgc):
    seq, nh = b.shape
    tb = min(GATE_TB, seq)

    def body(b_ref, a_ref, al_ref, dt_ref, dbeta_ref, dgc_ref, db_ref, da_ref, dal_ref, ddt_ref):
        @pl.when(pl.program_id(0) == 0)
        def _():
            dal_ref[...] = jnp.zeros_like(dal_ref)
            ddt_ref[...] = jnp.zeros_like(ddt_ref)

        _, vjp = jax.vjp(_gates, b_ref[...], a_ref[...], al_ref[...], dt_ref[...])
        db, da, dal, ddt = vjp((dbeta_ref[...], dgc_ref[...]))
        db_ref[...] = db
        da_ref[...] = da
        dal_ref[...] += dal
        ddt_ref[...] += ddt

    tok = pl.BlockSpec((tb, nh), lambda i: (i, 0))
    vec = pl.BlockSpec((1, nh), lambda i: (0, 0))
    return pl.pallas_call(
        body, name="gates_bwd", grid=(seq // tb,),
        in_specs=[tok, tok, vec, vec, tok, tok], out_specs=[tok, tok, vec, vec],
        out_shape=[jax.ShapeDtypeStruct((seq, nh), F32)] * 2 + [jax.ShapeDtypeStruct((1, nh), F32)] * 2,
        compiler_params=_cparams(("arbitrary",)),
    )(b, a, a_log, dt_bias, dbeta, dgc)


LN_TR = 256


def _ln_stats(x, s):
    z = DN_ALPHA * x + s
    mu = jnp.mean(z, -1, keepdims=True)
    zc = z - mu
    var = jnp.mean(zc * zc, -1, keepdims=True)
    rstd = lax.rsqrt(var + LN_EPS)
    return zc * rstd, rstd


def _ln_fwd(x, s, g, b):
    seq, d = x.shape
    tr = min(LN_TR, seq)

    def body(x_ref, s_ref, g_ref, b_ref, o_ref, ob_ref):
        xhat, _ = _ln_stats(x_ref[...], s_ref[...])
        y = xhat * g_ref[...] + b_ref[...]
        o_ref[...] = y
        ob_ref[...] = y.astype(BF16)

    tok = pl.BlockSpec((tr, d), lambda i: (i, 0))
    vec = pl.BlockSpec((1, d), lambda i: (0, 0))
    return pl.pallas_call(
        body, name="ln_fwd", grid=(seq // tr,),
        in_specs=[tok, tok, vec, vec], out_specs=[tok, tok],
        out_shape=[jax.ShapeDtypeStruct((seq, d), F32), jax.ShapeDtypeStruct((seq, d), BF16)],
        compiler_params=_cparams(("parallel",)),
    )(x, s, g.reshape(1, d), b.reshape(1, d))


def _ln_bwd(dy, x, s, g):
    seq, d = x.shape
    tr = min(LN_TR, seq)

    def body(dy_ref, x_ref, s_ref, g_ref, dz_ref, dzb_ref, dg_ref, db_ref):
        @pl.when(pl.program_id(0) == 0)
        def _():
            dg_ref[...] = jnp.zeros_like(dg_ref)
            db_ref[...] = jnp.zeros_like(db_ref)

        dy = dy_ref[...]
        xhat, rstd = _ln_stats(x_ref[...], s_ref[...])
        dyg = dy * g_ref[...]
        m1 = jnp.mean(dyg, -1, keepdims=True)
        m2 = jnp.mean(dyg * xhat, -1, keepdims=True)
        dz = rstd * (dyg - m1 - xhat * m2)
        dz_ref[...] = dz
        dzb_ref[...] = dz.astype(BF16)
        dg_ref[...] += jnp.sum(dy * xhat, axis=0, keepdims=True)
        db_ref[...] += jnp.sum(dy, axis=0, keepdims=True)

    tok = pl.BlockSpec((tr, d), lambda i: (i, 0))
    vec = pl.BlockSpec((1, d), lambda i: (0, 0))
    return pl.pallas_call(
        body, name="ln_bwd", grid=(seq // tr,),
        in_specs=[tok, tok, tok, vec], out_specs=[tok, tok, vec, vec],
        out_shape=[jax.ShapeDtypeStruct((seq, d), F32), jax.ShapeDtypeStruct((seq, d), BF16),
                   jax.ShapeDtypeStruct((1, d), F32), jax.ShapeDtypeStruct((1, d), F32)],
        compiler_params=_cparams(("arbitrary",)),
    )(dy, x, s, g.reshape(1, d))


def _loss_head(y, target):
    seq, d = y.shape
    tr = min(LN_TR, seq)

    def body(y_ref, t_ref, loss_ref, dy_ref):
        @pl.when(pl.program_id(0) == 0)
        def _():
            loss_ref[...] = jnp.zeros_like(loss_ref)

        err = y_ref[...] - t_ref[...]
        dy_ref[...] = err * (1.0 / d)
        part = jnp.sum(jnp.sum(err * err, axis=0, keepdims=True), axis=1, keepdims=True)
        loss_ref[...] += part * (0.5 / d)

    tok = pl.BlockSpec((tr, d), lambda i: (i, 0))
    return pl.pallas_call(
        body, name="loss_head", grid=(seq // tr,),
        in_specs=[tok, tok], out_specs=[pl.BlockSpec((8, 128), lambda i: (0, 0)), tok],
        out_shape=[jax.ShapeDtypeStruct((8, 128), F32), jax.ShapeDtypeStruct((seq, d), F32)],
        compiler_params=_cparams(("arbitrary",)),
    )(y, target)


COMM_MID = 0.8


def _matmul(a, b, *, ta=False, tb=False, b_sharded=False, out_sharded=False, out_dtypes=(F32,), epilogue=None,
            extras=(), tm=1024, tn=1024, tk=2048, name="matmul", comm=None, b_cols=None):
    m, k = (a.shape[1], a.shape[0]) if ta else a.shape
    if b_sharded:
        bk, bn = b.shape[1], N_DEV * b.shape[2]
        shard_w = b.shape[2]
    else:
        bk, bn = b.shape[0], b_cols or b.shape[1]
    n = bk if tb else bn
    assert k == (bn if tb else bk), (a.shape, b.shape)
    tm, tn, tk = min(tm, m), min(tn, n), min(tk, k)
    if b_sharded:
        if tb:
            tk = math.gcd(tk, shard_w)
        else:
            tn = math.gcd(tn, shard_w)
    if out_sharded:
        tn = math.gcd(tn, n // N_DEV)
    assert m % tm == 0 and n % tn == 0 and k % tk == 0, (m, n, k, tm, tn, tk)
    ni, nj, nk = m // tm, n // tn, k // tk
    dims = (((0 if ta else 1,), (1 if tb else 0,)), ((), ()))
    n_ex, n_out = len(extras), len(out_dtypes)
    n_ci = len(comm.ins) if comm else 0
    n_co = len(comm.out_shapes) if comm else 0
    total = ni * nj * nk
    mid_step = min(int(COMM_MID * total), total - 1)

    def body(*refs):
        a_ref, b_ref = refs[0], refs[1]
        ex_refs = refs[2:2 + n_ex]
        ci_refs = refs[2 + n_ex:2 + n_ex + n_ci]
        out_refs = refs[2 + n_ex + n_ci:2 + n_ex + n_ci + n_out]
        co_refs = refs[2 + n_ex + n_ci + n_out:2 + n_ex + n_ci + n_out + n_co]
        scratch = refs[2 + n_ex + n_ci + n_out + n_co:]
        acc, sems = (None, scratch) if nk == 1 else (scratch[0], scratch[1:])
        kk = pl.program_id(2)
        step = (pl.program_id(0) * nj + pl.program_id(1)) * nk + kk

        if comm:
            @pl.when(step == 0)
            def _():
                comm.start(ci_refs, co_refs, *sems)

        prod = lax.dot_general(a_ref[...].astype(BF16), b_ref[...].astype(BF16), dims, preferred_element_type=F32)

        def write(res):
            outs = (res,) if epilogue is None else epilogue(res, *[r[...] for r in ex_refs])
            for o_ref, val in zip(out_refs, outs, strict=True):
                o_ref[...] = val.astype(o_ref.dtype)

        if nk == 1:
            write(prod)
        else:
            @pl.when(kk == 0)
            def _():
                acc[...] = prod

            @pl.when(kk > 0)
            def _():
                acc[...] += prod

            @pl.when(kk == nk - 1)
            def _():
                write(acc[...])

        if comm:
            @pl.when(step == mid_step)
            def _():
                comm.mid(ci_refs, co_refs, *sems)

            @pl.when(step == total - 1)
            def _():
                comm.finish(ci_refs, co_refs, *sems)

    a_spec = pl.BlockSpec((tk, tm), lambda i, j, kk: (kk, i)) if ta else pl.BlockSpec((tm, tk), lambda i, j, kk: (i, kk))
    if b_sharded and tb:
        per = shard_w // tk
        b_spec = pl.BlockSpec((None, tn, tk), lambda i, j, kk: (kk // per, j, kk % per))
    elif b_sharded:
        per = shard_w // tn
        b_spec = pl.BlockSpec((None, tk, tn), lambda i, j, kk: (j // per, kk, j % per))
    elif tb:
        b_spec = pl.BlockSpec((tn, tk), lambda i, j, kk: (j, kk))
    else:
        b_spec = pl.BlockSpec((tk, tn), lambda i, j, kk: (kk, j))
    ex_spec = pl.BlockSpec((tm, tn), lambda i, j, kk: (i, j))
    if out_sharded:
        per_o = n // N_DEV // tn
        o_spec = pl.BlockSpec((None, tm, tn), lambda i, j, kk: (j // per_o, i, j % per_o))
        o_shape = (N_DEV, m, n // N_DEV)
    else:
        o_spec, o_shape = ex_spec, (m, n)
    hbm = pl.BlockSpec(memory_space=pl.ANY)
    outs = pl.pallas_call(
        body, name=name, grid=(ni, nj, nk),
        in_specs=[a_spec, b_spec] + [ex_spec] * n_ex + [hbm] * n_ci,
        out_specs=[o_spec] * n_out + [hbm] * n_co,
        out_shape=[jax.ShapeDtypeStruct(o_shape, dt) for dt in out_dtypes] + (list(comm.out_shapes) if comm else []),
        scratch_shapes=([] if nk == 1 else [pltpu.VMEM((tm, tn), F32)]) + (list(comm.scratch) if comm else []),
        compiler_params=_cparams(("arbitrary",) * 3 if comm else ("parallel", "parallel", "arbitrary")),
    )(a, b, *extras, *(comm.ins if comm else ()))
    return outs[0] if len(outs) == 1 else outs


def _epi_relu2(acc):
    r = jnp.maximum(acc, 0.0)
    return acc, r * r


def _epi_drelu2(acc, pre):
    return (acc * (2.0 * jnp.maximum(pre, 0.0)),)


def _epi_add(scale):
    return lambda acc, other: (acc + scale * other,)


def _adamw(parts, w, m, v, *, rows_per_step, name, layer=None, n_layers=None, into=None):
    n_parts, rows, cols = parts.shape
    tr = min(rows_per_step, rows)
    assert rows % tr == 0

    def body(p_ref, w_ref, m_ref, v_ref, *rest):
        g_ref, d_ref, mo_ref, vo_ref = rest[-4:]
        g = p_ref[0].astype(F32)
        for i in range(1, n_parts):
            g = g + p_ref[i].astype(F32)
        m_new = ADAM_B1 * m_ref[...] + (1.0 - ADAM_B1) * g
        v_new = ADAM_B2 * v_ref[...] + (1.0 - ADAM_B2) * (g * g)
        m_hat = m_new / (1.0 - ADAM_B1 ** ADAM_STEP)
        v_hat = v_new / (1.0 - ADAM_B2 ** ADAM_STEP)
        g_ref[...] = g
        d_ref[...] = -ADAM_LR * (m_hat / (jnp.sqrt(v_hat) + ADAM_EPS) + ADAM_WD * w_ref[...])
        mo_ref[...] = m_new
        vo_ref[...] = v_new

    blk = pl.BlockSpec((tr, cols), lambda i: (i, 0))
    if layer is None:
        out_blk, out_shape = blk, (rows, cols)
    else:
        out_blk, out_shape = pl.BlockSpec((None, tr, cols), lambda i: (layer, i, 0)), (n_layers, rows, cols)
    into = list(into or ())
    return pl.pallas_call(
        body, name=name, grid=(rows // tr,),
        in_specs=[pl.BlockSpec((n_parts, tr, cols), lambda i: (0, i, 0)), blk, blk, blk]
        + [pl.BlockSpec(memory_space=pl.ANY)] * len(into),
        out_specs=[out_blk] * 4,
        out_shape=[jax.ShapeDtypeStruct(out_shape, F32)] * 4,
        input_output_aliases={4 + k: k for k in range(len(into))},
        compiler_params=_cparams(("parallel",)),
    )(parts, w, m, v, *into)


def _position():
    return lax.axis_index("x"), lax.axis_index("y"), lax.axis_index("c")


def _comm_scratch(n):
    return [pltpu.SemaphoreType.DMA((7 * n,)), pltpu.SemaphoreType.DMA((7 * n,)), pltpu.SemaphoreType.DMA((n,))]


class _Gather:
    def __init__(self, blocks):
        self.ins = list(blocks)
        self.out_shapes = [jax.ShapeDtypeStruct((N_DEV,) + b.shape, b.dtype) for b in blocks]
        self.scratch = _comm_scratch(len(blocks))

    def _plan(self, n, ins, outs, send_sems, recv_sems, local_sems):
        x, y, c = _position()
        me, sibling = (x, y, c), (x, y, 1 - c)
        chips = [(1 - x, y), (x, 1 - y), (1 - x, 1 - y)]
        x_ref, out_ref = ins[n], outs[n]

        def slot(px, py, pc):
            return out_ref.at[4 * px + 2 * py + pc]

        def copy(k, blk, to, src=None):
            return pltpu.make_async_remote_copy(
                src_ref=slot(*blk) if src is None else src, dst_ref=slot(*blk),
                send_sem=send_sems.at[7 * n + k], recv_sem=recv_sems.at[7 * n + k], device_id=to, device_id_type=MESH)

        mine = lambda: pltpu.make_async_copy(x_ref, slot(*me), local_sems.at[n])
        first = lambda: [copy(0, me, sibling, src=x_ref)] + [copy(1 + j, me, (*chip, c), src=x_ref)
                                                             for j, chip in enumerate(chips)]
        passed = lambda j: copy(4 + j, (*chips[j], c), sibling)
        landed = lambda j: copy(1 + j, (*chips[j], c), me)
        from_sibling = lambda: [copy(0, sibling, me)] + [copy(4 + j, (*chip, 1 - c), me) for j, chip in enumerate(chips)]
        return mine, first, passed, landed, from_sibling

    def start(self, ins, outs, *sems):
        for n in range(len(self.ins)):
            mine, first, _, _, _ = self._plan(n, ins, outs, *sems)
            mine().start()
            for cp in first():
                cp.start()

    def mid(self, ins, outs, *sems):
        plans = [self._plan(n, ins, outs, *sems) for n in range(len(self.ins))]
        for j in range(3):
            for _, _, passed, landed, _ in plans:
                landed(j).wait_recv()
                passed(j).start()

    def finish(self, ins, outs, *sems):
        for n in range(len(self.ins)):
            mine, first, passed, _, from_sibling = self._plan(n, ins, outs, *sems)
            for cp in from_sibling():
                cp.wait_recv()
            for cp in first() + [passed(j) for j in range(3)]:
                cp.wait_send()
            mine().wait()


class _Exchange:
    def __init__(self, parts):
        self.ins = list(parts)
        self.out_shapes = [jax.ShapeDtypeStruct(p.shape, p.dtype) for p in parts]
        self.scratch = _comm_scratch(len(parts))

    def _plan(self, n, ins, outs, send_sems, recv_sems, local_sems):
        x, y, c = _position()
        me = 4 * x + 2 * y + c
        p_ref, out_ref = ins[n], outs[n]
        mine = lambda: pltpu.make_async_copy(p_ref.at[me], out_ref.at[me], local_sems.at[n])

        def copies(landing):
            out = []
            for k in range(1, N_DEV):
                px = 1 - x if k & 4 else x
                py = 1 - y if k & 2 else y
                pc = 1 - c if k & 1 else c
                peer_slot = 4 * px + 2 * py + pc
                out.append(pltpu.make_async_remote_copy(
                    src_ref=p_ref.at[peer_slot], dst_ref=out_ref.at[peer_slot if landing else me],
                    send_sem=send_sems.at[7 * n + k - 1], recv_sem=recv_sems.at[7 * n + k - 1],
                    device_id=(px, py, pc), device_id_type=MESH))
            return out

        return mine, copies

    def start(self, ins, outs, *sems):
        for n in range(len(self.ins)):
            mine, copies = self._plan(n, ins, outs, *sems)
            mine().start()
            for cp in copies(False):
                cp.start()

    def mid(self, ins, outs, *sems):
        pass

    def finish(self, ins, outs, *sems):
        for n in range(len(self.ins)):
            mine, copies = self._plan(n, ins, outs, *sems)
            for cp in copies(True):
                cp.wait_recv()
            for cp in copies(False):
                cp.wait_send()
            mine().wait()


def _comm_alone(comm, name):
    def body(*refs):
        n_i, n_o = len(comm.ins), len(comm.out_shapes)
        ins, outs, sems = refs[:n_i], refs[n_i:n_i + n_o], refs[n_i + n_o:]
        comm.start(ins, outs, *sems)
        comm.mid(ins, outs, *sems)
        comm.finish(ins, outs, *sems)

    hbm = pl.BlockSpec(memory_space=pl.ANY)
    return pl.pallas_call(
        body, name=name, out_shape=list(comm.out_shapes),
        in_specs=[hbm] * len(comm.ins), out_specs=[hbm] * len(comm.out_shapes),
        scratch_shapes=list(comm.scratch),
    )(*comm.ins)


RET_IN_W = 2 * RET_QK + 2 * RET_VW
GDN_IN_W = GDN_QKV + GDN_VW + 2 * GDN_V_HEADS
GDN_TAIL = 2 * GDN_V_HEADS
TAIL_PAD = 128

SMALL_SIZES = (RET_VW, GDN_V_HEADS, GDN_V_HEADS, GDN_DV, DEPTH * D_MODEL, DEPTH * D_MODEL, DEPTH * D_MODEL,
               DEPTH * D_MODEL, GDN_CONV * GDN_QKV)
SMALL_LANES = 128
SMALL_ROWS = -(-sum(SMALL_SIZES) // (8 * SMALL_LANES)) * 8


def _pack_small(*vecs):
    flat = jnp.concatenate([v.reshape(-1).astype(F32) for v in vecs])
    return jnp.pad(flat, (0, SMALL_ROWS * SMALL_LANES - flat.shape[0])).reshape(SMALL_ROWS, SMALL_LANES)


def _unpack_small(buf, shapes):
    flat, out, at = buf.reshape(-1), [], 0
    for shp in shapes:
        n = int(np.prod(shp))
        out.append(flat[at:at + n].reshape(shp))
        at += n
    return out


def _mlp_bwd(dz, h, a, r, w1, w2, name, exchange_dw2=False):
    dz, dzb = dz
    da = _matmul(dzb, w2, tb=True, out_dtypes=(BF16,), epilogue=_epi_drelu2, extras=(a,), name=name + "_da")
    dw2 = _matmul(r, dzb, ta=True, out_dtypes=(BF16,), name=name + "_dw2").reshape(N_DEV, -1, D_MODEL)
    dw1 = _matmul(h, da, ta=True, out_sharded=True, out_dtypes=(BF16,), name=name + "_dw1")
    dh = _matmul(da, w1, tb=True, b_sharded=True, epilogue=_epi_add(DN_ALPHA), extras=(dz,), name=name + "_dh",
                 comm=_Exchange([dw2]) if exchange_dw2 else None)
    if exchange_dw2:
        dh, dw2 = dh
    return dh, dw1, dw2


def _chunk_cols(t):
    seq, nh = t.shape
    return t.reshape(seq // CHUNK, CHUNK, nh).transpose(0, 2, 1)[..., None]


def _from_chunk_cols(t):
    nc, nh = t.shape[:2]
    return t[..., 0].transpose(0, 2, 1).reshape(nc * CHUNK, nh)


def _chunk_rows(t):
    seq, nh = t.shape
    return t.reshape(seq // CHUNK, CHUNK, nh).transpose(0, 2, 1)[:, :, None, :]


def _from_chunk_rows(t):
    nc, nh = t.shape[:2]
    return t[:, :, 0, :].transpose(0, 2, 1).reshape(nc * CHUNK, nh)


def kernel(x, ret_w_in, ret_gn_g, ret_w_out, gdn_w_in, gdn_conv_w, gdn_a_log, gdn_dt_bias, gdn_norm_g, gdn_w_out, ln_mix_g, ln_mix_b, mlp_w1, mlp_w2, ln_ffn_g, ln_ffn_b, loss_target, m_ret_w_in, m_ret_gn_g, m_ret_w_out, m_gdn_w_in, m_gdn_conv_w, m_gdn_a_log, m_gdn_dt_bias, m_gdn_norm_g, m_gdn_w_out, m_ln_mix_g, m_ln_mix_b, m_mlp_w1, m_mlp_w2, m_ln_ffn_g, m_ln_ffn_b, v_ret_w_in, v_ret_gn_g, v_ret_w_out, v_gdn_w_in, v_gdn_conv_w, v_gdn_a_log, v_gdn_dt_bias, v_gdn_norm_g, v_gdn_w_out, v_ln_mix_g, v_ln_mix_b, v_mlp_w1, v_mlp_w2, v_ln_ffn_g, v_ln_ffn_b):
    xt, target = x[0], loss_target[0]
    seq = xt.shape[0]
    me = 4 * lax.axis_index("x") + 2 * lax.axis_index("y") + lax.axis_index("c")

    bf = lambda t: t.astype(BF16)
    cos, sin = _rope_tables(seq)
    w_ret_in, = _comm_alone(_Gather([bf(ret_w_in[0])]), "gather_ret_in")
    conv_blk = jnp.pad(gdn_conv_w[0], ((0, HALO - GDN_CONV), (0, 0)))
    shard_in = RET_IN_W // N_DEV
    xb = bf(xt)
    p0, w_ret_out, w1_0, conv_all = _matmul(
        xb, w_ret_in, b_sharded=True, tn=shard_in, name="ret_in",
        comm=_Gather([bf(ret_w_out[0]), bf(mlp_w1[0]), conv_blk]))
    w_ret_out = w_ret_out.reshape(RET_VW, D_MODEL)
    conv_w = conv_all[:, :GDN_CONV].transpose(1, 0, 2).reshape(GDN_CONV, GDN_QKV)
    o0, s0, w2_0 = _ret_fwd(p0, cos, sin, ret_gn_g[0], comm=_Gather([bf(mlp_w2[0])]))
    w2_0 = w2_0.reshape(D_FF, D_MODEL)
    mix0 = _matmul(o0, w_ret_out, name="ret_out")
    h1, h1b = _ln_fwd(xt, mix0, ln_mix_g[0], ln_mix_b[0])
    a0, r0, gdn_in_all = _matmul(h1b, w1_0, b_sharded=True, out_dtypes=(F32, BF16), epilogue=_epi_relu2,
                                 name="mlp0_up", comm=_Gather([bf(gdn_w_in[0])]))
    m0, w_gdn_out, w1_1 = _matmul(r0, w2_0, name="mlp0_down", comm=_Gather([bf(gdn_w_out[0]), bf(mlp_w1[1])]))
    w_gdn_out = w_gdn_out.reshape(GDN_VW, D_MODEL)
    h2, h2b = _ln_fwd(h1, m0, ln_ffn_g[0], ln_ffn_b[0])

    w_gdn_in = gdn_in_all.transpose(1, 0, 2).reshape(D_MODEL, GDN_IN_W)
    main_w = GDN_IN_W - GDN_TAIL
    w_gdn_tail = jnp.pad(w_gdn_in[:, main_w:], ((0, 0), (0, TAIL_PAD - GDN_TAIL)))
    p1, w2_1 = _matmul(h2b, w_gdn_in, b_cols=main_w, name="gdn_in", comm=_Gather([bf(mlp_w2[1])]))
    w2_1 = w2_1.reshape(D_FF, D_MODEL)
    pt = _matmul(h2b, w_gdn_tail, name="gdn_in_tail")
    c1 = _conv_fwd(p1, conv_w)
    b_in, a_in = pt[:, :GDN_V_HEADS], pt[:, GDN_V_HEADS:GDN_TAIL]
    beta, gc = _gates_fwd(b_in, a_in, gdn_a_log, gdn_dt_bias)
    beta_c, gc_c, gc_r = _chunk_cols(beta), _chunk_cols(gc), _chunk_rows(gc)
    o1, t1, s1 = _gdn_fwd(c1, p1, beta_c, gc_c, gc_r, gdn_norm_g[0])
    mix1 = _matmul(o1, w_gdn_out, name="gdn_out")
    h3, h3b = _ln_fwd(h2, mix1, ln_mix_g[1], ln_mix_b[1])
    a1, r1 = _matmul(h3b, w1_1, b_sharded=True, out_dtypes=(F32, BF16), epilogue=_epi_relu2, name="mlp1_up")
    m1 = _matmul(r1, w2_1, name="mlp1_down")
    h4, _ = _ln_fwd(h3, m1, ln_ffn_g[1], ln_ffn_b[1])
    loss_blk, dh4 = _loss_head(h4, target)
    loss = lax.psum(loss_blk[0, 0], ("x", "y", "c"))

    dz, dzb, dg_ffn1, db_ffn1 = _ln_bwd(dh4, h3, m1, ln_ffn_g[1])
    dh3, dw1_1, dw2_1 = _mlp_bwd((dz, dzb), h3b, a1, r1, w1_1, w2_1, "mlp1")
    dz, dzb, dg_mix1, db_mix1 = _ln_bwd(dh3, h2, mix1, ln_mix_g[1])
    do1 = _matmul(dzb, w_gdn_out, tb=True, name="gdn_out_do")
    dw_gdn_out = _matmul(o1, dzb, ta=True, out_dtypes=(BF16,), name="gdn_out_dw").reshape(N_DEV, -1, D_MODEL)
    dc1, dp1, dbeta_c, dgc_c, dgc_r, dng, x_w1_1, x_w2_1, x_gdn_out = _gdn_bwd(
        c1, p1, beta_c, gc_c, gc_r, gdn_norm_g[0], t1, s1, do1, comm=_Exchange([dw1_1, dw2_1, dw_gdn_out]))
    dp1, dconv = _conv_bwd(p1, dc1, conv_w, dp1)
    db_in, da_in, dalog, ddt = _gates_bwd(b_in, a_in, gdn_a_log, gdn_dt_bias, _from_chunk_cols(dbeta_c),
                                          _from_chunk_cols(dgc_c) + _from_chunk_rows(dgc_r))
    dpt = jnp.concatenate([db_in, da_in, jnp.zeros((seq, TAIL_PAD - GDN_TAIL), F32)], axis=-1)
    dw_gdn_main = _matmul(h2b, dp1, ta=True, out_dtypes=(BF16,), name="gdn_in_dw")
    dw_gdn_tail = _matmul(h2b, dpt, ta=True, out_dtypes=(BF16,), name="gdn_in_tail_dw")
    dw_gdn_in = jnp.concatenate([dw_gdn_main, dw_gdn_tail[:, :GDN_TAIL]], axis=-1)
    dw_gdn_in = dw_gdn_in.reshape(D_MODEL, N_DEV, GDN_IN_W // N_DEV).transpose(1, 0, 2)
    dh2 = _matmul(dpt, w_gdn_tail, tb=True, epilogue=_epi_add(DN_ALPHA), extras=(dz,), name="gdn_in_tail_dh")
    dh2, x_gdn_in = _matmul(dp1, w_gdn_in, tb=True, b_cols=main_w, epilogue=_epi_add(1.0), extras=(dh2,),
                            name="gdn_in_dh", comm=_Exchange([dw_gdn_in]))

    dz, dzb, dg_ffn0, db_ffn0 = _ln_bwd(dh2, h1, m0, ln_ffn_g[0])
    dh1, dw1_0, x_w2_0 = _mlp_bwd((dz, dzb), h1b, a0, r0, w1_0, w2_0, "mlp0", exchange_dw2=True)
    dz, dzb, dg_mix0, db_mix0 = _ln_bwd(dh1, xt, mix0, ln_mix_g[0])
    do0 = _matmul(dzb, w_ret_out, tb=True, name="ret_out_do")
    dw_ret_out = _matmul(o0, dzb, ta=True, out_dtypes=(BF16,), name="ret_out_dw").reshape(N_DEV, -1, D_MODEL)
    dp0, dgng, x_w1_0 = _ret_bwd(p0, cos, sin, ret_gn_g[0], s0, do0, comm=_Exchange([dw1_0]))
    dw_ret_in, x_ret_out = _matmul(xb, dp0, ta=True, out_sharded=True, out_dtypes=(BF16,), tn=shard_in,
                                   name="ret_in_dw", comm=_Exchange([dw_ret_out]))
    dx, x_ret_in = _matmul(dp0, w_ret_in, tb=True, b_sharded=True, epilogue=_epi_add(DN_ALPHA), extras=(dz,),
                           tk=shard_in, name="ret_in_dx", comm=_Exchange([dw_ret_in]))

    def update(parts, w, m, v, name, **slab):
        shape = parts.shape[1:]
        outs = _adamw(parts, w.reshape(shape), m.reshape(shape), v.reshape(shape), rows_per_step=128, name=name,
                      **slab)
        return outs if slab else [t.reshape(w.shape) for t in outs]

    u_w1 = update(x_w1_1, mlp_w1[1], m_mlp_w1[1], v_mlp_w1[1], "adamw_w1_1", layer=1, n_layers=DEPTH)
    u_w1 = update(x_w1_0, mlp_w1[0], m_mlp_w1[0], v_mlp_w1[0], "adamw_w1_0", layer=0, n_layers=DEPTH, into=u_w1)
    u_w2 = update(x_w2_1, mlp_w2[1], m_mlp_w2[1], v_mlp_w2[1], "adamw_w2_1", layer=1, n_layers=DEPTH)
    u_w2 = update(x_w2_0, mlp_w2[0], m_mlp_w2[0], v_mlp_w2[0], "adamw_w2_0", layer=0, n_layers=DEPTH, into=u_w2)
    big_out = list(zip(
        update(x_ret_in, ret_w_in, m_ret_w_in, v_ret_w_in, "adamw_ret_in"),
        update(x_ret_out, ret_w_out, m_ret_w_out, v_ret_w_out, "adamw_ret_out"),
        update(x_gdn_in, gdn_w_in, m_gdn_w_in, v_gdn_w_in, "adamw_gdn_in"),
        update(x_gdn_out, gdn_w_out, m_gdn_w_out, v_gdn_w_out, "adamw_gdn_out"),
        u_w1, u_w2))

    small_w = (ret_gn_g, gdn_a_log, gdn_dt_bias, gdn_norm_g, ln_mix_g, ln_mix_b, ln_ffn_g, ln_ffn_b)
    small_m = (m_ret_gn_g, m_gdn_a_log, m_gdn_dt_bias, m_gdn_norm_g, m_ln_mix_g, m_ln_mix_b, m_ln_ffn_g, m_ln_ffn_b)
    small_v = (v_ret_gn_g, v_gdn_a_log, v_gdn_dt_bias, v_gdn_norm_g, v_ln_mix_g, v_ln_mix_b, v_ln_ffn_g, v_ln_ffn_b)
    small_g = (dgng, dalog, ddt, jnp.sum(dng, axis=0),
               jnp.concatenate([dg_mix0, dg_mix1]), jnp.concatenate([db_mix0, db_mix1]),
               jnp.concatenate([dg_ffn0, dg_ffn1]), jnp.concatenate([db_ffn0, db_ffn1]), dconv)
    small_parts, = _comm_alone(_Gather([_pack_small(*small_g)]), "gather_small_grads")
    zero_conv = jnp.zeros((GDN_CONV, GDN_QKV), F32)
    small_out = _adamw(small_parts, _pack_small(*small_w, zero_conv), _pack_small(*small_m, zero_conv),
                       _pack_small(*small_v, zero_conv), rows_per_step=SMALL_ROWS, name="adamw_small")
    shapes = [t.shape for t in small_w] + [(GDN_CONV, GDN_QKV)]
    small_out = [_unpack_small(t, shapes) for t in small_out]
    conv_g = lax.dynamic_slice(small_out[0][-1], (0, me * (GDN_QKV // N_DEV)), (GDN_CONV, GDN_QKV // N_DEV))
    conv_out = _adamw(conv_g[None], gdn_conv_w[0], m_gdn_conv_w[0], v_gdn_conv_w[0],
                      rows_per_step=GDN_CONV, name="adamw_conv")

    def ordered(kind):
        b, s, cv = big_out[kind], small_out[kind], conv_out[kind][None]
        return [b[0], s[0], b[1], b[2], cv, s[1], s[2], s[3], b[3], s[4], s[5], b[4], b[5], s[6], s[7]]

    return (loss, dx[None], *ordered(0), *ordered(1), *ordered(2), *ordered(3))
```

```python
import functools
import math

import jax
import jax.numpy as jnp
import numpy as np
from jax import lax
from jax.experimental import pallas as pl
from jax.experimental.pallas import tpu as pltpu

F32 = jnp.float32
BF16 = jnp.bfloat16

N_DEV = 8
D_MODEL = 2048
CHUNK = 64
RET_HEADS = 8
RET_DK = 256
RET_DV = 512
RET_QK = RET_HEADS * RET_DK
RET_VW = RET_HEADS * RET_DV
ROPE_BASE = 10000.0
GN_EPS = 1e-6
GDN_K_HEADS = 16
GDN_V_HEADS = 32
GDN_DK = 128
GDN_DV = 128
GDN_QK = GDN_K_HEADS * GDN_DK
GDN_VW = GDN_V_HEADS * GDN_DV
GDN_QKV = 2 * GDN_QK + GDN_VW
GDN_CONV = 4
RMS_EPS = 1e-6
L2_EPS = 1e-6
D_FF = 4 * D_MODEL
DEPTH = 2
DN_ALPHA = (2.0 * DEPTH) ** 0.25
LN_EPS = 1e-5
ADAM_LR = 0.001
ADAM_B1 = 0.9
ADAM_B2 = 0.999
ADAM_EPS = 1e-08
ADAM_WD = 0.01
ADAM_STEP = 10

VMEM_LIMIT = 56 * 1024 * 1024
MESH = pl.DeviceIdType.MESH


def _cparams(sem=None):
    return pltpu.CompilerParams(dimension_semantics=sem, vmem_limit_bytes=VMEM_LIMIT)


_NT = (((2,), (2,)), ((0,), (0,)))
_NN = (((2,), (1,)), ((0,), (0,)))
_TN = (((1,), (1,)), ((0,), (0,)))


def _dg(a, b, dims):
    return lax.dot_general(a.astype(BF16), b.astype(BF16), dims, preferred_element_type=F32)


@jax.custom_vjp
def _nt(a, b):
    return _dg(a, b, _NT)


@jax.custom_vjp
def _nn(a, b):
    return _dg(a, b, _NN)


@jax.custom_vjp
def _tn(a, b):
    return _dg(a, b, _TN)


_nt.defvjp(lambda a, b: (_dg(a, b, _NT), (a, b)), lambda r, g: (_nn(g, r[1]), _tn(g, r[0])))
_nn.defvjp(lambda a, b: (_dg(a, b, _NN), (a, b)), lambda r, g: (_nt(g, r[1]), _tn(r[0], g)))
_tn.defvjp(lambda a, b: (_dg(a, b, _TN), (a, b)), lambda r, g: (_nt(r[1], g), _nn(r[0], g)))


def _iota2(shape, dim):
    return lax.broadcasted_iota(jnp.int32, shape, dim)


def _inv_unit_lower(a):
    c = a.shape[-1]
    eye = (_iota2((c, c), 0) == _iota2((c, c), 1)).astype(F32)
    m = -a
    p = eye + m
    for _ in range(int(math.log2(c)) - 1):
        m = _dg(m, m, _NN)
        p = p + _dg(p, m, _NN)
    return p


def _silu(x):
    return x * jax.nn.sigmoid(x)


def _rep2(t):
    h = t.shape[0]
    return jnp.broadcast_to(t[:, None], (h, 2) + t.shape[1:]).reshape((2 * h,) + t.shape[1:])


def _ret_chunk(q1, q2, k1, k2, v, gate, gn_g, s, cos, sin, intra, qdec, kdec, cdec):
    q = jnp.concatenate([q1 * cos - q2 * sin, q1 * sin + q2 * cos], axis=-1)
    k = jnp.concatenate([k1 * cos - k2 * sin, k1 * sin + k2 * cos], axis=-1) * (RET_DK ** -0.5)
    scores = _nt(q, k) * intra
    y = _nn(scores, v) + _nn(q * qdec, s)
    s_new = s * cdec + _tn(k * kdec, v)
    mu = jnp.mean(y, -1, keepdims=True)
    yc = y - mu
    var = jnp.mean(yc * yc, -1, keepdims=True)
    o = _silu(gate) * (yc * lax.rsqrt(var + GN_EPS) * gn_g)
    return o, s_new


def _ret_consts():
    log_gamma = np.log1p(-np.exp2(-5.0 - np.arange(RET_HEADS, dtype=np.float64)))
    idx = np.arange(CHUNK, dtype=np.float64)
    lg = log_gamma[:, None]
    intra = np.exp(lg[..., None] * np.abs(idx[:, None] - idx[None, :]))
    qdec = np.exp(lg * (idx + 1.0))[..., None]
    kdec = np.exp(lg * (CHUNK - 1.0 - idx))[..., None]
    cdec = np.exp(log_gamma * CHUNK)[:, None, None]
    return [jnp.asarray(t, F32) for t in (intra, qdec, kdec, cdec)]


def _rope_tables(seq):
    half = RET_DK // 2
    inv = ROPE_BASE ** (-jnp.arange(half, dtype=F32) / half)
    ang = jnp.arange(seq).astype(F32)[:, None] * inv[None, :]
    return jnp.cos(ang), jnp.sin(ang)


RET_HB = 8


def _ret_load(q_ref, k_ref, v_ref, gate_ref):
    hb, dk, dv, h = RET_HB, RET_DK, RET_DV, RET_DK // 2
    q, k, v, gate = q_ref[...], k_ref[...], v_ref[...], gate_ref[...]
    q1 = jnp.stack([q[:, i * dk:i * dk + h] for i in range(hb)])
    q2 = jnp.stack([q[:, i * dk + h:(i + 1) * dk] for i in range(hb)])
    k1 = jnp.stack([k[:, i * dk:i * dk + h] for i in range(hb)])
    k2 = jnp.stack([k[:, i * dk + h:(i + 1) * dk] for i in range(hb)])
    vs = jnp.stack([v[:, i * dv:(i + 1) * dv] for i in range(hb)])
    gs = jnp.stack([gate[:, i * dv:(i + 1) * dv] for i in range(hb)])
    return q1, q2, k1, k2, vs, gs


def _ret_specs(n_chunks, rev):
    hb = RET_HB
    cidx = (lambda n: n_chunks - 1 - n) if rev else (lambda n: n)
    qw, vw = hb * RET_DK, hb * RET_DV
    tok = [
        pl.BlockSpec((CHUNK, qw), lambda h, n: (cidx(n), h)),
        pl.BlockSpec((CHUNK, qw), lambda h, n: (cidx(n), RET_QK // qw + h)),
        pl.BlockSpec((CHUNK, vw), lambda h, n: (cidx(n), 2 * RET_QK // vw + h)),
        pl.BlockSpec((CHUNK, vw), lambda h, n: (cidx(n), (2 * RET_QK + RET_VW) // vw + h)),
        pl.BlockSpec((CHUNK, RET_DK // 2), lambda h, n: (cidx(n), 0)),
        pl.BlockSpec((CHUNK, RET_DK // 2), lambda h, n: (cidx(n), 0)),
    ]
    const = [
        pl.BlockSpec((hb, CHUNK, CHUNK), lambda h, n: (h, 0, 0)),
        pl.BlockSpec((hb, CHUNK, 1), lambda h, n: (h, 0, 0)),
        pl.BlockSpec((hb, CHUNK, 1), lambda h, n: (h, 0, 0)),
        pl.BlockSpec((hb, 1, 1), lambda h, n: (h, 0, 0)),
        pl.BlockSpec((hb, 1, RET_DV), lambda h, n: (h, 0, 0)),
    ]
    state = pl.BlockSpec((1, hb, RET_DK, RET_DV), lambda h, n: (cidx(n), h, 0, 0))
    return tok, const, state, cidx


def _grid_call(body, *, name, grid, in_specs, out_specs, out_shape, scratch_shapes, args, comm=None):
    if comm is None:
        return pl.pallas_call(body, name=name, grid=grid, in_specs=in_specs, out_specs=out_specs, out_shape=out_shape,
                              scratch_shapes=scratch_shapes,
                              compiler_params=_cparams(("parallel", "arbitrary")))(*args)
    n_in, n_out, n_scr = len(in_specs), len(out_specs), len(scratch_shapes)
    n_ci, n_co = len(comm.ins), len(comm.out_shapes)
    total = grid[0] * grid[1]
    mid_step = min(int(COMM_MID * total), total - 1)

    def carrying(*refs):
        ins, ci = refs[:n_in], refs[n_in:n_in + n_ci]
        at = n_in + n_ci
        outs, co = refs[at:at + n_out], refs[at + n_out:at + n_out + n_co]
        at += n_out + n_co
        scr, sems = refs[at:at + n_scr], refs[at + n_scr:]
        step = pl.program_id(0) * grid[1] + pl.program_id(1)
        pl.when(step == 0)(lambda: comm.start(ci, co, *sems))
        body(*ins, *outs, *scr)
        pl.when(step == mid_step)(lambda: comm.mid(ci, co, *sems))
        pl.when(step == total - 1)(lambda: comm.finish(ci, co, *sems))

    hbm = pl.BlockSpec(memory_space=pl.ANY)
    return pl.pallas_call(
        carrying, name=name, grid=grid,
        in_specs=list(in_specs) + [hbm] * n_ci, out_specs=list(out_specs) + [hbm] * n_co,
        out_shape=list(out_shape) + list(comm.out_shapes),
        scratch_shapes=list(scratch_shapes) + list(comm.scratch),
        compiler_params=_cparams(("arbitrary", "arbitrary")))(*args, *comm.ins)


def _ret_fwd(p, cos, sin, gn_g, comm=None):
    seq = p.shape[0]
    nc = seq // CHUNK
    hb = RET_HB
    tok, const, state, _ = _ret_specs(nc, False)

    def body(q_ref, k_ref, v_ref, gate_ref, cos_ref, sin_ref, intra_ref, qdec_ref, kdec_ref, cdec_ref, gng_ref,
             o_ref, ssave_ref, s_scr):
        @pl.when(pl.program_id(1) == 0)
        def _():
            s_scr[...] = jnp.zeros_like(s_scr)

        q1, q2, k1, k2, v, gate = _ret_load(q_ref, k_ref, v_ref, gate_ref)
        s = s_scr[...]
        ssave_ref[0] = s.astype(BF16)
        o, s_new = _ret_chunk(q1, q2, k1, k2, v, gate, gng_ref[...], s, cos_ref[...], sin_ref[...],
                              intra_ref[...], qdec_ref[...], kdec_ref[...], cdec_ref[...])
        s_scr[...] = s_new
        o_ref[...] = jnp.concatenate([o[i] for i in range(hb)], axis=-1).astype(o_ref.dtype)

    return _grid_call(
        body, name="ret_fwd", comm=comm,
        grid=(RET_HEADS // hb, nc),
        in_specs=tok + const,
        out_specs=[pl.BlockSpec((CHUNK, hb * RET_DV), lambda h, n: (n, h)), state],
        out_shape=[jax.ShapeDtypeStruct((seq, RET_VW), BF16),
                   jax.ShapeDtypeStruct((nc, RET_HEADS, RET_DK, RET_DV), BF16)],
        scratch_shapes=[pltpu.VMEM((hb, RET_DK, RET_DV), F32)],
        args=(p, p, p, p, cos, sin, *_ret_consts(), gn_g.reshape(RET_HEADS, 1, RET_DV)))


def _ret_bwd(p, cos, sin, gn_g, ssave, do, comm=None):
    seq = p.shape[0]
    nc = seq // CHUNK
    hb = RET_HB
    tok, const, state, cidx = _ret_specs(nc, True)

    assert hb == RET_HEADS

    def body(q_ref, k_ref, v_ref, gate_ref, cos_ref, sin_ref, intra_ref, qdec_ref, kdec_ref, cdec_ref, gng_ref,
             ssave_ref, do_ref, dp_ref, dgng_ref, ds_scr):
        @pl.when(pl.program_id(1) == 0)
        def _():
            ds_scr[...] = jnp.zeros_like(ds_scr)
            dgng_ref[...] = jnp.zeros_like(dgng_ref)

        q1, q2, k1, k2, v, gate = _ret_load(q_ref, k_ref, v_ref, gate_ref)
        do = do_ref[...]
        dos = jnp.stack([do[:, i * RET_DV:(i + 1) * RET_DV] for i in range(hb)]).astype(F32)
        fn = functools.partial(_ret_chunk, cos=cos_ref[...], sin=sin_ref[...], intra=intra_ref[...],
                               qdec=qdec_ref[...], kdec=kdec_ref[...], cdec=cdec_ref[...])
        _, vjp = jax.vjp(fn, q1, q2, k1, k2, v, gate, gng_ref[...], ssave_ref[0].astype(F32))
        dq1, dq2, dk1, dk2, dv, dgate, dgng, ds = vjp((dos, ds_scr[...]))
        ds_scr[...] = ds
        dgng_ref[...] += dgng
        pieces = ([t[i] for i in range(hb) for t in (dq1, dq2)] + [t[i] for i in range(hb) for t in (dk1, dk2)]
                  + [dv[i] for i in range(hb)] + [dgate[i] for i in range(hb)])
        dp_ref[...] = jnp.concatenate([t.astype(dp_ref.dtype) for t in pieces], axis=-1)

    vw = hb * RET_DV
    width = 2 * RET_QK + 2 * RET_VW
    return _grid_call(
        body, name="ret_bwd", comm=comm,
        grid=(RET_HEADS // hb, nc),
        in_specs=tok + const + [state, pl.BlockSpec((CHUNK, vw), lambda h, n: (cidx(n), h))],
        out_specs=[pl.BlockSpec((CHUNK, width), lambda h, n: (cidx(n), 0)),
                   pl.BlockSpec((hb, 1, RET_DV), lambda h, n: (h, 0, 0))],
        out_shape=[jax.ShapeDtypeStruct((seq, width), BF16),
                   jax.ShapeDtypeStruct((RET_HEADS, 1, RET_DV), F32)],
        scratch_shapes=[pltpu.VMEM((hb, RET_DK, RET_DV), F32)],
        args=(p, p, p, p, cos, sin, *_ret_consts(), gn_g.reshape(RET_HEADS, 1, RET_DV), ssave, do))


def _gdn_common(qr, kr, gc_c, gc_r):
    qn = qr * lax.rsqrt(jnp.sum(qr * qr, -1, keepdims=True) + L2_EPS) * (GDN_DK ** -0.5)
    kn = kr * lax.rsqrt(jnp.sum(kr * kr, -1, keepdims=True) + L2_EPS)
    causal = _iota2((CHUNK, CHUNK), 0) >= _iota2((CHUNK, CHUNK), 1)
    decay = jnp.exp(jnp.where(causal, gc_c - gc_r, -1e30))
    return _rep2(qn), _rep2(kn), decay


def _gdn_a(k, decay, beta_c):
    strict = _iota2((CHUNK, CHUNK), 0) > _iota2((CHUNK, CHUNK), 1)
    return jnp.where(strict, _nt(k * beta_c, k) * decay, 0.0)


@jax.custom_vjp
def _inv_saved(a, t):
    return t


_inv_saved.defvjp(lambda a, t: (t, t),
                  lambda t, dt: (-_dg(_dg(t, dt, _TN), t, _NT), jnp.zeros_like(t)))


def _gdn_chunk(qr, kr, v, z, beta_c, gc_c, gc_r, norm_g, t_saved, s):
    q, k, decay = _gdn_common(qr, kr, gc_c, gc_r)
    t = _inv_saved(_gdn_a(k, decay, beta_c), t_saved)
    return _gdn_rest(q, k, decay, v, z, beta_c, gc_c, norm_g, t, s)


def _gdn_rest(q, k, decay, v, z, beta_c, gc_c, norm_g, t, s):
    eg = jnp.exp(gc_c)
    u = _nn(t, v * beta_c)
    w = _nn(t, k * (beta_c * eg))
    attn = _nt(q, k) * decay
    v_new = u - _nn(w, s)
    y = _nn(q * eg, s) + _nn(attn, v_new)
    last = _iota2((1, CHUNK, 1), 1) == CHUNK - 1
    gl = jnp.sum(jnp.where(last, gc_c, 0.0), axis=1, keepdims=True)
    s_new = s * jnp.exp(gl) + _tn(k * jnp.exp(gl - gc_c), v_new)
    yn = y * lax.rsqrt(jnp.mean(y * y, -1, keepdims=True) + RMS_EPS) * norm_g
    return yn * _silu(z), s_new


GDN_HK = 16


def _gdn_load(q_ref, k_ref, v_ref, z_ref):
    hk, hb, d = GDN_HK, 2 * GDN_HK, GDN_DK
    q, k, v, z = q_ref[...], k_ref[...], v_ref[...], z_ref[...]
    qs = jnp.stack([q[:, i * d:(i + 1) * d] for i in range(hk)])
    ks = jnp.stack([k[:, i * d:(i + 1) * d] for i in range(hk)])
    vs = jnp.stack([v[:, i * d:(i + 1) * d] for i in range(hb)])
    zs = jnp.stack([z[:, i * d:(i + 1) * d] for i in range(hb)])
    return qs, ks, vs, zs


def _gdn_specs(n_chunks, rev):
    hk, hb = GDN_HK, 2 * GDN_HK
    cidx = (lambda n: n_chunks - 1 - n) if rev else (lambda n: n)
    qw, vw = hk * GDN_DK, hb * GDN_DV
    tok = [
        pl.BlockSpec((CHUNK, qw), lambda h, n: (cidx(n), h)),
        pl.BlockSpec((CHUNK, qw), lambda h, n: (cidx(n), GDN_QK // qw + h)),
        pl.BlockSpec((CHUNK, vw), lambda h, n: (cidx(n), 2 * GDN_QK // vw + h)),
        pl.BlockSpec((CHUNK, vw), lambda h, n: (cidx(n), GDN_QKV // vw + h)),
        pl.BlockSpec((1, hb, CHUNK, 1), lambda h, n: (cidx(n), h, 0, 0)),
        pl.BlockSpec((1, hb, CHUNK, 1), lambda h, n: (cidx(n), h, 0, 0)),
        pl.BlockSpec((1, hb, 1, CHUNK), lambda h, n: (cidx(n), h, 0, 0)),
        pl.BlockSpec((1, GDN_DV), lambda h, n: (0, 0)),
    ]
    tsave = pl.BlockSpec((1, hb, CHUNK, CHUNK), lambda h, n: (cidx(n), h, 0, 0))
    ssave = pl.BlockSpec((1, hb, GDN_DK, GDN_DV), lambda h, n: (cidx(n), h, 0, 0))
    return tok, tsave, ssave, cidx


def _gdn_fwd(c, p, beta_c, gc_c, gc_r, norm_g):
    seq = c.shape[0]
    nc = seq // CHUNK
    hk, hb = GDN_HK, 2 * GDN_HK
    tok, tsave, ssave, _ = _gdn_specs(nc, False)

    def body(q_ref, k_ref, v_ref, z_ref, beta_ref, gcc_ref, gcr_ref, ng_ref, o_ref, tsave_ref, ssave_ref, s_scr):
        @pl.when(pl.program_id(1) == 0)
        def _():
            s_scr[...] = jnp.zeros_like(s_scr)

        qr, kr, v, z = _gdn_load(q_ref, k_ref, v_ref, z_ref)
        beta, gcc, gcr = beta_ref[0], gcc_ref[0], gcr_ref[0]
        s = s_scr[...]
        ssave_ref[0] = s.astype(BF16)
        q, k, decay = _gdn_common(qr, kr, gcc, gcr)
        t = _inv_unit_lower(_gdn_a(k, decay, beta))
        tsave_ref[0] = t.astype(BF16)
        o, s_new = _gdn_rest(q, k, decay, v, z, beta, gcc, ng_ref[...], t, s)
        s_scr[...] = s_new
        o_ref[...] = jnp.concatenate([o[i] for i in range(hb)], axis=-1).astype(o_ref.dtype)

    return pl.pallas_call(
        body, name="gdn_fwd",
        grid=(GDN_K_HEADS // hk, nc),
        in_specs=tok,
        out_specs=[pl.BlockSpec((CHUNK, hb * GDN_DV), lambda h, n: (n, h)), tsave, ssave],
        out_shape=[jax.ShapeDtypeStruct((seq, GDN_VW), BF16),
                   jax.ShapeDtypeStruct((nc, GDN_V_HEADS, CHUNK, CHUNK), BF16),
                   jax.ShapeDtypeStruct((nc, GDN_V_HEADS, GDN_DK, GDN_DV), BF16)],
        scratch_shapes=[pltpu.VMEM((hb, GDN_DK, GDN_DV), F32)],
        compiler_params=_cparams(("parallel", "arbitrary")),
    )(c, c, c, p, beta_c, gc_c, gc_r, norm_g.reshape(1, GDN_DV))


def _gdn_bwd(c, p, beta_c, gc_c, gc_r, norm_g, tsave, ssave, do, comm=None):
    seq = c.shape[0]
    nc = seq // CHUNK
    hk, hb = GDN_HK, 2 * GDN_HK
    nhb = GDN_K_HEADS // hk
    tok, tsave_spec, ssave_spec, cidx = _gdn_specs(nc, True)

    assert hk == GDN_K_HEADS

    def body(q_ref, k_ref, v_ref, z_ref, beta_ref, gcc_ref, gcr_ref, ng_ref, t_ref, s_ref, do_ref,
             dc_ref, dz_ref, dbeta_ref, dgcc_ref, dgcr_ref, dng_ref, ds_scr):
        @pl.when(pl.program_id(1) == 0)
        def _():
            ds_scr[...] = jnp.zeros_like(ds_scr)
            dng_ref[...] = jnp.zeros_like(dng_ref)

        qr, kr, v, z = _gdn_load(q_ref, k_ref, v_ref, z_ref)
        beta, gcc, gcr = beta_ref[0], gcc_ref[0], gcr_ref[0]
        do = do_ref[...]
        dos = jnp.stack([do[:, i * GDN_DV:(i + 1) * GDN_DV] for i in range(hb)]).astype(F32)
        _, vjp = jax.vjp(_gdn_chunk, qr, kr, v, z, beta, gcc, gcr, ng_ref[...], t_ref[0].astype(F32),
                         s_ref[0].astype(F32))
        dqr, dkr, dv, dz, dbeta, dgcc, dgcr, dng, _, ds = vjp((dos, ds_scr[...]))
        ds_scr[...] = ds
        dng_ref[...] += dng[None]
        dc_ref[...] = jnp.concatenate([dqr[i] for i in range(hk)] + [dkr[i] for i in range(hk)]
                                      + [dv[i] for i in range(hb)], axis=-1)
        dz_ref[...] = jnp.concatenate([dz[i] for i in range(hb)], axis=-1).astype(dz_ref.dtype)
        dbeta_ref[0] = dbeta
        dgcc_ref[0] = dgcc
        dgcr_ref[0] = dgcr

    qw, vw = hk * GDN_DK, hb * GDN_DV
    col = pl.BlockSpec((1, hb, CHUNK, 1), lambda h, n: (cidx(n), h, 0, 0))
    row = pl.BlockSpec((1, hb, 1, CHUNK), lambda h, n: (cidx(n), h, 0, 0))
    return _grid_call(
        body, name="gdn_bwd", comm=comm,
        grid=(nhb, nc),
        in_specs=tok + [tsave_spec, ssave_spec, pl.BlockSpec((CHUNK, vw), lambda h, n: (cidx(n), h))],
        out_specs=[pl.BlockSpec((CHUNK, GDN_QKV), lambda h, n: (cidx(n), 0)),
                   pl.BlockSpec((CHUNK, vw), lambda h, n: (cidx(n), GDN_QKV // vw)),
                   col, col, row,
                   pl.BlockSpec((1, 1, GDN_DV), lambda h, n: (h, 0, 0))],
        out_shape=[jax.ShapeDtypeStruct((seq, GDN_QKV), F32), jax.ShapeDtypeStruct((seq, GDN_QKV + GDN_VW), BF16),
                   jax.ShapeDtypeStruct((nc, GDN_V_HEADS, CHUNK, 1), F32),
                   jax.ShapeDtypeStruct((nc, GDN_V_HEADS, CHUNK, 1), F32),
                   jax.ShapeDtypeStruct((nc, GDN_V_HEADS, 1, CHUNK), F32),
                   jax.ShapeDtypeStruct((nhb, 1, GDN_DV), F32)],
        scratch_shapes=[pltpu.VMEM((hb, GDN_DK, GDN_DV), F32)],
        args=(c, c, c, p, beta_c, gc_c, gc_r, norm_g.reshape(1, GDN_DV), tsave, ssave, do))


CONV_TB = 512
CONV_CB = 1024
HALO = 8


def _conv_taps(ext, w):
    acc = w[GDN_CONV - 1:GDN_CONV] * ext
    for j in range(GDN_CONV - 1):
        acc = acc + w[j:j + 1] * pltpu.roll(ext, GDN_CONV - 1 - j, 0)
    return acc


def _conv_fwd(p, w):
    seq = p.shape[0]
    tb, cb = min(CONV_TB, seq), CONV_CB

    def body(prev_ref, cur_ref, w_ref, o_ref):
        first = pl.program_id(1) == 0
        prev = jnp.where(first, 0.0, prev_ref[...])
        ext = jnp.concatenate([prev, cur_ref[...]], axis=0)
        o_ref[...] = _silu(_conv_taps(ext, w_ref[...])[HALO:])

    return pl.pallas_call(
        body, name="conv_fwd",
        grid=(GDN_QKV // cb, seq // tb),
        in_specs=[pl.BlockSpec((HALO, cb), lambda j, i: (jnp.maximum(i * (tb // HALO) - 1, 0), j)),
                  pl.BlockSpec((tb, cb), lambda j, i: (i, j)),
                  pl.BlockSpec((GDN_CONV, cb), lambda j, i: (0, j))],
        out_specs=pl.BlockSpec((tb, cb), lambda j, i: (i, j)),
        out_shape=jax.ShapeDtypeStruct((seq, GDN_QKV), F32),
        compiler_params=_cparams(("parallel", "arbitrary")),
    )(p, p, w)


def _conv_bwd(p, dc, w, dp):
    seq = p.shape[0]
    tb, cb = min(CONV_TB, seq), CONV_CB
    nt = seq // tb
    last_halo = seq // HALO - 1

    def body(prev_ref, cur_ref, next_ref, dcur_ref, dnext_ref, w_ref, _, du_ref, dw_ref):
        i = pl.program_id(1)

        @pl.when(i == 0)
        def _():
            dw_ref[...] = jnp.zeros_like(dw_ref)

        w = w_ref[...]
        prev = jnp.where(i == 0, 0.0, prev_ref[...])
        ext = jnp.concatenate([prev, cur_ref[...], next_ref[...]], axis=0)
        pre = _conv_taps(ext, w)
        dnext = jnp.where(i == nt - 1, 0.0, dnext_ref[...])
        dext = jnp.concatenate([jnp.zeros((HALO, cb), F32), dcur_ref[...], dnext], axis=0)
        sig = jax.nn.sigmoid(pre)
        dpre = dext * (sig * (1.0 + pre * (1.0 - sig)))
        rows = tb + 2 * HALO
        du = w[GDN_CONV - 1:GDN_CONV] * dpre
        for j in range(GDN_CONV - 1):
            du = du + w[j:j + 1] * pltpu.roll(dpre, rows - (GDN_CONV - 1 - j), 0)
        du_ref[...] = du[HALO:HALO + tb].astype(du_ref.dtype)
        dcore = dpre[HALO:HALO + tb]
        dws = []
        for j in range(GDN_CONV):
            sh = ext if j == GDN_CONV - 1 else pltpu.roll(ext, GDN_CONV - 1 - j, 0)
            dws.append(jnp.sum(dcore * sh[HALO:HALO + tb], axis=0, keepdims=True))
        dw_ref[...] += jnp.concatenate(dws, axis=0)

    hb = tb // HALO
    cur = pl.BlockSpec((tb, cb), lambda j, i: (i, j))
    nxt = pl.BlockSpec((HALO, cb), lambda j, i: (jnp.minimum((i + 1) * hb, last_halo), j))
    return pl.pallas_call(
        body, name="conv_bwd",
        grid=(GDN_QKV // cb, nt),
        in_specs=[pl.BlockSpec((HALO, cb), lambda j, i: (jnp.maximum(i * hb - 1, 0), j)), cur, nxt, cur, nxt,
                  pl.BlockSpec((GDN_CONV, cb), lambda j, i: (0, j)), pl.BlockSpec(memory_space=pl.ANY)],
        out_specs=[cur, pl.BlockSpec((GDN_CONV, cb), lambda j, i: (0, j))],
        out_shape=[jax.ShapeDtypeStruct(dp.shape, dp.dtype), jax.ShapeDtypeStruct((GDN_CONV, GDN_QKV), F32)],
        input_output_aliases={6: 0},
        compiler_params=_cparams(("parallel", "arbitrary")),
    )(p, p, p, dc, dc, w, dp)


GATE_TB = 512


def _split3(g):
    hi = g.astype(BF16)
    r = g - hi.astype(F32)
    mid = r.astype(BF16)
    lo = (r - mid.astype(F32)).astype(BF16)
    return hi, mid, lo


def _tri_chunks(n, upper):
    i, j = _iota2((n, n), 0), _iota2((n, n), 1)
    tri = (i <= j) if upper else (i >= j)
    return jnp.where(tri & ((i // CHUNK) == (j // CHUNK)), 1.0, 0.0).astype(BF16)


def _tri_apply(g, upper):
    tri = _tri_chunks(g.shape[0], upper)
    return sum(jnp.dot(tri, part, preferred_element_type=F32) for part in _split3(g))


@jax.custom_vjp
def _chunk_cumsum(g):
    return _tri_apply(g, False)


_chunk_cumsum.defvjp(lambda g: (_tri_apply(g, False), None), lambda _, d: (_tri_apply(d, True),))


def _gates(b, a, a_log, dt_bias):
    z = a + dt_bias
    softplus = jnp.maximum(z, 0.0) + jnp.log1p(jnp.exp(-jnp.abs(z)))
    g = -jnp.exp(a_log) * softplus
    return jax.nn.sigmoid(b), _chunk_cumsum(g)


def _gates_fwd(b, a, a_log, dt_bias):
    seq, nh = b.shape
    tb = min(GATE_TB, seq)

    def body(b_ref, a_ref, al_ref, dt_ref, beta_ref, gc_ref):
        beta, gc = _gates(b_ref[...], a_ref[...], al_ref[...], dt_ref[...])
        beta_ref[...] = beta
        gc_ref[...] = gc

    tok = pl.BlockSpec((tb, nh), lambda i: (i, 0))
    vec = pl.BlockSpec((1, nh), lambda i: (0, 0))
    return pl.pallas_call(
        body, name="gates_fwd", grid=(seq // tb,),
        in_specs=[tok, tok, vec, vec], out_specs=[tok, tok],
        out_shape=[jax.ShapeDtypeStruct((seq, nh), F32)] * 2,
        compiler_params=_cparams(("parallel",)),
    )(b, a, a_log, dt_bias)


def _gates_bwd(b, a, a_log, dt_bias, dbeta, dgc):
    seq, nh = b.shape
    tb = min(GATE_TB, seq)

    def body(b_ref, a_ref, al_ref, dt_ref, dbeta_ref, dgc_ref, db_ref, da_ref, dal_ref, ddt_ref):
        @pl.when(pl.program_id(0) == 0)
        def _():
            dal_ref[...] = jnp.zeros_like(dal_ref)
            ddt_ref[...] = jnp.zeros_like(ddt_ref)

        _, vjp = jax.vjp(_gates, b_ref[...], a_ref[...], al_ref[...], dt_ref[...])
        db, da, dal, ddt = vjp((dbeta_ref[...], dgc_ref[...]))
        db_ref[...] = db
        da_ref[...] = da
        dal_ref[...] += dal
        ddt_ref[...] += ddt

    tok = pl.BlockSpec((tb, nh), lambda i: (i, 0))
    vec = pl.BlockSpec((1, nh), lambda i: (0, 0))
    return pl.pallas_call(
        body, name="gates_bwd", grid=(seq // tb,),
        in_specs=[tok, tok, vec, vec, tok, tok], out_specs=[tok, tok, vec, vec],
        out_shape=[jax.ShapeDtypeStruct((seq, nh), F32)] * 2 + [jax.ShapeDtypeStruct((1, nh), F32)] * 2,
        compiler_params=_cparams(("arbitrary",)),
    )(b, a, a_log, dt_bias, dbeta, dgc)


LN_TR = 256


def _ln_stats(x, s):
    z = DN_ALPHA * x + s
    mu = jnp.mean(z, -1, keepdims=True)
    zc = z - mu
    var = jnp.mean(zc * zc, -1, keepdims=True)
    rstd = lax.rsqrt(var + LN_EPS)
    return zc * rstd, rstd


def _ln_fwd(x, s, g, b):
    seq, d = x.shape
    tr = min(LN_TR, seq)

    def body(x_ref, s_ref, g_ref, b_ref, o_ref, ob_ref):
        xhat, _ = _ln_stats(x_ref[...], s_ref[...])
        y = xhat * g_ref[...] + b_ref[...]
        o_ref[...] = y
        ob_ref[...] = y.astype(BF16)

    tok = pl.BlockSpec((tr, d), lambda i: (i, 0))
    vec = pl.BlockSpec((1, d), lambda i: (0, 0))
    return pl.pallas_call(
        body, name="ln_fwd", grid=(seq // tr,),
        in_specs=[tok, tok, vec, vec], out_specs=[tok, tok],
        out_shape=[jax.ShapeDtypeStruct((seq, d), F32), jax.ShapeDtypeStruct((seq, d), BF16)],
        compiler_params=_cparams(("parallel",)),
    )(x, s, g.reshape(1, d), b.reshape(1, d))


def _ln_bwd(dy, x, s, g, res=None):
    seq, d = x.shape
    tr = min(LN_TR, seq)

    def body(*refs):
        dy_ref, x_ref, s_ref, g_ref = refs[:4]
        dz_ref, dzb_ref, dg_ref, db_ref = refs[-4:]

        @pl.when(pl.program_id(0) == 0)
        def _():
            dg_ref[...] = jnp.zeros_like(dg_ref)
            db_ref[...] = jnp.zeros_like(db_ref)

        dy = dy_ref[...] if res is None else dy_ref[...] + DN_ALPHA * refs[4][...]
        xhat, rstd = _ln_stats(x_ref[...], s_ref[...])
        dyg = dy * g_ref[...]
        m1 = jnp.mean(dyg, -1, keepdims=True)
        m2 = jnp.mean(dyg * xhat, -1, keepdims=True)
        dz = rstd * (dyg - m1 - xhat * m2)
        dz_ref[...] = dz
        dzb_ref[...] = dz.astype(BF16)
        dg_ref[...] += jnp.sum(dy * xhat, axis=0, keepdims=True)
        db_ref[...] += jnp.sum(dy, axis=0, keepdims=True)

    tok = pl.BlockSpec((tr, d), lambda i: (i, 0))
    vec = pl.BlockSpec((1, d), lambda i: (0, 0))
    return pl.pallas_call(
        body, name="ln_bwd", grid=(seq // tr,),
        in_specs=[tok, tok, tok, vec] + ([] if res is None else [tok]), out_specs=[tok, tok, vec, vec],
        out_shape=[jax.ShapeDtypeStruct((seq, d), F32), jax.ShapeDtypeStruct((seq, d), BF16),
                   jax.ShapeDtypeStruct((1, d), F32), jax.ShapeDtypeStruct((1, d), F32)],
        compiler_params=_cparams(("arbitrary",)),
    )(dy, x, s, g.reshape(1, d), *(() if res is None else (res,)))


def _ln_loss(x, s, g, b, target):
    seq, d = x.shape
    tr = min(LN_TR, seq)

    def body(x_ref, s_ref, g_ref, b_ref, t_ref, loss_ref, dy_ref):
        @pl.when(pl.program_id(0) == 0)
        def _():
            loss_ref[...] = jnp.zeros_like(loss_ref)

        xhat, _ = _ln_stats(x_ref[...], s_ref[...])
        err = xhat * g_ref[...] + b_ref[...] - t_ref[...]
        dy_ref[...] = err * (1.0 / d)
        part = jnp.sum(jnp.sum(err * err, axis=0, keepdims=True), axis=1, keepdims=True)
        loss_ref[...] += part * (0.5 / d)

    tok = pl.BlockSpec((tr, d), lambda i: (i, 0))
    vec = pl.BlockSpec((1, d), lambda i: (0, 0))
    return pl.pallas_call(
        body, name="ln_loss", grid=(seq // tr,),
        in_specs=[tok, tok, vec, vec, tok], out_specs=[pl.BlockSpec((8, 128), lambda i: (0, 0)), tok],
        out_shape=[jax.ShapeDtypeStruct((8, 128), F32), jax.ShapeDtypeStruct((seq, d), F32)],
        compiler_params=_cparams(("arbitrary",)),
    )(x, s, g.reshape(1, d), b.reshape(1, d), target)


COMM_MID = 0.8


def _matmul(a, b, *, ta=False, tb=False, b_sharded=False, out_sharded=False, out_dtypes=(F32,), epilogue=None,
            extras=(), tm=1024, tn=1024, tk=2048, name="matmul", comm=None, b_cols=None):
    m, k = (a.shape[1], a.shape[0]) if ta else a.shape
    if b_sharded:
        bk, bn = b.shape[1], N_DEV * b.shape[2]
        shard_w = b.shape[2]
    else:
        bk, bn = b.shape[0], b_cols or b.shape[1]
    n = bk if tb else bn
    assert k == (bn if tb else bk), (a.shape, b.shape)
    tm, tn, tk = min(tm, m), min(tn, n), min(tk, k)
    if b_sharded:
        if tb:
            tk = math.gcd(tk, shard_w)
        else:
            tn = math.gcd(tn, shard_w)
    if out_sharded:
        tn = math.gcd(tn, n // N_DEV)
    assert m % tm == 0 and n % tn == 0 and k % tk == 0, (m, n, k, tm, tn, tk)
    ni, nj, nk = m // tm, n // tn, k // tk
    dims = (((0 if ta else 1,), (1 if tb else 0,)), ((), ()))
    n_ex, n_out = len(extras), len(out_dtypes)
    n_ci = len(comm.ins) if comm else 0
    n_co = len(comm.out_shapes) if comm else 0
    total = ni * nj * nk
    mid_step = min(int(COMM_MID * total), total - 1)
    in_place = nk > 1 and epilogue is None and tuple(out_dtypes) == (F32,)

    def body(*refs):
        a_ref, b_ref = refs[0], refs[1]
        ex_refs = refs[2:2 + n_ex]
        ci_refs = refs[2 + n_ex:2 + n_ex + n_ci]
        out_refs = refs[2 + n_ex + n_ci:2 + n_ex + n_ci + n_out]
        co_refs = refs[2 + n_ex + n_ci + n_out:2 + n_ex + n_ci + n_out + n_co]
        scratch = refs[2 + n_ex + n_ci + n_out + n_co:]
        if nk == 1:
            acc, sems = None, scratch
        elif in_place:
            acc, sems = out_refs[0], scratch
        else:
            acc, sems = scratch[0], scratch[1:]
        kk = pl.program_id(2)
        step = (pl.program_id(0) * nj + pl.program_id(1)) * nk + kk

        if comm:
            @pl.when(step == 0)
            def _():
                comm.start(ci_refs, co_refs, *sems)

        prod = lax.dot_general(a_ref[...].astype(BF16), b_ref[...].astype(BF16), dims, preferred_element_type=F32)

        def write(res):
            outs = (res,) if epilogue is None else epilogue(res, *[r[...] for r in ex_refs])
            for o_ref, val in zip(out_refs, outs, strict=True):
                o_ref[...] = val.astype(o_ref.dtype)

        if nk == 1:
            write(prod)
        else:
            @pl.when(kk == 0)
            def _():
                acc[...] = prod

            @pl.when(kk > 0)
            def _():
                acc[...] += prod

            if not in_place:
                @pl.when(kk == nk - 1)
                def _():
                    write(acc[...])

        if comm:
            @pl.when(step == mid_step)
            def _():
                comm.mid(ci_refs, co_refs, *sems)

            @pl.when(step == total - 1)
            def _():
                comm.finish(ci_refs, co_refs, *sems)

    a_spec = pl.BlockSpec((tk, tm), lambda i, j, kk: (kk, i)) if ta else pl.BlockSpec((tm, tk), lambda i, j, kk: (i, kk))
    if b_sharded and tb:
        per = shard_w // tk
        b_spec = pl.BlockSpec((None, tn, tk), lambda i, j, kk: (kk // per, j, kk % per))
    elif b_sharded:
        per = shard_w // tn
        b_spec = pl.BlockSpec((None, tk, tn), lambda i, j, kk: (j // per, kk, j % per))
    elif tb:
        b_spec = pl.BlockSpec((tn, tk), lambda i, j, kk: (j, kk))
    else:
        b_spec = pl.BlockSpec((tk, tn), lambda i, j, kk: (kk, j))
    ex_spec = pl.BlockSpec((tm, tn), lambda i, j, kk: (i, j))
    if out_sharded:
        per_o = n // N_DEV // tn
        o_spec = pl.BlockSpec((None, tm, tn), lambda i, j, kk: (j // per_o, i, j % per_o))
        o_shape = (N_DEV, m, n // N_DEV)
    else:
        o_spec, o_shape = ex_spec, (m, n)
    hbm = pl.BlockSpec(memory_space=pl.ANY)
    outs = pl.pallas_call(
        body, name=name, grid=(ni, nj, nk),
        in_specs=[a_spec, b_spec] + [ex_spec] * n_ex + [hbm] * n_ci,
        out_specs=[o_spec] * n_out + [hbm] * n_co,
        out_shape=[jax.ShapeDtypeStruct(o_shape, dt) for dt in out_dtypes] + (list(comm.out_shapes) if comm else []),
        scratch_shapes=([] if nk == 1 or in_place else [pltpu.VMEM((tm, tn), F32)])
        + (list(comm.scratch) if comm else []),
        compiler_params=_cparams(("arbitrary",) * 3 if comm else ("parallel", "parallel", "arbitrary")),
    )(a, b, *extras, *(comm.ins if comm else ()))
    return outs[0] if len(outs) == 1 else outs


def _epi_relu2(acc):
    r = jnp.maximum(acc, 0.0)
    return acc, r * r


def _epi_drelu2(acc, pre):
    return (acc * (2.0 * jnp.maximum(pre, 0.0)),)


def _epi_add(scale):
    return lambda acc, other: (acc + scale * other,)


def _adamw(parts, w, m, v, *, rows_per_step, name, layer=None, n_layers=None, into=None):
    n_parts, rows, cols = parts.shape
    tr = min(rows_per_step, rows)
    assert rows % tr == 0

    def body(p_ref, w_ref, m_ref, v_ref, *rest):
        g_ref, d_ref, mo_ref, vo_ref = rest[-4:]
        g = p_ref[0].astype(F32)
        for i in range(1, n_parts):
            g = g + p_ref[i].astype(F32)
        m_new = ADAM_B1 * m_ref[...] + (1.0 - ADAM_B1) * g
        v_new = ADAM_B2 * v_ref[...] + (1.0 - ADAM_B2) * (g * g)
        m_hat = m_new / (1.0 - ADAM_B1 ** ADAM_STEP)
        v_hat = v_new / (1.0 - ADAM_B2 ** ADAM_STEP)
        g_ref[...] = g
        d_ref[...] = -ADAM_LR * (m_hat / (jnp.sqrt(v_hat) + ADAM_EPS) + ADAM_WD * w_ref[...])
        mo_ref[...] = m_new
        vo_ref[...] = v_new

    blk = pl.BlockSpec((tr, cols), lambda i: (i, 0))
    if layer is None:
        out_blk, out_shape = blk, (rows, cols)
    else:
        out_blk, out_shape = pl.BlockSpec((None, tr, cols), lambda i: (layer, i, 0)), (n_layers, rows, cols)
    into = list(into or ())
    return pl.pallas_call(
        body, name=name, grid=(rows // tr,),
        in_specs=[pl.BlockSpec((n_parts, tr, cols), lambda i: (0, i, 0)), blk, blk, blk]
        + [pl.BlockSpec(memory_space=pl.ANY)] * len(into),
        out_specs=[out_blk] * 4,
        out_shape=[jax.ShapeDtypeStruct(out_shape, F32)] * 4,
        input_output_aliases={4 + k: k for k in range(len(into))},
        compiler_params=_cparams(("parallel",)),
    )(parts, w, m, v, *into)


def _position():
    return lax.axis_index("x"), lax.axis_index("y"), lax.axis_index("c")


def _comm_scratch(n):
    return [pltpu.SemaphoreType.DMA((7 * n,)), pltpu.SemaphoreType.DMA((7 * n,)), pltpu.SemaphoreType.DMA((n,))]


class _Gather:
    def __init__(self, blocks):
        self.ins = list(blocks)
        self.out_shapes = [jax.ShapeDtypeStruct((N_DEV,) + b.shape, b.dtype) for b in blocks]
        self.scratch = _comm_scratch(len(blocks))

    def _plan(self, n, ins, outs, send_sems, recv_sems, local_sems):
        x, y, c = _position()
        me, sibling = (x, y, c), (x, y, 1 - c)
        chips = [(1 - x, y), (x, 1 - y), (1 - x, 1 - y)]
        x_ref, out_ref = ins[n], outs[n]

        def slot(px, py, pc):
            return out_ref.at[4 * px + 2 * py + pc]

        def copy(k, blk, to, src=None):
            return pltpu.make_async_remote_copy(
                src_ref=slot(*blk) if src is None else src, dst_ref=slot(*blk),
                send_sem=send_sems.at[7 * n + k], recv_sem=recv_sems.at[7 * n + k], device_id=to, device_id_type=MESH)

        mine = lambda: pltpu.make_async_copy(x_ref, slot(*me), local_sems.at[n])
        first = lambda: [copy(0, me, sibling, src=x_ref)] + [copy(1 + j, me, (*chip, c), src=x_ref)
                                                             for j, chip in enumerate(chips)]
        passed = lambda j: copy(4 + j, (*chips[j], c), sibling)
        landed = lambda j: copy(1 + j, (*chips[j], c), me)
        from_sibling = lambda: [copy(0, sibling, me)] + [copy(4 + j, (*chip, 1 - c), me) for j, chip in enumerate(chips)]
        return mine, first, passed, landed, from_sibling

    def start(self, ins, outs, *sems):
        for n in range(len(self.ins)):
            mine, first, _, _, _ = self._plan(n, ins, outs, *sems)
            mine().start()
            for cp in first():
                cp.start()

    def mid(self, ins, outs, *sems):
        plans = [self._plan(n, ins, outs, *sems) for n in range(len(self.ins))]
        for j in range(3):
            for _, _, passed, landed, _ in plans:
                landed(j).wait_recv()
                passed(j).start()

    def finish(self, ins, outs, *sems):
        for n in range(len(self.ins)):
            mine, first, passed, _, from_sibling = self._plan(n, ins, outs, *sems)
            for cp in from_sibling():
                cp.wait_recv()
            for cp in first() + [passed(j) for j in range(3)]:
                cp.wait_send()
            mine().wait()


class _Exchange:
    def __init__(self, parts):
        self.ins = list(parts)
        self.out_shapes = [jax.ShapeDtypeStruct(p.shape, p.dtype) for p in parts]
        self.scratch = _comm_scratch(len(parts))

    def _plan(self, n, ins, outs, send_sems, recv_sems, local_sems):
        x, y, c = _position()
        me = 4 * x + 2 * y + c
        p_ref, out_ref = ins[n], outs[n]
        mine = lambda: pltpu.make_async_copy(p_ref.at[me], out_ref.at[me], local_sems.at[n])

        def copies(landing):
            out = []
            for k in range(1, N_DEV):
                px = 1 - x if k & 4 else x
                py = 1 - y if k & 2 else y
                pc = 1 - c if k & 1 else c
                peer_slot = 4 * px + 2 * py + pc
                out.append(pltpu.make_async_remote_copy(
                    src_ref=p_ref.at[peer_slot], dst_ref=out_ref.at[peer_slot if landing else me],
                    send_sem=send_sems.at[7 * n + k - 1], recv_sem=recv_sems.at[7 * n + k - 1],
                    device_id=(px, py, pc), device_id_type=MESH))
            return out

        return mine, copies

    def start(self, ins, outs, *sems):
        for n in range(len(self.ins)):
            mine, copies = self._plan(n, ins, outs, *sems)
            mine().start()
            for cp in copies(False):
                cp.start()

    def mid(self, ins, outs, *sems):
        pass

    def finish(self, ins, outs, *sems):
        for n in range(len(self.ins)):
            mine, copies = self._plan(n, ins, outs, *sems)
            for cp in copies(True):
                cp.wait_recv()
            for cp in copies(False):
                cp.wait_send()
            mine().wait()


def _comm_alone(comm, name):
    def body(*refs):
        n_i, n_o = len(comm.ins), len(comm.out_shapes)
        ins, outs, sems = refs[:n_i], refs[n_i:n_i + n_o], refs[n_i + n_o:]
        comm.start(ins, outs, *sems)
        comm.mid(ins, outs, *sems)
        comm.finish(ins, outs, *sems)

    hbm = pl.BlockSpec(memory_space=pl.ANY)
    return pl.pallas_call(
        body, name=name, out_shape=list(comm.out_shapes),
        in_specs=[hbm] * len(comm.ins), out_specs=[hbm] * len(comm.out_shapes),
        scratch_shapes=list(comm.scratch),
    )(*comm.ins)


RET_IN_W = 2 * RET_QK + 2 * RET_VW
GDN_IN_W = GDN_QKV + GDN_VW + 2 * GDN_V_HEADS
GDN_TAIL = 2 * GDN_V_HEADS
TAIL_PAD = 128

SMALL_SIZES = (RET_VW, GDN_V_HEADS, GDN_V_HEADS, GDN_DV, DEPTH * D_MODEL, DEPTH * D_MODEL, DEPTH * D_MODEL,
               DEPTH * D_MODEL, GDN_CONV * GDN_QKV)
SMALL_LANES = 128
SMALL_ROWS = -(-sum(SMALL_SIZES) // (8 * SMALL_LANES)) * 8


def _pack_small(*vecs):
    flat = jnp.concatenate([v.reshape(-1).astype(F32) for v in vecs])
    return jnp.pad(flat, (0, SMALL_ROWS * SMALL_LANES - flat.shape[0])).reshape(SMALL_ROWS, SMALL_LANES)


def _unpack_small(buf, shapes):
    flat, out, at = buf.reshape(-1), [], 0
    for shp in shapes:
        n = int(np.prod(shp))
        out.append(flat[at:at + n].reshape(shp))
        at += n
    return out


def _mlp_bwd(dz, h, a, r, w1, w2, name, exchange_dw2=False):
    dz, dzb = dz
    da = _matmul(dzb, w2, tb=True, out_dtypes=(BF16,), epilogue=_epi_drelu2, extras=(a,), name=name + "_da")
    dw2 = _matmul(r, dzb, ta=True, out_dtypes=(BF16,), name=name + "_dw2").reshape(N_DEV, -1, D_MODEL)
    dw1 = _matmul(h, da, ta=True, out_sharded=True, out_dtypes=(BF16,), name=name + "_dw1")
    dh = _matmul(da, w1, tb=True, b_sharded=True, tn=D_MODEL, name=name + "_dh",
                 comm=_Exchange([dw2]) if exchange_dw2 else None)
    if exchange_dw2:
        dh, dw2 = dh
    return dh, dw1, dw2


def _chunk_cols(t):
    seq, nh = t.shape
    return t.reshape(seq // CHUNK, CHUNK, nh).transpose(0, 2, 1)[..., None]


def _from_chunk_cols(t):
    nc, nh = t.shape[:2]
    return t[..., 0].transpose(0, 2, 1).reshape(nc * CHUNK, nh)


def _chunk_rows(t):
    seq, nh = t.shape
    return t.reshape(seq // CHUNK, CHUNK, nh).transpose(0, 2, 1)[:, :, None, :]


def _from_chunk_rows(t):
    nc, nh = t.shape[:2]
    return t[:, :, 0, :].transpose(0, 2, 1).reshape(nc * CHUNK, nh)


def kernel(x, ret_w_in, ret_gn_g, ret_w_out, gdn_w_in, gdn_conv_w, gdn_a_log, gdn_dt_bias, gdn_norm_g, gdn_w_out, ln_mix_g, ln_mix_b, mlp_w1, mlp_w2, ln_ffn_g, ln_ffn_b, loss_target, m_ret_w_in, m_ret_gn_g, m_ret_w_out, m_gdn_w_in, m_gdn_conv_w, m_gdn_a_log, m_gdn_dt_bias, m_gdn_norm_g, m_gdn_w_out, m_ln_mix_g, m_ln_mix_b, m_mlp_w1, m_mlp_w2, m_ln_ffn_g, m_ln_ffn_b, v_ret_w_in, v_ret_gn_g, v_ret_w_out, v_gdn_w_in, v_gdn_conv_w, v_gdn_a_log, v_gdn_dt_bias, v_gdn_norm_g, v_gdn_w_out, v_ln_mix_g, v_ln_mix_b, v_mlp_w1, v_mlp_w2, v_ln_ffn_g, v_ln_ffn_b):
    xt, target = x[0], loss_target[0]
    seq = xt.shape[0]
    me = 4 * lax.axis_index("x") + 2 * lax.axis_index("y") + lax.axis_index("c")

    bf = lambda t: t.astype(BF16)
    cos, sin = _rope_tables(seq)
    w_ret_in, = _comm_alone(_Gather([bf(ret_w_in[0])]), "gather_ret_in")
    conv_blk = jnp.pad(gdn_conv_w[0], ((0, HALO - GDN_CONV), (0, 0)))
    shard_in = RET_IN_W // N_DEV
    xb = bf(xt)
    p0, w_ret_out, w1_0, conv_all = _matmul(
        xb, w_ret_in, b_sharded=True, tn=shard_in, name="ret_in",
        comm=_Gather([bf(ret_w_out[0]), bf(mlp_w1[0]), conv_blk]))
    w_ret_out = w_ret_out.reshape(RET_VW, D_MODEL)
    conv_w = conv_all[:, :GDN_CONV].transpose(1, 0, 2).reshape(GDN_CONV, GDN_QKV)
    o0, s0, w2_0 = _ret_fwd(p0, cos, sin, ret_gn_g[0], comm=_Gather([bf(mlp_w2[0])]))
    w2_0 = w2_0.reshape(D_FF, D_MODEL)
    mix0 = _matmul(o0, w_ret_out, name="ret_out")
    h1, h1b = _ln_fwd(xt, mix0, ln_mix_g[0], ln_mix_b[0])
    a0, r0, gdn_in_all = _matmul(h1b, w1_0, b_sharded=True, out_dtypes=(F32, BF16), epilogue=_epi_relu2,
                                 name="mlp0_up", comm=_Gather([bf(gdn_w_in[0])]))
    m0, w_gdn_out, w1_1 = _matmul(r0, w2_0, name="mlp0_down", comm=_Gather([bf(gdn_w_out[0]), bf(mlp_w1[1])]))
    w_gdn_out = w_gdn_out.reshape(GDN_VW, D_MODEL)
    h2, h2b = _ln_fwd(h1, m0, ln_ffn_g[0], ln_ffn_b[0])

    w_gdn_in = gdn_in_all.transpose(1, 0, 2).reshape(D_MODEL, GDN_IN_W)
    main_w = GDN_IN_W - GDN_TAIL
    w_gdn_tail = jnp.pad(w_gdn_in[:, main_w:], ((0, 0), (0, TAIL_PAD - GDN_TAIL)))
    p1, w2_1 = _matmul(h2b, w_gdn_in, b_cols=main_w, name="gdn_in", comm=_Gather([bf(mlp_w2[1])]))
    w2_1 = w2_1.reshape(D_FF, D_MODEL)
    pt = _matmul(h2b, w_gdn_tail, name="gdn_in_tail")
    c1 = _conv_fwd(p1, conv_w)
    b_in, a_in = pt[:, :GDN_V_HEADS], pt[:, GDN_V_HEADS:GDN_TAIL]
    beta, gc = _gates_fwd(b_in, a_in, gdn_a_log, gdn_dt_bias)
    beta_c, gc_c, gc_r = _chunk_cols(beta), _chunk_cols(gc), _chunk_rows(gc)
    o1, t1, s1 = _gdn_fwd(c1, p1, beta_c, gc_c, gc_r, gdn_norm_g[0])
    mix1 = _matmul(o1, w_gdn_out, name="gdn_out")
    h3, h3b = _ln_fwd(h2, mix1, ln_mix_g[1], ln_mix_b[1])
    a1, r1 = _matmul(h3b, w1_1, b_sharded=True, out_dtypes=(F32, BF16), epilogue=_epi_relu2, name="mlp1_up")
    m1 = _matmul(r1, w2_1, name="mlp1_down")
    loss_blk, dh4 = _ln_loss(h3, m1, ln_ffn_g[1], ln_ffn_b[1], target)
    loss = lax.psum(loss_blk[0, 0], ("x", "y", "c"))

    dz, dzb, dg_ffn1, db_ffn1 = _ln_bwd(dh4, h3, m1, ln_ffn_g[1])
    dh3, dw1_1, dw2_1 = _mlp_bwd((dz, dzb), h3b, a1, r1, w1_1, w2_1, "mlp1")
    dz, dzb, dg_mix1, db_mix1 = _ln_bwd(dh3, h2, mix1, ln_mix_g[1], res=dz)
    do1 = _matmul(dzb, w_gdn_out, tb=True, name="gdn_out_do")
    dw_gdn_out = _matmul(o1, dzb, ta=True, out_dtypes=(BF16,), name="gdn_out_dw").reshape(N_DEV, -1, D_MODEL)
    dc1, dp1, dbeta_c, dgc_c, dgc_r, dng, x_w1_1, x_w2_1, x_gdn_out = _gdn_bwd(
        c1, p1, beta_c, gc_c, gc_r, gdn_norm_g[0], t1, s1, do1, comm=_Exchange([dw1_1, dw2_1, dw_gdn_out]))
    dp1, dconv = _conv_bwd(p1, dc1, conv_w, dp1)
    db_in, da_in, dalog, ddt = _gates_bwd(b_in, a_in, gdn_a_log, gdn_dt_bias, _from_chunk_cols(dbeta_c),
                                          _from_chunk_cols(dgc_c) + _from_chunk_rows(dgc_r))
    dpt = jnp.concatenate([db_in, da_in, jnp.zeros((seq, TAIL_PAD - GDN_TAIL), F32)], axis=-1)
    dw_gdn_main = _matmul(h2b, dp1, ta=True, out_dtypes=(BF16,), name="gdn_in_dw")
    dw_gdn_tail = _matmul(h2b, dpt, ta=True, out_dtypes=(BF16,), name="gdn_in_tail_dw")
    dw_gdn_in = jnp.concatenate([dw_gdn_main, dw_gdn_tail[:, :GDN_TAIL]], axis=-1)
    dw_gdn_in = dw_gdn_in.reshape(D_MODEL, N_DEV, GDN_IN_W // N_DEV).transpose(1, 0, 2)
    dh2 = _matmul(dpt, w_gdn_tail, tb=True, epilogue=_epi_add(DN_ALPHA), extras=(dz,), name="gdn_in_tail_dh")
    dh2, x_gdn_in = _matmul(dp1, w_gdn_in, tb=True, b_cols=main_w, epilogue=_epi_add(1.0), extras=(dh2,),
                            name="gdn_in_dh", comm=_Exchange([dw_gdn_in]))

    dz, dzb, dg_ffn0, db_ffn0 = _ln_bwd(dh2, h1, m0, ln_ffn_g[0])
    dh1, dw1_0, x_w2_0 = _mlp_bwd((dz, dzb), h1b, a0, r0, w1_0, w2_0, "mlp0", exchange_dw2=True)
    dz, dzb, dg_mix0, db_mix0 = _ln_bwd(dh1, xt, mix0, ln_mix_g[0], res=dz)
    do0 = _matmul(dzb, w_ret_out, tb=True, name="ret_out_do")
    dw_ret_out = _matmul(o0, dzb, ta=True, out_dtypes=(BF16,), name="ret_out_dw").reshape(N_DEV, -1, D_MODEL)
    dp0, dgng, x_w1_0 = _ret_bwd(p0, cos, sin, ret_gn_g[0], s0, do0, comm=_Exchange([dw1_0]))
    dw_ret_in, x_ret_out = _matmul(xb, dp0, ta=True, out_sharded=True, out_dtypes=(BF16,), tn=shard_in,
                                   name="ret_in_dw", comm=_Exchange([dw_ret_out]))
    dx, x_ret_in = _matmul(dp0, w_ret_in, tb=True, b_sharded=True, epilogue=_epi_add(DN_ALPHA), extras=(dz,),
                           tk=shard_in, name="ret_in_dx", comm=_Exchange([dw_ret_in]))

    def update(parts, w, m, v, name, **slab):
        shape = parts.shape[1:]
        outs = _adamw(parts, w.reshape(shape), m.reshape(shape), v.reshape(shape), rows_per_step=128, name=name,
                      **slab)
        return outs if slab else [t.reshape(w.shape) for t in outs]

    u_w1 = update(x_w1_1, mlp_w1[1], m_mlp_w1[1], v_mlp_w1[1], "adamw_w1_1", layer=1, n_layers=DEPTH)
    u_w1 = update(x_w1_0, mlp_w1[0], m_mlp_w1[0], v_mlp_w1[0], "adamw_w1_0", layer=0, n_layers=DEPTH, into=u_w1)
    u_w2 = update(x_w2_1, mlp_w2[1], m_mlp_w2[1], v_mlp_w2[1], "adamw_w2_1", layer=1, n_layers=DEPTH)
    u_w2 = update(x_w2_0, mlp_w2[0], m_mlp_w2[0], v_mlp_w2[0], "adamw_w2_0", layer=0, n_layers=DEPTH, into=u_w2)
    big_out = list(zip(
        update(x_ret_in, ret_w_in, m_ret_w_in, v_ret_w_in, "adamw_ret_in"),
        update(x_ret_out, ret_w_out, m_ret_w_out, v_ret_w_out, "adamw_ret_out"),
        update(x_gdn_in, gdn_w_in, m_gdn_w_in, v_gdn_w_in, "adamw_gdn_in"),
        update(x_gdn_out, gdn_w_out, m_gdn_w_out, v_gdn_w_out, "adamw_gdn_out"),
        u_w1, u_w2))

    small_w = (ret_gn_g, gdn_a_log, gdn_dt_bias, gdn_norm_g, ln_mix_g, ln_mix_b, ln_ffn_g, ln_ffn_b)
    small_m = (m_ret_gn_g, m_gdn_a_log, m_gdn_dt_bias, m_gdn_norm_g, m_ln_mix_g, m_ln_mix_b, m_ln_ffn_g, m_ln_ffn_b)
    small_v = (v_ret_gn_g, v_gdn_a_log, v_gdn_dt_bias, v_gdn_norm_g, v_ln_mix_g, v_ln_mix_b, v_ln_ffn_g, v_ln_ffn_b)
    small_g = (dgng, dalog, ddt, jnp.sum(dng, axis=0),
               jnp.concatenate([dg_mix0, dg_mix1]), jnp.concatenate([db_mix0, db_mix1]),
               jnp.concatenate([dg_ffn0, dg_ffn1]), jnp.concatenate([db_ffn0, db_ffn1]), dconv)
    small_parts, = _comm_alone(_Gather([_pack_small(*small_g)]), "gather_small_grads")
    zero_conv = jnp.zeros((GDN_CONV, GDN_QKV), F32)
    small_out = _adamw(small_parts, _pack_small(*small_w, zero_conv), _pack_small(*small_m, zero_conv),
                       _pack_small(*small_v, zero_conv), rows_per_step=SMALL_ROWS, name="adamw_small")
    shapes = [t.shape for t in small_w] + [(GDN_CONV, GDN_QKV)]
    small_out = [_unpack_small(t, shapes) for t in small_out]
    conv_g = lax.dynamic_slice(small_out[0][-1], (0, me * (GDN_QKV // N_DEV)), (GDN_CONV, GDN_QKV // N_DEV))
    conv_out = _adamw(conv_g[None], gdn_conv_w[0], m_gdn_conv_w[0], v_gdn_conv_w[0],
                      rows_per_step=GDN_CONV, name="adamw_conv")

    def ordered(kind):
        b, s, cv = big_out[kind], small_out[kind], conv_out[kind][None]
        return [b[0], s[0], b[1], b[2], cv, s[1], s[2], s[3], b[3], s[4], s[5], b[4], b[5], s[6], s[7]]

    return (loss, dx[None], *ordered(0), *ordered(1), *ordered(2), *ordered(3))
```

```python
import functools
import math

import jax
import jax.numpy as jnp
import numpy as np
from jax import lax
from jax.experimental import pallas as pl
from jax.experimental.pallas import tpu as pltpu

F32 = jnp.float32
BF16 = jnp.bfloat16

N_DEV = 8
D_MODEL = 2048
CHUNK = 64
RET_HEADS = 8
RET_DK = 256
RET_DV = 512
RET_QK = RET_HEADS * RET_DK
RET_VW = RET_HEADS * RET_DV
ROPE_BASE = 10000.0
GN_EPS = 1e-6
GDN_K_HEADS = 16
GDN_V_HEADS = 32
GDN_DK = 128
GDN_DV = 128
GDN_QK = GDN_K_HEADS * GDN_DK
GDN_VW = GDN_V_HEADS * GDN_DV
GDN_QKV = 2 * GDN_QK + GDN_VW
GDN_CONV = 4
RMS_EPS = 1e-6
L2_EPS = 1e-6
D_FF = 4 * D_MODEL
DEPTH = 2
DN_ALPHA = (2.0 * DEPTH) ** 0.25
LN_EPS = 1e-5
ADAM_LR = 0.001
ADAM_B1 = 0.9
ADAM_B2 = 0.999
ADAM_EPS = 1e-08
ADAM_WD = 0.01
ADAM_STEP = 10

VMEM_LIMIT = 56 * 1024 * 1024
MESH = pl.DeviceIdType.MESH


def _cparams(sem=None):
    return pltpu.CompilerParams(dimension_semantics=sem, vmem_limit_bytes=VMEM_LIMIT)


_NT = (((2,), (2,)), ((0,), (0,)))
_NN = (((2,), (1,)), ((0,), (0,)))
_TN = (((1,), (1,)), ((0,), (0,)))


def _dg(a, b, dims):
    return lax.dot_general(a.astype(BF16), b.astype(BF16), dims, preferred_element_type=F32)


@jax.custom_vjp
def _nt(a, b):
    return _dg(a, b, _NT)


@jax.custom_vjp
def _nn(a, b):
    return _dg(a, b, _NN)


@jax.custom_vjp
def _tn(a, b):
    return _dg(a, b, _TN)


_nt.defvjp(lambda a, b: (_dg(a, b, _NT), (a, b)), lambda r, g: (_nn(g, r[1]), _tn(g, r[0])))
_nn.defvjp(lambda a, b: (_dg(a, b, _NN), (a, b)), lambda r, g: (_nt(g, r[1]), _tn(r[0], g)))
_tn.defvjp(lambda a, b: (_dg(a, b, _TN), (a, b)), lambda r, g: (_nt(r[1], g), _nn(r[0], g)))


def _iota2(shape, dim):
    return lax.broadcasted_iota(jnp.int32, shape, dim)


def _inv_unit_lower(a):
    c = a.shape[-1]
    eye = (_iota2((c, c), 0) == _iota2((c, c), 1)).astype(F32)
    m = -a
    p = eye + m
    for _ in range(int(math.log2(c)) - 1):
        m = _dg(m, m, _NN)
        p = p + _dg(p, m, _NN)
    return p


def _silu(x):
    return x * jax.nn.sigmoid(x)


def _rep2(t):
    h = t.shape[0]
    return jnp.broadcast_to(t[:, None], (h, 2) + t.shape[1:]).reshape((2 * h,) + t.shape[1:])


def _ret_chunk(q1, q2, k1, k2, v, gate, gn_g, s, cos, sin, intra, qdec, kdec, cdec):
    q = jnp.concatenate([q1 * cos - q2 * sin, q1 * sin + q2 * cos], axis=-1)
    k = jnp.concatenate([k1 * cos - k2 * sin, k1 * sin + k2 * cos], axis=-1) * (RET_DK ** -0.5)
    scores = _nt(q, k) * intra
    y = _nn(scores, v) + _nn(q * qdec, s)
    s_new = s * cdec + _tn(k * kdec, v)
    mu = jnp.mean(y, -1, keepdims=True)
    yc = y - mu
    var = jnp.mean(yc * yc, -1, keepdims=True)
    o = _silu(gate) * (yc * lax.rsqrt(var + GN_EPS) * gn_g)
    return o, s_new


def _ret_consts():
    log_gamma = np.log1p(-np.exp2(-5.0 - np.arange(RET_HEADS, dtype=np.float64)))
    idx = np.arange(CHUNK, dtype=np.float64)
    lg = log_gamma[:, None]
    intra = np.exp(lg[..., None] * np.abs(idx[:, None] - idx[None, :]))
    qdec = np.exp(lg * (idx + 1.0))[..., None]
    kdec = np.exp(lg * (CHUNK - 1.0 - idx))[..., None]
    cdec = np.exp(log_gamma * CHUNK)[:, None, None]
    return [jnp.asarray(t, F32) for t in (intra, qdec, kdec, cdec)]


def _rope_tables(seq):
    half = RET_DK // 2
    inv = ROPE_BASE ** (-jnp.arange(half, dtype=F32) / half)
    ang = jnp.arange(seq).astype(F32)[:, None] * inv[None, :]
    return jnp.cos(ang), jnp.sin(ang)


RET_HB = 8


def _ret_load(q_ref, k_ref, v_ref, gate_ref):
    hb, dk, dv, h = RET_HB, RET_DK, RET_DV, RET_DK // 2
    q, k, v, gate = q_ref[...], k_ref[...], v_ref[...], gate_ref[...]
    q1 = jnp.stack([q[:, i * dk:i * dk + h] for i in range(hb)])
    q2 = jnp.stack([q[:, i * dk + h:(i + 1) * dk] for i in range(hb)])
    k1 = jnp.stack([k[:, i * dk:i * dk + h] for i in range(hb)])
    k2 = jnp.stack([k[:, i * dk + h:(i + 1) * dk] for i in range(hb)])
    vs = jnp.stack([v[:, i * dv:(i + 1) * dv] for i in range(hb)])
    gs = jnp.stack([gate[:, i * dv:(i + 1) * dv] for i in range(hb)])
    return q1, q2, k1, k2, vs, gs


def _ret_specs(n_chunks, rev):
    hb = RET_HB
    cidx = (lambda n: n_chunks - 1 - n) if rev else (lambda n: n)
    qw, vw = hb * RET_DK, hb * RET_DV
    tok = [
        pl.BlockSpec((CHUNK, qw), lambda h, n: (cidx(n), h)),
        pl.BlockSpec((CHUNK, qw), lambda h, n: (cidx(n), RET_QK // qw + h)),
        pl.BlockSpec((CHUNK, vw), lambda h, n: (cidx(n), 2 * RET_QK // vw + h)),
        pl.BlockSpec((CHUNK, vw), lambda h, n: (cidx(n), (2 * RET_QK + RET_VW) // vw + h)),
        pl.BlockSpec((CHUNK, RET_DK // 2), lambda h, n: (cidx(n), 0)),
        pl.BlockSpec((CHUNK, RET_DK // 2), lambda h, n: (cidx(n), 0)),
    ]
    const = [
        pl.BlockSpec((hb, CHUNK, CHUNK), lambda h, n: (h, 0, 0)),
        pl.BlockSpec((hb, CHUNK, 1), lambda h, n: (h, 0, 0)),
        pl.BlockSpec((hb, CHUNK, 1), lambda h, n: (h, 0, 0)),
        pl.BlockSpec((hb, 1, 1), lambda h, n: (h, 0, 0)),
        pl.BlockSpec((hb, 1, RET_DV), lambda h, n: (h, 0, 0)),
    ]
    state = pl.BlockSpec((1, hb, RET_DK, RET_DV), lambda h, n: (cidx(n), h, 0, 0))
    return tok, const, state, cidx


def _grid_call(body, *, name, grid, in_specs, out_specs, out_shape, scratch_shapes, args, comm=None):
    if comm is None:
        return pl.pallas_call(body, name=name, grid=grid, in_specs=in_specs, out_specs=out_specs, out_shape=out_shape,
                              scratch_shapes=scratch_shapes,
                              compiler_params=_cparams(("parallel", "arbitrary")))(*args)
    n_in, n_out, n_scr = len(in_specs), len(out_specs), len(scratch_shapes)
    n_ci, n_co = len(comm.ins), len(comm.out_shapes)
    total = grid[0] * grid[1]
    mid_step = min(int(COMM_MID * total), total - 1)

    def carrying(*refs):
        ins, ci = refs[:n_in], refs[n_in:n_in + n_ci]
        at = n_in + n_ci
        outs, co = refs[at:at + n_out], refs[at + n_out:at + n_out + n_co]
        at += n_out + n_co
        scr, sems = refs[at:at + n_scr], refs[at + n_scr:]
        step = pl.program_id(0) * grid[1] + pl.program_id(1)
        pl.when(step == 0)(lambda: comm.start(ci, co, *sems))
        body(*ins, *outs, *scr)
        pl.when(step == mid_step)(lambda: comm.mid(ci, co, *sems))
        pl.when(step == total - 1)(lambda: comm.finish(ci, co, *sems))

    hbm = pl.BlockSpec(memory_space=pl.ANY)
    return pl.pallas_call(
        carrying, name=name, grid=grid,
        in_specs=list(in_specs) + [hbm] * n_ci, out_specs=list(out_specs) + [hbm] * n_co,
        out_shape=list(out_shape) + list(comm.out_shapes),
        scratch_shapes=list(scratch_shapes) + list(comm.scratch),
        compiler_params=_cparams(("arbitrary", "arbitrary")))(*args, *comm.ins)


def _ret_fwd(p, cos, sin, gn_g, comm=None):
    seq = p.shape[0]
    nc = seq // CHUNK
    hb = RET_HB
    tok, const, state, _ = _ret_specs(nc, False)

    def body(q_ref, k_ref, v_ref, gate_ref, cos_ref, sin_ref, intra_ref, qdec_ref, kdec_ref, cdec_ref, gng_ref,
             o_ref, ssave_ref, s_scr):
        @pl.when(pl.program_id(1) == 0)
        def _():
            s_scr[...] = jnp.zeros_like(s_scr)

        q1, q2, k1, k2, v, gate = _ret_load(q_ref, k_ref, v_ref, gate_ref)
        s = s_scr[...]
        ssave_ref[0] = s.astype(BF16)
        o, s_new = _ret_chunk(q1, q2, k1, k2, v, gate, gng_ref[...], s, cos_ref[...], sin_ref[...],
                              intra_ref[...], qdec_ref[...], kdec_ref[...], cdec_ref[...])
        s_scr[...] = s_new
        o_ref[...] = jnp.concatenate([o[i] for i in range(hb)], axis=-1).astype(o_ref.dtype)

    return _grid_call(
        body, name="ret_fwd", comm=comm,
        grid=(RET_HEADS // hb, nc),
        in_specs=tok + const,
        out_specs=[pl.BlockSpec((CHUNK, hb * RET_DV), lambda h, n: (n, h)), state],
        out_shape=[jax.ShapeDtypeStruct((seq, RET_VW), BF16),
                   jax.ShapeDtypeStruct((nc, RET_HEADS, RET_DK, RET_DV), BF16)],
        scratch_shapes=[pltpu.VMEM((hb, RET_DK, RET_DV), F32)],
        args=(p, p, p, p, cos, sin, *_ret_consts(), gn_g.reshape(RET_HEADS, 1, RET_DV)))


def _ret_bwd(p, cos, sin, gn_g, ssave, do, comm=None):
    seq = p.shape[0]
    nc = seq // CHUNK
    hb = RET_HB
    tok, const, state, cidx = _ret_specs(nc, True)

    assert hb == RET_HEADS

    def body(q_ref, k_ref, v_ref, gate_ref, cos_ref, sin_ref, intra_ref, qdec_ref, kdec_ref, cdec_ref, gng_ref,
             ssave_ref, do_ref, dp_ref, dgng_ref, ds_scr):
        @pl.when(pl.program_id(1) == 0)
        def _():
            ds_scr[...] = jnp.zeros_like(ds_scr)
            dgng_ref[...] = jnp.zeros_like(dgng_ref)

        q1, q2, k1, k2, v, gate = _ret_load(q_ref, k_ref, v_ref, gate_ref)
        do = do_ref[...]
        dos = jnp.stack([do[:, i * RET_DV:(i + 1) * RET_DV] for i in range(hb)]).astype(F32)
        fn = functools.partial(_ret_chunk, cos=cos_ref[...], sin=sin_ref[...], intra=intra_ref[...],
                               qdec=qdec_ref[...], kdec=kdec_ref[...], cdec=cdec_ref[...])
        _, vjp = jax.vjp(fn, q1, q2, k1, k2, v, gate, gng_ref[...], ssave_ref[0].astype(F32))
        dq1, dq2, dk1, dk2, dv, dgate, dgng, ds = vjp((dos, ds_scr[...]))
        ds_scr[...] = ds
        dgng_ref[...] += dgng
        pieces = ([t[i] for i in range(hb) for t in (dq1, dq2)] + [t[i] for i in range(hb) for t in (dk1, dk2)]
                  + [dv[i] for i in range(hb)] + [dgate[i] for i in range(hb)])
        dp_ref[...] = jnp.concatenate([t.astype(dp_ref.dtype) for t in pieces], axis=-1)

    vw = hb * RET_DV
    width = 2 * RET_QK + 2 * RET_VW
    return _grid_call(
        body, name="ret_bwd", comm=comm,
        grid=(RET_HEADS // hb, nc),
        in_specs=tok + const + [state, pl.BlockSpec((CHUNK, vw), lambda h, n: (cidx(n), h))],
        out_specs=[pl.BlockSpec((CHUNK, width), lambda h, n: (cidx(n), 0)),
                   pl.BlockSpec((hb, 1, RET_DV), lambda h, n: (h, 0, 0))],
        out_shape=[jax.ShapeDtypeStruct((seq, width), BF16),
                   jax.ShapeDtypeStruct((RET_HEADS, 1, RET_DV), F32)],
        scratch_shapes=[pltpu.VMEM((hb, RET_DK, RET_DV), F32)],
        args=(p, p, p, p, cos, sin, *_ret_consts(), gn_g.reshape(RET_HEADS, 1, RET_DV), ssave, do))


def _gdn_common(qr, kr, gc_c, gc_r):
    qn = qr * lax.rsqrt(jnp.sum(qr * qr, -1, keepdims=True) + L2_EPS) * (GDN_DK ** -0.5)
    kn = kr * lax.rsqrt(jnp.sum(kr * kr, -1, keepdims=True) + L2_EPS)
    causal = _iota2((CHUNK, CHUNK), 0) >= _iota2((CHUNK, CHUNK), 1)
    decay = jnp.exp(jnp.where(causal, gc_c - gc_r, -1e30))
    return _rep2(qn), _rep2(kn), decay


def _gdn_a(k, decay, beta_c):
    strict = _iota2((CHUNK, CHUNK), 0) > _iota2((CHUNK, CHUNK), 1)
    return jnp.where(strict, _nt(k * beta_c, k) * decay, 0.0)


@jax.custom_vjp
def _inv_saved(a, t):
    return t


_inv_saved.defvjp(lambda a, t: (t, t),
                  lambda t, dt: (-_dg(_dg(t, dt, _TN), t, _NT), jnp.zeros_like(t)))


def _gdn_chunk(qr, kr, v, z, beta_c, gc_c, gc_r, norm_g, t_saved, s):
    q, k, decay = _gdn_common(qr, kr, gc_c, gc_r)
    t = _inv_saved(_gdn_a(k, decay, beta_c), t_saved)
    return _gdn_rest(q, k, decay, v, z, beta_c, gc_c, norm_g, t, s)


def _gdn_rest(q, k, decay, v, z, beta_c, gc_c, norm_g, t, s):
    eg = jnp.exp(gc_c)
    u = _nn(t, v * beta_c)
    w = _nn(t, k * (beta_c * eg))
    attn = _nt(q, k) * decay
    v_new = u - _nn(w, s)
    y = _nn(q * eg, s) + _nn(attn, v_new)
    last = _iota2((1, CHUNK, 1), 1) == CHUNK - 1
    gl = jnp.sum(jnp.where(last, gc_c, 0.0), axis=1, keepdims=True)
    s_new = s * jnp.exp(gl) + _tn(k * jnp.exp(gl - gc_c), v_new)
    yn = y * lax.rsqrt(jnp.mean(y * y, -1, keepdims=True) + RMS_EPS) * norm_g
    return yn * _silu(z), s_new


GDN_HK = 16


def _gdn_load(q_ref, k_ref, v_ref, z_ref):
    hk, hb, d = GDN_HK, 2 * GDN_HK, GDN_DK
    q, k, v, z = q_ref[...], k_ref[...], v_ref[...], z_ref[...]
    qs = jnp.stack([q[:, i * d:(i + 1) * d] for i in range(hk)])
    ks = jnp.stack([k[:, i * d:(i + 1) * d] for i in range(hk)])
    vs = jnp.stack([v[:, i * d:(i + 1) * d] for i in range(hb)])
    zs = jnp.stack([z[:, i * d:(i + 1) * d] for i in range(hb)])
    return qs, ks, vs, zs


def _gdn_specs(n_chunks, rev):
    hk, hb = GDN_HK, 2 * GDN_HK
    cidx = (lambda n: n_chunks - 1 - n) if rev else (lambda n: n)
    qw, vw = hk * GDN_DK, hb * GDN_DV
    tok = [
        pl.BlockSpec((CHUNK, qw), lambda h, n: (cidx(n), h)),
        pl.BlockSpec((CHUNK, qw), lambda h, n: (cidx(n), GDN_QK // qw + h)),
        pl.BlockSpec((CHUNK, vw), lambda h, n: (cidx(n), 2 * GDN_QK // vw + h)),
        pl.BlockSpec((CHUNK, vw), lambda h, n: (cidx(n), GDN_QKV // vw + h)),
        pl.BlockSpec((1, hb, CHUNK, 1), lambda h, n: (cidx(n), h, 0, 0)),
        pl.BlockSpec((1, hb, CHUNK, 1), lambda h, n: (cidx(n), h, 0, 0)),
        pl.BlockSpec((1, hb, 1, CHUNK), lambda h, n: (cidx(n), h, 0, 0)),
        pl.BlockSpec((1, GDN_DV), lambda h, n: (0, 0)),
    ]
    tsave = pl.BlockSpec((1, hb, CHUNK, CHUNK), lambda h, n: (cidx(n), h, 0, 0))
    ssave = pl.BlockSpec((1, hb, GDN_DK, GDN_DV), lambda h, n: (cidx(n), h, 0, 0))
    return tok, tsave, ssave, cidx


def _gdn_fwd(c, p, beta_c, gc_c, gc_r, norm_g):
    seq = c.shape[0]
    nc = seq // CHUNK
    hk, hb = GDN_HK, 2 * GDN_HK
    tok, tsave, ssave, _ = _gdn_specs(nc, False)

    def body(q_ref, k_ref, v_ref, z_ref, beta_ref, gcc_ref, gcr_ref, ng_ref, o_ref, tsave_ref, ssave_ref, s_scr):
        @pl.when(pl.program_id(1) == 0)
        def _():
            s_scr[...] = jnp.zeros_like(s_scr)

        qr, kr, v, z = _gdn_load(q_ref, k_ref, v_ref, z_ref)
        beta, gcc, gcr = beta_ref[0], gcc_ref[0], gcr_ref[0]
        s = s_scr[...]
        ssave_ref[0] = s.astype(BF16)
        q, k, decay = _gdn_common(qr, kr, gcc, gcr)
        t = _inv_unit_lower(_gdn_a(k, decay, beta))
        tsave_ref[0] = t.astype(BF16)
        o, s_new = _gdn_rest(q, k, decay, v, z, beta, gcc, ng_ref[...], t, s)
        s_scr[...] = s_new
        o_ref[...] = jnp.concatenate([o[i] for i in range(hb)], axis=-1).astype(o_ref.dtype)

    return pl.pallas_call(
        body, name="gdn_fwd",
        grid=(GDN_K_HEADS // hk, nc),
        in_specs=tok,
        out_specs=[pl.BlockSpec((CHUNK, hb * GDN_DV), lambda h, n: (n, h)), tsave, ssave],
        out_shape=[jax.ShapeDtypeStruct((seq, GDN_VW), BF16),
                   jax.ShapeDtypeStruct((nc, GDN_V_HEADS, CHUNK, CHUNK), BF16),
                   jax.ShapeDtypeStruct((nc, GDN_V_HEADS, GDN_DK, GDN_DV), BF16)],
        scratch_shapes=[pltpu.VMEM((hb, GDN_DK, GDN_DV), F32)],
        compiler_params=_cparams(("parallel", "arbitrary")),
    )(c, c, c, p, beta_c, gc_c, gc_r, norm_g.reshape(1, GDN_DV))


def _gdn_bwd(c, p, beta_c, gc_c, gc_r, norm_g, tsave, ssave, do, comm=None):
    seq = c.shape[0]
    nc = seq // CHUNK
    hk, hb = GDN_HK, 2 * GDN_HK
    nhb = GDN_K_HEADS // hk
    tok, tsave_spec, ssave_spec, cidx = _gdn_specs(nc, True)

    assert hk == GDN_K_HEADS

    def body(q_ref, k_ref, v_ref, z_ref, beta_ref, gcc_ref, gcr_ref, ng_ref, t_ref, s_ref, do_ref,
             dc_ref, dz_ref, dbeta_ref, dgcc_ref, dgcr_ref, dng_ref, ds_scr):
        @pl.when(pl.program_id(1) == 0)
        def _():
            ds_scr[...] = jnp.zeros_like(ds_scr)
            dng_ref[...] = jnp.zeros_like(dng_ref)

        qr, kr, v, z = _gdn_load(q_ref, k_ref, v_ref, z_ref)
        beta, gcc, gcr = beta_ref[0], gcc_ref[0], gcr_ref[0]
        do = do_ref[...]
        dos = jnp.stack([do[:, i * GDN_DV:(i + 1) * GDN_DV] for i in range(hb)]).astype(F32)
        _, vjp = jax.vjp(_gdn_chunk, qr, kr, v, z, beta, gcc, gcr, ng_ref[...], t_ref[0].astype(F32),
                         s_ref[0].astype(F32))
        dqr, dkr, dv, dz, dbeta, dgcc, dgcr, dng, _, ds = vjp((dos, ds_scr[...]))
        ds_scr[...] = ds
        dng_ref[...] += dng[None]
        dc_ref[...] = jnp.concatenate([dqr[i] for i in range(hk)] + [dkr[i] for i in range(hk)]
                                      + [dv[i] for i in range(hb)], axis=-1)
        dz_ref[...] = jnp.concatenate([dz[i] for i in range(hb)], axis=-1).astype(dz_ref.dtype)
        dbeta_ref[0] = dbeta
        dgcc_ref[0] = dgcc
        dgcr_ref[0] = dgcr

    vw = hb * GDN_DV
    col = pl.BlockSpec((1, hb, CHUNK, 1), lambda h, n: (cidx(n), h, 0, 0))
    row = pl.BlockSpec((1, hb, 1, CHUNK), lambda h, n: (cidx(n), h, 0, 0))
    return _grid_call(
        body, name="gdn_bwd", comm=comm,
        grid=(nhb, nc),
        in_specs=tok + [tsave_spec, ssave_spec, pl.BlockSpec((CHUNK, vw), lambda h, n: (cidx(n), h))],
        out_specs=[pl.BlockSpec((CHUNK, GDN_QKV), lambda h, n: (cidx(n), 0)),
                   pl.BlockSpec((CHUNK, vw), lambda h, n: (cidx(n), GDN_QKV // vw)),
                   col, col, row,
                   pl.BlockSpec((1, 1, GDN_DV), lambda h, n: (h, 0, 0))],
        out_shape=[jax.ShapeDtypeStruct((seq, GDN_QKV), F32), jax.ShapeDtypeStruct((seq, GDN_QKV + GDN_VW), BF16),
                   jax.ShapeDtypeStruct((nc, GDN_V_HEADS, CHUNK, 1), F32),
                   jax.ShapeDtypeStruct((nc, GDN_V_HEADS, CHUNK, 1), F32),
                   jax.ShapeDtypeStruct((nc, GDN_V_HEADS, 1, CHUNK), F32),
                   jax.ShapeDtypeStruct((nhb, 1, GDN_DV), F32)],
        scratch_shapes=[pltpu.VMEM((hb, GDN_DK, GDN_DV), F32)],
        args=(c, c, c, p, beta_c, gc_c, gc_r, norm_g.reshape(1, GDN_DV), tsave, ssave, do))


CONV_TB = 512
CONV_CB = 1024
HALO = 8


def _conv_taps(ext, w):
    acc = w[GDN_CONV - 1:GDN_CONV] * ext
    for j in range(GDN_CONV - 1):
        acc = acc + w[j:j + 1] * pltpu.roll(ext, GDN_CONV - 1 - j, 0)
    return acc


def _conv_fwd(p, w):
    seq = p.shape[0]
    tb, cb = min(CONV_TB, seq), CONV_CB

    def body(prev_ref, cur_ref, w_ref, o_ref):
        first = pl.program_id(1) == 0
        prev = jnp.where(first, 0.0, prev_ref[...])
        ext = jnp.concatenate([prev, cur_ref[...]], axis=0)
        o_ref[...] = _silu(_conv_taps(ext, w_ref[...])[HALO:])

    return pl.pallas_call(
        body, name="conv_fwd",
        grid=(GDN_QKV // cb, seq // tb),
        in_specs=[pl.BlockSpec((HALO, cb), lambda j, i: (jnp.maximum(i * (tb // HALO) - 1, 0), j)),
                  pl.BlockSpec((tb, cb), lambda j, i: (i, j)),
                  pl.BlockSpec((GDN_CONV, cb), lambda j, i: (0, j))],
        out_specs=pl.BlockSpec((tb, cb), lambda j, i: (i, j)),
        out_shape=jax.ShapeDtypeStruct((seq, GDN_QKV), F32),
        compiler_params=_cparams(("parallel", "arbitrary")),
    )(p, p, w)


def _conv_bwd(p, dc, w, dp):
    seq = p.shape[0]
    tb, cb = min(CONV_TB, seq), CONV_CB
    nt = seq // tb
    last_halo = seq // HALO - 1

    def body(prev_ref, cur_ref, next_ref, dcur_ref, dnext_ref, w_ref, _, du_ref, dw_ref):
        i = pl.program_id(1)

        @pl.when(i == 0)
        def _():
            dw_ref[...] = jnp.zeros_like(dw_ref)

        w = w_ref[...]
        prev = jnp.where(i == 0, 0.0, prev_ref[...])
        ext = jnp.concatenate([prev, cur_ref[...], next_ref[...]], axis=0)
        pre = _conv_taps(ext, w)
        dnext = jnp.where(i == nt - 1, 0.0, dnext_ref[...])
        dext = jnp.concatenate([jnp.zeros((HALO, cb), F32), dcur_ref[...], dnext], axis=0)
        sig = jax.nn.sigmoid(pre)
        dpre = dext * (sig * (1.0 + pre * (1.0 - sig)))
        rows = tb + 2 * HALO
        du = w[GDN_CONV - 1:GDN_CONV] * dpre
        for j in range(GDN_CONV - 1):
            du = du + w[j:j + 1] * pltpu.roll(dpre, rows - (GDN_CONV - 1 - j), 0)
        du_ref[...] = du[HALO:HALO + tb].astype(du_ref.dtype)
        dcore = dpre[HALO:HALO + tb]
        dws = []
        for j in range(GDN_CONV):
            sh = ext if j == GDN_CONV - 1 else pltpu.roll(ext, GDN_CONV - 1 - j, 0)
            dws.append(jnp.sum(dcore * sh[HALO:HALO + tb], axis=0, keepdims=True))
        dw_ref[...] += jnp.concatenate(dws, axis=0)

    hb = tb // HALO
    cur = pl.BlockSpec((tb, cb), lambda j, i: (i, j))
    nxt = pl.BlockSpec((HALO, cb), lambda j, i: (jnp.minimum((i + 1) * hb, last_halo), j))
    return pl.pallas_call(
        body, name="conv_bwd",
        grid=(GDN_QKV // cb, nt),
        in_specs=[pl.BlockSpec((HALO, cb), lambda j, i: (jnp.maximum(i * hb - 1, 0), j)), cur, nxt, cur, nxt,
                  pl.BlockSpec((GDN_CONV, cb), lambda j, i: (0, j)), pl.BlockSpec(memory_space=pl.ANY)],
        out_specs=[cur, pl.BlockSpec((GDN_CONV, cb), lambda j, i: (0, j))],
        out_shape=[jax.ShapeDtypeStruct(dp.shape, dp.dtype), jax.ShapeDtypeStruct((GDN_CONV, GDN_QKV), F32)],
        input_output_aliases={6: 0},
        compiler_params=_cparams(("parallel", "arbitrary")),
    )(p, p, p, dc, dc, w, dp)


GATE_TB = 512


def _split3(g):
    hi = g.astype(BF16)
    r = g - hi.astype(F32)
    mid = r.astype(BF16)
    lo = (r - mid.astype(F32)).astype(BF16)
    return hi, mid, lo


def _tri_chunks(n, upper):
    i, j = _iota2((n, n), 0), _iota2((n, n), 1)
    tri = (i <= j) if upper else (i >= j)
    return jnp.where(tri & ((i // CHUNK) == (j // CHUNK)), 1.0, 0.0).astype(BF16)


def _tri_apply(g, upper):
    tri = _tri_chunks(g.shape[0], upper)
    return sum(jnp.dot(tri, part, preferred_element_type=F32) for part in _split3(g))


@jax.custom_vjp
def _chunk_cumsum(g):
    return _tri_apply(g, False)


_chunk_cumsum.defvjp(lambda g: (_tri_apply(g, False), None), lambda _, d: (_tri_apply(d, True),))


def _gates(b, a, a_log, dt_bias):
    z = a + dt_bias
    softplus = jnp.maximum(z, 0.0) + jnp.log1p(jnp.exp(-jnp.abs(z)))
    g = -jnp.exp(a_log) * softplus
    return jax.nn.sigmoid(b), _chunk_cumsum(g)


def _gates_fwd(b, a, a_log, dt_bias):
    seq, nh = b.shape
    tb = min(GATE_TB, seq)

    def body(b_ref, a_ref, al_ref, dt_ref, beta_ref, gc_ref):
        beta, gc = _gates(b_ref[...], a_ref[...], al_ref[...], dt_ref[...])
        beta_ref[...] = beta
        gc_ref[...] = gc

    tok = pl.BlockSpec((tb, nh), lambda i: (i, 0))
    vec = pl.BlockSpec((1, nh), lambda i: (0, 0))
    return pl.pallas_call(
        body, name="gates_fwd", grid=(seq // tb,),
        in_specs=[tok, tok, vec, vec], out_specs=[tok, tok],
        out_shape=[jax.ShapeDtypeStruct((seq, nh), F32)] * 2,
        compiler_params=_cparams(("parallel",)),
    )(b, a, a_log, dt_bias)


def _gates_bwd(b, a, a_log, dt_bias, dbeta, dgc):
    seq, nh = b.shape
    tb = min(GATE_TB, seq)

    def body(b_ref, a_ref, al_ref, dt_ref, dbeta_ref, dgc_ref, db_ref, da_ref, dal_ref, ddt_ref):
        @pl.when(pl.program_id(0) == 0)
        def _():
            dal_ref[...] = jnp.zeros_like(dal_ref)
            ddt_ref[...] = jnp.zeros_like(ddt_ref)

        _, vjp = jax.vjp(_gates, b_ref[...], a_ref[...], al_ref[...], dt_ref[...])
        db, da, dal, ddt = vjp((dbeta_ref[...], dgc_ref[...]))
        db_ref[...] = db
        da_ref[...] = da
        dal_ref[...] += dal
        ddt_ref[...] += ddt

    tok = pl.BlockSpec((tb, nh), lambda i: (i, 0))
    vec = pl.BlockSpec((1, nh), lambda i: (0, 0))
    return pl.pallas_call(
        body, name="gates_bwd", grid=(seq // tb,),
        in_specs=[tok, tok, vec, vec, tok, tok], out_specs=[tok, tok, vec, vec],
        out_shape=[jax.ShapeDtypeStruct((seq, nh), F32)] * 2 + [jax.ShapeDtypeStruct((1, nh), F32)] * 2,
        compiler_params=_cparams(("arbitrary",)),
    )(b, a, a_log, dt_bias, dbeta, dgc)


LN_TR = 256


def _ln_stats(x, s):
    z = DN_ALPHA * x + s
    mu = jnp.mean(z, -1, keepdims=True)
    zc = z - mu
    var = jnp.mean(zc * zc, -1, keepdims=True)
    rstd = lax.rsqrt(var + LN_EPS)
    return zc * rstd, rstd


def _ln_fwd(x, s, g, b):
    seq, d = x.shape
    tr = min(LN_TR, seq)

    def body(x_ref, s_ref, g_ref, b_ref, o_ref, ob_ref):
        xhat, _ = _ln_stats(x_ref[...], s_ref[...])
        y = xhat * g_ref[...] + b_ref[...]
        o_ref[...] = y
        ob_ref[...] = y.astype(BF16)

    tok = pl.BlockSpec((tr, d), lambda i: (i, 0))
    vec = pl.BlockSpec((1, d), lambda i: (0, 0))
    return pl.pallas_call(
        body, name="ln_fwd", grid=(seq // tr,),
        in_specs=[tok, tok, vec, vec], out_specs=[tok, tok],
        out_shape=[jax.ShapeDtypeStruct((seq, d), F32), jax.ShapeDtypeStruct((seq, d), BF16)],
        compiler_params=_cparams(("parallel",)),
    )(x, s, g.reshape(1, d), b.reshape(1, d))


def _ln_bwd(dy, x, s, g, res=None):
    seq, d = x.shape
    tr = min(LN_TR, seq)

    def body(*refs):
        dy_ref, x_ref, s_ref, g_ref = refs[:4]
        dz_ref, dzb_ref, dg_ref, db_ref = refs[-4:]

        @pl.when(pl.program_id(0) == 0)
        def _():
            dg_ref[...] = jnp.zeros_like(dg_ref)
            db_ref[...] = jnp.zeros_like(db_ref)

        dy = dy_ref[...] if res is None else dy_ref[...] + DN_ALPHA * refs[4][...]
        xhat, rstd = _ln_stats(x_ref[...], s_ref[...])
        dyg = dy * g_ref[...]
        m1 = jnp.mean(dyg, -1, keepdims=True)
        m2 = jnp.mean(dyg * xhat, -1, keepdims=True)
        dz = rstd * (dyg - m1 - xhat * m2)
        dz_ref[...] = dz
        dzb_ref[...] = dz.astype(BF16)
        dg_ref[...] += jnp.sum(dy * xhat, axis=0, keepdims=True)
        db_ref[...] += jnp.sum(dy, axis=0, keepdims=True)

    tok = pl.BlockSpec((tr, d), lambda i: (i, 0))
    vec = pl.BlockSpec((1, d), lambda i: (0, 0))
    return pl.pallas_call(
        body, name="ln_bwd", grid=(seq // tr,),
        in_specs=[tok, tok, tok, vec] + ([] if res is None else [tok]), out_specs=[tok, tok, vec, vec],
        out_shape=[jax.ShapeDtypeStruct((seq, d), F32), jax.ShapeDtypeStruct((seq, d), BF16),
                   jax.ShapeDtypeStruct((1, d), F32), jax.ShapeDtypeStruct((1, d), F32)],
        compiler_params=_cparams(("arbitrary",)),
    )(dy, x, s, g.reshape(1, d), *(() if res is None else (res,)))


def _ln_loss(x, s, g, b, target):
    seq, d = x.shape
    tr = min(LN_TR, seq)

    def body(x_ref, s_ref, g_ref, b_ref, t_ref, loss_ref, dy_ref):
        @pl.when(pl.program_id(0) == 0)
        def _():
            loss_ref[...] = jnp.zeros_like(loss_ref)

        xhat, _ = _ln_stats(x_ref[...], s_ref[...])
        err = xhat * g_ref[...] + b_ref[...] - t_ref[...]
        dy_ref[...] = err * (1.0 / d)
        part = jnp.sum(jnp.sum(err * err, axis=0, keepdims=True), axis=1, keepdims=True)
        loss_ref[...] += part * (0.5 / d)

    tok = pl.BlockSpec((tr, d), lambda i: (i, 0))
    vec = pl.BlockSpec((1, d), lambda i: (0, 0))
    return pl.pallas_call(
        body, name="ln_loss", grid=(seq // tr,),
        in_specs=[tok, tok, vec, vec, tok], out_specs=[pl.BlockSpec((8, 128), lambda i: (0, 0)), tok],
        out_shape=[jax.ShapeDtypeStruct((8, 128), F32), jax.ShapeDtypeStruct((seq, d), F32)],
        compiler_params=_cparams(("arbitrary",)),
    )(x, s, g.reshape(1, d), b.reshape(1, d), target)


COMM_MID = 0.8


def _matmul(a, b, *, ta=False, tb=False, b_sharded=False, out_sharded=False, out_dtypes=(F32,), epilogue=None,
            extras=(), tm=1024, tn=1024, tk=2048, name="matmul", comm=None, b_cols=None):
    m, k = (a.shape[1], a.shape[0]) if ta else a.shape
    if b_sharded:
        bk, bn = b.shape[1], N_DEV * b.shape[2]
        shard_w = b.shape[2]
    else:
        bk, bn = b.shape[0], b_cols or b.shape[1]
    n = bk if tb else bn
    assert k == (bn if tb else bk), (a.shape, b.shape)
    tm, tn, tk = min(tm, m), min(tn, n), min(tk, k)
    if b_sharded:
        if tb:
            tk = math.gcd(tk, shard_w)
        else:
            tn = math.gcd(tn, shard_w)
    if out_sharded:
        tn = math.gcd(tn, n // N_DEV)
    assert m % tm == 0 and n % tn == 0 and k % tk == 0, (m, n, k, tm, tn, tk)
    ni, nj, nk = m // tm, n // tn, k // tk
    dims = (((0 if ta else 1,), (1 if tb else 0,)), ((), ()))
    n_ex, n_out = len(extras), len(out_dtypes)
    n_ci = len(comm.ins) if comm else 0
    n_co = len(comm.out_shapes) if comm else 0
    total = ni * nj * nk
    mid_step = min(int(COMM_MID * total), total - 1)
    in_place = nk > 1 and epilogue is None and tuple(out_dtypes) == (F32,)

    def body(*refs):
        a_ref, b_ref = refs[0], refs[1]
        ex_refs = refs[2:2 + n_ex]
        ci_refs = refs[2 + n_ex:2 + n_ex + n_ci]
        out_refs = refs[2 + n_ex + n_ci:2 + n_ex + n_ci + n_out]
        co_refs = refs[2 + n_ex + n_ci + n_out:2 + n_ex + n_ci + n_out + n_co]
        scratch = refs[2 + n_ex + n_ci + n_out + n_co:]
        if nk == 1:
            acc, sems = None, scratch
        elif in_place:
            acc, sems = out_refs[0], scratch
        else:
            acc, sems = scratch[0], scratch[1:]
        kk = pl.program_id(2)
        step = (pl.program_id(0) * nj + pl.program_id(1)) * nk + kk

        if comm:
            @pl.when(step == 0)
            def _():
                comm.start(ci_refs, co_refs, *sems)

        prod = lax.dot_general(a_ref[...].astype(BF16), b_ref[...].astype(BF16), dims, preferred_element_type=F32)

        def write(res):
            outs = (res,) if epilogue is None else epilogue(res, *[r[...] for r in ex_refs])
            for o_ref, val in zip(out_refs, outs, strict=True):
                o_ref[...] = val.astype(o_ref.dtype)

        if nk == 1:
            write(prod)
        else:
            @pl.when(kk == 0)
            def _():
                acc[...] = prod

            @pl.when(kk > 0)
            def _():
                acc[...] += prod

            if not in_place:
                @pl.when(kk == nk - 1)
                def _():
                    write(acc[...])

        if comm:
            @pl.when(step == mid_step)
            def _():
                comm.mid(ci_refs, co_refs, *sems)

            @pl.when(step == total - 1)
            def _():
                comm.finish(ci_refs, co_refs, *sems)

    a_spec = pl.BlockSpec((tk, tm), lambda i, j, kk: (kk, i)) if ta else pl.BlockSpec((tm, tk), lambda i, j, kk: (i, kk))
    if b_sharded and tb:
        per = shard_w // tk
        b_spec = pl.BlockSpec((None, tn, tk), lambda i, j, kk: (kk // per, j, kk % per))
    elif b_sharded:
        per = shard_w // tn
        b_spec = pl.BlockSpec((None, tk, tn), lambda i, j, kk: (j // per, kk, j % per))
    elif tb:
        b_spec = pl.BlockSpec((tn, tk), lambda i, j, kk: (j, kk))
    else:
        b_spec = pl.BlockSpec((tk, tn), lambda i, j, kk: (kk, j))
    ex_spec = pl.BlockSpec((tm, tn), lambda i, j, kk: (i, j))
    if out_sharded:
        per_o = n // N_DEV // tn
        o_spec = pl.BlockSpec((None, tm, tn), lambda i, j, kk: (j // per_o, i, j % per_o))
        o_shape = (N_DEV, m, n // N_DEV)
    else:
        o_spec, o_shape = ex_spec, (m, n)
    hbm = pl.BlockSpec(memory_space=pl.ANY)
    outs = pl.pallas_call(
        body, name=name, grid=(ni, nj, nk),
        in_specs=[a_spec, b_spec] + [ex_spec] * n_ex + [hbm] * n_ci,
        out_specs=[o_spec] * n_out + [hbm] * n_co,
        out_shape=[jax.ShapeDtypeStruct(o_shape, dt) for dt in out_dtypes] + (list(comm.out_shapes) if comm else []),
        scratch_shapes=([] if nk == 1 or in_place else [pltpu.VMEM((tm, tn), F32)])
        + (list(comm.scratch) if comm else []),
        compiler_params=_cparams(("arbitrary",) * 3 if comm else ("parallel", "parallel", "arbitrary")),
    )(a, b, *extras, *(comm.ins if comm else ()))
    return outs[0] if len(outs) == 1 else outs


def _epi_relu2(acc):
    r = jnp.maximum(acc, 0.0)
    return acc, r * r


def _epi_drelu2(acc, pre):
    return (acc * (2.0 * jnp.maximum(pre, 0.0)),)


def _epi_add(scale):
    return lambda acc, other: (acc + scale * other,)


def _adamw(parts, w, m, v, *, rows_per_step, name, layer=None, n_layers=None, into=None, stacked_inputs=False):
    n_parts, rows, cols = parts.shape
    tr = min(rows_per_step, rows)
    assert rows % tr == 0
    first = layer * (rows // tr) if stacked_inputs else 0

    def body(p_ref, w_ref, m_ref, v_ref, *rest):
        g_ref, d_ref, mo_ref, vo_ref = rest[-4:]
        g = p_ref[0].astype(F32)
        for i in range(1, n_parts):
            g = g + p_ref[i].astype(F32)
        m_new = ADAM_B1 * m_ref[...] + (1.0 - ADAM_B1) * g
        v_new = ADAM_B2 * v_ref[...] + (1.0 - ADAM_B2) * (g * g)
        m_hat = m_new / (1.0 - ADAM_B1 ** ADAM_STEP)
        v_hat = v_new / (1.0 - ADAM_B2 ** ADAM_STEP)
        g_ref[...] = g
        d_ref[...] = -ADAM_LR * (m_hat / (jnp.sqrt(v_hat) + ADAM_EPS) + ADAM_WD * w_ref[...])
        mo_ref[...] = m_new
        vo_ref[...] = v_new

    blk = pl.BlockSpec((tr, cols), lambda i: (first + i, 0))
    if layer is None:
        out_blk, out_shape = blk, (rows, cols)
    else:
        out_blk, out_shape = pl.BlockSpec((None, tr, cols), lambda i: (layer, i, 0)), (n_layers, rows, cols)
    into = list(into or ())
    return pl.pallas_call(
        body, name=name, grid=(rows // tr,),
        in_specs=[pl.BlockSpec((n_parts, tr, cols), lambda i: (0, i, 0)), blk, blk, blk]
        + [pl.BlockSpec(memory_space=pl.ANY)] * len(into),
        out_specs=[out_blk] * 4,
        out_shape=[jax.ShapeDtypeStruct(out_shape, F32)] * 4,
        input_output_aliases={4 + k: k for k in range(len(into))},
        compiler_params=_cparams(("parallel",)),
    )(parts, w, m, v, *into)


def _position():
    return lax.axis_index("x"), lax.axis_index("y"), lax.axis_index("c")


def _comm_scratch(n):
    return [pltpu.SemaphoreType.DMA((7 * n,)), pltpu.SemaphoreType.DMA((7 * n,)), pltpu.SemaphoreType.DMA((n,))]


class _Gather:
    def __init__(self, blocks):
        self.ins = list(blocks)
        self.out_shapes = [jax.ShapeDtypeStruct((N_DEV,) + b.shape, b.dtype) for b in blocks]
        self.scratch = _comm_scratch(len(blocks))

    def _plan(self, n, ins, outs, send_sems, recv_sems, local_sems):
        x, y, c = _position()
        me, sibling = (x, y, c), (x, y, 1 - c)
        chips = [(1 - x, y), (x, 1 - y), (1 - x, 1 - y)]
        x_ref, out_ref = ins[n], outs[n]

        def slot(px, py, pc):
            return out_ref.at[4 * px + 2 * py + pc]

        def copy(k, blk, to, src=None):
            return pltpu.make_async_remote_copy(
                src_ref=slot(*blk) if src is None else src, dst_ref=slot(*blk),
                send_sem=send_sems.at[7 * n + k], recv_sem=recv_sems.at[7 * n + k], device_id=to, device_id_type=MESH)

        mine = lambda: pltpu.make_async_copy(x_ref, slot(*me), local_sems.at[n])
        first = lambda: [copy(0, me, sibling, src=x_ref)] + [copy(1 + j, me, (*chip, c), src=x_ref)
                                                             for j, chip in enumerate(chips)]
        passed = lambda j: copy(4 + j, (*chips[j], c), sibling)
        landed = lambda j: copy(1 + j, (*chips[j], c), me)
        from_sibling = lambda: [copy(0, sibling, me)] + [copy(4 + j, (*chip, 1 - c), me) for j, chip in enumerate(chips)]
        return mine, first, passed, landed, from_sibling

    def start(self, ins, outs, *sems):
        for n in range(len(self.ins)):
            mine, first, _, _, _ = self._plan(n, ins, outs, *sems)
            mine().start()
            for cp in first():
                cp.start()

    def mid(self, ins, outs, *sems):
        plans = [self._plan(n, ins, outs, *sems) for n in range(len(self.ins))]
        for j in range(3):
            for _, _, passed, landed, _ in plans:
                landed(j).wait_recv()
                passed(j).start()

    def finish(self, ins, outs, *sems):
        for n in range(len(self.ins)):
            mine, first, passed, _, from_sibling = self._plan(n, ins, outs, *sems)
            for cp in from_sibling():
                cp.wait_recv()
            for cp in first() + [passed(j) for j in range(3)]:
                cp.wait_send()
            mine().wait()


class _Exchange:
    def __init__(self, parts):
        self.ins = list(parts)
        self.out_shapes = [jax.ShapeDtypeStruct(p.shape, p.dtype) for p in parts]
        self.scratch = _comm_scratch(len(parts))

    def _plan(self, n, ins, outs, send_sems, recv_sems, local_sems):
        x, y, c = _position()
        me = 4 * x + 2 * y + c
        p_ref, out_ref = ins[n], outs[n]
        mine = lambda: pltpu.make_async_copy(p_ref.at[me], out_ref.at[me], local_sems.at[n])

        def copies(landing):
            out = []
            for k in range(1, N_DEV):
                px = 1 - x if k & 4 else x
                py = 1 - y if k & 2 else y
                pc = 1 - c if k & 1 else c
                peer_slot = 4 * px + 2 * py + pc
                out.append(pltpu.make_async_remote_copy(
                    src_ref=p_ref.at[peer_slot], dst_ref=out_ref.at[peer_slot if landing else me],
                    send_sem=send_sems.at[7 * n + k - 1], recv_sem=recv_sems.at[7 * n + k - 1],
                    device_id=(px, py, pc), device_id_type=MESH))
            return out

        return mine, copies

    def start(self, ins, outs, *sems):
        for n in range(len(self.ins)):
            mine, copies = self._plan(n, ins, outs, *sems)
            mine().start()
            for cp in copies(False):
                cp.start()

    def mid(self, ins, outs, *sems):
        pass

    def finish(self, ins, outs, *sems):
        for n in range(len(self.ins)):
            mine, copies = self._plan(n, ins, outs, *sems)
            for cp in copies(True):
                cp.wait_recv()
            for cp in copies(False):
                cp.wait_send()
            mine().wait()


def _comm_alone(comm, name):
    def body(*refs):
        n_i, n_o = len(comm.ins), len(comm.out_shapes)
        ins, outs, sems = refs[:n_i], refs[n_i:n_i + n_o], refs[n_i + n_o:]
        comm.start(ins, outs, *sems)
        comm.mid(ins, outs, *sems)
        comm.finish(ins, outs, *sems)

    hbm = pl.BlockSpec(memory_space=pl.ANY)
    return pl.pallas_call(
        body, name=name, out_shape=list(comm.out_shapes),
        in_specs=[hbm] * len(comm.ins), out_specs=[hbm] * len(comm.out_shapes),
        scratch_shapes=list(comm.scratch),
    )(*comm.ins)


RET_IN_W = 2 * RET_QK + 2 * RET_VW
GDN_IN_W = GDN_QKV + GDN_VW + 2 * GDN_V_HEADS
GDN_TAIL = 2 * GDN_V_HEADS
TAIL_PAD = 128

SMALL_SIZES = (RET_VW, GDN_V_HEADS, GDN_V_HEADS, GDN_DV, DEPTH * D_MODEL, DEPTH * D_MODEL, DEPTH * D_MODEL,
               DEPTH * D_MODEL, GDN_CONV * GDN_QKV)
SMALL_LANES = 128
SMALL_ROWS = -(-sum(SMALL_SIZES) // (8 * SMALL_LANES)) * 8


def _pack_small(*vecs):
    flat = jnp.concatenate([v.reshape(-1).astype(F32) for v in vecs])
    return jnp.pad(flat, (0, SMALL_ROWS * SMALL_LANES - flat.shape[0])).reshape(SMALL_ROWS, SMALL_LANES)


def _unpack_small(buf, shapes):
    flat, out, at = buf.reshape(-1), [], 0
    for shp in shapes:
        n = int(np.prod(shp))
        out.append(flat[at:at + n].reshape(shp))
        at += n
    return out


def _mlp_bwd(dz, h, a, r, w1, w2, name, exchange_dw2=False, da_comm=None):
    dz, dzb = dz
    da = _matmul(dzb, w2, tb=True, out_dtypes=(BF16,), epilogue=_epi_drelu2, extras=(a,), name=name + "_da",
                 comm=da_comm)
    carried = []
    if da_comm is not None:
        da, *carried = da
    dw2 = _matmul(r, dzb, ta=True, out_dtypes=(BF16,), name=name + "_dw2").reshape(N_DEV, -1, D_MODEL)
    dw1 = _matmul(h, da, ta=True, out_sharded=True, out_dtypes=(BF16,), name=name + "_dw1")
    dh = _matmul(da, w1, tb=True, b_sharded=True, tn=D_MODEL, name=name + "_dh",
                 comm=_Exchange([dw2]) if exchange_dw2 else None)
    if exchange_dw2:
        dh, dw2 = dh
    return (dh, dw1, dw2, *carried)


def _chunk_cols(t):
    seq, nh = t.shape
    return t.reshape(seq // CHUNK, CHUNK, nh).transpose(0, 2, 1)[..., None]


def _from_chunk_cols(t):
    nc, nh = t.shape[:2]
    return t[..., 0].transpose(0, 2, 1).reshape(nc * CHUNK, nh)


def _chunk_rows(t):
    seq, nh = t.shape
    return t.reshape(seq // CHUNK, CHUNK, nh).transpose(0, 2, 1)[:, :, None, :]


def _from_chunk_rows(t):
    nc, nh = t.shape[:2]
    return t[:, :, 0, :].transpose(0, 2, 1).reshape(nc * CHUNK, nh)


def kernel(x, ret_w_in, ret_gn_g, ret_w_out, gdn_w_in, gdn_conv_w, gdn_a_log, gdn_dt_bias, gdn_norm_g, gdn_w_out, ln_mix_g, ln_mix_b, mlp_w1, mlp_w2, ln_ffn_g, ln_ffn_b, loss_target, m_ret_w_in, m_ret_gn_g, m_ret_w_out, m_gdn_w_in, m_gdn_conv_w, m_gdn_a_log, m_gdn_dt_bias, m_gdn_norm_g, m_gdn_w_out, m_ln_mix_g, m_ln_mix_b, m_mlp_w1, m_mlp_w2, m_ln_ffn_g, m_ln_ffn_b, v_ret_w_in, v_ret_gn_g, v_ret_w_out, v_gdn_w_in, v_gdn_conv_w, v_gdn_a_log, v_gdn_dt_bias, v_gdn_norm_g, v_gdn_w_out, v_ln_mix_g, v_ln_mix_b, v_mlp_w1, v_mlp_w2, v_ln_ffn_g, v_ln_ffn_b):
    xt, target = x[0], loss_target[0]
    seq = xt.shape[0]
    me = 4 * lax.axis_index("x") + 2 * lax.axis_index("y") + lax.axis_index("c")

    bf = lambda t: t.astype(BF16)
    cos, sin = _rope_tables(seq)
    w_ret_in, = _comm_alone(_Gather([bf(ret_w_in[0])]), "gather_ret_in")
    conv_blk = jnp.pad(gdn_conv_w[0], ((0, HALO - GDN_CONV), (0, 0)))
    shard_in = RET_IN_W // N_DEV
    xb = bf(xt)
    p0, w_ret_out, w1_0, conv_all = _matmul(
        xb, w_ret_in, b_sharded=True, tn=shard_in, name="ret_in",
        comm=_Gather([bf(ret_w_out[0]), bf(mlp_w1[0]), conv_blk]))
    w_ret_out = w_ret_out.reshape(RET_VW, D_MODEL)
    conv_w = conv_all[:, :GDN_CONV].transpose(1, 0, 2).reshape(GDN_CONV, GDN_QKV)
    o0, s0, w2_0 = _ret_fwd(p0, cos, sin, ret_gn_g[0], comm=_Gather([bf(mlp_w2[0])]))
    w2_0 = w2_0.reshape(D_FF, D_MODEL)
    mix0 = _matmul(o0, w_ret_out, name="ret_out")
    h1, h1b = _ln_fwd(xt, mix0, ln_mix_g[0], ln_mix_b[0])
    a0, r0, gdn_in_all = _matmul(h1b, w1_0, b_sharded=True, out_dtypes=(F32, BF16), epilogue=_epi_relu2,
                                 name="mlp0_up", comm=_Gather([bf(gdn_w_in[0])]))
    m0, w_gdn_out, w1_1 = _matmul(r0, w2_0, name="mlp0_down", comm=_Gather([bf(gdn_w_out[0]), bf(mlp_w1[1])]))
    w_gdn_out = w_gdn_out.reshape(GDN_VW, D_MODEL)
    h2, h2b = _ln_fwd(h1, m0, ln_ffn_g[0], ln_ffn_b[0])

    w_gdn_in = gdn_in_all.transpose(1, 0, 2).reshape(D_MODEL, GDN_IN_W)
    main_w = GDN_IN_W - GDN_TAIL
    w_gdn_tail = jnp.pad(w_gdn_in[:, main_w:], ((0, 0), (0, TAIL_PAD - GDN_TAIL)))
    p1, w2_1 = _matmul(h2b, w_gdn_in, b_cols=main_w, name="gdn_in", comm=_Gather([bf(mlp_w2[1])]))
    w2_1 = w2_1.reshape(D_FF, D_MODEL)
    pt = _matmul(h2b, w_gdn_tail, name="gdn_in_tail")
    c1 = _conv_fwd(p1, conv_w)
    b_in, a_in = pt[:, :GDN_V_HEADS], pt[:, GDN_V_HEADS:GDN_TAIL]
    beta, gc = _gates_fwd(b_in, a_in, gdn_a_log, gdn_dt_bias)
    beta_c, gc_c, gc_r = _chunk_cols(beta), _chunk_cols(gc), _chunk_rows(gc)
    o1, t1, s1 = _gdn_fwd(c1, p1, beta_c, gc_c, gc_r, gdn_norm_g[0])
    mix1 = _matmul(o1, w_gdn_out, name="gdn_out")
    h3, h3b = _ln_fwd(h2, mix1, ln_mix_g[1], ln_mix_b[1])
    a1, r1 = _matmul(h3b, w1_1, b_sharded=True, out_dtypes=(F32, BF16), epilogue=_epi_relu2, name="mlp1_up")
    m1 = _matmul(r1, w2_1, name="mlp1_down")
    loss_blk, dh4 = _ln_loss(h3, m1, ln_ffn_g[1], ln_ffn_b[1], target)
    loss = lax.psum(loss_blk[0, 0], ("x", "y", "c"))

    dz, dzb, dg_ffn1, db_ffn1 = _ln_bwd(dh4, h3, m1, ln_ffn_g[1])
    dh3, dw1_1, dw2_1 = _mlp_bwd((dz, dzb), h3b, a1, r1, w1_1, w2_1, "mlp1")
    dz, dzb, dg_mix1, db_mix1 = _ln_bwd(dh3, h2, mix1, ln_mix_g[1], res=dz)
    do1 = _matmul(dzb, w_gdn_out, tb=True, name="gdn_out_do")
    dw_gdn_out = _matmul(o1, dzb, ta=True, out_dtypes=(BF16,), name="gdn_out_dw").reshape(N_DEV, -1, D_MODEL)
    dc1, dp1, dbeta_c, dgc_c, dgc_r, dng, x_w1_1, x_w2_1, x_gdn_out = _gdn_bwd(
        c1, p1, beta_c, gc_c, gc_r, gdn_norm_g[0], t1, s1, do1, comm=_Exchange([dw1_1, dw2_1, dw_gdn_out]))
    dp1, dconv = _conv_bwd(p1, dc1, conv_w, dp1)
    db_in, da_in, dalog, ddt = _gates_bwd(b_in, a_in, gdn_a_log, gdn_dt_bias, _from_chunk_cols(dbeta_c),
                                          _from_chunk_cols(dgc_c) + _from_chunk_rows(dgc_r))
    dpt = jnp.concatenate([db_in, da_in, jnp.zeros((seq, TAIL_PAD - GDN_TAIL), F32)], axis=-1)
    dw_gdn_main = _matmul(h2b, dp1, ta=True, out_dtypes=(BF16,), name="gdn_in_dw")
    dw_gdn_tail = _matmul(h2b, dpt, ta=True, out_dtypes=(BF16,), name="gdn_in_tail_dw")
    dw_gdn_in = jnp.concatenate([dw_gdn_main, dw_gdn_tail[:, :GDN_TAIL]], axis=-1)
    by_owner = lambda t: t.reshape(t.shape[0], N_DEV, GDN_IN_W // N_DEV).transpose(1, 0, 2)
    dw_gdn_top, dw_gdn_bot = by_owner(dw_gdn_in[:D_MODEL // 2]), by_owner(dw_gdn_in[D_MODEL // 2:])
    dh2 = _matmul(dpt, w_gdn_tail, tb=True, epilogue=_epi_add(DN_ALPHA), extras=(dz,), name="gdn_in_tail_dh")
    dh2, x_gdn_top = _matmul(dp1, w_gdn_in, tb=True, b_cols=main_w, epilogue=_epi_add(1.0), extras=(dh2,),
                             name="gdn_in_dh", comm=_Exchange([dw_gdn_top]))

    dz, dzb, dg_ffn0, db_ffn0 = _ln_bwd(dh2, h1, m0, ln_ffn_g[0])
    dh1, dw1_0, x_w2_0, x_gdn_bot = _mlp_bwd((dz, dzb), h1b, a0, r0, w1_0, w2_0, "mlp0", exchange_dw2=True,
                                             da_comm=_Exchange([dw_gdn_bot]))
    dz, dzb, dg_mix0, db_mix0 = _ln_bwd(dh1, xt, mix0, ln_mix_g[0], res=dz)
    do0 = _matmul(dzb, w_ret_out, tb=True, name="ret_out_do")
    dw_ret_out = _matmul(o0, dzb, ta=True, out_dtypes=(BF16,), name="ret_out_dw").reshape(N_DEV, -1, D_MODEL)
    dp0, dgng, x_w1_0 = _ret_bwd(p0, cos, sin, ret_gn_g[0], s0, do0, comm=_Exchange([dw1_0]))
    dw_ret_in, x_ret_out = _matmul(xb, dp0, ta=True, out_sharded=True, out_dtypes=(BF16,), tn=shard_in,
                                   name="ret_in_dw", comm=_Exchange([dw_ret_out]))
    dx, x_ret_in = _matmul(dp0, w_ret_in, tb=True, b_sharded=True, epilogue=_epi_add(DN_ALPHA), extras=(dz,),
                           tk=shard_in, name="ret_in_dx", comm=_Exchange([dw_ret_in]))

    def update(parts, w, m, v, name, **slab):
        shape = parts.shape[1:]
        outs = _adamw(parts, w.reshape(shape), m.reshape(shape), v.reshape(shape), rows_per_step=128, name=name,
                      **slab)
        return outs if slab else [t.reshape(w.shape) for t in outs]

    u_w1 = update(x_w1_1, mlp_w1[1], m_mlp_w1[1], v_mlp_w1[1], "adamw_w1_1", layer=1, n_layers=DEPTH)
    u_w1 = update(x_w1_0, mlp_w1[0], m_mlp_w1[0], v_mlp_w1[0], "adamw_w1_0", layer=0, n_layers=DEPTH, into=u_w1)
    u_w2 = update(x_w2_1, mlp_w2[1], m_mlp_w2[1], v_mlp_w2[1], "adamw_w2_1", layer=1, n_layers=DEPTH)
    u_w2 = update(x_w2_0, mlp_w2[0], m_mlp_w2[0], v_mlp_w2[0], "adamw_w2_0", layer=0, n_layers=DEPTH, into=u_w2)
    halves = dict(n_layers=2, stacked_inputs=True, rows_per_step=128)
    gdn_in_state = (gdn_w_in[0], m_gdn_w_in[0], v_gdn_w_in[0])
    u_gdn_in = _adamw(x_gdn_top, *gdn_in_state, name="adamw_gdn_in_top", layer=0, **halves)
    u_gdn_in = _adamw(x_gdn_bot, *gdn_in_state, name="adamw_gdn_in_bot", layer=1, into=u_gdn_in, **halves)
    u_gdn_in = [t.reshape(gdn_w_in.shape) for t in u_gdn_in]
    big_out = list(zip(
        update(x_ret_in, ret_w_in, m_ret_w_in, v_ret_w_in, "adamw_ret_in"),
        update(x_ret_out, ret_w_out, m_ret_w_out, v_ret_w_out, "adamw_ret_out"),
        u_gdn_in,
        update(x_gdn_out, gdn_w_out, m_gdn_w_out, v_gdn_w_out, "adamw_gdn_out"),
        u_w1, u_w2))

    small_w = (ret_gn_g, gdn_a_log, gdn_dt_bias, gdn_norm_g, ln_mix_g, ln_mix_b, ln_ffn_g, ln_ffn_b)
    small_m = (m_ret_gn_g, m_gdn_a_log, m_gdn_dt_bias, m_gdn_norm_g, m_ln_mix_g, m_ln_mix_b, m_ln_ffn_g, m_ln_ffn_b)
    small_v = (v_ret_gn_g, v_gdn_a_log, v_gdn_dt_bias, v_gdn_norm_g, v_ln_mix_g, v_ln_mix_b, v_ln_ffn_g, v_ln_ffn_b)
    small_g = (dgng, dalog, ddt, jnp.sum(dng, axis=0),
               jnp.concatenate([dg_mix0, dg_mix1]), jnp.concatenate([db_mix0, db_mix1]),
               jnp.concatenate([dg_ffn0, dg_ffn1]), jnp.concatenate([db_ffn0, db_ffn1]), dconv)
    small_parts, = _comm_alone(_Gather([_pack_small(*small_g)]), "gather_small_grads")
    zero_conv = jnp.zeros((GDN_CONV, GDN_QKV), F32)
    small_out = _adamw(small_parts, _pack_small(*small_w, zero_conv), _pack_small(*small_m, zero_conv),
                       _pack_small(*small_v, zero_conv), rows_per_step=SMALL_ROWS, name="adamw_small")
    shapes = [t.shape for t in small_w] + [(GDN_CONV, GDN_QKV)]
    small_out = [_unpack_small(t, shapes) for t in small_out]
    conv_g = lax.dynamic_slice(small_out[0][-1], (0, me * (GDN_QKV // N_DEV)), (GDN_CONV, GDN_QKV // N_DEV))
    conv_out = _adamw(conv_g[None], gdn_conv_w[0], m_gdn_conv_w[0], v_gdn_conv_w[0],
                      rows_per_step=GDN_CONV, name="adamw_conv")

    def ordered(kind):
        b, s, cv = big_out[kind], small_out[kind], conv_out[kind][None]
        return [b[0], s[0], b[1], b[2], cv, s[1], s[2], s[3], b[3], s[4], s[5], b[4], b[5], s[6], s[7]]

    return (loss, dx[None], *ordered(0), *ordered(1), *ordered(2), *ordered(3))
```

```python
import functools
import math

import jax
import jax.numpy as jnp
import numpy as np
from jax import lax
from jax.experimental import pallas as pl
from jax.experimental.pallas import tpu as pltpu

F32 = jnp.float32
BF16 = jnp.bfloat16

N_DEV = 8
D_MODEL = 2048
CHUNK = 64
RET_HEADS = 8
RET_DK = 256
RET_DV = 512
RET_QK = RET_HEADS * RET_DK
RET_VW = RET_HEADS * RET_DV
ROPE_BASE = 10000.0
GN_EPS = 1e-6
GDN_K_HEADS = 16
GDN_V_HEADS = 32
GDN_DK = 128
GDN_DV = 128
GDN_QK = GDN_K_HEADS * GDN_DK
GDN_VW = GDN_V_HEADS * GDN_DV
GDN_QKV = 2 * GDN_QK + GDN_VW
GDN_CONV = 4
RMS_EPS = 1e-6
L2_EPS = 1e-6
D_FF = 4 * D_MODEL
DEPTH = 2
DN_ALPHA = (2.0 * DEPTH) ** 0.25
LN_EPS = 1e-5
ADAM_LR = 0.001
ADAM_B1 = 0.9
ADAM_B2 = 0.999
ADAM_EPS = 1e-08
ADAM_WD = 0.01
ADAM_STEP = 10

VMEM_LIMIT = 56 * 1024 * 1024
MESH = pl.DeviceIdType.MESH


def _cparams(sem=None):
    return pltpu.CompilerParams(dimension_semantics=sem, vmem_limit_bytes=VMEM_LIMIT)


_NT = (((2,), (2,)), ((0,), (0,)))
_NN = (((2,), (1,)), ((0,), (0,)))
_TN = (((1,), (1,)), ((0,), (0,)))


def _dg(a, b, dims):
    return lax.dot_general(a.astype(BF16), b.astype(BF16), dims, preferred_element_type=F32)


@jax.custom_vjp
def _nt(a, b):
    return _dg(a, b, _NT)


@jax.custom_vjp
def _nn(a, b):
    return _dg(a, b, _NN)


@jax.custom_vjp
def _tn(a, b):
    return _dg(a, b, _TN)


_nt.defvjp(lambda a, b: (_dg(a, b, _NT), (a, b)), lambda r, g: (_nn(g, r[1]), _tn(g, r[0])))
_nn.defvjp(lambda a, b: (_dg(a, b, _NN), (a, b)), lambda r, g: (_nt(g, r[1]), _tn(r[0], g)))
_tn.defvjp(lambda a, b: (_dg(a, b, _TN), (a, b)), lambda r, g: (_nt(r[1], g), _nn(r[0], g)))


def _iota2(shape, dim):
    return lax.broadcasted_iota(jnp.int32, shape, dim)


def _inv_unit_lower(a):
    c = a.shape[-1]
    eye = (_iota2((c, c), 0) == _iota2((c, c), 1)).astype(F32)
    m = -a
    p = eye + m
    for _ in range(int(math.log2(c)) - 1):
        m = _dg(m, m, _NN)
        p = p + _dg(p, m, _NN)
    return p


def _silu(x):
    return x * jax.nn.sigmoid(x)


def _rep2(t):
    h = t.shape[0]
    return jnp.broadcast_to(t[:, None], (h, 2) + t.shape[1:]).reshape((2 * h,) + t.shape[1:])


def _ret_chunk(q1, q2, k1, k2, v, gate, gn_g, s, cos, sin, intra, qdec, kdec, cdec):
    q = jnp.concatenate([q1 * cos - q2 * sin, q1 * sin + q2 * cos], axis=-1)
    k = jnp.concatenate([k1 * cos - k2 * sin, k1 * sin + k2 * cos], axis=-1) * (RET_DK ** -0.5)
    scores = _nt(q, k) * intra
    y = _nn(scores, v) + _nn(q * qdec, s)
    s_new = s * cdec + _tn(k * kdec, v)
    mu = jnp.mean(y, -1, keepdims=True)
    yc = y - mu
    var = jnp.mean(yc * yc, -1, keepdims=True)
    o = _silu(gate) * (yc * lax.rsqrt(var + GN_EPS) * gn_g)
    return o, s_new


def _ret_consts():
    log_gamma = np.log1p(-np.exp2(-5.0 - np.arange(RET_HEADS, dtype=np.float64)))
    idx = np.arange(CHUNK, dtype=np.float64)
    lg = log_gamma[:, None]
    intra = np.exp(lg[..., None] * np.abs(idx[:, None] - idx[None, :]))
    qdec = np.exp(lg * (idx + 1.0))[..., None]
    kdec = np.exp(lg * (CHUNK - 1.0 - idx))[..., None]
    cdec = np.exp(log_gamma * CHUNK)[:, None, None]
    return [jnp.asarray(t, F32) for t in (intra, qdec, kdec, cdec)]


def _rope_tables(seq):
    half = RET_DK // 2
    inv = ROPE_BASE ** (-jnp.arange(half, dtype=F32) / half)
    ang = jnp.arange(seq).astype(F32)[:, None] * inv[None, :]
    return jnp.cos(ang), jnp.sin(ang)


RET_HB = 8


def _ret_load(q_ref, k_ref, v_ref, gate_ref):
    hb, dk, dv, h = RET_HB, RET_DK, RET_DV, RET_DK // 2
    q, k, v, gate = q_ref[...], k_ref[...], v_ref[...], gate_ref[...]
    q1 = jnp.stack([q[:, i * dk:i * dk + h] for i in range(hb)])
    q2 = jnp.stack([q[:, i * dk + h:(i + 1) * dk] for i in range(hb)])
    k1 = jnp.stack([k[:, i * dk:i * dk + h] for i in range(hb)])
    k2 = jnp.stack([k[:, i * dk + h:(i + 1) * dk] for i in range(hb)])
    vs = jnp.stack([v[:, i * dv:(i + 1) * dv] for i in range(hb)])
    gs = jnp.stack([gate[:, i * dv:(i + 1) * dv] for i in range(hb)])
    return q1, q2, k1, k2, vs, gs


def _ret_specs(n_chunks, rev):
    hb = RET_HB
    cidx = (lambda n: n_chunks - 1 - n) if rev else (lambda n: n)
    qw, vw = hb * RET_DK, hb * RET_DV
    tok = [
        pl.BlockSpec((CHUNK, qw), lambda h, n: (cidx(n), h)),
        pl.BlockSpec((CHUNK, qw), lambda h, n: (cidx(n), RET_QK // qw + h)),
        pl.BlockSpec((CHUNK, vw), lambda h, n: (cidx(n), 2 * RET_QK // vw + h)),
        pl.BlockSpec((CHUNK, vw), lambda h, n: (cidx(n), (2 * RET_QK + RET_VW) // vw + h)),
        pl.BlockSpec((CHUNK, RET_DK // 2), lambda h, n: (cidx(n), 0)),
        pl.BlockSpec((CHUNK, RET_DK // 2), lambda h, n: (cidx(n), 0)),
    ]
    const = [
        pl.BlockSpec((hb, CHUNK, CHUNK), lambda h, n: (h, 0, 0)),
        pl.BlockSpec((hb, CHUNK, 1), lambda h, n: (h, 0, 0)),
        pl.BlockSpec((hb, CHUNK, 1), lambda h, n: (h, 0, 0)),
        pl.BlockSpec((hb, 1, 1), lambda h, n: (h, 0, 0)),
        pl.BlockSpec((hb, 1, RET_DV), lambda h, n: (h, 0, 0)),
    ]
    state = pl.BlockSpec((1, hb, RET_DK, RET_DV), lambda h, n: (cidx(n), h, 0, 0))
    return tok, const, state, cidx


def _grid_call(body, *, name, grid, in_specs, out_specs, out_shape, scratch_shapes, args, comm=None):
    if comm is None:
        return pl.pallas_call(body, name=name, grid=grid, in_specs=in_specs, out_specs=out_specs, out_shape=out_shape,
                              scratch_shapes=scratch_shapes,
                              compiler_params=_cparams(("parallel", "arbitrary")))(*args)
    n_in, n_out, n_scr = len(in_specs), len(out_specs), len(scratch_shapes)
    n_ci, n_co = len(comm.ins), len(comm.out_shapes)
    total = grid[0] * grid[1]
    mid_step = min(int(COMM_MID * total), total - 1)

    def carrying(*refs):
        ins, ci = refs[:n_in], refs[n_in:n_in + n_ci]
        at = n_in + n_ci
        outs, co = refs[at:at + n_out], refs[at + n_out:at + n_out + n_co]
        at += n_out + n_co
        scr, sems = refs[at:at + n_scr], refs[at + n_scr:]
        step = pl.program_id(0) * grid[1] + pl.program_id(1)
        pl.when(step == 0)(lambda: comm.start(ci, co, *sems))
        body(*ins, *outs, *scr)
        pl.when(step == mid_step)(lambda: comm.mid(ci, co, *sems))
        pl.when(step == total - 1)(lambda: comm.finish(ci, co, *sems))

    hbm = pl.BlockSpec(memory_space=pl.ANY)
    return pl.pallas_call(
        carrying, name=name, grid=grid,
        in_specs=list(in_specs) + [hbm] * n_ci, out_specs=list(out_specs) + [hbm] * n_co,
        out_shape=list(out_shape) + list(comm.out_shapes),
        scratch_shapes=list(scratch_shapes) + list(comm.scratch),
        compiler_params=_cparams(("arbitrary", "arbitrary")))(*args, *comm.ins)


def _ret_fwd(p, cos, sin, gn_g, comm=None):
    seq = p.shape[0]
    nc = seq // CHUNK
    hb = RET_HB
    tok, const, state, _ = _ret_specs(nc, False)

    def body(q_ref, k_ref, v_ref, gate_ref, cos_ref, sin_ref, intra_ref, qdec_ref, kdec_ref, cdec_ref, gng_ref,
             o_ref, ssave_ref, s_scr):
        @pl.when(pl.program_id(1) == 0)
        def _():
            s_scr[...] = jnp.zeros_like(s_scr)

        q1, q2, k1, k2, v, gate = _ret_load(q_ref, k_ref, v_ref, gate_ref)
        s = s_scr[...]
        ssave_ref[0] = s.astype(BF16)
        o, s_new = _ret_chunk(q1, q2, k1, k2, v, gate, gng_ref[...], s, cos_ref[...], sin_ref[...],
                              intra_ref[...], qdec_ref[...], kdec_ref[...], cdec_ref[...])
        s_scr[...] = s_new
        o_ref[...] = jnp.concatenate([o[i] for i in range(hb)], axis=-1).astype(o_ref.dtype)

    return _grid_call(
        body, name="ret_fwd", comm=comm,
        grid=(RET_HEADS // hb, nc),
        in_specs=tok + const,
        out_specs=[pl.BlockSpec((CHUNK, hb * RET_DV), lambda h, n: (n, h)), state],
        out_shape=[jax.ShapeDtypeStruct((seq, RET_VW), BF16),
                   jax.ShapeDtypeStruct((nc, RET_HEADS, RET_DK, RET_DV), BF16)],
        scratch_shapes=[pltpu.VMEM((hb, RET_DK, RET_DV), F32)],
        args=(p, p, p, p, cos, sin, *_ret_consts(), gn_g.reshape(RET_HEADS, 1, RET_DV)))


def _ret_bwd(p, cos, sin, gn_g, ssave, do, comm=None):
    seq = p.shape[0]
    nc = seq // CHUNK
    hb = RET_HB
    tok, const, state, cidx = _ret_specs(nc, True)

    assert hb == RET_HEADS

    def body(q_ref, k_ref, v_ref, gate_ref, cos_ref, sin_ref, intra_ref, qdec_ref, kdec_ref, cdec_ref, gng_ref,
             ssave_ref, do_ref, dp_ref, dgng_ref, ds_scr):
        @pl.when(pl.program_id(1) == 0)
        def _():
            ds_scr[...] = jnp.zeros_like(ds_scr)
            dgng_ref[...] = jnp.zeros_like(dgng_ref)

        q1, q2, k1, k2, v, gate = _ret_load(q_ref, k_ref, v_ref, gate_ref)
        do = do_ref[...]
        dos = jnp.stack([do[:, i * RET_DV:(i + 1) * RET_DV] for i in range(hb)]).astype(F32)
        fn = functools.partial(_ret_chunk, cos=cos_ref[...], sin=sin_ref[...], intra=intra_ref[...],
                               qdec=qdec_ref[...], kdec=kdec_ref[...], cdec=cdec_ref[...])
        _, vjp = jax.vjp(fn, q1, q2, k1, k2, v, gate, gng_ref[...], ssave_ref[0].astype(F32))
        dq1, dq2, dk1, dk2, dv, dgate, dgng, ds = vjp((dos, ds_scr[...]))
        ds_scr[...] = ds
        dgng_ref[...] += dgng
        pieces = ([t[i] for i in range(hb) for t in (dq1, dq2)] + [t[i] for i in range(hb) for t in (dk1, dk2)]
                  + [dv[i] for i in range(hb)] + [dgate[i] for i in range(hb)])
        dp_ref[...] = jnp.concatenate([t.astype(dp_ref.dtype) for t in pieces], axis=-1)

    vw = hb * RET_DV
    width = 2 * RET_QK + 2 * RET_VW
    return _grid_call(
        body, name="ret_bwd", comm=comm,
        grid=(RET_HEADS // hb, nc),
        in_specs=tok + const + [state, pl.BlockSpec((CHUNK, vw), lambda h, n: (cidx(n), h))],
        out_specs=[pl.BlockSpec((CHUNK, width), lambda h, n: (cidx(n), 0)),
                   pl.BlockSpec((hb, 1, RET_DV), lambda h, n: (h, 0, 0))],
        out_shape=[jax.ShapeDtypeStruct((seq, width), BF16),
                   jax.ShapeDtypeStruct((RET_HEADS, 1, RET_DV), F32)],
        scratch_shapes=[pltpu.VMEM((hb, RET_DK, RET_DV), F32)],
        args=(p, p, p, p, cos, sin, *_ret_consts(), gn_g.reshape(RET_HEADS, 1, RET_DV), ssave, do))


def _gdn_common(qr, kr, gc_c, gc_r):
    qn = qr * lax.rsqrt(jnp.sum(qr * qr, -1, keepdims=True) + L2_EPS) * (GDN_DK ** -0.5)
    kn = kr * lax.rsqrt(jnp.sum(kr * kr, -1, keepdims=True) + L2_EPS)
    causal = _iota2((CHUNK, CHUNK), 0) >= _iota2((CHUNK, CHUNK), 1)
    decay = jnp.exp(jnp.where(causal, gc_c - gc_r, -1e30))
    return _rep2(qn), _rep2(kn), decay


def _gdn_a(k, decay, beta_c):
    strict = _iota2((CHUNK, CHUNK), 0) > _iota2((CHUNK, CHUNK), 1)
    return jnp.where(strict, _nt(k * beta_c, k) * decay, 0.0)


@jax.custom_vjp
def _inv_saved(a, t):
    return t


_inv_saved.defvjp(lambda a, t: (t, t),
                  lambda t, dt: (-_dg(_dg(t, dt, _TN), t, _NT), jnp.zeros_like(t)))


def _gdn_chunk(qr, kr, v, z, beta_c, gc_c, gc_r, norm_g, t_saved, s):
    q, k, decay = _gdn_common(qr, kr, gc_c, gc_r)
    t = _inv_saved(_gdn_a(k, decay, beta_c), t_saved)
    return _gdn_rest(q, k, decay, v, z, beta_c, gc_c, norm_g, t, s)


def _gdn_rest(q, k, decay, v, z, beta_c, gc_c, norm_g, t, s):
    eg = jnp.exp(gc_c)
    u = _nn(t, v * beta_c)
    w = _nn(t, k * (beta_c * eg))
    attn = _nt(q, k) * decay
    v_new = u - _nn(w, s)
    y = _nn(q * eg, s) + _nn(attn, v_new)
    last = _iota2((1, CHUNK, 1), 1) == CHUNK - 1
    gl = jnp.sum(jnp.where(last, gc_c, 0.0), axis=1, keepdims=True)
    s_new = s * jnp.exp(gl) + _tn(k * jnp.exp(gl - gc_c), v_new)
    yn = y * lax.rsqrt(jnp.mean(y * y, -1, keepdims=True) + RMS_EPS) * norm_g
    return yn * _silu(z), s_new


GDN_HK = 16


def _gdn_load(q_ref, k_ref, v_ref, z_ref):
    hk, hb, d = GDN_HK, 2 * GDN_HK, GDN_DK
    q, k, v, z = q_ref[...], k_ref[...], v_ref[...], z_ref[...]
    qs = jnp.stack([q[:, i * d:(i + 1) * d] for i in range(hk)])
    ks = jnp.stack([k[:, i * d:(i + 1) * d] for i in range(hk)])
    vs = jnp.stack([v[:, i * d:(i + 1) * d] for i in range(hb)])
    zs = jnp.stack([z[:, i * d:(i + 1) * d] for i in range(hb)])
    return qs, ks, vs, zs


def _gdn_specs(n_chunks, rev):
    hk, hb = GDN_HK, 2 * GDN_HK
    cidx = (lambda n: n_chunks - 1 - n) if rev else (lambda n: n)
    qw, vw = hk * GDN_DK, hb * GDN_DV
    tok = [
        pl.BlockSpec((CHUNK, qw), lambda h, n: (cidx(n), h)),
        pl.BlockSpec((CHUNK, qw), lambda h, n: (cidx(n), GDN_QK // qw + h)),
        pl.BlockSpec((CHUNK, vw), lambda h, n: (cidx(n), 2 * GDN_QK // vw + h)),
        pl.BlockSpec((CHUNK, vw), lambda h, n: (cidx(n), GDN_QKV // vw + h)),
        pl.BlockSpec((1, hb, CHUNK, 1), lambda h, n: (cidx(n), h, 0, 0)),
        pl.BlockSpec((1, hb, CHUNK, 1), lambda h, n: (cidx(n), h, 0, 0)),
        pl.BlockSpec((1, hb, 1, CHUNK), lambda h, n: (cidx(n), h, 0, 0)),
        pl.BlockSpec((1, GDN_DV), lambda h, n: (0, 0)),
    ]
    tsave = pl.BlockSpec((1, hb, CHUNK, CHUNK), lambda h, n: (cidx(n), h, 0, 0))
    ssave = pl.BlockSpec((1, hb, GDN_DK, GDN_DV), lambda h, n: (cidx(n), h, 0, 0))
    return tok, tsave, ssave, cidx


def _gdn_fwd(c, p, beta_c, gc_c, gc_r, norm_g):
    seq = c.shape[0]
    nc = seq // CHUNK
    hk, hb = GDN_HK, 2 * GDN_HK
    tok, tsave, ssave, _ = _gdn_specs(nc, False)

    def body(q_ref, k_ref, v_ref, z_ref, beta_ref, gcc_ref, gcr_ref, ng_ref, o_ref, tsave_ref, ssave_ref, s_scr):
        @pl.when(pl.program_id(1) == 0)
        def _():
            s_scr[...] = jnp.zeros_like(s_scr)

        qr, kr, v, z = _gdn_load(q_ref, k_ref, v_ref, z_ref)
        beta, gcc, gcr = beta_ref[0], gcc_ref[0], gcr_ref[0]
        s = s_scr[...]
        ssave_ref[0] = s.astype(BF16)
        q, k, decay = _gdn_common(qr, kr, gcc, gcr)
        t = _inv_unit_lower(_gdn_a(k, decay, beta))
        tsave_ref[0] = t.astype(BF16)
        o, s_new = _gdn_rest(q, k, decay, v, z, beta, gcc, ng_ref[...], t, s)
        s_scr[...] = s_new
        o_ref[...] = jnp.concatenate([o[i] for i in range(hb)], axis=-1).astype(o_ref.dtype)

    return pl.pallas_call(
        body, name="gdn_fwd",
        grid=(GDN_K_HEADS // hk, nc),
        in_specs=tok,
        out_specs=[pl.BlockSpec((CHUNK, hb * GDN_DV), lambda h, n: (n, h)), tsave, ssave],
        out_shape=[jax.ShapeDtypeStruct((seq, GDN_VW), BF16),
                   jax.ShapeDtypeStruct((nc, GDN_V_HEADS, CHUNK, CHUNK), BF16),
                   jax.ShapeDtypeStruct((nc, GDN_V_HEADS, GDN_DK, GDN_DV), BF16)],
        scratch_shapes=[pltpu.VMEM((hb, GDN_DK, GDN_DV), F32)],
        compiler_params=_cparams(("parallel", "arbitrary")),
    )(c, c, c, p, beta_c, gc_c, gc_r, norm_g.reshape(1, GDN_DV))


def _gdn_bwd(c, p, beta_c, gc_c, gc_r, norm_g, tsave, ssave, do, comm=None):
    seq = c.shape[0]
    nc = seq // CHUNK
    hk, hb = GDN_HK, 2 * GDN_HK
    nhb = GDN_K_HEADS // hk
    tok, tsave_spec, ssave_spec, cidx = _gdn_specs(nc, True)

    assert hk == GDN_K_HEADS

    def body(q_ref, k_ref, v_ref, z_ref, beta_ref, gcc_ref, gcr_ref, ng_ref, t_ref, s_ref, do_ref,
             dc_ref, dz_ref, dbeta_ref, dgcc_ref, dgcr_ref, dng_ref, ds_scr):
        @pl.when(pl.program_id(1) == 0)
        def _():
            ds_scr[...] = jnp.zeros_like(ds_scr)
            dng_ref[...] = jnp.zeros_like(dng_ref)

        qr, kr, v, z = _gdn_load(q_ref, k_ref, v_ref, z_ref)
        beta, gcc, gcr = beta_ref[0], gcc_ref[0], gcr_ref[0]
        do = do_ref[...]
        dos = jnp.stack([do[:, i * GDN_DV:(i + 1) * GDN_DV] for i in range(hb)]).astype(F32)
        _, vjp = jax.vjp(_gdn_chunk, qr, kr, v, z, beta, gcc, gcr, ng_ref[...], t_ref[0].astype(F32),
                         s_ref[0].astype(F32))
        dqr, dkr, dv, dz, dbeta, dgcc, dgcr, dng, _, ds = vjp((dos, ds_scr[...]))
        ds_scr[...] = ds
        dng_ref[...] += dng[None]
        dc_ref[...] = jnp.concatenate([dqr[i] for i in range(hk)] + [dkr[i] for i in range(hk)]
                                      + [dv[i] for i in range(hb)], axis=-1)
        dz_ref[...] = jnp.concatenate([dz[i] for i in range(hb)], axis=-1).astype(dz_ref.dtype)
        dbeta_ref[0] = dbeta
        dgcc_ref[0] = dgcc
        dgcr_ref[0] = dgcr

    vw = hb * GDN_DV
    col = pl.BlockSpec((1, hb, CHUNK, 1), lambda h, n: (cidx(n), h, 0, 0))
    row = pl.BlockSpec((1, hb, 1, CHUNK), lambda h, n: (cidx(n), h, 0, 0))
    return _grid_call(
        body, name="gdn_bwd", comm=comm,
        grid=(nhb, nc),
        in_specs=tok + [tsave_spec, ssave_spec, pl.BlockSpec((CHUNK, vw), lambda h, n: (cidx(n), h))],
        out_specs=[pl.BlockSpec((CHUNK, GDN_QKV), lambda h, n: (cidx(n), 0)),
                   pl.BlockSpec((CHUNK, vw), lambda h, n: (cidx(n), GDN_QKV // vw)),
                   col, col, row,
                   pl.BlockSpec((1, 1, GDN_DV), lambda h, n: (h, 0, 0))],
        out_shape=[jax.ShapeDtypeStruct((seq, GDN_QKV), F32), jax.ShapeDtypeStruct((seq, GDN_QKV + GDN_VW), BF16),
                   jax.ShapeDtypeStruct((nc, GDN_V_HEADS, CHUNK, 1), F32),
                   jax.ShapeDtypeStruct((nc, GDN_V_HEADS, CHUNK, 1), F32),
                   jax.ShapeDtypeStruct((nc, GDN_V_HEADS, 1, CHUNK), F32),
                   jax.ShapeDtypeStruct((nhb, 1, GDN_DV), F32)],
        scratch_shapes=[pltpu.VMEM((hb, GDN_DK, GDN_DV), F32)],
        args=(c, c, c, p, beta_c, gc_c, gc_r, norm_g.reshape(1, GDN_DV), tsave, ssave, do))


CONV_TB = 512
CONV_CB = 1024
HALO = 8


def _conv_taps(ext, w):
    acc = w[GDN_CONV - 1:GDN_CONV] * ext
    for j in range(GDN_CONV - 1):
        acc = acc + w[j:j + 1] * pltpu.roll(ext, GDN_CONV - 1 - j, 0)
    return acc


def _conv_fwd(p, w):
    seq = p.shape[0]
    tb, cb = min(CONV_TB, seq), CONV_CB

    def body(prev_ref, cur_ref, w_ref, o_ref):
        first = pl.program_id(1) == 0
        prev = jnp.where(first, 0.0, prev_ref[...])
        ext = jnp.concatenate([prev, cur_ref[...]], axis=0)
        o_ref[...] = _silu(_conv_taps(ext, w_ref[...])[HALO:])

    return pl.pallas_call(
        body, name="conv_fwd",
        grid=(GDN_QKV // cb, seq // tb),
        in_specs=[pl.BlockSpec((HALO, cb), lambda j, i: (jnp.maximum(i * (tb // HALO) - 1, 0), j)),
                  pl.BlockSpec((tb, cb), lambda j, i: (i, j)),
                  pl.BlockSpec((GDN_CONV, cb), lambda j, i: (0, j))],
        out_specs=pl.BlockSpec((tb, cb), lambda j, i: (i, j)),
        out_shape=jax.ShapeDtypeStruct((seq, GDN_QKV), F32),
        compiler_params=_cparams(("parallel", "arbitrary")),
    )(p, p, w)


def _conv_bwd(p, dc, w, dp):
    seq = p.shape[0]
    tb, cb = min(CONV_TB, seq), CONV_CB
    nt = seq // tb
    last_halo = seq // HALO - 1

    def body(prev_ref, cur_ref, next_ref, dcur_ref, dnext_ref, w_ref, _, du_ref, dw_ref):
        i = pl.program_id(1)

        @pl.when(i == 0)
        def _():
            dw_ref[...] = jnp.zeros_like(dw_ref)

        w = w_ref[...]
        prev = jnp.where(i == 0, 0.0, prev_ref[...])
        ext = jnp.concatenate([prev, cur_ref[...], next_ref[...]], axis=0)
        pre = _conv_taps(ext, w)
        dnext = jnp.where(i == nt - 1, 0.0, dnext_ref[...])
        dext = jnp.concatenate([jnp.zeros((HALO, cb), F32), dcur_ref[...], dnext], axis=0)
        sig = jax.nn.sigmoid(pre)
        dpre = dext * (sig * (1.0 + pre * (1.0 - sig)))
        rows = tb + 2 * HALO
        du = w[GDN_CONV - 1:GDN_CONV] * dpre
        for j in range(GDN_CONV - 1):
            du = du + w[j:j + 1] * pltpu.roll(dpre, rows - (GDN_CONV - 1 - j), 0)
        du_ref[...] = du[HALO:HALO + tb].astype(du_ref.dtype)
        dcore = dpre[HALO:HALO + tb]
        dws = []
        for j in range(GDN_CONV):
            sh = ext if j == GDN_CONV - 1 else pltpu.roll(ext, GDN_CONV - 1 - j, 0)
            dws.append(jnp.sum(dcore * sh[HALO:HALO + tb], axis=0, keepdims=True))
        dw_ref[...] += jnp.concatenate(dws, axis=0)

    hb = tb // HALO
    cur = pl.BlockSpec((tb, cb), lambda j, i: (i, j))
    nxt = pl.BlockSpec((HALO, cb), lambda j, i: (jnp.minimum((i + 1) * hb, last_halo), j))
    return pl.pallas_call(
        body, name="conv_bwd",
        grid=(GDN_QKV // cb, nt),
        in_specs=[pl.BlockSpec((HALO, cb), lambda j, i: (jnp.maximum(i * hb - 1, 0), j)), cur, nxt, cur, nxt,
                  pl.BlockSpec((GDN_CONV, cb), lambda j, i: (0, j)), pl.BlockSpec(memory_space=pl.ANY)],
        out_specs=[cur, pl.BlockSpec((GDN_CONV, cb), lambda j, i: (0, j))],
        out_shape=[jax.ShapeDtypeStruct(dp.shape, dp.dtype), jax.ShapeDtypeStruct((GDN_CONV, GDN_QKV), F32)],
        input_output_aliases={6: 0},
        compiler_params=_cparams(("parallel", "arbitrary")),
    )(p, p, p, dc, dc, w, dp)


GATE_TB = 512


def _split3(g):
    hi = g.astype(BF16)
    r = g - hi.astype(F32)
    mid = r.astype(BF16)
    lo = (r - mid.astype(F32)).astype(BF16)
    return hi, mid, lo


def _tri_chunks(n, upper):
    i, j = _iota2((n, n), 0), _iota2((n, n), 1)
    tri = (i <= j) if upper else (i >= j)
    return jnp.where(tri & ((i // CHUNK) == (j // CHUNK)), 1.0, 0.0).astype(BF16)


def _tri_apply(g, upper):
    tri = _tri_chunks(g.shape[0], upper)
    return sum(jnp.dot(tri, part, preferred_element_type=F32) for part in _split3(g))


@jax.custom_vjp
def _chunk_cumsum(g):
    return _tri_apply(g, False)


_chunk_cumsum.defvjp(lambda g: (_tri_apply(g, False), None), lambda _, d: (_tri_apply(d, True),))


def _gates(b, a, a_log, dt_bias):
    z = a + dt_bias
    softplus = jnp.maximum(z, 0.0) + jnp.log1p(jnp.exp(-jnp.abs(z)))
    g = -jnp.exp(a_log) * softplus
    return jax.nn.sigmoid(b), _chunk_cumsum(g)


def _gates_fwd(b, a, a_log, dt_bias):
    seq, nh = b.shape
    tb = min(GATE_TB, seq)

    def body(b_ref, a_ref, al_ref, dt_ref, beta_ref, gc_ref):
        beta, gc = _gates(b_ref[...], a_ref[...], al_ref[...], dt_ref[...])
        beta_ref[...] = beta
        gc_ref[...] = gc

    tok = pl.BlockSpec((tb, nh), lambda i: (i, 0))
    vec = pl.BlockSpec((1, nh), lambda i: (0, 0))
    return pl.pallas_call(
        body, name="gates_fwd", grid=(seq // tb,),
        in_specs=[tok, tok, vec, vec], out_specs=[tok, tok],
        out_shape=[jax.ShapeDtypeStruct((seq, nh), F32)] * 2,
        compiler_params=_cparams(("parallel",)),
    )(b, a, a_log, dt_bias)


def _gates_bwd(b, a, a_log, dt_bias, dbeta, dgc):
    seq, nh = b.shape
    tb = min(GATE_TB, seq)

    def body(b_ref, a_ref, al_ref, dt_ref, dbeta_ref, dgc_ref, db_ref, da_ref, dal_ref, ddt_ref):
        @pl.when(pl.program_id(0) == 0)
        def _():
            dal_ref[...] = jnp.zeros_like(dal_ref)
            ddt_ref[...] = jnp.zeros_like(ddt_ref)

        _, vjp = jax.vjp(_gates, b_ref[...], a_ref[...], al_ref[...], dt_ref[...])
        db, da, dal, ddt = vjp((dbeta_ref[...], dgc_ref[...]))
        db_ref[...] = db
        da_ref[...] = da
        dal_ref[...] += dal
        ddt_ref[...] += ddt

    tok = pl.BlockSpec((tb, nh), lambda i: (i, 0))
    vec = pl.BlockSpec((1, nh), lambda i: (0, 0))
    return pl.pallas_call(
        body, name="gates_bwd", grid=(seq // tb,),
        in_specs=[tok, tok, vec, vec, tok, tok], out_specs=[tok, tok, vec, vec],
        out_shape=[jax.ShapeDtypeStruct((seq, nh), F32)] * 2 + [jax.ShapeDtypeStruct((1, nh), F32)] * 2,
        compiler_params=_cparams(("arbitrary",)),
    )(b, a, a_log, dt_bias, dbeta, dgc)


LN_TR = 256


def _ln_stats(x, s):
    z = DN_ALPHA * x + s
    mu = jnp.mean(z, -1, keepdims=True)
    zc = z - mu
    var = jnp.mean(zc * zc, -1, keepdims=True)
    rstd = lax.rsqrt(var + LN_EPS)
    return zc * rstd, rstd


def _ln_fwd(x, s, g, b):
    seq, d = x.shape
    tr = min(LN_TR, seq)

    def body(x_ref, s_ref, g_ref, b_ref, o_ref, ob_ref):
        xhat, _ = _ln_stats(x_ref[...], s_ref[...])
        y = xhat * g_ref[...] + b_ref[...]
        o_ref[...] = y
        ob_ref[...] = y.astype(BF16)

    tok = pl.BlockSpec((tr, d), lambda i: (i, 0))
    vec = pl.BlockSpec((1, d), lambda i: (0, 0))
    return pl.pallas_call(
        body, name="ln_fwd", grid=(seq // tr,),
        in_specs=[tok, tok, vec, vec], out_specs=[tok, tok],
        out_shape=[jax.ShapeDtypeStruct((seq, d), F32), jax.ShapeDtypeStruct((seq, d), BF16)],
        compiler_params=_cparams(("parallel",)),
    )(x, s, g.reshape(1, d), b.reshape(1, d))


def _ln_bwd(dy, x, s, g, res=None):
    seq, d = x.shape
    tr = min(LN_TR, seq)

    def body(*refs):
        dy_ref, x_ref, s_ref, g_ref = refs[:4]
        dz_ref, dzb_ref, dg_ref, db_ref = refs[-4:]

        @pl.when(pl.program_id(0) == 0)
        def _():
            dg_ref[...] = jnp.zeros_like(dg_ref)
            db_ref[...] = jnp.zeros_like(db_ref)

        dy = dy_ref[...] if res is None else dy_ref[...] + DN_ALPHA * refs[4][...]
        xhat, rstd = _ln_stats(x_ref[...], s_ref[...])
        dyg = dy * g_ref[...]
        m1 = jnp.mean(dyg, -1, keepdims=True)
        m2 = jnp.mean(dyg * xhat, -1, keepdims=True)
        dz = rstd * (dyg - m1 - xhat * m2)
        dz_ref[...] = dz
        dzb_ref[...] = dz.astype(BF16)
        dg_ref[...] += jnp.sum(dy * xhat, axis=0, keepdims=True)
        db_ref[...] += jnp.sum(dy, axis=0, keepdims=True)

    tok = pl.BlockSpec((tr, d), lambda i: (i, 0))
    vec = pl.BlockSpec((1, d), lambda i: (0, 0))
    return pl.pallas_call(
        body, name="ln_bwd", grid=(seq // tr,),
        in_specs=[tok, tok, tok, vec] + ([] if res is None else [tok]), out_specs=[tok, tok, vec, vec],
        out_shape=[jax.ShapeDtypeStruct((seq, d), F32), jax.ShapeDtypeStruct((seq, d), BF16),
                   jax.ShapeDtypeStruct((1, d), F32), jax.ShapeDtypeStruct((1, d), F32)],
        compiler_params=_cparams(("arbitrary",)),
    )(dy, x, s, g.reshape(1, d), *(() if res is None else (res,)))


def _ln_loss(x, s, g, b, target):
    seq, d = x.shape
    tr = min(LN_TR, seq)

    def body(x_ref, s_ref, g_ref, b_ref, t_ref, loss_ref, dy_ref):
        @pl.when(pl.program_id(0) == 0)
        def _():
            loss_ref[...] = jnp.zeros_like(loss_ref)

        xhat, _ = _ln_stats(x_ref[...], s_ref[...])
        err = xhat * g_ref[...] + b_ref[...] - t_ref[...]
        dy_ref[...] = err * (1.0 / d)
        part = jnp.sum(jnp.sum(err * err, axis=0, keepdims=True), axis=1, keepdims=True)
        loss_ref[...] += part * (0.5 / d)

    tok = pl.BlockSpec((tr, d), lambda i: (i, 0))
    vec = pl.BlockSpec((1, d), lambda i: (0, 0))
    return pl.pallas_call(
        body, name="ln_loss", grid=(seq // tr,),
        in_specs=[tok, tok, vec, vec, tok], out_specs=[pl.BlockSpec((8, 128), lambda i: (0, 0)), tok],
        out_shape=[jax.ShapeDtypeStruct((8, 128), F32), jax.ShapeDtypeStruct((seq, d), F32)],
        compiler_params=_cparams(("arbitrary",)),
    )(x, s, g.reshape(1, d), b.reshape(1, d), target)


COMM_MID = 0.9


def _matmul(a, b, *, ta=False, tb=False, b_sharded=False, out_sharded=False, out_dtypes=(F32,), epilogue=None,
            extras=(), tm=1024, tn=1024, tk=2048, name="matmul", comm=None, b_cols=None):
    m, k = (a.shape[1], a.shape[0]) if ta else a.shape
    if b_sharded:
        bk, bn = b.shape[1], N_DEV * b.shape[2]
        shard_w = b.shape[2]
    else:
        bk, bn = b.shape[0], b_cols or b.shape[1]
    n = bk if tb else bn
    assert k == (bn if tb else bk), (a.shape, b.shape)
    tm, tn, tk = min(tm, m), min(tn, n), min(tk, k)
    if b_sharded:
        if tb:
            tk = math.gcd(tk, shard_w)
        else:
            tn = math.gcd(tn, shard_w)
    if out_sharded:
        tn = math.gcd(tn, n // N_DEV)
    assert m % tm == 0 and n % tn == 0 and k % tk == 0, (m, n, k, tm, tn, tk)
    ni, nj, nk = m // tm, n // tn, k // tk
    dims = (((0 if ta else 1,), (1 if tb else 0,)), ((), ()))
    n_ex, n_out = len(extras), len(out_dtypes)
    n_ci = len(comm.ins) if comm else 0
    n_co = len(comm.out_shapes) if comm else 0
    total = ni * nj * nk
    mid_step = min(int(COMM_MID * total), total - 1)
    in_place = nk > 1 and epilogue is None and tuple(out_dtypes) == (F32,)

    def body(*refs):
        a_ref, b_ref = refs[0], refs[1]
        ex_refs = refs[2:2 + n_ex]
        ci_refs = refs[2 + n_ex:2 + n_ex + n_ci]
        out_refs = refs[2 + n_ex + n_ci:2 + n_ex + n_ci + n_out]
        co_refs = refs[2 + n_ex + n_ci + n_out:2 + n_ex + n_ci + n_out + n_co]
        scratch = refs[2 + n_ex + n_ci + n_out + n_co:]
        if nk == 1:
            acc, sems = None, scratch
        elif in_place:
            acc, sems = out_refs[0], scratch
        else:
            acc, sems = scratch[0], scratch[1:]
        kk = pl.program_id(2)
        step = (pl.program_id(0) * nj + pl.program_id(1)) * nk + kk

        if comm:
            @pl.when(step == 0)
            def _():
                comm.start(ci_refs, co_refs, *sems)

        prod = lax.dot_general(a_ref[...].astype(BF16), b_ref[...].astype(BF16), dims, preferred_element_type=F32)

        def write(res):
            outs = (res,) if epilogue is None else epilogue(res, *[r[...] for r in ex_refs])
            for o_ref, val in zip(out_refs, outs, strict=True):
                o_ref[...] = val.astype(o_ref.dtype)

        if nk == 1:
            write(prod)
        else:
            @pl.when(kk == 0)
            def _():
                acc[...] = prod

            @pl.when(kk > 0)
            def _():
                acc[...] += prod

            if not in_place:
                @pl.when(kk == nk - 1)
                def _():
                    write(acc[...])

        if comm:
            @pl.when(step == mid_step)
            def _():
                comm.mid(ci_refs, co_refs, *sems)

            @pl.when(step == total - 1)
            def _():
                comm.finish(ci_refs, co_refs, *sems)

    a_spec = pl.BlockSpec((tk, tm), lambda i, j, kk: (kk, i)) if ta else pl.BlockSpec((tm, tk), lambda i, j, kk: (i, kk))
    if b_sharded and tb:
        per = shard_w // tk
        b_spec = pl.BlockSpec((None, tn, tk), lambda i, j, kk: (kk // per, j, kk % per))
    elif b_sharded:
        per = shard_w // tn
        b_spec = pl.BlockSpec((None, tk, tn), lambda i, j, kk: (j // per, kk, j % per))
    elif tb:
        b_spec = pl.BlockSpec((tn, tk), lambda i, j, kk: (j, kk))
    else:
        b_spec = pl.BlockSpec((tk, tn), lambda i, j, kk: (kk, j))
    ex_spec = pl.BlockSpec((tm, tn), lambda i, j, kk: (i, j))
    if out_sharded:
        per_o = n // N_DEV // tn
        o_spec = pl.BlockSpec((None, tm, tn), lambda i, j, kk: (j // per_o, i, j % per_o))
        o_shape = (N_DEV, m, n // N_DEV)
    else:
        o_spec, o_shape = ex_spec, (m, n)
    hbm = pl.BlockSpec(memory_space=pl.ANY)
    outs = pl.pallas_call(
        body, name=name, grid=(ni, nj, nk),
        in_specs=[a_spec, b_spec] + [ex_spec] * n_ex + [hbm] * n_ci,
        out_specs=[o_spec] * n_out + [hbm] * n_co,
        out_shape=[jax.ShapeDtypeStruct(o_shape, dt) for dt in out_dtypes] + (list(comm.out_shapes) if comm else []),
        scratch_shapes=([] if nk == 1 or in_place else [pltpu.VMEM((tm, tn), F32)])
        + (list(comm.scratch) if comm else []),
        compiler_params=_cparams(("arbitrary",) * 3 if comm else ("parallel", "parallel", "arbitrary")),
    )(a, b, *extras, *(comm.ins if comm else ()))
    return outs[0] if len(outs) == 1 else outs


def _epi_relu2(acc):
    r = jnp.maximum(acc, 0.0)
    return acc, r * r


def _epi_drelu2(acc, pre):
    return (acc * (2.0 * jnp.maximum(pre, 0.0)),)


def _epi_add(scale):
    return lambda acc, other: (acc + scale * other,)


def _adamw(parts, w, m, v, *, rows_per_step, name, layer=None, n_layers=None, into=None, stacked_inputs=False):
    n_parts, rows, cols = parts.shape
    tr = min(rows_per_step, rows)
    assert rows % tr == 0
    first = layer * (rows // tr) if stacked_inputs else 0

    def body(p_ref, w_ref, m_ref, v_ref, *rest):
        g_ref, d_ref, mo_ref, vo_ref = rest[-4:]
        g = p_ref[0].astype(F32)
        for i in range(1, n_parts):
            g = g + p_ref[i].astype(F32)
        m_new = ADAM_B1 * m_ref[...] + (1.0 - ADAM_B1) * g
        v_new = ADAM_B2 * v_ref[...] + (1.0 - ADAM_B2) * (g * g)
        m_hat = m_new / (1.0 - ADAM_B1 ** ADAM_STEP)
        v_hat = v_new / (1.0 - ADAM_B2 ** ADAM_STEP)
        g_ref[...] = g
        d_ref[...] = -ADAM_LR * (m_hat / (jnp.sqrt(v_hat) + ADAM_EPS) + ADAM_WD * w_ref[...])
        mo_ref[...] = m_new
        vo_ref[...] = v_new

    blk = pl.BlockSpec((tr, cols), lambda i: (first + i, 0))
    if layer is None:
        out_blk, out_shape = blk, (rows, cols)
    else:
        out_blk, out_shape = pl.BlockSpec((None, tr, cols), lambda i: (layer, i, 0)), (n_layers, rows, cols)
    into = list(into or ())
    return pl.pallas_call(
        body, name=name, grid=(rows // tr,),
        in_specs=[pl.BlockSpec((n_parts, tr, cols), lambda i: (0, i, 0)), blk, blk, blk]
        + [pl.BlockSpec(memory_space=pl.ANY)] * len(into),
        out_specs=[out_blk] * 4,
        out_shape=[jax.ShapeDtypeStruct(out_shape, F32)] * 4,
        input_output_aliases={4 + k: k for k in range(len(into))},
        compiler_params=_cparams(("parallel",)),
    )(parts, w, m, v, *into)


def _position():
    return lax.axis_index("x"), lax.axis_index("y"), lax.axis_index("c")


def _comm_scratch(n):
    return [pltpu.SemaphoreType.DMA((7 * n,)), pltpu.SemaphoreType.DMA((7 * n,)), pltpu.SemaphoreType.DMA((n,))]


class _Gather:
    def __init__(self, blocks):
        self.ins = list(blocks)
        self.out_shapes = [jax.ShapeDtypeStruct((N_DEV,) + b.shape, b.dtype) for b in blocks]
        self.scratch = _comm_scratch(len(blocks))

    def _plan(self, n, ins, outs, send_sems, recv_sems, local_sems):
        x, y, c = _position()
        me, sibling = (x, y, c), (x, y, 1 - c)
        chips = [(1 - x, y), (x, 1 - y), (1 - x, 1 - y)]
        x_ref, out_ref = ins[n], outs[n]

        def slot(px, py, pc):
            return out_ref.at[4 * px + 2 * py + pc]

        def copy(k, blk, to, src=None):
            return pltpu.make_async_remote_copy(
                src_ref=slot(*blk) if src is None else src, dst_ref=slot(*blk),
                send_sem=send_sems.at[7 * n + k], recv_sem=recv_sems.at[7 * n + k], device_id=to, device_id_type=MESH)

        mine = lambda: pltpu.make_async_copy(x_ref, slot(*me), local_sems.at[n])
        first = lambda: [copy(0, me, sibling, src=x_ref)] + [copy(1 + j, me, (*chip, c), src=x_ref)
                                                             for j, chip in enumerate(chips)]
        passed = lambda j: copy(4 + j, (*chips[j], c), sibling)
        landed = lambda j: copy(1 + j, (*chips[j], c), me)
        from_sibling = lambda: [copy(0, sibling, me)] + [copy(4 + j, (*chip, 1 - c), me) for j, chip in enumerate(chips)]
        return mine, first, passed, landed, from_sibling

    def start(self, ins, outs, *sems):
        for n in range(len(self.ins)):
            mine, first, _, _, _ = self._plan(n, ins, outs, *sems)
            mine().start()
            for cp in first():
                cp.start()

    def mid(self, ins, outs, *sems):
        plans = [self._plan(n, ins, outs, *sems) for n in range(len(self.ins))]
        for j in range(3):
            for _, _, passed, landed, _ in plans:
                landed(j).wait_recv()
                passed(j).start()

    def finish(self, ins, outs, *sems):
        for n in range(len(self.ins)):
            mine, first, passed, _, from_sibling = self._plan(n, ins, outs, *sems)
            for cp in from_sibling():
                cp.wait_recv()
            for cp in first() + [passed(j) for j in range(3)]:
                cp.wait_send()
            mine().wait()


class _Exchange:
    def __init__(self, parts):
        self.ins = list(parts)
        self.out_shapes = [jax.ShapeDtypeStruct(p.shape, p.dtype) for p in parts]
        self.scratch = _comm_scratch(len(parts))

    def _plan(self, n, ins, outs, send_sems, recv_sems, local_sems):
        x, y, c = _position()
        me = 4 * x + 2 * y + c
        p_ref, out_ref = ins[n], outs[n]
        mine = lambda: pltpu.make_async_copy(p_ref.at[me], out_ref.at[me], local_sems.at[n])

        def copies(landing):
            out = []
            for k in range(1, N_DEV):
                px = 1 - x if k & 4 else x
                py = 1 - y if k & 2 else y
                pc = 1 - c if k & 1 else c
                peer_slot = 4 * px + 2 * py + pc
                out.append(pltpu.make_async_remote_copy(
                    src_ref=p_ref.at[peer_slot], dst_ref=out_ref.at[peer_slot if landing else me],
                    send_sem=send_sems.at[7 * n + k - 1], recv_sem=recv_sems.at[7 * n + k - 1],
                    device_id=(px, py, pc), device_id_type=MESH))
            return out

        return mine, copies

    def start(self, ins, outs, *sems):
        for n in range(len(self.ins)):
            mine, copies = self._plan(n, ins, outs, *sems)
            mine().start()
            for cp in copies(False):
                cp.start()

    def mid(self, ins, outs, *sems):
        pass

    def finish(self, ins, outs, *sems):
        for n in range(len(self.ins)):
            mine, copies = self._plan(n, ins, outs, *sems)
            for cp in copies(True):
                cp.wait_recv()
            for cp in copies(False):
                cp.wait_send()
            mine().wait()


def _comm_alone(comm, name):
    def body(*refs):
        n_i, n_o = len(comm.ins), len(comm.out_shapes)
        ins, outs, sems = refs[:n_i], refs[n_i:n_i + n_o], refs[n_i + n_o:]
        comm.start(ins, outs, *sems)
        comm.mid(ins, outs, *sems)
        comm.finish(ins, outs, *sems)

    hbm = pl.BlockSpec(memory_space=pl.ANY)
    return pl.pallas_call(
        body, name=name, out_shape=list(comm.out_shapes),
        in_specs=[hbm] * len(comm.ins), out_specs=[hbm] * len(comm.out_shapes),
        scratch_shapes=list(comm.scratch),
    )(*comm.ins)


RET_IN_W = 2 * RET_QK + 2 * RET_VW
GDN_IN_W = GDN_QKV + GDN_VW + 2 * GDN_V_HEADS
GDN_TAIL = 2 * GDN_V_HEADS
TAIL_PAD = 128

SMALL_SIZES = (RET_VW, GDN_V_HEADS, GDN_V_HEADS, GDN_DV, DEPTH * D_MODEL, DEPTH * D_MODEL, DEPTH * D_MODEL,
               DEPTH * D_MODEL, GDN_CONV * GDN_QKV)
SMALL_LANES = 128
SMALL_ROWS = -(-sum(SMALL_SIZES) // (8 * SMALL_LANES)) * 8


def _pack_small(*vecs):
    flat = jnp.concatenate([v.reshape(-1).astype(F32) for v in vecs])
    return jnp.pad(flat, (0, SMALL_ROWS * SMALL_LANES - flat.shape[0])).reshape(SMALL_ROWS, SMALL_LANES)


def _unpack_small(buf, shapes):
    flat, out, at = buf.reshape(-1), [], 0
    for shp in shapes:
        n = int(np.prod(shp))
        out.append(flat[at:at + n].reshape(shp))
        at += n
    return out


def _mlp_bwd(dz, h, a, r, w1, w2, name, exchange_dw2=False, da_comm=None):
    dz, dzb = dz
    da = _matmul(dzb, w2, tb=True, out_dtypes=(BF16,), epilogue=_epi_drelu2, extras=(a,), name=name + "_da",
                 comm=da_comm)
    carried = []
    if da_comm is not None:
        da, *carried = da
    dw2 = _matmul(r, dzb, ta=True, out_dtypes=(BF16,), name=name + "_dw2").reshape(N_DEV, -1, D_MODEL)
    dw1 = _matmul(h, da, ta=True, out_sharded=True, out_dtypes=(BF16,), name=name + "_dw1")
    dh = _matmul(da, w1, tb=True, b_sharded=True, tn=D_MODEL, name=name + "_dh",
                 comm=_Exchange([dw2]) if exchange_dw2 else None)
    if exchange_dw2:
        dh, dw2 = dh
    return (dh, dw1, dw2, *carried)


def _chunk_cols(t):
    seq, nh = t.shape
    return t.reshape(seq // CHUNK, CHUNK, nh).transpose(0, 2, 1)[..., None]


def _from_chunk_cols(t):
    nc, nh = t.shape[:2]
    return t[..., 0].transpose(0, 2, 1).reshape(nc * CHUNK, nh)


def _chunk_rows(t):
    seq, nh = t.shape
    return t.reshape(seq // CHUNK, CHUNK, nh).transpose(0, 2, 1)[:, :, None, :]


def _from_chunk_rows(t):
    nc, nh = t.shape[:2]
    return t[:, :, 0, :].transpose(0, 2, 1).reshape(nc * CHUNK, nh)


def kernel(x, ret_w_in, ret_gn_g, ret_w_out, gdn_w_in, gdn_conv_w, gdn_a_log, gdn_dt_bias, gdn_norm_g, gdn_w_out, ln_mix_g, ln_mix_b, mlp_w1, mlp_w2, ln_ffn_g, ln_ffn_b, loss_target, m_ret_w_in, m_ret_gn_g, m_ret_w_out, m_gdn_w_in, m_gdn_conv_w, m_gdn_a_log, m_gdn_dt_bias, m_gdn_norm_g, m_gdn_w_out, m_ln_mix_g, m_ln_mix_b, m_mlp_w1, m_mlp_w2, m_ln_ffn_g, m_ln_ffn_b, v_ret_w_in, v_ret_gn_g, v_ret_w_out, v_gdn_w_in, v_gdn_conv_w, v_gdn_a_log, v_gdn_dt_bias, v_gdn_norm_g, v_gdn_w_out, v_ln_mix_g, v_ln_mix_b, v_mlp_w1, v_mlp_w2, v_ln_ffn_g, v_ln_ffn_b):
    xt, target = x[0], loss_target[0]
    seq = xt.shape[0]
    me = 4 * lax.axis_index("x") + 2 * lax.axis_index("y") + lax.axis_index("c")

    bf = lambda t: t.astype(BF16)
    cos, sin = _rope_tables(seq)
    w_ret_in, = _comm_alone(_Gather([bf(ret_w_in[0])]), "gather_ret_in")
    conv_blk = jnp.pad(gdn_conv_w[0], ((0, HALO - GDN_CONV), (0, 0)))
    shard_in = RET_IN_W // N_DEV
    xb = bf(xt)
    p0, w_ret_out, w1_0, conv_all = _matmul(
        xb, w_ret_in, b_sharded=True, tn=shard_in, name="ret_in",
        comm=_Gather([bf(ret_w_out[0]), bf(mlp_w1[0]), conv_blk]))
    w_ret_out = w_ret_out.reshape(RET_VW, D_MODEL)
    conv_w = conv_all[:, :GDN_CONV].transpose(1, 0, 2).reshape(GDN_CONV, GDN_QKV)
    o0, s0, w2_0 = _ret_fwd(p0, cos, sin, ret_gn_g[0], comm=_Gather([bf(mlp_w2[0])]))
    w2_0 = w2_0.reshape(D_FF, D_MODEL)
    mix0 = _matmul(o0, w_ret_out, name="ret_out")
    h1, h1b = _ln_fwd(xt, mix0, ln_mix_g[0], ln_mix_b[0])
    a0, r0, gdn_in_all = _matmul(h1b, w1_0, b_sharded=True, out_dtypes=(F32, BF16), epilogue=_epi_relu2,
                                 name="mlp0_up", comm=_Gather([bf(gdn_w_in[0])]))
    m0, w_gdn_out, w1_1 = _matmul(r0, w2_0, name="mlp0_down", comm=_Gather([bf(gdn_w_out[0]), bf(mlp_w1[1])]))
    w_gdn_out = w_gdn_out.reshape(GDN_VW, D_MODEL)
    h2, h2b = _ln_fwd(h1, m0, ln_ffn_g[0], ln_ffn_b[0])

    w_gdn_in = gdn_in_all.transpose(1, 0, 2).reshape(D_MODEL, GDN_IN_W)
    main_w = GDN_IN_W - GDN_TAIL
    w_gdn_tail = jnp.pad(w_gdn_in[:, main_w:], ((0, 0), (0, TAIL_PAD - GDN_TAIL)))
    p1, w2_1 = _matmul(h2b, w_gdn_in, b_cols=main_w, name="gdn_in", comm=_Gather([bf(mlp_w2[1])]))
    w2_1 = w2_1.reshape(D_FF, D_MODEL)
    pt = _matmul(h2b, w_gdn_tail, name="gdn_in_tail")
    c1 = _conv_fwd(p1, conv_w)
    b_in, a_in = pt[:, :GDN_V_HEADS], pt[:, GDN_V_HEADS:GDN_TAIL]
    beta, gc = _gates_fwd(b_in, a_in, gdn_a_log, gdn_dt_bias)
    beta_c, gc_c, gc_r = _chunk_cols(beta), _chunk_cols(gc), _chunk_rows(gc)
    o1, t1, s1 = _gdn_fwd(c1, p1, beta_c, gc_c, gc_r, gdn_norm_g[0])
    mix1 = _matmul(o1, w_gdn_out, name="gdn_out")
    h3, h3b = _ln_fwd(h2, mix1, ln_mix_g[1], ln_mix_b[1])
    a1, r1 = _matmul(h3b, w1_1, b_sharded=True, out_dtypes=(F32, BF16), epilogue=_epi_relu2, name="mlp1_up")
    m1 = _matmul(r1, w2_1, name="mlp1_down")
    loss_blk, dh4 = _ln_loss(h3, m1, ln_ffn_g[1], ln_ffn_b[1], target)
    loss = lax.psum(loss_blk[0, 0], ("x", "y", "c"))

    dz, dzb, dg_ffn1, db_ffn1 = _ln_bwd(dh4, h3, m1, ln_ffn_g[1])
    dh3, dw1_1, dw2_1 = _mlp_bwd((dz, dzb), h3b, a1, r1, w1_1, w2_1, "mlp1")
    dz, dzb, dg_mix1, db_mix1 = _ln_bwd(dh3, h2, mix1, ln_mix_g[1], res=dz)
    do1 = _matmul(dzb, w_gdn_out, tb=True, name="gdn_out_do")
    dw_gdn_out = _matmul(o1, dzb, ta=True, out_dtypes=(BF16,), name="gdn_out_dw").reshape(N_DEV, -1, D_MODEL)
    dc1, dp1, dbeta_c, dgc_c, dgc_r, dng, x_w1_1, x_w2_1, x_gdn_out = _gdn_bwd(
        c1, p1, beta_c, gc_c, gc_r, gdn_norm_g[0], t1, s1, do1, comm=_Exchange([dw1_1, dw2_1, dw_gdn_out]))
    dp1, dconv = _conv_bwd(p1, dc1, conv_w, dp1)
    db_in, da_in, dalog, ddt = _gates_bwd(b_in, a_in, gdn_a_log, gdn_dt_bias, _from_chunk_cols(dbeta_c),
                                          _from_chunk_cols(dgc_c) + _from_chunk_rows(dgc_r))
    dpt = jnp.concatenate([db_in, da_in, jnp.zeros((seq, TAIL_PAD - GDN_TAIL), F32)], axis=-1)
    dw_gdn_main = _matmul(h2b, dp1, ta=True, out_dtypes=(BF16,), name="gdn_in_dw")
    dw_gdn_tail = _matmul(h2b, dpt, ta=True, out_dtypes=(BF16,), name="gdn_in_tail_dw")
    dw_gdn_in = jnp.concatenate([dw_gdn_main, dw_gdn_tail[:, :GDN_TAIL]], axis=-1)
    by_owner = lambda t: t.reshape(t.shape[0], N_DEV, GDN_IN_W // N_DEV).transpose(1, 0, 2)
    dw_gdn_top, dw_gdn_bot = by_owner(dw_gdn_in[:D_MODEL // 2]), by_owner(dw_gdn_in[D_MODEL // 2:])
    dh2 = _matmul(dpt, w_gdn_tail, tb=True, epilogue=_epi_add(DN_ALPHA), extras=(dz,), name="gdn_in_tail_dh")
    dh2, x_gdn_top = _matmul(dp1, w_gdn_in, tb=True, b_cols=main_w, epilogue=_epi_add(1.0), extras=(dh2,),
                             name="gdn_in_dh", comm=_Exchange([dw_gdn_top]))

    dz, dzb, dg_ffn0, db_ffn0 = _ln_bwd(dh2, h1, m0, ln_ffn_g[0])
    dh1, dw1_0, x_w2_0, x_gdn_bot = _mlp_bwd((dz, dzb), h1b, a0, r0, w1_0, w2_0, "mlp0", exchange_dw2=True,
                                             da_comm=_Exchange([dw_gdn_bot]))
    dz, dzb, dg_mix0, db_mix0 = _ln_bwd(dh1, xt, mix0, ln_mix_g[0], res=dz)
    do0 = _matmul(dzb, w_ret_out, tb=True, name="ret_out_do")
    dw_ret_out = _matmul(o0, dzb, ta=True, out_dtypes=(BF16,), name="ret_out_dw").reshape(N_DEV, -1, D_MODEL)
    dp0, dgng, x_w1_0 = _ret_bwd(p0, cos, sin, ret_gn_g[0], s0, do0, comm=_Exchange([dw1_0]))
    dw_ret_in, x_ret_out = _matmul(xb, dp0, ta=True, out_sharded=True, out_dtypes=(BF16,), tn=shard_in,
                                   name="ret_in_dw", comm=_Exchange([dw_ret_out]))
    dx, x_ret_in = _matmul(dp0, w_ret_in, tb=True, b_sharded=True, epilogue=_epi_add(DN_ALPHA), extras=(dz,),
                           tk=shard_in, name="ret_in_dx", comm=_Exchange([dw_ret_in]))

    def update(parts, w, m, v, name, **slab):
        shape = parts.shape[1:]
        outs = _adamw(parts, w.reshape(shape), m.reshape(shape), v.reshape(shape), rows_per_step=128, name=name,
                      **slab)
        return outs if slab else [t.reshape(w.shape) for t in outs]

    u_w1 = update(x_w1_1, mlp_w1[1], m_mlp_w1[1], v_mlp_w1[1], "adamw_w1_1", layer=1, n_layers=DEPTH)
    u_w1 = update(x_w1_0, mlp_w1[0], m_mlp_w1[0], v_mlp_w1[0], "adamw_w1_0", layer=0, n_layers=DEPTH, into=u_w1)
    u_w2 = update(x_w2_1, mlp_w2[1], m_mlp_w2[1], v_mlp_w2[1], "adamw_w2_1", layer=1, n_layers=DEPTH)
    u_w2 = update(x_w2_0, mlp_w2[0], m_mlp_w2[0], v_mlp_w2[0], "adamw_w2_0", layer=0, n_layers=DEPTH, into=u_w2)
    halves = dict(n_layers=2, stacked_inputs=True, rows_per_step=128)
    gdn_in_state = (gdn_w_in[0], m_gdn_w_in[0], v_gdn_w_in[0])
    u_gdn_in = _adamw(x_gdn_top, *gdn_in_state, name="adamw_gdn_in_top", layer=0, **halves)
    u_gdn_in = _adamw(x_gdn_bot, *gdn_in_state, name="adamw_gdn_in_bot", layer=1, into=u_gdn_in, **halves)
    u_gdn_in = [t.reshape(gdn_w_in.shape) for t in u_gdn_in]
    big_out = list(zip(
        update(x_ret_in, ret_w_in, m_ret_w_in, v_ret_w_in, "adamw_ret_in"),
        update(x_ret_out, ret_w_out, m_ret_w_out, v_ret_w_out, "adamw_ret_out"),
        u_gdn_in,
        update(x_gdn_out, gdn_w_out, m_gdn_w_out, v_gdn_w_out, "adamw_gdn_out"),
        u_w1, u_w2))

    small_w = (ret_gn_g, gdn_a_log, gdn_dt_bias, gdn_norm_g, ln_mix_g, ln_mix_b, ln_ffn_g, ln_ffn_b)
    small_m = (m_ret_gn_g, m_gdn_a_log, m_gdn_dt_bias, m_gdn_norm_g, m_ln_mix_g, m_ln_mix_b, m_ln_ffn_g, m_ln_ffn_b)
    small_v = (v_ret_gn_g, v_gdn_a_log, v_gdn_dt_bias, v_gdn_norm_g, v_ln_mix_g, v_ln_mix_b, v_ln_ffn_g, v_ln_ffn_b)
    small_g = (dgng, dalog, ddt, jnp.sum(dng, axis=0),
               jnp.concatenate([dg_mix0, dg_mix1]), jnp.concatenate([db_mix0, db_mix1]),
               jnp.concatenate([dg_ffn0, dg_ffn1]), jnp.concatenate([db_ffn0, db_ffn1]), dconv)
    small_parts, = _comm_alone(_Gather([_pack_small(*small_g)]), "gather_small_grads")
    zero_conv = jnp.zeros((GDN_CONV, GDN_QKV), F32)
    small_out = _adamw(small_parts, _pack_small(*small_w, zero_conv), _pack_small(*small_m, zero_conv),
                       _pack_small(*small_v, zero_conv), rows_per_step=SMALL_ROWS, name="adamw_small")
    shapes = [t.shape for t in small_w] + [(GDN_CONV, GDN_QKV)]
    small_out = [_unpack_small(t, shapes) for t in small_out]
    conv_g = lax.dynamic_slice(small_out[0][-1], (0, me * (GDN_QKV // N_DEV)), (GDN_CONV, GDN_QKV // N_DEV))
    conv_out = _adamw(conv_g[None], gdn_conv_w[0], m_gdn_conv_w[0], v_gdn_conv_w[0],
                      rows_per_step=GDN_CONV, name="adamw_conv")

    def ordered(kind):
        b, s, cv = big_out[kind], small_out[kind], conv_out[kind][None]
        return [b[0], s[0], b[1], b[2], cv, s[1], s[2], s[3], b[3], s[4], s[5], b[4], b[5], s[6], s[7]]

    return (loss, dx[None], *ordered(0), *ordered(1), *ordered(2), *ordered(3))
```

```python
import functools
import math

import jax
import jax.numpy as jnp
import numpy as np
from jax import lax
from jax.experimental import pallas as pl
from jax.experimental.pallas import tpu as pltpu

F32 = jnp.float32
BF16 = jnp.bfloat16

N_DEV = 8
D_MODEL = 2048
CHUNK = 64
RET_HEADS = 8
RET_DK = 256
RET_DV = 512
RET_QK = RET_HEADS * RET_DK
RET_VW = RET_HEADS * RET_DV
ROPE_BASE = 10000.0
GN_EPS = 1e-6
GDN_K_HEADS = 16
GDN_V_HEADS = 32
GDN_DK = 128
GDN_DV = 128
GDN_QK = GDN_K_HEADS * GDN_DK
GDN_VW = GDN_V_HEADS * GDN_DV
GDN_QKV = 2 * GDN_QK + GDN_VW
GDN_CONV = 4
RMS_EPS = 1e-6
L2_EPS = 1e-6
D_FF = 4 * D_MODEL
DEPTH = 2
DN_ALPHA = (2.0 * DEPTH) ** 0.25
LN_EPS = 1e-5
ADAM_LR = 0.001
ADAM_B1 = 0.9
ADAM_B2 = 0.999
ADAM_EPS = 1e-08
ADAM_WD = 0.01
ADAM_STEP = 10

VMEM_LIMIT = 56 * 1024 * 1024
MESH = pl.DeviceIdType.MESH


def _cparams(sem=None):
    return pltpu.CompilerParams(dimension_semantics=sem, vmem_limit_bytes=VMEM_LIMIT)


_NT = (((2,), (2,)), ((0,), (0,)))
_NN = (((2,), (1,)), ((0,), (0,)))
_TN = (((1,), (1,)), ((0,), (0,)))


def _dg(a, b, dims):
    return lax.dot_general(a.astype(BF16), b.astype(BF16), dims, preferred_element_type=F32)


@jax.custom_vjp
def _nt(a, b):
    return _dg(a, b, _NT)


@jax.custom_vjp
def _nn(a, b):
    return _dg(a, b, _NN)


@jax.custom_vjp
def _tn(a, b):
    return _dg(a, b, _TN)


_nt.defvjp(lambda a, b: (_dg(a, b, _NT), (a, b)), lambda r, g: (_nn(g, r[1]), _tn(g, r[0])))
_nn.defvjp(lambda a, b: (_dg(a, b, _NN), (a, b)), lambda r, g: (_nt(g, r[1]), _tn(r[0], g)))
_tn.defvjp(lambda a, b: (_dg(a, b, _TN), (a, b)), lambda r, g: (_nt(r[1], g), _nn(r[0], g)))


def _iota2(shape, dim):
    return lax.broadcasted_iota(jnp.int32, shape, dim)


def _inv_unit_lower(a):
    c = a.shape[-1]
    eye = (_iota2((c, c), 0) == _iota2((c, c), 1)).astype(F32)
    m = -a
    p = eye + m
    for _ in range(int(math.log2(c)) - 1):
        m = _dg(m, m, _NN)
        p = p + _dg(p, m, _NN)
    return p


def _silu(x):
    return x * jax.nn.sigmoid(x)


def _rep2(t):
    h = t.shape[0]
    return jnp.broadcast_to(t[:, None], (h, 2) + t.shape[1:]).reshape((2 * h,) + t.shape[1:])


def _ret_chunk(q1, q2, k1, k2, v, gate, gn_g, s, cos, sin, intra, qdec, kdec, cdec):
    q = jnp.concatenate([q1 * cos - q2 * sin, q1 * sin + q2 * cos], axis=-1)
    k = jnp.concatenate([k1 * cos - k2 * sin, k1 * sin + k2 * cos], axis=-1) * (RET_DK ** -0.5)
    scores = _nt(q, k) * intra
    y = _nn(scores, v) + _nn(q * qdec, s)
    s_new = s * cdec + _tn(k * kdec, v)
    mu = jnp.mean(y, -1, keepdims=True)
    yc = y - mu
    var = jnp.mean(yc * yc, -1, keepdims=True)
    o = _silu(gate) * (yc * lax.rsqrt(var + GN_EPS) * gn_g)
    return o, s_new


def _ret_consts():
    log_gamma = np.log1p(-np.exp2(-5.0 - np.arange(RET_HEADS, dtype=np.float64)))
    idx = np.arange(CHUNK, dtype=np.float64)
    lg = log_gamma[:, None]
    intra = np.exp(lg[..., None] * np.abs(idx[:, None] - idx[None, :]))
    qdec = np.exp(lg * (idx + 1.0))[..., None]
    kdec = np.exp(lg * (CHUNK - 1.0 - idx))[..., None]
    cdec = np.exp(log_gamma * CHUNK)[:, None, None]
    return [jnp.asarray(t, F32) for t in (intra, qdec, kdec, cdec)]


def _rope_tables(seq):
    half = RET_DK // 2
    inv = ROPE_BASE ** (-jnp.arange(half, dtype=F32) / half)
    ang = jnp.arange(seq).astype(F32)[:, None] * inv[None, :]
    return jnp.cos(ang), jnp.sin(ang)


RET_HB = 8


def _ret_load(q_ref, k_ref, v_ref, gate_ref):
    hb, dk, dv, h = RET_HB, RET_DK, RET_DV, RET_DK // 2
    q, k, v, gate = q_ref[...], k_ref[...], v_ref[...], gate_ref[...]
    q1 = jnp.stack([q[:, i * dk:i * dk + h] for i in range(hb)])
    q2 = jnp.stack([q[:, i * dk + h:(i + 1) * dk] for i in range(hb)])
    k1 = jnp.stack([k[:, i * dk:i * dk + h] for i in range(hb)])
    k2 = jnp.stack([k[:, i * dk + h:(i + 1) * dk] for i in range(hb)])
    vs = jnp.stack([v[:, i * dv:(i + 1) * dv] for i in range(hb)])
    gs = jnp.stack([gate[:, i * dv:(i + 1) * dv] for i in range(hb)])
    return q1, q2, k1, k2, vs, gs


def _ret_specs(n_chunks, rev):
    hb = RET_HB
    cidx = (lambda n: n_chunks - 1 - n) if rev else (lambda n: n)
    qw, vw = hb * RET_DK, hb * RET_DV
    tok = [
        pl.BlockSpec((CHUNK, qw), lambda h, n: (cidx(n), h)),
        pl.BlockSpec((CHUNK, qw), lambda h, n: (cidx(n), RET_QK // qw + h)),
        pl.BlockSpec((CHUNK, vw), lambda h, n: (cidx(n), 2 * RET_QK // vw + h)),
        pl.BlockSpec((CHUNK, vw), lambda h, n: (cidx(n), (2 * RET_QK + RET_VW) // vw + h)),
        pl.BlockSpec((CHUNK, RET_DK // 2), lambda h, n: (cidx(n), 0)),
        pl.BlockSpec((CHUNK, RET_DK // 2), lambda h, n: (cidx(n), 0)),
    ]
    const = [
        pl.BlockSpec((hb, CHUNK, CHUNK), lambda h, n: (h, 0, 0)),
        pl.BlockSpec((hb, CHUNK, 1), lambda h, n: (h, 0, 0)),
        pl.BlockSpec((hb, CHUNK, 1), lambda h, n: (h, 0, 0)),
        pl.BlockSpec((hb, 1, 1), lambda h, n: (h, 0, 0)),
        pl.BlockSpec((hb, 1, RET_DV), lambda h, n: (h, 0, 0)),
    ]
    state = pl.BlockSpec((1, hb, RET_DK, RET_DV), lambda h, n: (cidx(n), h, 0, 0))
    return tok, const, state, cidx


def _grid_call(body, *, name, grid, in_specs, out_specs, out_shape, scratch_shapes, args, comm=None):
    if comm is None:
        return pl.pallas_call(body, name=name, grid=grid, in_specs=in_specs, out_specs=out_specs, out_shape=out_shape,
                              scratch_shapes=scratch_shapes,
                              compiler_params=_cparams(("parallel", "arbitrary")))(*args)
    n_in, n_out, n_scr = len(in_specs), len(out_specs), len(scratch_shapes)
    n_ci, n_co = len(comm.ins), len(comm.out_shapes)
    total = grid[0] * grid[1]
    mid_step = min(int(COMM_MID * total), total - 1)

    def carrying(*refs):
        ins, ci = refs[:n_in], refs[n_in:n_in + n_ci]
        at = n_in + n_ci
        outs, co = refs[at:at + n_out], refs[at + n_out:at + n_out + n_co]
        at += n_out + n_co
        scr, sems = refs[at:at + n_scr], refs[at + n_scr:]
        step = pl.program_id(0) * grid[1] + pl.program_id(1)
        pl.when(step == 0)(lambda: comm.start(ci, co, *sems))
        body(*ins, *outs, *scr)
        pl.when(step == mid_step)(lambda: comm.mid(ci, co, *sems))
        pl.when(step == total - 1)(lambda: comm.finish(ci, co, *sems))

    hbm = pl.BlockSpec(memory_space=pl.ANY)
    return pl.pallas_call(
        carrying, name=name, grid=grid,
        in_specs=list(in_specs) + [hbm] * n_ci, out_specs=list(out_specs) + [hbm] * n_co,
        out_shape=list(out_shape) + list(comm.out_shapes),
        scratch_shapes=list(scratch_shapes) + list(comm.scratch),
        compiler_params=_cparams(("arbitrary", "arbitrary")))(*args, *comm.ins)


def _ret_fwd(p, cos, sin, gn_g, comm=None):
    seq = p.shape[0]
    nc = seq // CHUNK
    hb = RET_HB
    tok, const, state, _ = _ret_specs(nc, False)

    def body(q_ref, k_ref, v_ref, gate_ref, cos_ref, sin_ref, intra_ref, qdec_ref, kdec_ref, cdec_ref, gng_ref,
             o_ref, ssave_ref, s_scr):
        @pl.when(pl.program_id(1) == 0)
        def _():
            s_scr[...] = jnp.zeros_like(s_scr)

        q1, q2, k1, k2, v, gate = _ret_load(q_ref, k_ref, v_ref, gate_ref)
        s = s_scr[...]
        ssave_ref[0] = s.astype(BF16)
        o, s_new = _ret_chunk(q1, q2, k1, k2, v, gate, gng_ref[...], s, cos_ref[...], sin_ref[...],
                              intra_ref[...], qdec_ref[...], kdec_ref[...], cdec_ref[...])
        s_scr[...] = s_new
        o_ref[...] = jnp.concatenate([o[i] for i in range(hb)], axis=-1).astype(o_ref.dtype)

    return _grid_call(
        body, name="ret_fwd", comm=comm,
        grid=(RET_HEADS // hb, nc),
        in_specs=tok + const,
        out_specs=[pl.BlockSpec((CHUNK, hb * RET_DV), lambda h, n: (n, h)), state],
        out_shape=[jax.ShapeDtypeStruct((seq, RET_VW), BF16),
                   jax.ShapeDtypeStruct((nc, RET_HEADS, RET_DK, RET_DV), BF16)],
        scratch_shapes=[pltpu.VMEM((hb, RET_DK, RET_DV), F32)],
        args=(p, p, p, p, cos, sin, *_ret_consts(), gn_g.reshape(RET_HEADS, 1, RET_DV)))


def _ret_bwd(p, cos, sin, gn_g, ssave, do, comm=None):
    seq = p.shape[0]
    nc = seq // CHUNK
    hb = RET_HB
    tok, const, state, cidx = _ret_specs(nc, True)

    assert hb == RET_HEADS

    def body(q_ref, k_ref, v_ref, gate_ref, cos_ref, sin_ref, intra_ref, qdec_ref, kdec_ref, cdec_ref, gng_ref,
             ssave_ref, do_ref, dp_ref, dgng_ref, ds_scr):
        @pl.when(pl.program_id(1) == 0)
        def _():
            ds_scr[...] = jnp.zeros_like(ds_scr)
            dgng_ref[...] = jnp.zeros_like(dgng_ref)

        q1, q2, k1, k2, v, gate = _ret_load(q_ref, k_ref, v_ref, gate_ref)
        do = do_ref[...]
        dos = jnp.stack([do[:, i * RET_DV:(i + 1) * RET_DV] for i in range(hb)]).astype(F32)
        fn = functools.partial(_ret_chunk, cos=cos_ref[...], sin=sin_ref[...], intra=intra_ref[...],
                               qdec=qdec_ref[...], kdec=kdec_ref[...], cdec=cdec_ref[...])
        _, vjp = jax.vjp(fn, q1, q2, k1, k2, v, gate, gng_ref[...], ssave_ref[0].astype(F32))
        dq1, dq2, dk1, dk2, dv, dgate, dgng, ds = vjp((dos, ds_scr[...]))
        ds_scr[...] = ds
        dgng_ref[...] += dgng
        pieces = ([t[i] for i in range(hb) for t in (dq1, dq2)] + [t[i] for i in range(hb) for t in (dk1, dk2)]
                  + [dv[i] for i in range(hb)] + [dgate[i] for i in range(hb)])
        dp_ref[...] = jnp.concatenate([t.astype(dp_ref.dtype) for t in pieces], axis=-1)

    vw = hb * RET_DV
    width = 2 * RET_QK + 2 * RET_VW
    return _grid_call(
        body, name="ret_bwd", comm=comm,
        grid=(RET_HEADS // hb, nc),
        in_specs=tok + const + [state, pl.BlockSpec((CHUNK, vw), lambda h, n: (cidx(n), h))],
        out_specs=[pl.BlockSpec((CHUNK, width), lambda h, n: (cidx(n), 0)),
                   pl.BlockSpec((hb, 1, RET_DV), lambda h, n: (h, 0, 0))],
        out_shape=[jax.ShapeDtypeStruct((seq, width), BF16),
                   jax.ShapeDtypeStruct((RET_HEADS, 1, RET_DV), F32)],
        scratch_shapes=[pltpu.VMEM((hb, RET_DK, RET_DV), F32)],
        args=(p, p, p, p, cos, sin, *_ret_consts(), gn_g.reshape(RET_HEADS, 1, RET_DV), ssave, do))


def _gdn_common(qr, kr, gc_c, gc_r):
    qn = qr * lax.rsqrt(jnp.sum(qr * qr, -1, keepdims=True) + L2_EPS) * (GDN_DK ** -0.5)
    kn = kr * lax.rsqrt(jnp.sum(kr * kr, -1, keepdims=True) + L2_EPS)
    causal = _iota2((CHUNK, CHUNK), 0) >= _iota2((CHUNK, CHUNK), 1)
    decay = jnp.exp(jnp.where(causal, gc_c - gc_r, -1e30))
    return _rep2(qn), _rep2(kn), decay


def _gdn_a(k, decay, beta_c):
    strict = _iota2((CHUNK, CHUNK), 0) > _iota2((CHUNK, CHUNK), 1)
    return jnp.where(strict, _nt(k * beta_c, k) * decay, 0.0)


@jax.custom_vjp
def _inv_saved(a, t):
    return t


_inv_saved.defvjp(lambda a, t: (t, t),
                  lambda t, dt: (-_dg(_dg(t, dt, _TN), t, _NT), jnp.zeros_like(t)))


def _gdn_chunk(qr, kr, v, z, beta_c, gc_c, gc_r, norm_g, t_saved, s):
    q, k, decay = _gdn_common(qr, kr, gc_c, gc_r)
    t = _inv_saved(_gdn_a(k, decay, beta_c), t_saved)
    return _gdn_rest(q, k, decay, v, z, beta_c, gc_c, norm_g, t, s)


def _gdn_rest(q, k, decay, v, z, beta_c, gc_c, norm_g, t, s):
    eg = jnp.exp(gc_c)
    u = _nn(t, v * beta_c)
    w = _nn(t, k * (beta_c * eg))
    attn = _nt(q, k) * decay
    v_new = u - _nn(w, s)
    y = _nn(q * eg, s) + _nn(attn, v_new)
    last = _iota2((1, CHUNK, 1), 1) == CHUNK - 1
    gl = jnp.sum(jnp.where(last, gc_c, 0.0), axis=1, keepdims=True)
    s_new = s * jnp.exp(gl) + _tn(k * jnp.exp(gl - gc_c), v_new)
    yn = y * lax.rsqrt(jnp.mean(y * y, -1, keepdims=True) + RMS_EPS) * norm_g
    return yn * _silu(z), s_new


GDN_HK = 16


def _gdn_load(q_ref, k_ref, v_ref, z_ref):
    hk, hb, d = GDN_HK, 2 * GDN_HK, GDN_DK
    q, k, v, z = q_ref[...], k_ref[...], v_ref[...], z_ref[...]
    qs = jnp.stack([q[:, i * d:(i + 1) * d] for i in range(hk)])
    ks = jnp.stack([k[:, i * d:(i + 1) * d] for i in range(hk)])
    vs = jnp.stack([v[:, i * d:(i + 1) * d] for i in range(hb)])
    zs = jnp.stack([z[:, i * d:(i + 1) * d] for i in range(hb)])
    return qs, ks, vs, zs


def _gdn_specs(n_chunks, rev):
    hk, hb = GDN_HK, 2 * GDN_HK
    cidx = (lambda n: n_chunks - 1 - n) if rev else (lambda n: n)
    qw, vw = hk * GDN_DK, hb * GDN_DV
    tok = [
        pl.BlockSpec((CHUNK, qw), lambda h, n: (cidx(n), h)),
        pl.BlockSpec((CHUNK, qw), lambda h, n: (cidx(n), GDN_QK // qw + h)),
        pl.BlockSpec((CHUNK, vw), lambda h, n: (cidx(n), 2 * GDN_QK // vw + h)),
        pl.BlockSpec((CHUNK, vw), lambda h, n: (cidx(n), GDN_QKV // vw + h)),
        pl.BlockSpec((1, hb, CHUNK, 1), lambda h, n: (cidx(n), h, 0, 0)),
        pl.BlockSpec((1, hb, CHUNK, 1), lambda h, n: (cidx(n), h, 0, 0)),
        pl.BlockSpec((1, hb, 1, CHUNK), lambda h, n: (cidx(n), h, 0, 0)),
        pl.BlockSpec((1, GDN_DV), lambda h, n: (0, 0)),
    ]
    tsave = pl.BlockSpec((1, hb, CHUNK, CHUNK), lambda h, n: (cidx(n), h, 0, 0))
    ssave = pl.BlockSpec((1, hb, GDN_DK, GDN_DV), lambda h, n: (cidx(n), h, 0, 0))
    return tok, tsave, ssave, cidx


def _gdn_fwd(c, p, beta_c, gc_c, gc_r, norm_g):
    seq = c.shape[0]
    nc = seq // CHUNK
    hk, hb = GDN_HK, 2 * GDN_HK
    tok, tsave, ssave, _ = _gdn_specs(nc, False)

    def body(q_ref, k_ref, v_ref, z_ref, beta_ref, gcc_ref, gcr_ref, ng_ref, o_ref, tsave_ref, ssave_ref, s_scr):
        @pl.when(pl.program_id(1) == 0)
        def _():
            s_scr[...] = jnp.zeros_like(s_scr)

        qr, kr, v, z = _gdn_load(q_ref, k_ref, v_ref, z_ref)
        beta, gcc, gcr = beta_ref[0], gcc_ref[0], gcr_ref[0]
        s = s_scr[...]
        ssave_ref[0] = s.astype(BF16)
        q, k, decay = _gdn_common(qr, kr, gcc, gcr)
        t = _inv_unit_lower(_gdn_a(k, decay, beta))
        tsave_ref[0] = t.astype(BF16)
        o, s_new = _gdn_rest(q, k, decay, v, z, beta, gcc, ng_ref[...], t, s)
        s_scr[...] = s_new
        o_ref[...] = jnp.concatenate([o[i] for i in range(hb)], axis=-1).astype(o_ref.dtype)

    return pl.pallas_call(
        body, name="gdn_fwd",
        grid=(GDN_K_HEADS // hk, nc),
        in_specs=tok,
        out_specs=[pl.BlockSpec((CHUNK, hb * GDN_DV), lambda h, n: (n, h)), tsave, ssave],
        out_shape=[jax.ShapeDtypeStruct((seq, GDN_VW), BF16),
                   jax.ShapeDtypeStruct((nc, GDN_V_HEADS, CHUNK, CHUNK), BF16),
                   jax.ShapeDtypeStruct((nc, GDN_V_HEADS, GDN_DK, GDN_DV), BF16)],
        scratch_shapes=[pltpu.VMEM((hb, GDN_DK, GDN_DV), F32)],
        compiler_params=_cparams(("parallel", "arbitrary")),
    )(c, c, c, p, beta_c, gc_c, gc_r, norm_g.reshape(1, GDN_DV))


def _gdn_bwd(c, p, beta_c, gc_c, gc_r, norm_g, tsave, ssave, do, comm=None):
    seq = c.shape[0]
    nc = seq // CHUNK
    hk, hb = GDN_HK, 2 * GDN_HK
    nhb = GDN_K_HEADS // hk
    tok, tsave_spec, ssave_spec, cidx = _gdn_specs(nc, True)

    assert hk == GDN_K_HEADS

    def body(q_ref, k_ref, v_ref, z_ref, beta_ref, gcc_ref, gcr_ref, ng_ref, t_ref, s_ref, do_ref,
             dc_ref, dz_ref, dbeta_ref, dgcc_ref, dgcr_ref, dng_ref, ds_scr):
        @pl.when(pl.program_id(1) == 0)
        def _():
            ds_scr[...] = jnp.zeros_like(ds_scr)
            dng_ref[...] = jnp.zeros_like(dng_ref)

        qr, kr, v, z = _gdn_load(q_ref, k_ref, v_ref, z_ref)
        beta, gcc, gcr = beta_ref[0], gcc_ref[0], gcr_ref[0]
        do = do_ref[...]
        dos = jnp.stack([do[:, i * GDN_DV:(i + 1) * GDN_DV] for i in range(hb)]).astype(F32)
        _, vjp = jax.vjp(_gdn_chunk, qr, kr, v, z, beta, gcc, gcr, ng_ref[...], t_ref[0].astype(F32),
                         s_ref[0].astype(F32))
        dqr, dkr, dv, dz, dbeta, dgcc, dgcr, dng, _, ds = vjp((dos, ds_scr[...]))
        ds_scr[...] = ds
        dng_ref[...] += dng[None]
        dc_ref[...] = jnp.concatenate([dqr[i] for i in range(hk)] + [dkr[i] for i in range(hk)]
                                      + [dv[i] for i in range(hb)], axis=-1)
        dz_ref[...] = jnp.concatenate([dz[i] for i in range(hb)], axis=-1).astype(dz_ref.dtype)
        dbeta_ref[0] = dbeta
        dgcc_ref[0] = dgcc
        dgcr_ref[0] = dgcr

    vw = hb * GDN_DV
    col = pl.BlockSpec((1, hb, CHUNK, 1), lambda h, n: (cidx(n), h, 0, 0))
    row = pl.BlockSpec((1, hb, 1, CHUNK), lambda h, n: (cidx(n), h, 0, 0))
    return _grid_call(
        body, name="gdn_bwd", comm=comm,
        grid=(nhb, nc),
        in_specs=tok + [tsave_spec, ssave_spec, pl.BlockSpec((CHUNK, vw), lambda h, n: (cidx(n), h))],
        out_specs=[pl.BlockSpec((CHUNK, GDN_QKV), lambda h, n: (cidx(n), 0)),
                   pl.BlockSpec((CHUNK, vw), lambda h, n: (cidx(n), GDN_QKV // vw)),
                   col, col, row,
                   pl.BlockSpec((1, 1, GDN_DV), lambda h, n: (h, 0, 0))],
        out_shape=[jax.ShapeDtypeStruct((seq, GDN_QKV), F32), jax.ShapeDtypeStruct((seq, GDN_QKV + GDN_VW), BF16),
                   jax.ShapeDtypeStruct((nc, GDN_V_HEADS, CHUNK, 1), F32),
                   jax.ShapeDtypeStruct((nc, GDN_V_HEADS, CHUNK, 1), F32),
                   jax.ShapeDtypeStruct((nc, GDN_V_HEADS, 1, CHUNK), F32),
                   jax.ShapeDtypeStruct((nhb, 1, GDN_DV), F32)],
        scratch_shapes=[pltpu.VMEM((hb, GDN_DK, GDN_DV), F32)],
        args=(c, c, c, p, beta_c, gc_c, gc_r, norm_g.reshape(1, GDN_DV), tsave, ssave, do))


CONV_TB = 512
CONV_CB = 1024
HALO = 8


def _conv_taps(ext, w):
    acc = w[GDN_CONV - 1:GDN_CONV] * ext
    for j in range(GDN_CONV - 1):
        acc = acc + w[j:j + 1] * pltpu.roll(ext, GDN_CONV - 1 - j, 0)
    return acc


def _conv_fwd(p, w):
    seq = p.shape[0]
    tb, cb = min(CONV_TB, seq), CONV_CB

    def body(prev_ref, cur_ref, w_ref, o_ref):
        first = pl.program_id(1) == 0
        prev = jnp.where(first, 0.0, prev_ref[...])
        ext = jnp.concatenate([prev, cur_ref[...]], axis=0)
        o_ref[...] = _silu(_conv_taps(ext, w_ref[...])[HALO:])

    return pl.pallas_call(
        body, name="conv_fwd",
        grid=(GDN_QKV // cb, seq // tb),
        in_specs=[pl.BlockSpec((HALO, cb), lambda j, i: (jnp.maximum(i * (tb // HALO) - 1, 0), j)),
                  pl.BlockSpec((tb, cb), lambda j, i: (i, j)),
                  pl.BlockSpec((GDN_CONV, cb), lambda j, i: (0, j))],
        out_specs=pl.BlockSpec((tb, cb), lambda j, i: (i, j)),
        out_shape=jax.ShapeDtypeStruct((seq, GDN_QKV), F32),
        compiler_params=_cparams(("parallel", "arbitrary")),
    )(p, p, w)


def _conv_bwd(p, dc, w, dp):
    seq = p.shape[0]
    tb, cb = min(CONV_TB, seq), CONV_CB
    nt = seq // tb
    last_halo = seq // HALO - 1

    def body(prev_ref, cur_ref, next_ref, dcur_ref, dnext_ref, w_ref, _, du_ref, dw_ref):
        i = pl.program_id(1)

        @pl.when(i == 0)
        def _():
            dw_ref[...] = jnp.zeros_like(dw_ref)

        w = w_ref[...]
        prev = jnp.where(i == 0, 0.0, prev_ref[...])
        ext = jnp.concatenate([prev, cur_ref[...], next_ref[...]], axis=0)
        pre = _conv_taps(ext, w)
        dnext = jnp.where(i == nt - 1, 0.0, dnext_ref[...])
        dext = jnp.concatenate([jnp.zeros((HALO, cb), F32), dcur_ref[...], dnext], axis=0)
        sig = jax.nn.sigmoid(pre)
        dpre = dext * (sig * (1.0 + pre * (1.0 - sig)))
        rows = tb + 2 * HALO
        du = w[GDN_CONV - 1:GDN_CONV] * dpre
        for j in range(GDN_CONV - 1):
            du = du + w[j:j + 1] * pltpu.roll(dpre, rows - (GDN_CONV - 1 - j), 0)
        du_ref[...] = du[HALO:HALO + tb].astype(du_ref.dtype)
        dcore = dpre[HALO:HALO + tb]
        dws = []
        for j in range(GDN_CONV):
            sh = ext if j == GDN_CONV - 1 else pltpu.roll(ext, GDN_CONV - 1 - j, 0)
            dws.append(jnp.sum(dcore * sh[HALO:HALO + tb], axis=0, keepdims=True))
        dw_ref[...] += jnp.concatenate(dws, axis=0)

    hb = tb // HALO
    cur = pl.BlockSpec((tb, cb), lambda j, i: (i, j))
    nxt = pl.BlockSpec((HALO, cb), lambda j, i: (jnp.minimum((i + 1) * hb, last_halo), j))
    return pl.pallas_call(
        body, name="conv_bwd",
        grid=(GDN_QKV // cb, nt),
        in_specs=[pl.BlockSpec((HALO, cb), lambda j, i: (jnp.maximum(i * hb - 1, 0), j)), cur, nxt, cur, nxt,
                  pl.BlockSpec((GDN_CONV, cb), lambda j, i: (0, j)), pl.BlockSpec(memory_space=pl.ANY)],
        out_specs=[cur, pl.BlockSpec((GDN_CONV, cb), lambda j, i: (0, j))],
        out_shape=[jax.ShapeDtypeStruct(dp.shape, dp.dtype), jax.ShapeDtypeStruct((GDN_CONV, GDN_QKV), F32)],
        input_output_aliases={6: 0},
        compiler_params=_cparams(("parallel", "arbitrary")),
    )(p, p, p, dc, dc, w, dp)


GATE_TB = 512


def _split3(g):
    hi = g.astype(BF16)
    r = g - hi.astype(F32)
    mid = r.astype(BF16)
    lo = (r - mid.astype(F32)).astype(BF16)
    return hi, mid, lo


def _tri_chunks(n, upper):
    i, j = _iota2((n, n), 0), _iota2((n, n), 1)
    tri = (i <= j) if upper else (i >= j)
    return jnp.where(tri & ((i // CHUNK) == (j // CHUNK)), 1.0, 0.0).astype(BF16)


def _tri_apply(g, upper):
    tri = _tri_chunks(g.shape[0], upper)
    return sum(jnp.dot(tri, part, preferred_element_type=F32) for part in _split3(g))


@jax.custom_vjp
def _chunk_cumsum(g):
    return _tri_apply(g, False)


_chunk_cumsum.defvjp(lambda g: (_tri_apply(g, False), None), lambda _, d: (_tri_apply(d, True),))


def _gates(b, a, a_log, dt_bias):
    z = a + dt_bias
    softplus = jnp.maximum(z, 0.0) + jnp.log1p(jnp.exp(-jnp.abs(z)))
    g = -jnp.exp(a_log) * softplus
    return jax.nn.sigmoid(b), _chunk_cumsum(g)


def _gates_fwd(b, a, a_log, dt_bias):
    seq, nh = b.shape
    tb = min(GATE_TB, seq)

    def body(b_ref, a_ref, al_ref, dt_ref, beta_ref, gc_ref):
        beta, gc = _gates(b_ref[...], a_ref[...], al_ref[...], dt_ref[...])
        beta_ref[...] = beta
        gc_ref[...] = gc

    tok = pl.BlockSpec((tb, nh), lambda i: (i, 0))
    vec = pl.BlockSpec((1, nh), lambda i: (0, 0))
    return pl.pallas_call(
        body, name="gates_fwd", grid=(seq // tb,),
        in_specs=[tok, tok, vec, vec], out_specs=[tok, tok],
        out_shape=[jax.ShapeDtypeStruct((seq, nh), F32)] * 2,
        compiler_params=_cparams(("parallel",)),
    )(b, a, a_log, dt_bias)


def _gates_bwd(b, a, a_log, dt_bias, dbeta, dgc):
    seq, nh = b.shape
    tb = min(GATE_TB, seq)

    def body(b_ref, a_ref, al_ref, dt_ref, dbeta_ref, dgc_ref, db_ref, da_ref, dal_ref, ddt_ref):
        @pl.when(pl.program_id(0) == 0)
        def _():
            dal_ref[...] = jnp.zeros_like(dal_ref)
            ddt_ref[...] = jnp.zeros_like(ddt_ref)

        _, vjp = jax.vjp(_gates, b_ref[...], a_ref[...], al_ref[...], dt_ref[...])
        db, da, dal, ddt = vjp((dbeta_ref[...], dgc_ref[...]))
        db_ref[...] = db
        da_ref[...] = da
        dal_ref[...] += dal
        ddt_ref[...] += ddt

    tok = pl.BlockSpec((tb, nh), lambda i: (i, 0))
    vec = pl.BlockSpec((1, nh), lambda i: (0, 0))
    return pl.pallas_call(
        body, name="gates_bwd", grid=(seq // tb,),
        in_specs=[tok, tok, vec, vec, tok, tok], out_specs=[tok, tok, vec, vec],
        out_shape=[jax.ShapeDtypeStruct((seq, nh), F32)] * 2 + [jax.ShapeDtypeStruct((1, nh), F32)] * 2,
        compiler_params=_cparams(("arbitrary",)),
    )(b, a, a_log, dt_bias, dbeta, dgc)


LN_TR = 256


def _ln_stats(x, s):
    z = DN_ALPHA * x + s
    mu = jnp.mean(z, -1, keepdims=True)
    zc = z - mu
    var = jnp.mean(zc * zc, -1, keepdims=True)
    rstd = lax.rsqrt(var + LN_EPS)
    return zc * rstd, rstd


def _ln_fwd(x, s, g, b):
    seq, d = x.shape
    tr = min(LN_TR, seq)

    def body(x_ref, s_ref, g_ref, b_ref, o_ref, ob_ref):
        xhat, _ = _ln_stats(x_ref[...], s_ref[...])
        y = xhat * g_ref[...] + b_ref[...]
        o_ref[...] = y
        ob_ref[...] = y.astype(BF16)

    tok = pl.BlockSpec((tr, d), lambda i: (i, 0))
    vec = pl.BlockSpec((1, d), lambda i: (0, 0))
    return pl.pallas_call(
        body, name="ln_fwd", grid=(seq // tr,),
        in_specs=[tok, tok, vec, vec], out_specs=[tok, tok],
        out_shape=[jax.ShapeDtypeStruct((seq, d), F32), jax.ShapeDtypeStruct((seq, d), BF16)],
        compiler_params=_cparams(("parallel",)),
    )(x, s, g.reshape(1, d), b.reshape(1, d))


def _ln_bwd(dy, x, s, g, res=None):
    seq, d = x.shape
    tr = min(LN_TR, seq)

    def body(*refs):
        dy_ref, x_ref, s_ref, g_ref = refs[:4]
        dz_ref, dzb_ref, dg_ref, db_ref = refs[-4:]

        @pl.when(pl.program_id(0) == 0)
        def _():
            dg_ref[...] = jnp.zeros_like(dg_ref)
            db_ref[...] = jnp.zeros_like(db_ref)

        dy = dy_ref[...] if res is None else dy_ref[...] + DN_ALPHA * refs[4][...]
        xhat, rstd = _ln_stats(x_ref[...], s_ref[...])
        dyg = dy * g_ref[...]
        m1 = jnp.mean(dyg, -1, keepdims=True)
        m2 = jnp.mean(dyg * xhat, -1, keepdims=True)
        dz = rstd * (dyg - m1 - xhat * m2)
        dz_ref[...] = dz
        dzb_ref[...] = dz.astype(BF16)
        dg_ref[...] += jnp.sum(dy * xhat, axis=0, keepdims=True)
        db_ref[...] += jnp.sum(dy, axis=0, keepdims=True)

    tok = pl.BlockSpec((tr, d), lambda i: (i, 0))
    vec = pl.BlockSpec((1, d), lambda i: (0, 0))
    return pl.pallas_call(
        body, name="ln_bwd", grid=(seq // tr,),
        in_specs=[tok, tok, tok, vec] + ([] if res is None else [tok]), out_specs=[tok, tok, vec, vec],
        out_shape=[jax.ShapeDtypeStruct((seq, d), F32), jax.ShapeDtypeStruct((seq, d), BF16),
                   jax.ShapeDtypeStruct((1, d), F32), jax.ShapeDtypeStruct((1, d), F32)],
        compiler_params=_cparams(("arbitrary",)),
    )(dy, x, s, g.reshape(1, d), *(() if res is None else (res,)))


def _ln_loss(x, s, g, b, target):
    seq, d = x.shape
    tr = min(LN_TR, seq)

    def body(x_ref, s_ref, g_ref, b_ref, t_ref, loss_ref, dy_ref):
        @pl.when(pl.program_id(0) == 0)
        def _():
            loss_ref[...] = jnp.zeros_like(loss_ref)

        xhat, _ = _ln_stats(x_ref[...], s_ref[...])
        err = xhat * g_ref[...] + b_ref[...] - t_ref[...]
        dy_ref[...] = err * (1.0 / d)
        part = jnp.sum(jnp.sum(err * err, axis=0, keepdims=True), axis=1, keepdims=True)
        loss_ref[...] += part * (0.5 / d)

    tok = pl.BlockSpec((tr, d), lambda i: (i, 0))
    vec = pl.BlockSpec((1, d), lambda i: (0, 0))
    return pl.pallas_call(
        body, name="ln_loss", grid=(seq // tr,),
        in_specs=[tok, tok, vec, vec, tok], out_specs=[pl.BlockSpec((8, 128), lambda i: (0, 0)), tok],
        out_shape=[jax.ShapeDtypeStruct((8, 128), F32), jax.ShapeDtypeStruct((seq, d), F32)],
        compiler_params=_cparams(("arbitrary",)),
    )(x, s, g.reshape(1, d), b.reshape(1, d), target)


COMM_MID = 0.8


def _matmul(a, b, *, ta=False, tb=False, b_sharded=False, out_sharded=False, out_dtypes=(F32,), epilogue=None,
            extras=(), tm=1024, tn=1024, tk=2048, name="matmul", comm=None, b_cols=None):
    m, k = (a.shape[1], a.shape[0]) if ta else a.shape
    if b_sharded:
        bk, bn = b.shape[1], N_DEV * b.shape[2]
        shard_w = b.shape[2]
    else:
        bk, bn = b.shape[0], b_cols or b.shape[1]
    n = bk if tb else bn
    assert k == (bn if tb else bk), (a.shape, b.shape)
    tm, tn, tk = min(tm, m), min(tn, n), min(tk, k)
    if b_sharded:
        if tb:
            tk = math.gcd(tk, shard_w)
        else:
            tn = math.gcd(tn, shard_w)
    if out_sharded:
        tn = math.gcd(tn, n // N_DEV)
    assert m % tm == 0 and n % tn == 0 and k % tk == 0, (m, n, k, tm, tn, tk)
    ni, nj, nk = m // tm, n // tn, k // tk
    dims = (((0 if ta else 1,), (1 if tb else 0,)), ((), ()))
    n_ex, n_out = len(extras), len(out_dtypes)
    n_ci = len(comm.ins) if comm else 0
    n_co = len(comm.out_shapes) if comm else 0
    total = ni * nj * nk
    mid_step = min(int(COMM_MID * total), total - 1)
    in_place = nk > 1 and epilogue is None and tuple(out_dtypes) == (F32,)

    def body(*refs):
        a_ref, b_ref = refs[0], refs[1]
        ex_refs = refs[2:2 + n_ex]
        ci_refs = refs[2 + n_ex:2 + n_ex + n_ci]
        out_refs = refs[2 + n_ex + n_ci:2 + n_ex + n_ci + n_out]
        co_refs = refs[2 + n_ex + n_ci + n_out:2 + n_ex + n_ci + n_out + n_co]
        scratch = refs[2 + n_ex + n_ci + n_out + n_co:]
        if nk == 1:
            acc, sems = None, scratch
        elif in_place:
            acc, sems = out_refs[0], scratch
        else:
            acc, sems = scratch[0], scratch[1:]
        kk = pl.program_id(2)
        step = (pl.program_id(0) * nj + pl.program_id(1)) * nk + kk

        if comm:
            @pl.when(step == 0)
            def _():
                comm.start(ci_refs, co_refs, *sems)

        prod = lax.dot_general(a_ref[...].astype(BF16), b_ref[...].astype(BF16), dims, preferred_element_type=F32)

        def write(res):
            outs = (res,) if epilogue is None else epilogue(res, *[r[...] for r in ex_refs])
            for o_ref, val in zip(out_refs, outs, strict=True):
                o_ref[...] = val.astype(o_ref.dtype)

        if nk == 1:
            write(prod)
        else:
            @pl.when(kk == 0)
            def _():
                acc[...] = prod

            @pl.when(kk > 0)
            def _():
                acc[...] += prod

            if not in_place:
                @pl.when(kk == nk - 1)
                def _():
                    write(acc[...])

        if comm:
            @pl.when(step == mid_step)
            def _():
                comm.mid(ci_refs, co_refs, *sems)

            @pl.when(step == total - 1)
            def _():
                comm.finish(ci_refs, co_refs, *sems)

    a_spec = pl.BlockSpec((tk, tm), lambda i, j, kk: (kk, i)) if ta else pl.BlockSpec((tm, tk), lambda i, j, kk: (i, kk))
    if b_sharded and tb:
        per = shard_w // tk
        b_spec = pl.BlockSpec((None, tn, tk), lambda i, j, kk: (kk // per, j, kk % per))
    elif b_sharded:
        per = shard_w // tn
        b_spec = pl.BlockSpec((None, tk, tn), lambda i, j, kk: (j // per, kk, j % per))
    elif tb:
        b_spec = pl.BlockSpec((tn, tk), lambda i, j, kk: (j, kk))
    else:
        b_spec = pl.BlockSpec((tk, tn), lambda i, j, kk: (kk, j))
    ex_spec = pl.BlockSpec((tm, tn), lambda i, j, kk: (i, j))
    if out_sharded:
        per_o = n // N_DEV // tn
        o_spec = pl.BlockSpec((None, tm, tn), lambda i, j, kk: (j // per_o, i, j % per_o))
        o_shape = (N_DEV, m, n // N_DEV)
    else:
        o_spec, o_shape = ex_spec, (m, n)
    hbm = pl.BlockSpec(memory_space=pl.ANY)
    outs = pl.pallas_call(
        body, name=name, grid=(ni, nj, nk),
        in_specs=[a_spec, b_spec] + [ex_spec] * n_ex + [hbm] * n_ci,
        out_specs=[o_spec] * n_out + [hbm] * n_co,
        out_shape=[jax.ShapeDtypeStruct(o_shape, dt) for dt in out_dtypes] + (list(comm.out_shapes) if comm else []),
        scratch_shapes=([] if nk == 1 or in_place else [pltpu.VMEM((tm, tn), F32)])
        + (list(comm.scratch) if comm else []),
        compiler_params=_cparams(("arbitrary",) * 3 if comm else ("parallel", "parallel", "arbitrary")),
    )(a, b, *extras, *(comm.ins if comm else ()))
    return outs[0] if len(outs) == 1 else outs


def _epi_relu2(acc):
    r = jnp.maximum(acc, 0.0)
    return acc, r * r


def _epi_drelu2(acc, pre):
    return (acc * (2.0 * jnp.maximum(pre, 0.0)),)


def _epi_add(scale):
    return lambda acc, other: (acc + scale * other,)


def _adamw(parts, w, m, v, *, rows_per_step, name, layer=None, n_layers=None, into=None, stacked_inputs=False):
    n_parts, rows, cols = parts.shape
    tr = min(rows_per_step, rows)
    assert rows % tr == 0
    first = layer * (rows // tr) if stacked_inputs else 0

    def body(p_ref, w_ref, m_ref, v_ref, *rest):
        g_ref, d_ref, mo_ref, vo_ref = rest[-4:]
        g = p_ref[0].astype(F32)
        for i in range(1, n_parts):
            g = g + p_ref[i].astype(F32)
        m_new = ADAM_B1 * m_ref[...] + (1.0 - ADAM_B1) * g
        v_new = ADAM_B2 * v_ref[...] + (1.0 - ADAM_B2) * (g * g)
        m_hat = m_new / (1.0 - ADAM_B1 ** ADAM_STEP)
        v_hat = v_new / (1.0 - ADAM_B2 ** ADAM_STEP)
        g_ref[...] = g
        d_ref[...] = -ADAM_LR * (m_hat / (jnp.sqrt(v_hat) + ADAM_EPS) + ADAM_WD * w_ref[...])
        mo_ref[...] = m_new
        vo_ref[...] = v_new

    blk = pl.BlockSpec((tr, cols), lambda i: (first + i, 0))
    if layer is None:
        out_blk, out_shape = blk, (rows, cols)
    else:
        out_blk, out_shape = pl.BlockSpec((None, tr, cols), lambda i: (layer, i, 0)), (n_layers, rows, cols)
    into = list(into or ())
    return pl.pallas_call(
        body, name=name, grid=(rows // tr,),
        in_specs=[pl.BlockSpec((n_parts, tr, cols), lambda i: (0, i, 0)), blk, blk, blk]
        + [pl.BlockSpec(memory_space=pl.ANY)] * len(into),
        out_specs=[out_blk] * 4,
        out_shape=[jax.ShapeDtypeStruct(out_shape, F32)] * 4,
        input_output_aliases={4 + k: k for k in range(len(into))},
        compiler_params=_cparams(("parallel",)),
    )(parts, w, m, v, *into)


def _position():
    return lax.axis_index("x"), lax.axis_index("y"), lax.axis_index("c")


def _comm_scratch(n):
    return [pltpu.SemaphoreType.DMA((7 * n,)), pltpu.SemaphoreType.DMA((7 * n,)), pltpu.SemaphoreType.DMA((n,))]


class _Gather:
    def __init__(self, blocks):
        self.ins = list(blocks)
        self.out_shapes = [jax.ShapeDtypeStruct((N_DEV,) + b.shape, b.dtype) for b in blocks]
        self.scratch = _comm_scratch(len(blocks))

    def _plan(self, n, ins, outs, send_sems, recv_sems, local_sems):
        x, y, c = _position()
        me, sibling = (x, y, c), (x, y, 1 - c)
        chips = [(1 - x, y), (x, 1 - y), (1 - x, 1 - y)]
        x_ref, out_ref = ins[n], outs[n]

        def slot(px, py, pc):
            return out_ref.at[4 * px + 2 * py + pc]

        def copy(k, blk, to, src=None):
            return pltpu.make_async_remote_copy(
                src_ref=slot(*blk) if src is None else src, dst_ref=slot(*blk),
                send_sem=send_sems.at[7 * n + k], recv_sem=recv_sems.at[7 * n + k], device_id=to, device_id_type=MESH)

        mine = lambda: pltpu.make_async_copy(x_ref, slot(*me), local_sems.at[n])
        first = lambda: [copy(0, me, sibling, src=x_ref)] + [copy(1 + j, me, (*chip, c), src=x_ref)
                                                             for j, chip in enumerate(chips)]
        passed = lambda j: copy(4 + j, (*chips[j], c), sibling)
        landed = lambda j: copy(1 + j, (*chips[j], c), me)
        from_sibling = lambda: [copy(0, sibling, me)] + [copy(4 + j, (*chip, 1 - c), me) for j, chip in enumerate(chips)]
        return mine, first, passed, landed, from_sibling

    def start(self, ins, outs, *sems):
        for n in range(len(self.ins)):
            mine, first, _, _, _ = self._plan(n, ins, outs, *sems)
            mine().start()
            for cp in first():
                cp.start()

    def mid(self, ins, outs, *sems):
        plans = [self._plan(n, ins, outs, *sems) for n in range(len(self.ins))]
        for j in range(3):
            for _, _, passed, landed, _ in plans:
                landed(j).wait_recv()
                passed(j).start()

    def finish(self, ins, outs, *sems):
        for n in range(len(self.ins)):
            mine, first, passed, _, from_sibling = self._plan(n, ins, outs, *sems)
            for cp in from_sibling():
                cp.wait_recv()
            for cp in first() + [passed(j) for j in range(3)]:
                cp.wait_send()
            mine().wait()


class _Exchange:
    def __init__(self, parts):
        self.ins = list(parts)
        self.out_shapes = [jax.ShapeDtypeStruct(p.shape, p.dtype) for p in parts]
        self.scratch = _comm_scratch(len(parts))

    def _plan(self, n, ins, outs, send_sems, recv_sems, local_sems):
        x, y, c = _position()
        me = 4 * x + 2 * y + c
        p_ref, out_ref = ins[n], outs[n]
        mine = lambda: pltpu.make_async_copy(p_ref.at[me], out_ref.at[me], local_sems.at[n])

        def copies(landing):
            out = []
            for k in range(1, N_DEV):
                px = 1 - x if k & 4 else x
                py = 1 - y if k & 2 else y
                pc = 1 - c if k & 1 else c
                peer_slot = 4 * px + 2 * py + pc
                out.append(pltpu.make_async_remote_copy(
                    src_ref=p_ref.at[peer_slot], dst_ref=out_ref.at[peer_slot if landing else me],
                    send_sem=send_sems.at[7 * n + k - 1], recv_sem=recv_sems.at[7 * n + k - 1],
                    device_id=(px, py, pc), device_id_type=MESH))
            return out

        return mine, copies

    def start(self, ins, outs, *sems):
        for n in range(len(self.ins)):
            mine, copies = self._plan(n, ins, outs, *sems)
            mine().start()
            for cp in copies(False):
                cp.start()

    def mid(self, ins, outs, *sems):
        pass

    def finish(self, ins, outs, *sems):
        for n in range(len(self.ins)):
            mine, copies = self._plan(n, ins, outs, *sems)
            for cp in copies(True):
                cp.wait_recv()
            for cp in copies(False):
                cp.wait_send()
            mine().wait()


def _comm_alone(comm, name):
    def body(*refs):
        n_i, n_o = len(comm.ins), len(comm.out_shapes)
        ins, outs, sems = refs[:n_i], refs[n_i:n_i + n_o], refs[n_i + n_o:]
        comm.start(ins, outs, *sems)
        comm.mid(ins, outs, *sems)
        comm.finish(ins, outs, *sems)

    hbm = pl.BlockSpec(memory_space=pl.ANY)
    return pl.pallas_call(
        body, name=name, out_shape=list(comm.out_shapes),
        in_specs=[hbm] * len(comm.ins), out_specs=[hbm] * len(comm.out_shapes),
        scratch_shapes=list(comm.scratch),
    )(*comm.ins)


RET_IN_W = 2 * RET_QK + 2 * RET_VW
GDN_IN_W = GDN_QKV + GDN_VW + 2 * GDN_V_HEADS
GDN_TAIL = 2 * GDN_V_HEADS
TAIL_PAD = 128

SMALL_SIZES = (RET_VW, GDN_V_HEADS, GDN_V_HEADS, GDN_DV, DEPTH * D_MODEL, DEPTH * D_MODEL, DEPTH * D_MODEL,
               DEPTH * D_MODEL, GDN_CONV * GDN_QKV)
SMALL_LANES = 128
SMALL_ROWS = -(-sum(SMALL_SIZES) // (8 * SMALL_LANES)) * 8


def _pack_small(*vecs):
    flat = jnp.concatenate([v.reshape(-1).astype(F32) for v in vecs])
    return jnp.pad(flat, (0, SMALL_ROWS * SMALL_LANES - flat.shape[0])).reshape(SMALL_ROWS, SMALL_LANES)


def _unpack_small(buf, shapes):
    flat, out, at = buf.reshape(-1), [], 0
    for shp in shapes:
        n = int(np.prod(shp))
        out.append(flat[at:at + n].reshape(shp))
        at += n
    return out


def _mlp_bwd(dz, h, a, r, w1, w2, name, exchange_dw2=False, da_comm=None):
    dz, dzb = dz
    da = _matmul(dzb, w2, tb=True, out_dtypes=(BF16,), epilogue=_epi_drelu2, extras=(a,), name=name + "_da",
                 comm=da_comm)
    carried = []
    if da_comm is not None:
        da, *carried = da
    dw2 = _matmul(r, dzb, ta=True, out_dtypes=(BF16,), name=name + "_dw2").reshape(N_DEV, -1, D_MODEL)
    dw1 = _matmul(h, da, ta=True, out_sharded=True, out_dtypes=(BF16,), name=name + "_dw1")
    dh = _matmul(da, w1, tb=True, b_sharded=True, tn=D_MODEL, name=name + "_dh",
                 comm=_Exchange([dw2]) if exchange_dw2 else None)
    if exchange_dw2:
        dh, dw2 = dh
    return (dh, dw1, dw2, *carried)


def _chunk_cols(t):
    seq, nh = t.shape
    return t.reshape(seq // CHUNK, CHUNK, nh).transpose(0, 2, 1)[..., None]


def _from_chunk_cols(t):
    nc, nh = t.shape[:2]
    return t[..., 0].transpose(0, 2, 1).reshape(nc * CHUNK, nh)


def _chunk_rows(t):
    seq, nh = t.shape
    return t.reshape(seq // CHUNK, CHUNK, nh).transpose(0, 2, 1)[:, :, None, :]


def _from_chunk_rows(t):
    nc, nh = t.shape[:2]
    return t[:, :, 0, :].transpose(0, 2, 1).reshape(nc * CHUNK, nh)


def kernel(x, ret_w_in, ret_gn_g, ret_w_out, gdn_w_in, gdn_conv_w, gdn_a_log, gdn_dt_bias, gdn_norm_g, gdn_w_out, ln_mix_g, ln_mix_b, mlp_w1, mlp_w2, ln_ffn_g, ln_ffn_b, loss_target, m_ret_w_in, m_ret_gn_g, m_ret_w_out, m_gdn_w_in, m_gdn_conv_w, m_gdn_a_log, m_gdn_dt_bias, m_gdn_norm_g, m_gdn_w_out, m_ln_mix_g, m_ln_mix_b, m_mlp_w1, m_mlp_w2, m_ln_ffn_g, m_ln_ffn_b, v_ret_w_in, v_ret_gn_g, v_ret_w_out, v_gdn_w_in, v_gdn_conv_w, v_gdn_a_log, v_gdn_dt_bias, v_gdn_norm_g, v_gdn_w_out, v_ln_mix_g, v_ln_mix_b, v_mlp_w1, v_mlp_w2, v_ln_ffn_g, v_ln_ffn_b):
    xt, target = x[0], loss_target[0]
    seq = xt.shape[0]
    me = 4 * lax.axis_index("x") + 2 * lax.axis_index("y") + lax.axis_index("c")

    bf = lambda t: t.astype(BF16)
    cos, sin = _rope_tables(seq)
    w_ret_in, = _comm_alone(_Gather([bf(ret_w_in[0])]), "gather_ret_in")
    conv_blk = jnp.pad(gdn_conv_w[0], ((0, HALO - GDN_CONV), (0, 0)))
    shard_in = RET_IN_W // N_DEV
    xb = bf(xt)
    p0, w_ret_out, w1_0, conv_all = _matmul(
        xb, w_ret_in, b_sharded=True, tn=shard_in, name="ret_in",
        comm=_Gather([bf(ret_w_out[0]), bf(mlp_w1[0]), conv_blk]))
    w_ret_out = w_ret_out.reshape(RET_VW, D_MODEL)
    conv_w = conv_all[:, :GDN_CONV].transpose(1, 0, 2).reshape(GDN_CONV, GDN_QKV)
    o0, s0, w2_0 = _ret_fwd(p0, cos, sin, ret_gn_g[0], comm=_Gather([bf(mlp_w2[0])]))
    w2_0 = w2_0.reshape(D_FF, D_MODEL)
    mix0 = _matmul(o0, w_ret_out, name="ret_out")
    h1, h1b = _ln_fwd(xt, mix0, ln_mix_g[0], ln_mix_b[0])
    a0, r0, gdn_in_all = _matmul(h1b, w1_0, b_sharded=True, out_dtypes=(F32, BF16), epilogue=_epi_relu2,
                                 name="mlp0_up", comm=_Gather([bf(gdn_w_in[0])]))
    m0, w_gdn_out, w1_1 = _matmul(r0, w2_0, name="mlp0_down", comm=_Gather([bf(gdn_w_out[0]), bf(mlp_w1[1])]))
    w_gdn_out = w_gdn_out.reshape(GDN_VW, D_MODEL)
    h2, h2b = _ln_fwd(h1, m0, ln_ffn_g[0], ln_ffn_b[0])

    w_gdn_in = gdn_in_all.transpose(1, 0, 2).reshape(D_MODEL, GDN_IN_W)
    main_w = GDN_IN_W - GDN_TAIL
    w_gdn_tail = jnp.pad(w_gdn_in[:, main_w:], ((0, 0), (0, TAIL_PAD - GDN_TAIL)))
    p1, w2_1 = _matmul(h2b, w_gdn_in, b_cols=main_w, name="gdn_in", comm=_Gather([bf(mlp_w2[1])]))
    w2_1 = w2_1.reshape(D_FF, D_MODEL)
    pt = _matmul(h2b, w_gdn_tail, name="gdn_in_tail")
    c1 = _conv_fwd(p1, conv_w)
    b_in, a_in = pt[:, :GDN_V_HEADS], pt[:, GDN_V_HEADS:GDN_TAIL]
    beta, gc = _gates_fwd(b_in, a_in, gdn_a_log, gdn_dt_bias)
    beta_c, gc_c, gc_r = _chunk_cols(beta), _chunk_cols(gc), _chunk_rows(gc)
    o1, t1, s1 = _gdn_fwd(c1, p1, beta_c, gc_c, gc_r, gdn_norm_g[0])
    mix1 = _matmul(o1, w_gdn_out, name="gdn_out")
    h3, h3b = _ln_fwd(h2, mix1, ln_mix_g[1], ln_mix_b[1])
    a1, r1 = _matmul(h3b, w1_1, b_sharded=True, out_dtypes=(F32, BF16), epilogue=_epi_relu2, name="mlp1_up")
    m1 = _matmul(r1, w2_1, name="mlp1_down")
    loss_blk, dh4 = _ln_loss(h3, m1, ln_ffn_g[1], ln_ffn_b[1], target)
    loss = lax.psum(loss_blk[0, 0], ("x", "y", "c"))

    dz, dzb, dg_ffn1, db_ffn1 = _ln_bwd(dh4, h3, m1, ln_ffn_g[1])
    dh3, dw1_1, dw2_1 = _mlp_bwd((dz, dzb), h3b, a1, r1, w1_1, w2_1, "mlp1")
    dz, dzb, dg_mix1, db_mix1 = _ln_bwd(dh3, h2, mix1, ln_mix_g[1], res=dz)
    do1 = _matmul(dzb, w_gdn_out, tb=True, name="gdn_out_do")
    dw_gdn_out = _matmul(o1, dzb, ta=True, out_dtypes=(BF16,), name="gdn_out_dw").reshape(N_DEV, -1, D_MODEL)
    dc1, dp1, dbeta_c, dgc_c, dgc_r, dng, x_w1_1, x_w2_1, x_gdn_out = _gdn_bwd(
        c1, p1, beta_c, gc_c, gc_r, gdn_norm_g[0], t1, s1, do1, comm=_Exchange([dw1_1, dw2_1, dw_gdn_out]))
    dp1, dconv = _conv_bwd(p1, dc1, conv_w, dp1)
    db_in, da_in, dalog, ddt = _gates_bwd(b_in, a_in, gdn_a_log, gdn_dt_bias, _from_chunk_cols(dbeta_c),
                                          _from_chunk_cols(dgc_c) + _from_chunk_rows(dgc_r))
    dpt = jnp.concatenate([db_in, da_in, jnp.zeros((seq, TAIL_PAD - GDN_TAIL), F32)], axis=-1)
    dw_gdn_main = _matmul(h2b, dp1, ta=True, out_dtypes=(BF16,), name="gdn_in_dw")
    dw_gdn_tail = _matmul(h2b, dpt, ta=True, out_dtypes=(BF16,), name="gdn_in_tail_dw")
    dw_gdn_in = jnp.concatenate([dw_gdn_main, dw_gdn_tail[:, :GDN_TAIL]], axis=-1)
    by_owner = lambda t: t.reshape(t.shape[0], N_DEV, GDN_IN_W // N_DEV).transpose(1, 0, 2)
    dw_gdn_top, dw_gdn_bot = by_owner(dw_gdn_in[:D_MODEL // 2]), by_owner(dw_gdn_in[D_MODEL // 2:])
    dh2 = _matmul(dpt, w_gdn_tail, tb=True, epilogue=_epi_add(DN_ALPHA), extras=(dz,), name="gdn_in_tail_dh")
    dh2, x_gdn_top = _matmul(dp1, w_gdn_in, tb=True, b_cols=main_w, epilogue=_epi_add(1.0), extras=(dh2,),
                             name="gdn_in_dh", comm=_Exchange([dw_gdn_top]))

    dz, dzb, dg_ffn0, db_ffn0 = _ln_bwd(dh2, h1, m0, ln_ffn_g[0])
    dh1, dw1_0, x_w2_0, x_gdn_bot = _mlp_bwd((dz, dzb), h1b, a0, r0, w1_0, w2_0, "mlp0", exchange_dw2=True,
                                             da_comm=_Exchange([dw_gdn_bot]))
    dz, dzb, dg_mix0, db_mix0 = _ln_bwd(dh1, xt, mix0, ln_mix_g[0], res=dz)
    do0 = _matmul(dzb, w_ret_out, tb=True, name="ret_out_do")
    dw_ret_out = _matmul(o0, dzb, ta=True, out_dtypes=(BF16,), name="ret_out_dw").reshape(N_DEV, -1, D_MODEL)
    dp0, dgng, x_w1_0 = _ret_bwd(p0, cos, sin, ret_gn_g[0], s0, do0, comm=_Exchange([dw1_0]))
    half = D_MODEL // 2
    dw_ret_top, x_ret_out = _matmul(xb[:, :half], dp0, ta=True, out_sharded=True, out_dtypes=(BF16,), tn=shard_in,
                                    name="ret_in_dw_top", comm=_Exchange([dw_ret_out]))
    dw_ret_bot, x_ret_top = _matmul(xb[:, half:], dp0, ta=True, out_sharded=True, out_dtypes=(BF16,), tn=shard_in,
                                    name="ret_in_dw_bot", comm=_Exchange([dw_ret_top]))
    dx, x_ret_bot = _matmul(dp0, w_ret_in, tb=True, b_sharded=True, epilogue=_epi_add(DN_ALPHA), extras=(dz,),
                            tk=shard_in, name="ret_in_dx", comm=_Exchange([dw_ret_bot]))

    def update(parts, w, m, v, name, **slab):
        shape = parts.shape[1:]
        outs = _adamw(parts, w.reshape(shape), m.reshape(shape), v.reshape(shape), rows_per_step=128, name=name,
                      **slab)
        return outs if slab else [t.reshape(w.shape) for t in outs]

    u_w1 = update(x_w1_1, mlp_w1[1], m_mlp_w1[1], v_mlp_w1[1], "adamw_w1_1", layer=1, n_layers=DEPTH)
    u_w1 = update(x_w1_0, mlp_w1[0], m_mlp_w1[0], v_mlp_w1[0], "adamw_w1_0", layer=0, n_layers=DEPTH, into=u_w1)
    u_w2 = update(x_w2_1, mlp_w2[1], m_mlp_w2[1], v_mlp_w2[1], "adamw_w2_1", layer=1, n_layers=DEPTH)
    u_w2 = update(x_w2_0, mlp_w2[0], m_mlp_w2[0], v_mlp_w2[0], "adamw_w2_0", layer=0, n_layers=DEPTH, into=u_w2)
    halves = dict(n_layers=2, stacked_inputs=True, rows_per_step=128)
    gdn_in_state = (gdn_w_in[0], m_gdn_w_in[0], v_gdn_w_in[0])
    u_gdn_in = _adamw(x_gdn_top, *gdn_in_state, name="adamw_gdn_in_top", layer=0, **halves)
    u_gdn_in = _adamw(x_gdn_bot, *gdn_in_state, name="adamw_gdn_in_bot", layer=1, into=u_gdn_in, **halves)
    u_gdn_in = [t.reshape(gdn_w_in.shape) for t in u_gdn_in]
    ret_in_state = (ret_w_in[0], m_ret_w_in[0], v_ret_w_in[0])
    u_ret_in = _adamw(x_ret_top, *ret_in_state, name="adamw_ret_in_top", layer=0, **halves)
    u_ret_in = _adamw(x_ret_bot, *ret_in_state, name="adamw_ret_in_bot", layer=1, into=u_ret_in, **halves)
    u_ret_in = [t.reshape(ret_w_in.shape) for t in u_ret_in]
    big_out = list(zip(
        u_ret_in,
        update(x_ret_out, ret_w_out, m_ret_w_out, v_ret_w_out, "adamw_ret_out"),
        u_gdn_in,
        update(x_gdn_out, gdn_w_out, m_gdn_w_out, v_gdn_w_out, "adamw_gdn_out"),
        u_w1, u_w2))

    small_w = (ret_gn_g, gdn_a_log, gdn_dt_bias, gdn_norm_g, ln_mix_g, ln_mix_b, ln_ffn_g, ln_ffn_b)
    small_m = (m_ret_gn_g, m_gdn_a_log, m_gdn_dt_bias, m_gdn_norm_g, m_ln_mix_g, m_ln_mix_b, m_ln_ffn_g, m_ln_ffn_b)
    small_v = (v_ret_gn_g, v_gdn_a_log, v_gdn_dt_bias, v_gdn_norm_g, v_ln_mix_g, v_ln_mix_b, v_ln_ffn_g, v_ln_ffn_b)
    small_g = (dgng, dalog, ddt, jnp.sum(dng, axis=0),
               jnp.concatenate([dg_mix0, dg_mix1]), jnp.concatenate([db_mix0, db_mix1]),
               jnp.concatenate([dg_ffn0, dg_ffn1]), jnp.concatenate([db_ffn0, db_ffn1]), dconv)
    small_parts, = _comm_alone(_Gather([_pack_small(*small_g)]), "gather_small_grads")
    zero_conv = jnp.zeros((GDN_CONV, GDN_QKV), F32)
    small_out = _adamw(small_parts, _pack_small(*small_w, zero_conv), _pack_small(*small_m, zero_conv),
                       _pack_small(*small_v, zero_conv), rows_per_step=SMALL_ROWS, name="adamw_small")
    shapes = [t.shape for t in small_w] + [(GDN_CONV, GDN_QKV)]
    small_out = [_unpack_small(t, shapes) for t in small_out]
    conv_g = lax.dynamic_slice(small_out[0][-1], (0, me * (GDN_QKV // N_DEV)), (GDN_CONV, GDN_QKV // N_DEV))
    conv_out = _adamw(conv_g[None], gdn_conv_w[0], m_gdn_conv_w[0], v_gdn_conv_w[0],
                      rows_per_step=GDN_CONV, name="adamw_conv")

    def ordered(kind):
        b, s, cv = big_out[kind], small_out[kind], conv_out[kind][None]
        return [b[0], s[0], b[1], b[2], cv, s[1], s[2], s[3], b[3], s[4], s[5], b[4], b[5], s[6], s[7]]

    return (loss, dx[None], *ordered(0), *ordered(1), *ordered(2), *ordered(3))
```
